```python
import jax
import jax.numpy as jnp
from jax import lax
import numpy as np

D_MODEL = 1024
BATCH = 16
SEQ = 256
DEPTH = 4
DEC_BATCH = 8
DEC_SEQ = 4096
PAST_LEN = 256

GRID_W = 64
N_MIXERS = 4
N_A = len(range(0, DEPTH, N_MIXERS))
N_B = len(range(1, DEPTH, N_MIXERS))
N_C = len(range(2, DEPTH, N_MIXERS))
N_D = len(range(3, DEPTH, N_MIXERS))
HEAD_DIM = 64
QBLK = 128
ROPE_BASE = 10000.0
LN_EPS = 1e-5
RMS_EPS = 1e-6
NEG_INF = -1e30
DEEPNORM_ALPHA = (2 * DEPTH) ** 0.25
DEEPNORM_BETA = (8 * DEPTH) ** -0.25
A_HEADS = 16
A_KV_HEADS = 4
A_WINDOW = 128
A_BLOCK = 128
B_HEADS = 16
B_Q_RANK = 384
B_KV_RANK = 256
B_NOPE = 64
B_ROPE = 32
B_VDIM = 64
C_ORDER = 2
C_SHORT = 3
C_POS_BANDS = 16
C_POS_FEAT = 1 + 2 * C_POS_BANDS
C_FILTER_HID = 64
D_HEADS = 16
MAX_NBR_ROWS = 8
NBR_COLS = 16
N_GROUPS = 4
EXPERTS_PER_GROUP = 8
N_EXPERTS = N_GROUPS * EXPERTS_PER_GROUP
TOP_K = 2
D_EXPERT = 512
MOE_BLK = 128

kernel_name = 'hybrid_diffusion_prefix_ctx_step'


def layer_norm(x, g, b):
    xf = x.astype(jnp.float32)
    mu = jnp.mean(xf, axis=-1, keepdims=True)
    var = jnp.mean(jnp.square(xf - mu), axis=-1, keepdims=True)
    return ((xf - mu) * lax.rsqrt(var + LN_EPS) * g + b).astype(x.dtype)


def rms_norm(x, g):
    xf = x.astype(jnp.float32)
    return (xf * lax.rsqrt(jnp.mean(xf * xf, axis=-1, keepdims=True) + RMS_EPS) * g).astype(x.dtype)


def modulation(cvec, w, b):
    m = jax.nn.silu(cvec) @ w + b
    return jnp.split(m[:, None, :], 6, axis=-1)


def axial_rope(x):
    L, dim = x.shape[1], x.shape[-1]
    half = dim // 2
    nf = half // 2
    t = jnp.arange(L)
    row = (t // GRID_W).astype(jnp.float32)
    col = (t % GRID_W).astype(jnp.float32)
    inv = ROPE_BASE ** (-jnp.arange(nf, dtype=jnp.float32) / nf)
    ang = jnp.concatenate([row[:, None] * inv, col[:, None] * inv], axis=-1)
    cos, sin = jnp.cos(ang)[:, None, :], jnp.sin(ang)[:, None, :]
    xf = x.astype(jnp.float32)
    x1, x2 = xf[..., :half], xf[..., half:]
    return jnp.concatenate([x1 * cos - x2 * sin, x2 * cos + x1 * sin], axis=-1).astype(x.dtype)


def dense_attention(q, k, v, scale, sink=None):
    B, Lq, H, dq = q.shape
    Lk = k.shape[1]
    nb = Lq // QBLK
    qb = jnp.moveaxis(q.reshape(B, nb, QBLK, H, dq), 1, 0)

    def one_block(qi):
        s = jnp.einsum('bqhd,bkhd->bhqk', qi, k, preferred_element_type=jnp.float32) * scale
        if sink is not None:
            s_sink = jnp.broadcast_to(sink.astype(jnp.float32)[None, :, None, None], (B, H, QBLK, 1))
            s = jnp.concatenate([s, s_sink], axis=-1)
        p = jax.nn.softmax(s, axis=-1)[..., :Lk].astype(v.dtype)
        return jnp.einsum('bhqk,bkhd->bqhd', p, v)

    o = lax.map(one_block, qb)
    return jnp.moveaxis(o, 0, 1).reshape(B, Lq, H, v.shape[-1])


def window_gqa_context(h, wq, wk, wv, wo, sink):
    B, L, _ = h.shape
    G = A_HEADS // A_KV_HEADS
    q = (h @ wq).reshape(B, L, A_HEADS, HEAD_DIM)
    k = (h @ wk).reshape(B, L, A_KV_HEADS, HEAD_DIM)
    v = (h @ wv).reshape(B, L, A_KV_HEADS, HEAD_DIM)
    o = dense_attention(q, jnp.repeat(k, G, axis=2), jnp.repeat(v, G, axis=2), HEAD_DIM ** -0.5, sink)
    return o.reshape(B, L, A_HEADS * HEAD_DIM) @ wo, k, v


def window_gqa_latent(h, k_ctx, v_ctx, wq, wk, wv, wo, sink):
    B, L, _ = h.shape
    G = A_HEADS // A_KV_HEADS
    nb = L // A_BLOCK
    n_band = 3 * A_BLOCK
    Lc = k_ctx.shape[1]
    scale = HEAD_DIM ** -0.5
    q = axial_rope((h @ wq).reshape(B, L, A_HEADS, HEAD_DIM)).reshape(B, L, A_KV_HEADS, G, HEAD_DIM)
    k = axial_rope((h @ wk).reshape(B, L, A_KV_HEADS, HEAD_DIM))
    v = (h @ wv).reshape(B, L, A_KV_HEADS, HEAD_DIM)
    pad = ((0, 0), (A_BLOCK, A_BLOCK), (0, 0), (0, 0))
    kp, vp = jnp.pad(k, pad), jnp.pad(v, pad)
    s_sink = jnp.broadcast_to(sink.astype(jnp.float32).reshape(1, A_KV_HEADS, G, 1, 1),
                              (B, A_KV_HEADS, G, A_BLOCK, 1))

    def one_block(n):
        qi = lax.dynamic_slice_in_dim(q, n * A_BLOCK, A_BLOCK, axis=1)
        ki = lax.dynamic_slice_in_dim(kp, n * A_BLOCK, n_band, axis=1)
        vi = lax.dynamic_slice_in_dim(vp, n * A_BLOCK, n_band, axis=1)
        qpos = n * A_BLOCK + jnp.arange(A_BLOCK)
        kpos = (n - 1) * A_BLOCK + jnp.arange(n_band)
        valid = (jnp.abs(kpos[None, :] - qpos[:, None]) <= A_WINDOW) & (kpos[None, :] >= 0) & (kpos[None, :] < L)
        s_lat = jnp.einsum('bqhgd,bkhd->bhgqk', qi, ki, preferred_element_type=jnp.float32) * scale
        s_lat = jnp.where(valid, s_lat, NEG_INF)
        s_ctx = jnp.einsum('bqhgd,bchd->bhgqc', qi, k_ctx, preferred_element_type=jnp.float32) * scale
        p = jax.nn.softmax(jnp.concatenate([s_lat, s_ctx, s_sink], axis=-1), axis=-1).astype(v.dtype)
        return (jnp.einsum('bhgqk,bkhd->bqhgd', p[..., :n_band], vi)
                + jnp.einsum('bhgqc,bchd->bqhgd', p[..., n_band:n_band + Lc], v_ctx))

    o = lax.map(one_block, jnp.arange(nb))
    return jnp.moveaxis(o, 0, 1).reshape(B, L, A_HEADS * HEAD_DIM) @ wo


def mla_project(h, wq_a, q_norm, wq_b, wkv_a, kv_norm, rotate):
    B, L, _ = h.shape
    q = (rms_norm(h @ wq_a, q_norm) @ wq_b).reshape(B, L, B_HEADS, B_NOPE + B_ROPE)
    kv = h @ wkv_a
    c_kv = rms_norm(kv[..., :B_KV_RANK], kv_norm)
    k_rope = kv[..., B_KV_RANK:]
    q_nope, q_rope = q[..., :B_NOPE], q[..., B_NOPE:]
    if rotate:
        q_rope = axial_rope(q_rope)
        k_rope = axial_rope(k_rope[:, :, None, :])[:, :, 0, :]
    return jnp.concatenate([q_nope, q_rope], axis=-1), c_kv, k_rope


def mla_keys(c_kv, k_rope, wk_b, wv_b):
    B, L, _ = c_kv.shape
    k_nope = (c_kv @ wk_b).reshape(B, L, B_HEADS, B_NOPE)
    k = jnp.concatenate([k_nope, jnp.broadcast_to(k_rope[:, :, None, :], (B, L, B_HEADS, B_ROPE))], axis=-1)
    v = (c_kv @ wv_b).reshape(B, L, B_HEADS, B_VDIM)
    return k, v


def mla_context(h, wq_a, q_norm, wq_b, wkv_a, kv_norm, wk_b, wv_b, wo):
    B, L, _ = h.shape
    q, c_kv, k_rope = mla_project(h, wq_a, q_norm, wq_b, wkv_a, kv_norm, False)
    k, v = mla_keys(c_kv, k_rope, wk_b, wv_b)
    o = dense_attention(q, k, v, (B_NOPE + B_ROPE) ** -0.5)
    return o.reshape(B, L, B_HEADS * B_VDIM) @ wo, c_kv, k_rope


def mla_latent(h, ckv_ctx, krope_ctx, wq_a, q_norm, wq_b, wkv_a, kv_norm, wk_b, wv_b, wo):
    B, L, _ = h.shape
    q, c_kv, k_rope = mla_project(h, wq_a, q_norm, wq_b, wkv_a, kv_norm, True)
    k_l, v_l = mla_keys(c_kv, k_rope, wk_b, wv_b)
    k_c, v_c = mla_keys(ckv_ctx, krope_ctx, wk_b, wv_b)
    k = jnp.concatenate([k_c, k_l], axis=1)
    v = jnp.concatenate([v_c, v_l], axis=1)
    o = dense_attention(q, k, v, (B_NOPE + B_ROPE) ** -0.5)
    return o.reshape(B, L, B_HEADS * B_VDIM) @ wo


def short_conv(u, w, b):
    L = u.shape[1]
    p = C_SHORT // 2
    up = jnp.pad(u, ((0, 0), (p, p), (0, 0)))
    return sum(up[:, j:j + L] * w[j] for j in range(C_SHORT)) + b


def hyena_filters(L, w1, b1, w2, b2, w3, freq, log_decay):
    t = jnp.linspace(0.0, 1.0, L, dtype=jnp.float32)[:, None]
    ang = 2.0 * jnp.pi * t * jnp.arange(1, C_POS_BANDS + 1, dtype=jnp.float32)
    z = jnp.concatenate([t, jnp.cos(ang), jnp.sin(ang)], axis=-1)
    a = jnp.sin(freq * (z @ w1 + b1))
    a = jnp.sin(freq * (a @ w2 + b2))
    filt = (a @ w3).reshape(L, C_ORDER, 2, D_MODEL)
    filt = filt * jnp.exp(-jnp.exp(log_decay.astype(jnp.float32)) * t[:, :, None, None])
    return filt / (jnp.sum(jnp.abs(filt), axis=0, keepdims=True) + 1e-6)


def fft_conv_bidir(u, h_fwd, h_bwd):
    L = u.shape[1]
    n = 2 * L
    uf = u.astype(jnp.float32)
    Hf = jnp.fft.rfft(h_fwd, n=n, axis=0)
    Hb = jnp.fft.rfft(h_bwd, n=n, axis=0)
    y_f = jnp.fft.irfft(jnp.fft.rfft(uf, n=n, axis=1) * Hf, n=n, axis=1)[:, :L]
    y_b = jnp.fft.irfft(jnp.fft.rfft(uf[:, ::-1], n=n, axis=1) * Hb, n=n, axis=1)[:, :L][:, ::-1]
    return (y_f + y_b).astype(u.dtype)


def hyena(h, w_in, b_in, conv_w, conv_b, w1, b1, w2, b2, w3, freq, log_decay, skip, wo):
    L = h.shape[1]
    u = short_conv(h @ w_in + b_in, conv_w, conv_b)
    z, x1, x2 = jnp.split(u, 3, axis=-1)
    gates = (x1, x2)
    filt = hyena_filters(L, w1, b1, w2, b2, w3, freq, log_decay)
    for o in range(C_ORDER):
        z = gates[o] * (fft_conv_bidir(z, filt[:, o, 0], filt[:, o, 1]) + skip[o] * z)
    return z @ wo


def na_context(h, wq, wk, wv, wo):
    B, L, _ = h.shape
    q = (h @ wq).reshape(B, L, D_HEADS, HEAD_DIM)
    k = (h @ wk).reshape(B, L, D_HEADS, HEAD_DIM)
    v = (h @ wv).reshape(B, L, D_HEADS, HEAD_DIM)
    o = dense_attention(q, k, v, HEAD_DIM ** -0.5)
    return o.reshape(B, L, D_HEADS * HEAD_DIM) @ wo, k, v


def na_latent(h, k_ctx, v_ctx, wq, wk, wv, wo, rel_bias):
    B, L, _ = h.shape
    rows = L // GRID_W
    kh = min(MAX_NBR_ROWS, rows)
    n_lat = kh * GRID_W
    scale = HEAD_DIM ** -0.5
    q = (h @ wq).reshape(B, rows, GRID_W, D_HEADS, HEAD_DIM)
    k = (h @ wk).reshape(B, rows, GRID_W, D_HEADS, HEAD_DIM)
    v = (h @ wv).reshape(B, rows, GRID_W, D_HEADS, HEAD_DIM)
    col = jnp.arange(GRID_W)
    c0 = jnp.clip(col - NBR_COLS // 2, 0, GRID_W - NBR_COLS)
    col_ok = (col[None, :] >= c0[:, None]) & (col[None, :] < c0[:, None] + NBR_COLS)
    dc = jnp.clip(col[None, :] - col[:, None], 1 - NBR_COLS, NBR_COLS - 1) + NBR_COLS - 1

    def one_row(r):
        r0 = jnp.clip(r - kh // 2, 0, rows - kh)
        qr = lax.dynamic_index_in_dim(q, r, axis=1, keepdims=False)
        kr = lax.dynamic_slice_in_dim(k, r0, kh, axis=1)
        vr = lax.dynamic_slice_in_dim(v, r0, kh, axis=1)
        dr = r0 + jnp.arange(kh) - r + MAX_NBR_ROWS - 1
        bias = rel_bias[:, dr[None, :, None], dc[:, None, :]].astype(jnp.float32)
        s_lat = jnp.einsum('bqhd,bkwhd->bhqkw', qr, kr, preferred_element_type=jnp.float32) * scale + bias
        s_lat = jnp.where(col_ok[:, None, :], s_lat, NEG_INF).reshape(B, D_HEADS, GRID_W, n_lat)
        s_ctx = jnp.einsum('bqhd,bchd->bhqc', qr, k_ctx, preferred_element_type=jnp.float32) * scale
        p = jax.nn.softmax(jnp.concatenate([s_lat, s_ctx], axis=-1), axis=-1).astype(v.dtype)
        p_lat = p[..., :n_lat].reshape(B, D_HEADS, GRID_W, kh, GRID_W)
        return (jnp.einsum('bhqkw,bkwhd->bqhd', p_lat, vr)
                + jnp.einsum('bhqc,bchd->bqhd', p[..., n_lat:], v_ctx))

    o = lax.map(one_row, jnp.arange(rows))
    return jnp.moveaxis(o, 0, 1).reshape(B, L, D_HEADS * HEAD_DIM) @ wo


def dispatch_experts(x, expert, weight, w_gate, w_up, w_down):
    T, D = x.shape
    N = T * TOP_K
    flat_e = expert.reshape(-1)
    flat_t = jnp.repeat(jnp.arange(T, dtype=jnp.int32), TOP_K)
    flat_w = weight.reshape(-1)
    order = jnp.argsort(flat_e)
    se, st, sw = flat_e[order], flat_t[order], flat_w[order]
    counts = jnp.bincount(flat_e, length=N_EXPERTS)
    padded = (counts + MOE_BLK - 1) // MOE_BLK * MOE_BLK
    start = jnp.cumsum(counts) - counts
    pend = jnp.cumsum(padded)
    pstart = pend - padded
    dest = pstart[se] + jnp.arange(N) - start[se]
    P = -(-N // MOE_BLK) * MOE_BLK + N_EXPERTS * MOE_BLK
    nblk = P // MOE_BLK
    buf_t = jnp.full((P,), T, dtype=jnp.int32).at[dest].set(st)
    buf_w = jnp.zeros((P,), dtype=x.dtype).at[dest].set(sw.astype(x.dtype))
    blk_e = jnp.clip(jnp.searchsorted(pend, jnp.arange(nblk) * MOE_BLK, side='right'), 0, N_EXPERTS - 1)
    xp = jnp.concatenate([x, jnp.zeros((1, D), dtype=x.dtype)], axis=0)
    xb = xp[buf_t].reshape(nblk, MOE_BLK, D)

    def one(args):
        xi, e = args
        return (jax.nn.silu(xi @ w_gate[e]) * (xi @ w_up[e])) @ w_down[e]

    yb = lax.map(one, (xb, blk_e)).reshape(P, D)
    return jax.ops.segment_sum(yb * buf_w[:, None], buf_t, num_segments=T + 1)[:T]


def hier_moe(h, wr_g, br_g, wr_e, br_e, w_gate, w_up, w_down):
    B, L, D = h.shape
    x = h.reshape(B * L, D)
    T = x.shape[0]
    lg = (x @ wr_g + br_g).astype(jnp.float32)
    pg = jax.nn.softmax(lg, axis=-1)
    g = jnp.argmax(lg, axis=-1)
    p_g = jnp.max(pg, axis=-1, keepdims=True)
    le = (x @ wr_e + br_e).astype(jnp.float32).reshape(T, N_GROUPS, EXPERTS_PER_GROUP)
    pe = jax.nn.softmax(le[jnp.arange(T), g], axis=-1)
    w_top, e_top = lax.top_k(pe, TOP_K)
    w_top = w_top / jnp.sum(w_top, axis=-1, keepdims=True) * p_g
    expert = g[:, None].astype(jnp.int32) * EXPERTS_PER_GROUP + e_top.astype(jnp.int32)
    y = dispatch_experts(x, expert, w_top, w_gate, w_up, w_down)
    return y.reshape(B, L, D)


def setup_inputs(seed: int = 0) -> dict:
    key = jax.random.key(seed)
    ks = iter(jax.random.split(key, 64))
    D = D_MODEL
    beta = DEEPNORM_BETA

    def nrm(shape, scale=1.0):
        return scale * jax.random.normal(next(ks), shape, jnp.float32)

    def gain(shape):
        return 1.0 + nrm(shape, 0.05)

    inp = {}
    inp['x_prompt'] = nrm((BATCH, SEQ, D))
    inp['x_sample'] = nrm((DEC_BATCH, DEC_SEQ, D))
    inp['cache_a_k'] = nrm((DEC_BATCH, N_A, PAST_LEN, A_KV_HEADS, HEAD_DIM))
    inp['cache_a_v'] = nrm((DEC_BATCH, N_A, PAST_LEN, A_KV_HEADS, HEAD_DIM), 0.5)
    inp['cache_b_ckv'] = nrm((DEC_BATCH, N_B, PAST_LEN, B_KV_RANK))
    inp['cache_b_krope'] = nrm((DEC_BATCH, N_B, PAST_LEN, B_ROPE))
    inp['cache_d_k'] = nrm((DEC_BATCH, N_D, PAST_LEN, D_HEADS, HEAD_DIM))
    inp['cache_d_v'] = nrm((DEC_BATCH, N_D, PAST_LEN, D_HEADS, HEAD_DIM), 0.5)
    inp['c_ctx'] = nrm((D,))
    inp['c'] = nrm((DEC_BATCH, D))
    inp['mod_w'] = nrm((DEPTH, D, 6 * D), D ** -0.5)
    inp['mod_b'] = nrm((DEPTH, 6 * D), 0.02)
    inp['ln_g'] = gain((DEPTH, 2, D))
    inp['ln_b'] = nrm((DEPTH, 2, D), 0.02)
    inp['a_wq'] = nrm((N_A, D, A_HEADS * HEAD_DIM), D ** -0.5)
    inp['a_wk'] = nrm((N_A, D, A_KV_HEADS * HEAD_DIM), D ** -0.5)
    inp['a_wv'] = nrm((N_A, D, A_KV_HEADS * HEAD_DIM), beta * D ** -0.5)
    inp['a_wo'] = nrm((N_A, A_HEADS * HEAD_DIM, D), beta * (A_HEADS * HEAD_DIM) ** -0.5)
    inp['a_sink'] = nrm((N_A, A_HEADS), 0.5)
    inp['b_wq_a'] = nrm((N_B, D, B_Q_RANK), D ** -0.5)
    inp['b_q_norm'] = gain((N_B, B_Q_RANK))
    inp['b_wq_b'] = nrm((N_B, B_Q_RANK, B_HEADS * (B_NOPE + B_ROPE)), B_Q_RANK ** -0.5)
    inp['b_wkv_a'] = nrm((N_B, D, B_KV_RANK + B_ROPE), D ** -0.5)
    inp['b_kv_norm'] = gain((N_B, B_KV_RANK))
    inp['b_wk_b'] = nrm((N_B, B_KV_RANK, B_HEADS * B_NOPE), B_KV_RANK ** -0.5)
    inp['b_wv_b'] = nrm((N_B, B_KV_RANK, B_HEADS * B_VDIM), beta * B_KV_RANK ** -0.5)
    inp['b_wo'] = nrm((N_B, B_HEADS * B_VDIM, D), beta * (B_HEADS * B_VDIM) ** -0.5)
    inp['c_w_in'] = nrm((N_C, D, 3 * D), D ** -0.5)
    inp['c_b_in'] = nrm((N_C, 3 * D), 0.02)
    inp['c_conv_w'] = nrm((N_C, C_SHORT, 3 * D), C_SHORT ** -0.5)
    inp['c_conv_b'] = nrm((N_C, 3 * D), 0.02)
    inp['c_ffn_w1'] = nrm((N_C, C_POS_FEAT, C_FILTER_HID), C_POS_FEAT ** -0.5)
    inp['c_ffn_b1'] = nrm((N_C, C_FILTER_HID), 0.5)
    inp['c_ffn_w2'] = nrm((N_C, C_FILTER_HID, C_FILTER_HID), C_FILTER_HID ** -0.5)
    inp['c_ffn_b2'] = nrm((N_C, C_FILTER_HID), 0.5)
    inp['c_ffn_w3'] = nrm((N_C, C_FILTER_HID, C_ORDER * 2 * D), C_FILTER_HID ** -0.5)
    inp['c_ffn_freq'] = 1.0 + nrm((N_C, C_FILTER_HID), 0.1)
    inp['c_log_decay'] = jax.random.uniform(next(ks), (N_C, C_ORDER, 2, D), jnp.float32, 0.0, float(np.log(32.0)))
    inp['c_skip'] = nrm((N_C, C_ORDER, D), 0.5)
    inp['c_wo'] = nrm((N_C, D, D), beta * D ** -0.5)
    inp['d_wq'] = nrm((N_D, D, D_HEADS * HEAD_DIM), D ** -0.5)
    inp['d_wk'] = nrm((N_D, D, D_HEADS * HEAD_DIM), D ** -0.5)
    inp['d_wv'] = nrm((N_D, D, D_HEADS * HEAD_DIM), beta * D ** -0.5)
    inp['d_wo'] = nrm((N_D, D_HEADS * HEAD_DIM, D), beta * (D_HEADS * HEAD_DIM) ** -0.5)
    inp['d_rel_bias'] = nrm((N_D, D_HEADS, 2 * MAX_NBR_ROWS - 1, 2 * NBR_COLS - 1), 0.2)
    inp['moe_wr_g'] = nrm((DEPTH, D, N_GROUPS), D ** -0.5)
    inp['moe_br_g'] = nrm((DEPTH, N_GROUPS), 0.01)
    inp['moe_wr_e'] = nrm((DEPTH, D, N_EXPERTS), D ** -0.5)
    inp['moe_br_e'] = nrm((DEPTH, N_EXPERTS), 0.01)
    inp['moe_w_gate'] = nrm((DEPTH, N_EXPERTS, D, D_EXPERT), D ** -0.5)
    inp['moe_w_up'] = nrm((DEPTH, N_EXPERTS, D, D_EXPERT), D ** -0.5)
    inp['moe_w_down'] = nrm((DEPTH, N_EXPERTS, D_EXPERT, D), beta * D_EXPERT ** -0.5)
    return inp


def reference(x_prompt, x_sample, cache_a_k, cache_a_v, cache_b_ckv, cache_b_krope, cache_d_k, cache_d_v,
              c_ctx, c, mod_w, mod_b, ln_g, ln_b,
              a_wq, a_wk, a_wv, a_wo, a_sink,
              b_wq_a, b_q_norm, b_wq_b, b_wkv_a, b_kv_norm, b_wk_b, b_wv_b, b_wo,
              c_w_in, c_b_in, c_conv_w, c_conv_b, c_ffn_w1, c_ffn_b1, c_ffn_w2, c_ffn_b2, c_ffn_w3,
              c_ffn_freq, c_log_decay, c_skip, c_wo,
              d_wq, d_wk, d_wv, d_wo, d_rel_bias,
              moe_wr_g, moe_br_g, moe_wr_e, moe_br_e, moe_w_gate, moe_w_up, moe_w_down):
    xc, xl = x_prompt, x_sample
    new_a_k, new_a_v, new_b_ckv, new_b_krope, new_d_k, new_d_v = [], [], [], [], [], []
    for i in range(DEPTH):
        kind, j = i % N_MIXERS, i // N_MIXERS
        shc, scc, gtc, sh2c, sc2c, gt2c = modulation(c_ctx[None, :], mod_w[i], mod_b[i])
        shl, scl, gtl, sh2l, sc2l, gt2l = modulation(c, mod_w[i], mod_b[i])
        hc = xc * (1 + scc) + shc
        hl = xl * (1 + scl) + shl
        if kind == 0:
            oc, kc, vc = window_gqa_context(hc, a_wq[j], a_wk[j], a_wv[j], a_wo[j], a_sink[j])
            ol = window_gqa_latent(hl, cache_a_k[:, j], cache_a_v[:, j], a_wq[j], a_wk[j], a_wv[j], a_wo[j], a_sink[j])
            new_a_k.append(kc)
            new_a_v.append(vc)
        elif kind == 1:
            mp = (b_wq_a[j], b_q_norm[j], b_wq_b[j], b_wkv_a[j], b_kv_norm[j], b_wk_b[j], b_wv_b[j], b_wo[j])
            oc, ckv, kr = mla_context(hc, *mp)
            ol = mla_latent(hl, cache_b_ckv[:, j], cache_b_krope[:, j], *mp)
            new_b_ckv.append(ckv)
            new_b_krope.append(kr)
        elif kind == 2:
            hp = (c_w_in[j], c_b_in[j], c_conv_w[j], c_conv_b[j], c_ffn_w1[j], c_ffn_b1[j], c_ffn_w2[j],
                  c_ffn_b2[j], c_ffn_w3[j], c_ffn_freq[j], c_log_decay[j], c_skip[j], c_wo[j])
            oc = hyena(hc, *hp)
            ol = hyena(hl, *hp)
        else:
            oc, kc, vc = na_context(hc, d_wq[j], d_wk[j], d_wv[j], d_wo[j])
            ol = na_latent(hl, cache_d_k[:, j], cache_d_v[:, j], d_wq[j], d_wk[j], d_wv[j], d_wo[j], d_rel_bias[j])
            new_d_k.append(kc)
            new_d_v.append(vc)
        xc = layer_norm(DEEPNORM_ALPHA * xc + gtc * oc, ln_g[i, 0], ln_b[i, 0])
        xl = layer_norm(DEEPNORM_ALPHA * xl + gtl * ol, ln_g[i, 0], ln_b[i, 0])
        ep = (moe_wr_g[i], moe_br_g[i], moe_wr_e[i], moe_br_e[i], moe_w_gate[i], moe_w_up[i], moe_w_down[i])
        xc = layer_norm(DEEPNORM_ALPHA * xc + gt2c * hier_moe(xc * (1 + sc2c) + sh2c, *ep), ln_g[i, 1], ln_b[i, 1])
        xl = layer_norm(DEEPNORM_ALPHA * xl + gt2l * hier_moe(xl * (1 + sc2l) + sh2l, *ep), ln_g[i, 1], ln_b[i, 1])
    out_a_k = jnp.stack(new_a_k, axis=1)
    out_a_v = jnp.stack(new_a_v, axis=1)
    out_b_ckv = jnp.stack(new_b_ckv, axis=1)
    out_b_krope = jnp.stack(new_b_krope, axis=1)
    out_d_k = jnp.stack(new_d_k, axis=1)
    out_d_v = jnp.stack(new_d_v, axis=1)
    return (xc, xl, out_a_k, out_a_v, out_b_ckv, out_b_krope, out_d_k, out_d_v)
```

```python
import functools

import jax
import jax.numpy as jnp
import numpy as np
from jax import lax
from jax.experimental import pallas as pl
from jax.experimental.pallas import tpu as pltpu

F32 = jnp.float32
BF16 = jnp.bfloat16

GRID_W = 64
HEAD_DIM = 64
ROPE_BASE = 10000.0
LN_EPS = 1e-5
RMS_EPS = 1e-6
NEG_INF = -1e30
DEPTH = 4
DEEPNORM_ALPHA = (2 * DEPTH) ** 0.25
A_HEADS = 16
A_KV_HEADS = 4
A_WINDOW = 128
B_HEADS = 16
B_Q_RANK = 384
B_KV_RANK = 256
B_NOPE = 64
B_ROPE = 32
B_VDIM = 64
B_QPAD = 128
C_POS_BANDS = 16
D_HEADS = 16
MAX_NBR_ROWS = 8
NBR_COLS = 16
N_GROUPS = 4
EXPERTS_PER_GROUP = 8
N_EXPERTS = N_GROUPS * EXPERTS_PER_GROUP
D_EXPERT = 512
MOE_BLK = 256
LANES = 128
FFT_N2 = 128
VMEM_LIMIT = 56 * 1024 * 1024
ROW_TILE = 512


def _params(n_axes):
    return pltpu.CompilerParams(dimension_semantics=("arbitrary",) * n_axes, vmem_limit_bytes=VMEM_LIMIT)


def _dot(a, b):
    return jnp.dot(a, b, preferred_element_type=F32)


def _split(x):
    hi = x.astype(BF16)
    lo = (x - hi.astype(F32)).astype(BF16)
    return hi, lo


def _dot3(a, b):
    ah, al = _split(a)
    bh, bl = _split(b)
    return _dot(ah, bh) + _dot(ah, bl) + _dot(al, bh)


def _layer_norm(y, g, b):
    mu = jnp.mean(y, axis=-1, keepdims=True)
    d = y - mu
    var = jnp.mean(d * d, axis=-1, keepdims=True)
    return d * lax.rsqrt(var + LN_EPS) * g + b


def _rms(y, g):
    return y * lax.rsqrt(jnp.mean(y * y, axis=-1, keepdims=True) + RMS_EPS) * g


def _tile_lanes(t, n):
    reps = n // t.shape[-1]
    return t if reps == 1 else jnp.concatenate([t] * reps, axis=-1)


def row_call(body, m_rows, ins, outs, *, tm=ROW_TILE, group_len=None, name=None):
    nb = None if group_len is None else group_len // tm
    in_specs = []
    for a, kind in ins:
        if kind == "row":
            in_specs.append(pl.BlockSpec((tm, a.shape[1]), lambda i: (i, 0)))
        elif kind == "full":
            in_specs.append(pl.BlockSpec(a.shape, lambda i, nd=a.ndim: (0,) * nd))
        elif kind == "group":
            in_specs.append(pl.BlockSpec((1,) + a.shape[1:], lambda i: (i // nb, 0, 0)))
        elif kind == "pos":
            in_specs.append(pl.BlockSpec((1, tm, a.shape[2]), lambda i: (jnp.minimum(i // nb, 1), i % nb, 0)))
        else:
            raise ValueError(kind)
    out_specs = [pl.BlockSpec((tm, c), lambda i: (i, 0)) for c, _ in outs]
    out_shape = [jax.ShapeDtypeStruct((m_rows, c), dt) for c, dt in outs]
    return pl.pallas_call(
        body, grid=(m_rows // tm,), in_specs=in_specs, out_specs=out_specs, out_shape=out_shape,
        compiler_params=_params(1), name=name)(*[a for a, _ in ins])


def _modulate(x_ref, mod_ref, shift_row, scale_row):
    m = mod_ref[0]
    return x_ref[...] * (1.0 + m[scale_row:scale_row + 1]) + m[shift_row:shift_row + 1]


def modulation_all(cvec, mod_w, mod_b):
    depth, d, d6 = mod_w.shape
    g = cvec.shape[0]
    gp = -(-g // 16) * 16
    cp = jnp.zeros((gp, d), F32).at[:g].set(cvec)
    tn = 1024

    def body(c_ref, w_ref, b_ref, o_ref):
        c = c_ref[...]
        a = (c * jax.nn.sigmoid(c)).astype(BF16)
        o_ref[0] = _dot(a, w_ref[0].astype(BF16)) + b_ref[0]

    out = pl.pallas_call(
        body, grid=(depth, d6 // tn),
        in_specs=[pl.BlockSpec((gp, d), lambda l, j: (0, 0)),
                  pl.BlockSpec((1, d, tn), lambda l, j: (l, 0, j)),
                  pl.BlockSpec((1, 1, tn), lambda l, j: (l, 0, j))],
        out_specs=pl.BlockSpec((1, gp, tn), lambda l, j: (l, 0, j)),
        out_shape=jax.ShapeDtypeStruct((depth, gp, d6), F32),
        compiler_params=_params(2), name="modulation")(cp, mod_w, mod_b.reshape(depth, 1, d6))
    return out[:, :g].reshape(depth, g, 6, d)


def attention(q, kl, vl, *, n_seq, lq, row_off, n_heads, dq, dv, hpb, kvpb, tq, win, scale,
              start_fn=None, kc=None, vc=None, bias=None, type_fn=None, sink=None, name=None):
    nq = lq // tq
    lk = kl.shape[2]
    rep = hpb // kvpb
    has_ctx, has_bias, has_sink = kc is not None, bias is not None, sink is not None
    bias_heads = has_bias and bias.shape[0] > 1
    off_blk = row_off // tq

    def kern(*refs):
        it = iter(refs)
        q_ref, kl_ref, vl_ref = next(it), next(it), next(it)
        kc_ref = next(it) if has_ctx else None
        vc_ref = next(it) if has_ctx else None
        b_ref = next(it) if has_bias else None
        s_ref = next(it) if has_sink else None
        o_ref = next(it)
        i = pl.program_id(2)
        hb = pl.program_id(1)
        if start_fn is None:
            start = 0
        else:
            start = pl.multiple_of(start_fn(i), 64)
        outs = []
        for j in range(hpb):
            kv = j // rep
            qj = q_ref[:, j * dq:(j + 1) * dq]
            k = kl_ref[0, kv, pl.ds(start, win), :]
            v = vl_ref[0, kv, pl.ds(start, win), :]
            s = lax.dot_general(qj, k, (((1,), (1,)), ((), ())), preferred_element_type=F32) * scale
            if has_bias:
                s = s + b_ref[j if bias_heads else 0, 0]
            m = jnp.max(s, axis=1, keepdims=True)
            if has_ctx:
                sc = lax.dot_general(qj, kc_ref[0, kv], (((1,), (1,)), ((), ())),
                                     preferred_element_type=F32) * scale
                m = jnp.maximum(m, jnp.max(sc, axis=1, keepdims=True))
            if has_sink:
                sk = s_ref[hb * hpb + j]
                m = jnp.maximum(m, sk)
            p = jnp.exp(s - m)
            l = jnp.sum(p, axis=1, keepdims=True)
            acc = _dot(p.astype(BF16), v)
            if has_ctx:
                pc = jnp.exp(sc - m)
                l = l + jnp.sum(pc, axis=1, keepdims=True)
                acc = acc + _dot(pc.astype(BF16), vc_ref[0, kv])
            if has_sink:
                l = l + jnp.exp(sk - m)
            outs.append(acc / l)
        o_ref[...] = jnp.concatenate(outs, axis=1).astype(o_ref.dtype)

    ins = [q, kl, vl]
    in_specs = [pl.BlockSpec((tq, hpb * dq), lambda b, h, i: (off_blk + b * nq + i, h)),
                pl.BlockSpec((1, kvpb, lk, dq), lambda b, h, i: (b, h, 0, 0)),
                pl.BlockSpec((1, kvpb, lk, dv), lambda b, h, i: (b, h, 0, 0))]
    if has_ctx:
        lc = kc.shape[2]
        ins += [kc, vc]
        in_specs += [pl.BlockSpec((1, kvpb, lc, dq), lambda b, h, i: (b, h, 0, 0)),
                     pl.BlockSpec((1, kvpb, lc, dv), lambda b, h, i: (b, h, 0, 0))]
    if has_bias:
        ins.append(bias)
        hb_blk = hpb if bias_heads else 1
        in_specs.append(pl.BlockSpec((hb_blk, 1, tq, win),
                                     lambda b, h, i: (h if bias_heads else 0, type_fn(i), 0, 0)))
    if has_sink:
        ins.append(sink.astype(F32))
        in_specs.append(pl.BlockSpec(memory_space=pltpu.SMEM))
    return pl.pallas_call(
        kern, grid=(n_seq, n_heads // hpb, nq), in_specs=in_specs,
        out_specs=pl.BlockSpec((tq, hpb * dv), lambda b, h, i: (b * nq + i, h)),
        out_shape=jax.ShapeDtypeStruct((n_seq * lq, n_heads * dv), BF16),
        compiler_params=_params(3), name=name)(*ins)


def _heads_major(x, n_seq, length, n_heads, d):
    return x.reshape(n_seq, length, n_heads, d).transpose(0, 2, 1, 3)


def _rope_tables(length, dim, lead, tail, reps):
    half = dim // 2
    nf = half // 2
    t = jnp.arange(length)
    row = (t // GRID_W).astype(F32)
    col = (t % GRID_W).astype(F32)
    inv = ROPE_BASE ** (-jnp.arange(nf, dtype=F32) / nf)
    ang = jnp.concatenate([row[:, None] * inv, col[:, None] * inv], axis=-1)
    cos, sin = jnp.cos(ang), jnp.sin(ang)
    c = jnp.concatenate([jnp.ones((length, lead), F32), cos, cos, jnp.ones((length, tail), F32)] * reps, axis=1)
    s = jnp.concatenate([jnp.zeros((length, lead), F32), sin, sin, jnp.zeros((length, tail), F32)] * reps, axis=1)
    return (jnp.stack([jnp.ones_like(c), c]), jnp.stack([jnp.zeros_like(s), s]))


def _rot_cols(w, dim):
    k, n = w.shape
    wb = w.reshape(k, n // dim, dim)
    half = dim // 2
    return jnp.concatenate([-wb[..., half:], wb[..., :half]], axis=-1).reshape(k, n)


def _cs(phase, n, sign):
    ang = (2.0 * np.pi / n) * (phase % n).astype(F32)
    return jnp.cos(ang), sign * jnp.sin(ang)


def _cblock(wr, wi):
    return jnp.concatenate([jnp.concatenate([wr, -wi], axis=-1), jnp.concatenate([wi, wr], axis=-1)], axis=-2)


def mm3(a, x, *, tn=2048, epi=None, name=None):
    bt, k, n = x.shape
    m = a.shape[0]

    def kern(*refs):
        if epi is None:
            a_ref, x_ref, o_ref = refs
        else:
            a_ref, x_ref, g_ref, z_ref, s_ref, o_ref = refs
        y = _dot3(a_ref[...], x_ref[0])
        if epi is not None:
            y = g_ref[0] * (y + s_ref[...] * z_ref[0])
        o_ref[0] = y

    ins = [a, x]
    in_specs = [pl.BlockSpec((m, k), lambda b, j: (0, 0)), pl.BlockSpec((1, k, tn), lambda b, j: (b, 0, j))]
    if epi is not None:
        ins += list(epi)
        in_specs += [pl.BlockSpec((1, m, tn), lambda b, j: (b, 0, j)),
                     pl.BlockSpec((1, m, tn), lambda b, j: (b, 0, j)),
                     pl.BlockSpec((1, tn), lambda b, j: (0, 0))]
    return pl.pallas_call(
        kern, grid=(bt, n // tn), in_specs=in_specs,
        out_specs=pl.BlockSpec((1, m, tn), lambda b, j: (b, 0, j)),
        out_shape=jax.ShapeDtypeStruct((bt, m, n), F32), compiler_params=_params(2), name=name)(*ins)


def spectral(x, mf, g=None, mi=None, *, dt, epi=None, name=None):
    p_n, planes, k1_n, nin, d = x.shape
    nf = mf.shape[1] // 2
    nout = nf if g is None else mi.shape[1] // 2

    def kern(*refs):
        it = iter(refs)
        x_ref, mf_ref = next(it), next(it)
        g_ref = next(it) if g is not None else None
        mi_ref = next(it) if g is not None else None
        if epi is not None:
            gate_ref, z_ref, skip_ref = next(it), next(it), next(it)
        o_ref = next(it)
        xs = [x_ref[0, pp, 0] for pp in range(planes)]
        xin = xs[0] if planes == 1 else jnp.concatenate(xs, axis=0)
        f = _dot3(mf_ref[0], xin)
        if g is not None:
            fr, fi = f[:nf], f[nf:]
            gr, gi = g_ref[0, 0], g_ref[1, 0]
            y = jnp.concatenate([fr * gr - fi * gi, fr * gi + fi * gr], axis=0)
            f = _dot3(mi_ref[0], y)
        for pp in range(2):
            y = f[pp * nout:(pp + 1) * nout]
            if epi is not None:
                y = gate_ref[0, pp, 0] * (y + skip_ref[...] * z_ref[0, pp, 0])
            o_ref[0, pp, 0] = y

    ins = [x, mf]
    in_specs = [pl.BlockSpec((1, planes, 1, nin, dt), lambda k, j, p: (p, 0, k, 0, j)),
                pl.BlockSpec((1,) + mf.shape[1:], lambda k, j, p: (k, 0, 0))]
    if g is not None:
        ins += [g, mi]
        in_specs += [pl.BlockSpec((2, 1, nf, dt), lambda k, j, p: (0, k, 0, j)),
                     pl.BlockSpec((1,) + mi.shape[1:], lambda k, j, p: (k, 0, 0))]
    if epi is not None:
        ins += list(epi)
        in_specs += [pl.BlockSpec((1, 2, 1, nout, dt), lambda k, j, p: (p, 0, k, 0, j)),
                     pl.BlockSpec((1, 2, 1, nout, dt), lambda k, j, p: (p, 0, k, 0, j)),
                     pl.BlockSpec((1, dt), lambda k, j, p: (0, j))]
    return pl.pallas_call(
        kern, grid=(k1_n, d // dt, p_n), in_specs=in_specs,
        out_specs=pl.BlockSpec((1, 2, 1, nout, dt), lambda k, j, p: (p, 0, k, 0, j)),
        out_shape=jax.ShapeDtypeStruct((p_n, 2, k1_n, nout, d), F32),
        compiler_params=_params(3), name=name)(*ins)


def _two_sided(filt, o):
    hf, hb = filt[:, o, 0], filt[:, o, 1]
    return jnp.concatenate([hf[:1] + hb[:1], hf[1:], jnp.zeros_like(hf[:1]), hb[1:][::-1]], axis=0)


def hyena_filters(length, w1, b1, w2, b2, w3, freq, log_decay, name):
    hid = w2.shape[0]
    t = jnp.linspace(0.0, 1.0, length, dtype=F32)[:, None]
    ang = 2.0 * jnp.pi * t * jnp.arange(1, C_POS_BANDS + 1, dtype=F32)
    z = jnp.concatenate([t, jnp.cos(ang), jnp.sin(ang)], axis=-1)
    kpad = LANES - z.shape[1]
    z = jnp.pad(z, ((0, 0), (0, kpad)))
    w1p = jnp.pad(w1, ((0, kpad), (0, 0)))
    ncol = w3.shape[1]
    tn = 256

    def kern(z_ref, w1_ref, b1_ref, w2_ref, b2_ref, f_ref, w3_ref, ld_ref, o_ref):
        fr = f_ref[...]
        a = jnp.sin(fr * (_dot3(z_ref[...], w1_ref[...]) + b1_ref[...]))
        a = jnp.sin(fr * (_dot3(a, w2_ref[...]) + b2_ref[...]))
        filt = _dot3(a, w3_ref[...])
        tt = lax.broadcasted_iota(jnp.int32, filt.shape, 0).astype(F32) * (1.0 / (length - 1))
        filt = filt * jnp.exp(-jnp.exp(ld_ref[...]) * tt)
        o_ref[...] = filt / (jnp.sum(jnp.abs(filt), axis=0, keepdims=True) + 1e-6)

    full = lambda a: pl.BlockSpec(a.shape, lambda j: (0, 0))
    args = [z, w1p, b1.reshape(1, hid), w2, b2.reshape(1, hid), freq.reshape(1, hid)]
    return pl.pallas_call(
        kern, grid=(ncol // tn,),
        in_specs=[full(a) for a in args] + [pl.BlockSpec((hid, tn), lambda j: (0, j)),
                                            pl.BlockSpec((1, tn), lambda j: (0, j))],
        out_specs=pl.BlockSpec((length, tn), lambda j: (0, j)),
        out_shape=jax.ShapeDtypeStruct((length, ncol), F32),
        compiler_params=_params(1), name=name)(*args, w3, log_decay.reshape(1, ncol))


def short_conv(u, w, b, name):
    n_seq, length, c = u.shape
    dt = 256

    def kern(u_ref, w_ref, b_ref, o_ref):
        x = u_ref[0]
        r = lax.broadcasted_iota(jnp.int32, x.shape, 0)
        prev = jnp.where(r == 0, 0.0, pltpu.roll(x, 1, 0))
        nxt = jnp.where(r == length - 1, 0.0, pltpu.roll(x, length - 1, 0))
        wv = w_ref[...]
        o_ref[0] = prev * wv[0:1] + x * wv[1:2] + nxt * wv[2:3] + b_ref[...]

    return pl.pallas_call(
        kern, grid=(n_seq, c // dt),
        in_specs=[pl.BlockSpec((1, length, dt), lambda s, j: (s, 0, j)),
                  pl.BlockSpec((3, dt), lambda s, j: (0, j)),
                  pl.BlockSpec((1, dt), lambda s, j: (0, j))],
        out_specs=pl.BlockSpec((1, length, dt), lambda s, j: (s, 0, j)),
        out_shape=jax.ShapeDtypeStruct(u.shape, F32), compiler_params=_params(2), name=name)(u, w, b)


def _dft_mats_single(n, nin_data, nout):
    k = jnp.arange(n, dtype=jnp.int32)
    fr, fi = _cs(k[:, None] * k[None, :nin_data], n, -1.0)
    mf_data = _cblock(fr, fi)[None]
    gr, gi = _cs(k[:, None] * k[None, :], n, -1.0)
    mf_filt = jnp.concatenate([gr, gi], axis=0)[None]
    ir, ii = _cs(k[:nout, None] * k[None, :], n, 1.0)
    mi = _cblock(ir / n, ii / n)[None]
    return mf_data, mf_filt, mi


def _dft_mats_two_stage(n, n2):
    n1 = n // n2
    k1 = jnp.arange(n1, dtype=jnp.int32)
    t1h = jnp.arange(n1 // 2, dtype=jnp.int32)
    ar, ai = _cs((n // n1) * k1[:, None] * t1h[None, :], n, -1.0)
    a_data = _cblock(ar, ai)
    fr, fi = _cs((n // n1) * k1[:, None] * k1[None, :], n, -1.0)
    a_filt = jnp.concatenate([fr, fi], axis=0)
    br, bi = _cs((n // n1) * t1h[:, None] * k1[None, :], n, 1.0)
    a_inv = _cblock(br, bi)
    t2 = jnp.arange(n2, dtype=jnp.int32)
    kk = k1[:, None, None] + n1 * t2[None, :, None]
    mr, mi_ = _cs(kk * t2[None, None, :], n, -1.0)
    mf = _cblock(mr, mi_)
    vr, vi = _cs(jnp.swapaxes(kk, 1, 2) * t2[None, :, None], n, 1.0)
    mi = _cblock(vr / n, vi / n)
    return a_data, a_filt, a_inv, mf, mi


def _route(logits):
    lane = lax.broadcasted_iota(jnp.int32, logits.shape, 1).astype(F32)
    big = 1e9
    lg = jnp.where(lane < N_GROUPS, logits, -jnp.inf)
    mg = jnp.max(lg, axis=1, keepdims=True)
    gi = jnp.min(jnp.where(lg == mg, lane, big), axis=1, keepdims=True)
    p_g = 1.0 / jnp.sum(jnp.exp(lg - mg), axis=1, keepdims=True)
    lo = N_GROUPS + EXPERTS_PER_GROUP * gi
    le = jnp.where((lane >= lo) & (lane < lo + EXPERTS_PER_GROUP), logits, -jnp.inf)
    m1 = jnp.max(le, axis=1, keepdims=True)
    i1 = jnp.min(jnp.where(le == m1, lane, big), axis=1, keepdims=True)
    le2 = jnp.where(lane == i1, -jnp.inf, le)
    m2 = jnp.max(le2, axis=1, keepdims=True)
    i2 = jnp.min(jnp.where(le2 == m2, lane, big), axis=1, keepdims=True)
    e2 = jnp.exp(m2 - m1)
    w1 = p_g / (1.0 + e2)
    w2 = p_g * e2 / (1.0 + e2)
    idx = jnp.where(lane == 0, i1 - N_GROUPS, jnp.where(lane == 1, i2 - N_GROUPS, 0.0)).astype(jnp.int32)
    w = jnp.where(lane == 0, w1, jnp.where(lane == 1, w2, 0.0))
    return idx, w


def expert_mlp(xs, blk_e, n_used, w_gate, w_up, w_down):
    p_rows, d = xs.shape
    de = w_gate.shape[2]
    nblk = p_rows // MOE_BLK

    def kern(be_ref, nu_ref, x_ref, wg_ref, wu_ref, wd_ref, o_ref):
        used = pl.program_id(0) < nu_ref[0]

        @pl.when(used)
        def _():
            x = x_ref[...]
            g = _dot(x, wg_ref[0])
            u = _dot(x, wu_ref[0])
            a = (g * jax.nn.sigmoid(g) * u).astype(BF16)
            o_ref[...] = _dot(a, wd_ref[0])

        @pl.when(jnp.logical_not(used))
        def _():
            o_ref[...] = jnp.zeros_like(o_ref)

    grid_spec = pltpu.PrefetchScalarGridSpec(
        num_scalar_prefetch=2, grid=(nblk,),
        in_specs=[pl.BlockSpec((MOE_BLK, d), lambda i, be, nu: (i, 0)),
                  pl.BlockSpec((1, d, de), lambda i, be, nu: (be[i], 0, 0)),
                  pl.BlockSpec((1, d, de), lambda i, be, nu: (be[i], 0, 0)),
                  pl.BlockSpec((1, de, d), lambda i, be, nu: (be[i], 0, 0))],
        out_specs=pl.BlockSpec((MOE_BLK, d), lambda i, be, nu: (i, 0)))
    return pl.pallas_call(
        kern, grid_spec=grid_spec, out_shape=jax.ShapeDtypeStruct((p_rows, d), F32),
        compiler_params=_params(1), name="expert_mlp")(blk_e, n_used, xs, w_gate, w_up, w_down)


def _dispatch_plan(idx):
    t = idx.shape[0]
    n = 2 * t
    flat_e = idx.reshape(-1)
    order = jnp.argsort(flat_e, stable=True).astype(jnp.int32)
    se = flat_e[order]
    counts = jnp.bincount(flat_e, length=N_EXPERTS).astype(jnp.int32)
    padded = (counts + MOE_BLK - 1) // MOE_BLK * MOE_BLK
    start = jnp.cumsum(counts) - counts
    pend = jnp.cumsum(padded)
    pstart = pend - padded
    dest = (pstart[se] + jnp.arange(n, dtype=jnp.int32) - start[se]).astype(jnp.int32)
    p_rows = -(-n // MOE_BLK) * MOE_BLK + N_EXPERTS * MOE_BLK
    nblk = p_rows // MOE_BLK
    slot_tok = jnp.zeros((p_rows,), jnp.int32).at[dest].set(order // 2)
    pos = jnp.zeros((n,), jnp.int32).at[order].set(dest).reshape(t, 2)
    blk_e = jnp.clip(jnp.searchsorted(pend, jnp.arange(nblk, dtype=jnp.int32) * MOE_BLK, side="right"),
                     0, N_EXPERTS - 1).astype(jnp.int32)
    n_used = (pend[-1:] // MOE_BLK).astype(jnp.int32)
    return slot_tok, pos, blk_e, n_used


def kernel(x_prompt, x_sample, cache_a_k, cache_a_v, cache_b_ckv, cache_b_krope, cache_d_k, cache_d_v, c_ctx, c, mod_w, mod_b, ln_g, ln_b, a_wq, a_wk, a_wv, a_wo, a_sink, b_wq_a, b_q_norm, b_wq_b, b_wkv_a, b_kv_norm, b_wk_b, b_wv_b, b_wo, c_w_in, c_b_in, c_conv_w, c_conv_b, c_ffn_w1, c_ffn_b1, c_ffn_w2, c_ffn_b2, c_ffn_w3, c_ffn_freq, c_log_decay, c_skip, c_wo, d_wq, d_wk, d_wv, d_wo, d_rel_bias, moe_wr_g, moe_br_g, moe_wr_e, moe_br_e, moe_w_gate, moe_w_up, moe_w_down):
    bc, lc, d = x_prompt.shape
    bl, ll, _ = x_sample.shape
    past = cache_a_k.shape[2]
    gl = ll
    assert bc * lc == gl and d == A_HEADS * HEAD_DIM
    ng = 1 + bl
    t_all = ng * gl
    x = jnp.concatenate([x_prompt.reshape(gl, d), x_sample.reshape(bl * gl, d)], axis=0)
    cvec = jnp.concatenate([c_ctx[None, :], c], axis=0)
    mods = modulation_all(cvec, mod_w, mod_b)
    rc = functools.partial(row_call, group_len=gl)
    row2 = lambda v: v.reshape(1, -1)

    def post_mixer(i, att, wo, x_in):
        wr = jnp.zeros((d, LANES), F32).at[:, :N_GROUPS].set(moe_wr_g[i]).at[:, N_GROUPS:N_GROUPS + N_EXPERTS].set(moe_wr_e[i])
        br = jnp.zeros((1, LANES), F32).at[0, :N_GROUPS].set(moe_br_g[i]).at[0, N_GROUPS:N_GROUPS + N_EXPERTS].set(moe_br_e[i])

        def body(att_ref, x_ref, mod_ref, wo_ref, g_ref, b_ref, wr_ref, br_ref, x1_ref, h_ref, idx_ref, w_ref):
            m = mod_ref[0]
            o = _dot(att_ref[...].astype(BF16), wo_ref[...])
            x1 = _layer_norm(DEEPNORM_ALPHA * x_ref[...] + m[2:3] * o, g_ref[...], b_ref[...])
            x1_ref[...] = x1
            h = x1 * (1.0 + m[4:5]) + m[3:4]
            h_ref[...] = h.astype(BF16)
            idx, w = _route(_dot3(h, wr_ref[...]) + br_ref[...])
            idx_ref[...] = idx
            w_ref[...] = w

        return rc(body, t_all,
                  [(att, "row"), (x_in, "row"), (mods[i], "group"), (wo.astype(BF16), "full"),
                   (row2(ln_g[i, 0]), "full"), (row2(ln_b[i, 0]), "full"), (wr, "full"), (br, "full")],
                  [(d, F32), (d, BF16), (LANES, jnp.int32), (LANES, F32)], name=f"post_mixer{i}")

    def moe(i, x1, h, idx, w):
        slot_tok, pos, blk_e, n_used = _dispatch_plan(idx[:, :2])
        xs = jnp.take(h, slot_tok, axis=0, mode="clip")
        ys = expert_mlp(xs, blk_e, n_used, moe_w_gate[i].astype(BF16), moe_w_up[i].astype(BF16),
                        moe_w_down[i].astype(BF16))
        o0 = jnp.take(ys, pos[:, 0], axis=0, mode="clip")
        o1 = jnp.take(ys, pos[:, 1], axis=0, mode="clip")

        def body(x_ref, o0_ref, o1_ref, w_ref, mod_ref, g_ref, b_ref, out_ref):
            m = mod_ref[0]
            wv = w_ref[...]
            y = wv[:, 0:1] * o0_ref[...] + wv[:, 1:2] * o1_ref[...]
            out_ref[...] = _layer_norm(DEEPNORM_ALPHA * x_ref[...] + m[5:6] * y, g_ref[...], b_ref[...])

        return rc(body, t_all,
                  [(x1, "row"), (o0, "row"), (o1, "row"), (w, "row"), (mods[i], "group"),
                   (row2(ln_g[i, 1]), "full"), (row2(ln_b[i, 1]), "full")],
                  [(d, F32)], name=f"moe_combine{i}")[0]

    def finish_layer(i, att, wo, x_in):
        x1, h, idx, w = post_mixer(i, att, wo, x_in)
        return moe(i, x1, h, idx, w)

    tq = 256
    i = 0
    hd = HEAD_DIM
    kvw = A_KV_HEADS * hd
    cq, sq = _rope_tables(gl, hd, 0, 0, LANES // hd)
    wq, wk, wv = a_wq[0], a_wk[0], a_wv[0]

    def body_a(x_ref, mod_ref, wq_ref, wqr_ref, wk_ref, wkr_ref, wv_ref, c_ref, s_ref, q_ref, k_ref, v_ref):
        h = _modulate(x_ref, mod_ref, 0, 1).astype(BF16)
        cc, ss = c_ref[0], s_ref[0]
        q = _dot(h, wq_ref[...]) * _tile_lanes(cc, d) + _dot(h, wqr_ref[...]) * _tile_lanes(ss, d)
        k = _dot(h, wk_ref[...]) * _tile_lanes(cc, kvw) + _dot(h, wkr_ref[...]) * _tile_lanes(ss, kvw)
        q_ref[...] = q.astype(BF16)
        k_ref[...] = k
        v_ref[...] = _dot(h, wv_ref[...])

    q, k, v = rc(body_a, t_all,
                 [(x, "row"), (mods[i], "group"), (wq.astype(BF16), "full"), (_rot_cols(wq, hd).astype(BF16), "full"),
                  (wk.astype(BF16), "full"), (_rot_cols(wk, hd).astype(BF16), "full"), (wv.astype(BF16), "full"),
                  (cq, "pos"), (sq, "pos")],
                 [(d, BF16), (kvw, F32), (kvw, F32)], name="proj_a")
    out_a_k = k[:gl].reshape(bc, 1, lc, A_KV_HEADS, hd)
    out_a_v = v[:gl].reshape(bc, 1, lc, A_KV_HEADS, hd)
    kb, vb = k.astype(BF16), v.astype(BF16)
    scale = hd ** -0.5
    common = dict(n_heads=A_HEADS, dq=hd, dv=hd, hpb=4, kvpb=1, tq=tq, scale=scale, sink=a_sink[0])
    att_c = attention(q, _heads_major(kb[:gl], bc, lc, A_KV_HEADS, hd), _heads_major(vb[:gl], bc, lc, A_KV_HEADS, hd),
                      n_seq=bc, lq=lc, row_off=0, win=lc, name="attn_a_ctx", **common)
    win_a = 2 * tq
    nq_l = gl // tq
    qi = jnp.arange(tq)[:, None]
    ki = jnp.arange(win_a)[None, :]
    band = jnp.stack([jnp.where(jnp.abs(ki - (qi + off)) <= A_WINDOW, 0.0, NEG_INF)
                      for off in (0, A_WINDOW, 2 * A_WINDOW)]).astype(F32)[None]
    att_l = attention(q, _heads_major(kb[gl:], bl, gl, A_KV_HEADS, hd), _heads_major(vb[gl:], bl, gl, A_KV_HEADS, hd),
                      n_seq=bl, lq=gl, row_off=gl, win=win_a,
                      start_fn=lambda ii: jnp.clip(ii * tq - A_WINDOW, 0, gl - win_a),
                      kc=cache_a_k[:, 0].transpose(0, 2, 1, 3).astype(BF16),
                      vc=cache_a_v[:, 0].transpose(0, 2, 1, 3).astype(BF16),
                      bias=band, type_fn=lambda ii: jnp.where(ii == 0, 0, jnp.where(ii == nq_l - 1, 2, 1)),
                      name="attn_a_lat", **common)
    x = finish_layer(i, jnp.concatenate([att_c, att_l], axis=0), a_wo[0], x)

    i = 1
    hq = B_NOPE + B_ROPE
    qw = B_HEADS * B_QPAD
    wqb = b_wq_b[0].reshape(B_Q_RANK, B_HEADS, hq)
    wqb_rot = jnp.concatenate([jnp.zeros_like(wqb[..., :B_NOPE]),
                               _rot_cols(wqb[..., B_NOPE:].reshape(B_Q_RANK, -1), B_ROPE).reshape(B_Q_RANK, B_HEADS, B_ROPE)],
                              axis=-1)
    padq = lambda wz: jnp.pad(wz, ((0, 0), (0, 0), (0, B_QPAD - hq))).reshape(B_Q_RANK, qw).astype(BF16)
    wkv_c, wkv_r = b_wkv_a[0][:, :B_KV_RANK], b_wkv_a[0][:, B_KV_RANK:]
    cqb, sqb = _rope_tables(gl, B_ROPE, B_NOPE, B_QPAD - hq, 1)
    ckr, skr = _rope_tables(gl, B_ROPE, 0, 0, 1)

    def body_b(x_ref, mod_ref, wqa_ref, qn_ref, wqb_ref, wqbr_ref, wc_ref, kn_ref, wr_ref, wrr_ref, wkb_ref, wvb_ref,
               cq_ref, sq_ref, ck_ref, sk_ref, q_ref, ckv_ref, kr_ref, kn_out, v_out):
        h = _modulate(x_ref, mod_ref, 0, 1).astype(BF16)
        qa = _rms(_dot(h, wqa_ref[...]), qn_ref[...]).astype(BF16)
        q = (_dot(qa, wqb_ref[...]) * _tile_lanes(cq_ref[0], qw) + _dot(qa, wqbr_ref[...]) * _tile_lanes(sq_ref[0], qw))
        q_ref[...] = q.astype(BF16)
        ckv = _rms(_dot(h, wc_ref[...]), kn_ref[...])
        ckv_ref[...] = ckv
        kr_ref[...] = _dot(h, wr_ref[...]) * ck_ref[0] + _dot(h, wrr_ref[...]) * sk_ref[0]
        cb = ckv.astype(BF16)
        kn_out[...] = _dot(cb, wkb_ref[...]).astype(BF16)
        v_out[...] = _dot(cb, wvb_ref[...]).astype(BF16)

    wkb, wvb = b_wk_b[0].astype(BF16), b_wv_b[0].astype(BF16)
    q, ckv, kr, kn, vv = rc(
        body_b, t_all,
        [(x, "row"), (mods[i], "group"), (b_wq_a[0].astype(BF16), "full"), (row2(b_q_norm[0]), "full"),
         (padq(wqb), "full"), (padq(wqb_rot), "full"), (wkv_c.astype(BF16), "full"), (row2(b_kv_norm[0]), "full"),
         (wkv_r.astype(BF16), "full"), (_rot_cols(wkv_r, B_ROPE).astype(BF16), "full"), (wkb, "full"), (wvb, "full"),
         (cqb, "pos"), (sqb, "pos"), (ckr, "pos"), (skr, "pos")],
        [(qw, BF16), (B_KV_RANK, F32), (B_ROPE, F32), (B_HEADS * B_NOPE, BF16), (B_HEADS * B_VDIM, BF16)], name="proj_b")
    out_b_ckv = ckv[:gl].reshape(bc, 1, lc, B_KV_RANK)
    out_b_krope = kr[:gl].reshape(bc, 1, lc, B_ROPE)

    def body_bc(c_ref, wkb_ref, wvb_ref, kn_out, v_out):
        cb = c_ref[...].astype(BF16)
        kn_out[...] = _dot(cb, wkb_ref[...]).astype(BF16)
        v_out[...] = _dot(cb, wvb_ref[...]).astype(BF16)

    n_pc = bl * past
    kn_p, vv_p = row_call(body_bc, n_pc, [(cache_b_ckv[:, 0].reshape(n_pc, B_KV_RANK), "row"), (wkb, "full"), (wvb, "full")],
                          [(B_HEADS * B_NOPE, BF16), (B_HEADS * B_VDIM, BF16)], tm=min(ROW_TILE, n_pc), name="proj_b_past")

    def mla_keys(knope, krope, n_seq, length):
        kn4 = knope.reshape(n_seq, length, B_HEADS, B_NOPE)
        kr4 = jnp.broadcast_to(krope.astype(BF16).reshape(n_seq, length, 1, B_ROPE), (n_seq, length, B_HEADS, B_ROPE))
        z4 = jnp.zeros((n_seq, length, B_HEADS, B_QPAD - hq), BF16)
        return jnp.concatenate([kn4, kr4, z4], axis=-1).transpose(0, 2, 1, 3)

    scale = hq ** -0.5
    common = dict(n_heads=B_HEADS, dq=B_QPAD, dv=B_VDIM, hpb=2, kvpb=2, tq=tq, scale=scale)
    att_c = attention(q, mla_keys(kn[:gl], kr[:gl], bc, lc), _heads_major(vv[:gl], bc, lc, B_HEADS, B_VDIM),
                      n_seq=bc, lq=lc, row_off=0, win=lc, name="attn_b_ctx", **common)
    k_lat = jnp.concatenate([mla_keys(kn_p, cache_b_krope[:, 0].reshape(n_pc, B_ROPE), bl, past),
                             mla_keys(kn[gl:], kr[gl:], bl, gl)], axis=2)
    v_lat = jnp.concatenate([_heads_major(vv_p, bl, past, B_HEADS, B_VDIM),
                             _heads_major(vv[gl:], bl, gl, B_HEADS, B_VDIM)], axis=2)
    att_l = attention(q, k_lat, v_lat, n_seq=bl, lq=gl, row_off=gl, win=past + gl, name="attn_b_lat", **common)
    x = finish_layer(i, jnp.concatenate([att_c, att_l], axis=0), b_wo[0], x)

    i = 2
    d3 = 3 * d

    def body_c(x_ref, mod_ref, w_ref, b_ref, u_ref):
        h = _modulate(x_ref, mod_ref, 0, 1).astype(BF16)
        for j in range(3):
            u_ref[:, j * d:(j + 1) * d] = _dot(h, w_ref[:, j * d:(j + 1) * d]) + b_ref[:, j * d:(j + 1) * d]

    (u,) = rc(body_c, t_all, [(x, "row"), (mods[i], "group"), (c_w_in[0].astype(BF16), "full"), (row2(c_b_in[0]), "full")],
              [(d3, F32)], name="proj_c")
    u_c = short_conv(u[:gl].reshape(bc, lc, d3), c_conv_w[0], row2(c_conv_b[0]), "short_conv_ctx")
    u_l = short_conv(u[gl:].reshape(bl, gl, d3), c_conv_w[0], row2(c_conv_b[0]), "short_conv_lat")
    fargs = (c_ffn_w1[0], c_ffn_b1[0], c_ffn_w2[0], c_ffn_b2[0], c_ffn_w3[0], c_ffn_freq[0], c_log_decay[0])
    filt_c = hyena_filters(lc, *fargs, name="filters_ctx").reshape(lc, 2, 2, d)
    filt_l = hyena_filters(gl, *fargs, name="filters_lat").reshape(gl, 2, 2, d)

    nc = 2 * lc
    mf_data, mf_filt, mi_c = _dft_mats_single(nc, lc, lc)
    g_c = jnp.stack([_two_sided(filt_c, o) for o in range(2)])
    spec_c = spectral(g_c.reshape(2, 1, 1, nc, d), mf_filt, dt=512, name="filt_spec_ctx")
    z_c = u_c[..., :d].reshape(bc // 2, 2, 1, lc, d)
    for o in range(2):
        gate = u_c[..., (o + 1) * d:(o + 2) * d].reshape(bc // 2, 2, 1, lc, d)
        z_c = spectral(z_c, mf_data, spec_c[o], mi_c, dt=512, epi=(gate, z_c, row2(c_skip[0, o])), name=f"conv_ctx{o}")
    zc_out = z_c.reshape(gl, d)

    nl = 2 * gl
    n1 = nl // FFT_N2
    a_data, a_filt, a_inv, mf_l, mi_l = _dft_mats_two_stage(nl, FFT_N2)
    cols = FFT_N2 * d
    g_l = jnp.stack([_two_sided(filt_l, o) for o in range(2)])
    ga = mm3(a_filt, g_l.reshape(2, n1, cols), name="filt_stage_a")
    spec_l = spectral(ga.reshape(2, 2, n1, FFT_N2, d), mf_l, dt=1024, name="filt_spec_lat")
    z_l = u_l[..., :d]
    tn = 2048
    for o in range(2):
        za = mm3(a_data, z_l.reshape(bl // 2, n1, cols), name=f"conv_lat_a{o}")
        zb = spectral(za.reshape(bl // 2, 2, n1, FFT_N2, d), mf_l, spec_l[o], mi_l, dt=1024, name=f"conv_lat_c{o}")
        gate = u_l[..., (o + 1) * d:(o + 2) * d].reshape(bl // 2, n1, cols)
        skip_t = jnp.tile(row2(c_skip[0, o]), (1, tn // d))
        z_l = mm3(a_inv, zb.reshape(bl // 2, 2 * n1, cols), tn=tn,
                  epi=(gate, z_l.reshape(bl // 2, n1, cols), skip_t), name=f"conv_lat_i{o}").reshape(bl, gl, d)
    zz = jnp.concatenate([zc_out, z_l.reshape(bl * gl, d)], axis=0)
    x = finish_layer(i, zz, c_wo[0], x)

    i = 3

    def body_d(x_ref, mod_ref, wq_ref, wk_ref, wv_ref, q_ref, k_ref, v_ref):
        h = _modulate(x_ref, mod_ref, 0, 1).astype(BF16)
        q_ref[...] = _dot(h, wq_ref[...]).astype(BF16)
        k_ref[...] = _dot(h, wk_ref[...])
        v_ref[...] = _dot(h, wv_ref[...])

    q, k, v = rc(body_d, t_all,
                 [(x, "row"), (mods[i], "group"), (d_wq[0].astype(BF16), "full"), (d_wk[0].astype(BF16), "full"),
                  (d_wv[0].astype(BF16), "full")],
                 [(d, BF16), (d, F32), (d, F32)], name="proj_d")
    out_d_k = k[:gl].reshape(bc, 1, lc, D_HEADS, hd)
    out_d_v = v[:gl].reshape(bc, 1, lc, D_HEADS, hd)
    kb, vb = k.astype(BF16), v.astype(BF16)
    scale = hd ** -0.5
    common = dict(n_heads=D_HEADS, dq=hd, dv=hd, hpb=2, kvpb=2, tq=tq, scale=scale)
    att_c = attention(q, _heads_major(kb[:gl], bc, lc, D_HEADS, hd), _heads_major(vb[:gl], bc, lc, D_HEADS, hd),
                      n_seq=bc, lq=lc, row_off=0, win=lc, name="attn_d_ctx", **common)
    rows = gl // GRID_W
    kh = min(MAX_NBR_ROWS, rows)
    qrows = tq // GRID_W
    krows = qrows + kh
    win_d = krows * GRID_W
    nq_l = gl // tq
    ql = jnp.arange(tq)
    kl_ = jnp.arange(win_d)
    qr_l, qc = (ql // GRID_W)[:, None], (ql % GRID_W)[:, None]
    kr_l, kc_ = (kl_ // GRID_W)[None, :], (kl_ % GRID_W)[None, :]
    c0 = jnp.clip(qc - NBR_COLS // 2, 0, GRID_W - NBR_COLS)
    col_ok = (kc_ >= c0) & (kc_ < c0 + NBR_COLS)
    dc = jnp.clip(kc_ - qc, 1 - NBR_COLS, NBR_COLS - 1) + NBR_COLS - 1
    tabs = []
    for off, lo in ((0, jnp.zeros_like(qr_l)), (kh // 2, qr_l), (kh, jnp.full_like(qr_l, qrows))):
        ok = col_ok & (kr_l >= lo) & (kr_l < lo + kh)
        dr = jnp.clip(kr_l - off - qr_l + MAX_NBR_ROWS - 1, 0, 2 * MAX_NBR_ROWS - 2)
        tabs.append(jnp.where(ok[None], d_rel_bias[0][:, dr, dc], NEG_INF))
    nbr_bias = jnp.stack(tabs, axis=1).astype(F32)
    att_l = attention(q, _heads_major(kb[gl:], bl, gl, D_HEADS, hd), _heads_major(vb[gl:], bl, gl, D_HEADS, hd),
                      n_seq=bl, lq=gl, row_off=gl, win=win_d,
                      start_fn=lambda ii: jnp.clip(ii * qrows - kh // 2, 0, rows - krows) * GRID_W,
                      kc=cache_d_k[:, 0].transpose(0, 2, 1, 3).astype(BF16),
                      vc=cache_d_v[:, 0].transpose(0, 2, 1, 3).astype(BF16),
                      bias=nbr_bias, type_fn=lambda ii: jnp.where(ii == 0, 0, jnp.where(ii == nq_l - 1, 2, 1)),
                      name="attn_d_lat", **common)
    x = finish_layer(i, jnp.concatenate([att_c, att_l], axis=0), d_wo[0], x)

    y_prompt = x[:gl].reshape(bc, lc, d)
    y_sample = x[gl:].reshape(bl, gl, d)
    return (y_prompt, y_sample, out_a_k, out_a_v, out_b_ckv, out_b_krope, out_d_k, out_d_v)
```

```python
import functools

import jax
import jax.numpy as jnp
import numpy as np
from jax import lax
from jax.experimental import pallas as pl
from jax.experimental.pallas import tpu as pltpu

F32 = jnp.float32
BF16 = jnp.bfloat16

GRID_W = 64
HEAD_DIM = 64
ROPE_BASE = 10000.0
LN_EPS = 1e-5
RMS_EPS = 1e-6
NEG_INF = -1e30
DEPTH = 4
DEEPNORM_ALPHA = (2 * DEPTH) ** 0.25
A_HEADS = 16
A_KV_HEADS = 4
A_WINDOW = 128
B_HEADS = 16
B_Q_RANK = 384
B_KV_RANK = 256
B_NOPE = 64
B_ROPE = 32
B_VDIM = 64
B_QPAD = 128
C_POS_BANDS = 16
D_HEADS = 16
MAX_NBR_ROWS = 8
NBR_COLS = 16
N_GROUPS = 4
EXPERTS_PER_GROUP = 8
N_EXPERTS = N_GROUPS * EXPERTS_PER_GROUP
D_EXPERT = 512
MOE_BLK = 256
LANES = 128
FFT_N2 = 128
VMEM_LIMIT = 56 * 1024 * 1024
ROW_TILE = 512


def _params(n_axes):
    return pltpu.CompilerParams(dimension_semantics=("arbitrary",) * n_axes, vmem_limit_bytes=VMEM_LIMIT)


def _dot(a, b):
    return jnp.dot(a, b, preferred_element_type=F32)


def _split(x):
    hi = x.astype(BF16)
    lo = (x - hi.astype(F32)).astype(BF16)
    return hi, lo


def _dot3(a, b):
    ah, al = _split(a)
    bh, bl = _split(b)
    return _dot(ah, bh) + _dot(ah, bl) + _dot(al, bh)


def _layer_norm(y, g, b):
    mu = jnp.mean(y, axis=-1, keepdims=True)
    d = y - mu
    var = jnp.mean(d * d, axis=-1, keepdims=True)
    return d * lax.rsqrt(var + LN_EPS) * g + b


def _rms(y, g):
    return y * lax.rsqrt(jnp.mean(y * y, axis=-1, keepdims=True) + RMS_EPS) * g


def _tile_lanes(t, n):
    reps = n // t.shape[-1]
    return t if reps == 1 else jnp.concatenate([t] * reps, axis=-1)


def row_call(body, m_rows, ins, outs, *, tm=ROW_TILE, group_len=None, name=None, scratch=(), params=None):
    nb = None if group_len is None else group_len // tm
    n_tiles = m_rows // tm
    in_specs = []
    for a, kind in ins:
        if kind == "row":
            in_specs.append(pl.BlockSpec((tm, a.shape[1]), lambda i: (i, 0)))
        elif kind == "any":
            in_specs.append(pl.BlockSpec(memory_space=pl.ANY))
        elif kind == "smem":
            in_specs.append(pl.BlockSpec(memory_space=pltpu.SMEM))
        elif kind == "smem_rows":
            in_specs.append(pl.BlockSpec((a.shape[0] // n_tiles,), lambda i: (i,), memory_space=pltpu.SMEM))
        elif kind == "full":
            in_specs.append(pl.BlockSpec(a.shape, lambda i, nd=a.ndim: (0,) * nd))
        elif kind == "group":
            in_specs.append(pl.BlockSpec((1,) + a.shape[1:], lambda i: (i // nb, 0, 0)))
        elif kind == "pos":
            in_specs.append(pl.BlockSpec((1, tm, a.shape[2]), lambda i: (jnp.minimum(i // nb, 1), i % nb, 0)))
        else:
            raise ValueError(kind)
    out_specs, out_shape = [], []
    for o in outs:
        if len(o) == 4:
            out_specs.append(pl.BlockSpec(memory_space=pl.ANY))
            out_shape.append(jax.ShapeDtypeStruct((o[0], o[1]), o[2]))
        else:
            out_specs.append(pl.BlockSpec((tm, o[0]), lambda i: (i, 0)))
            out_shape.append(jax.ShapeDtypeStruct((m_rows, o[0]), o[1]))
    return pl.pallas_call(
        body, grid=(n_tiles,), in_specs=in_specs, out_specs=out_specs, out_shape=out_shape,
        scratch_shapes=list(scratch), compiler_params=params or _params(1), name=name)(*[a for a, _ in ins])


def _modulate(x_ref, mod_ref, shift_row, scale_row):
    m = mod_ref[0]
    return x_ref[...] * (1.0 + m[scale_row:scale_row + 1]) + m[shift_row:shift_row + 1]


def modulation_all(cvec, mod_w, mod_b):
    depth, d, d6 = mod_w.shape
    g = cvec.shape[0]
    gp = -(-g // 16) * 16
    cp = jnp.zeros((gp, d), F32).at[:g].set(cvec)
    tn = 1024

    def body(c_ref, w_ref, b_ref, o_ref):
        c = c_ref[...]
        a = (c * jax.nn.sigmoid(c)).astype(BF16)
        o_ref[0] = _dot(a, w_ref[0].astype(BF16)) + b_ref[0]

    out = pl.pallas_call(
        body, grid=(depth, d6 // tn),
        in_specs=[pl.BlockSpec((gp, d), lambda l, j: (0, 0)),
                  pl.BlockSpec((1, d, tn), lambda l, j: (l, 0, j)),
                  pl.BlockSpec((1, 1, tn), lambda l, j: (l, 0, j))],
        out_specs=pl.BlockSpec((1, gp, tn), lambda l, j: (l, 0, j)),
        out_shape=jax.ShapeDtypeStruct((depth, gp, d6), F32),
        compiler_params=_params(2), name="modulation")(cp, mod_w, mod_b.reshape(depth, 1, d6))
    return out[:, :g].reshape(depth, g, 6, d)


def attention(q, kl, vl, *, n_seq, lq, row_off, n_heads, dq, dv, hpb, kvpb, tq, win, scale,
              start_fn=None, kc=None, vc=None, bias=None, type_fn=None, sink=None, name=None):
    nq = lq // tq
    lk = kl.shape[2]
    rep = hpb // kvpb
    has_ctx, has_bias, has_sink = kc is not None, bias is not None, sink is not None
    bias_heads = has_bias and bias.shape[0] > 1
    off_blk = row_off // tq

    def kern(*refs):
        it = iter(refs)
        q_ref, kl_ref, vl_ref = next(it), next(it), next(it)
        kc_ref = next(it) if has_ctx else None
        vc_ref = next(it) if has_ctx else None
        b_ref = next(it) if has_bias else None
        s_ref = next(it) if has_sink else None
        o_ref = next(it)
        i = pl.program_id(2)
        hb = pl.program_id(1)
        if start_fn is None:
            start = 0
        else:
            start = pl.multiple_of(start_fn(i), 64)
        outs = []
        for j in range(hpb):
            kv = j // rep
            qj = q_ref[:, j * dq:(j + 1) * dq]
            k = kl_ref[0, kv, pl.ds(start, win), :]
            v = vl_ref[0, kv, pl.ds(start, win), :]
            s = lax.dot_general(qj, k, (((1,), (1,)), ((), ())), preferred_element_type=F32) * scale
            if has_bias:
                s = s + b_ref[j if bias_heads else 0, 0]
            m = jnp.max(s, axis=1, keepdims=True)
            if has_ctx:
                sc = lax.dot_general(qj, kc_ref[0, kv], (((1,), (1,)), ((), ())),
                                     preferred_element_type=F32) * scale
                m = jnp.maximum(m, jnp.max(sc, axis=1, keepdims=True))
            if has_sink:
                sk = s_ref[hb * hpb + j]
                m = jnp.maximum(m, sk)
            p = jnp.exp(s - m)
            l = jnp.sum(p, axis=1, keepdims=True)
            acc = _dot(p.astype(BF16), v)
            if has_ctx:
                pc = jnp.exp(sc - m)
                l = l + jnp.sum(pc, axis=1, keepdims=True)
                acc = acc + _dot(pc.astype(BF16), vc_ref[0, kv])
            if has_sink:
                l = l + jnp.exp(sk - m)
            outs.append(acc / l)
        o_ref[...] = jnp.concatenate(outs, axis=1).astype(o_ref.dtype)

    ins = [q, kl, vl]
    in_specs = [pl.BlockSpec((tq, hpb * dq), lambda b, h, i: (off_blk + b * nq + i, h)),
                pl.BlockSpec((1, kvpb, lk, dq), lambda b, h, i: (b, h, 0, 0)),
                pl.BlockSpec((1, kvpb, lk, dv), lambda b, h, i: (b, h, 0, 0))]
    if has_ctx:
        lc = kc.shape[2]
        ins += [kc, vc]
        in_specs += [pl.BlockSpec((1, kvpb, lc, dq), lambda b, h, i: (b, h, 0, 0)),
                     pl.BlockSpec((1, kvpb, lc, dv), lambda b, h, i: (b, h, 0, 0))]
    if has_bias:
        ins.append(bias)
        hb_blk = hpb if bias_heads else 1
        in_specs.append(pl.BlockSpec((hb_blk, 1, tq, win),
                                     lambda b, h, i: (h if bias_heads else 0, type_fn(i), 0, 0)))
    if has_sink:
        ins.append(sink.astype(F32))
        in_specs.append(pl.BlockSpec(memory_space=pltpu.SMEM))
    return pl.pallas_call(
        kern, grid=(n_seq, n_heads // hpb, nq), in_specs=in_specs,
        out_specs=pl.BlockSpec((tq, hpb * dv), lambda b, h, i: (b * nq + i, h)),
        out_shape=jax.ShapeDtypeStruct((n_seq * lq, n_heads * dv), BF16),
        compiler_params=_params(3), name=name)(*ins)


def _heads_major(x, n_seq, length, n_heads, d):
    return x.reshape(n_seq, length, n_heads, d).transpose(0, 2, 1, 3)


def _rope_tables(length, dim, lead, tail, reps):
    half = dim // 2
    nf = half // 2
    t = jnp.arange(length)
    row = (t // GRID_W).astype(F32)
    col = (t % GRID_W).astype(F32)
    inv = ROPE_BASE ** (-jnp.arange(nf, dtype=F32) / nf)
    ang = jnp.concatenate([row[:, None] * inv, col[:, None] * inv], axis=-1)
    cos, sin = jnp.cos(ang), jnp.sin(ang)
    c = jnp.concatenate([jnp.ones((length, lead), F32), cos, cos, jnp.ones((length, tail), F32)] * reps, axis=1)
    s = jnp.concatenate([jnp.zeros((length, lead), F32), sin, sin, jnp.zeros((length, tail), F32)] * reps, axis=1)
    return (jnp.stack([jnp.ones_like(c), c]), jnp.stack([jnp.zeros_like(s), s]))


def _rot_cols(w, dim):
    k, n = w.shape
    wb = w.reshape(k, n // dim, dim)
    half = dim // 2
    return jnp.concatenate([-wb[..., half:], wb[..., :half]], axis=-1).reshape(k, n)


def _cs(phase, n, sign):
    ang = (2.0 * np.pi / n) * (phase % n).astype(F32)
    return jnp.cos(ang), sign * jnp.sin(ang)


def _cblock(wr, wi):
    return jnp.concatenate([jnp.concatenate([wr, -wi], axis=-1), jnp.concatenate([wi, wr], axis=-1)], axis=-2)


def mm3(a, x, *, tn=2048, epi=None, name=None):
    bt, k, n = x.shape
    m = a.shape[0]

    def kern(*refs):
        if epi is None:
            a_ref, x_ref, o_ref = refs
        else:
            a_ref, x_ref, g_ref, z_ref, s_ref, o_ref = refs
        y = _dot3(a_ref[...], x_ref[0])
        if epi is not None:
            y = g_ref[0] * (y + s_ref[...] * z_ref[0])
        o_ref[0] = y

    ins = [a, x]
    in_specs = [pl.BlockSpec((m, k), lambda b, j: (0, 0)), pl.BlockSpec((1, k, tn), lambda b, j: (b, 0, j))]
    if epi is not None:
        ins += list(epi)
        in_specs += [pl.BlockSpec((1, m, tn), lambda b, j: (b, 0, j)),
                     pl.BlockSpec((1, m, tn), lambda b, j: (b, 0, j)),
                     pl.BlockSpec((1, tn), lambda b, j: (0, 0))]
    return pl.pallas_call(
        kern, grid=(bt, n // tn), in_specs=in_specs,
        out_specs=pl.BlockSpec((1, m, tn), lambda b, j: (b, 0, j)),
        out_shape=jax.ShapeDtypeStruct((bt, m, n), F32), compiler_params=_params(2), name=name)(*ins)


def spectral(x, mf, g=None, mi=None, *, dt, epi=None, name=None):
    p_n, planes, k1_n, nin, d = x.shape
    nf = mf.shape[1] // 2
    nout = nf if g is None else mi.shape[1] // 2

    def kern(*refs):
        it = iter(refs)
        x_ref, mf_ref = next(it), next(it)
        g_ref = next(it) if g is not None else None
        mi_ref = next(it) if g is not None else None
        if epi is not None:
            gate_ref, z_ref, skip_ref = next(it), next(it), next(it)
        o_ref = next(it)
        xs = [x_ref[0, pp, 0] for pp in range(planes)]
        xin = xs[0] if planes == 1 else jnp.concatenate(xs, axis=0)
        f = _dot3(mf_ref[0], xin)
        if g is not None:
            fr, fi = f[:nf], f[nf:]
            gr, gi = g_ref[0, 0], g_ref[1, 0]
            y = jnp.concatenate([fr * gr - fi * gi, fr * gi + fi * gr], axis=0)
            f = _dot3(mi_ref[0], y)
        for pp in range(2):
            y = f[pp * nout:(pp + 1) * nout]
            if epi is not None:
                y = gate_ref[0, pp, 0] * (y + skip_ref[...] * z_ref[0, pp, 0])
            o_ref[0, pp, 0] = y

    ins = [x, mf]
    in_specs = [pl.BlockSpec((1, planes, 1, nin, dt), lambda k, j, p: (p, 0, k, 0, j)),
                pl.BlockSpec((1,) + mf.shape[1:], lambda k, j, p: (k, 0, 0))]
    if g is not None:
        ins += [g, mi]
        in_specs += [pl.BlockSpec((2, 1, nf, dt), lambda k, j, p: (0, k, 0, j)),
                     pl.BlockSpec((1,) + mi.shape[1:], lambda k, j, p: (k, 0, 0))]
    if epi is not None:
        ins += list(epi)
        in_specs += [pl.BlockSpec((1, 2, 1, nout, dt), lambda k, j, p: (p, 0, k, 0, j)),
                     pl.BlockSpec((1, 2, 1, nout, dt), lambda k, j, p: (p, 0, k, 0, j)),
                     pl.BlockSpec((1, dt), lambda k, j, p: (0, j))]
    return pl.pallas_call(
        kern, grid=(k1_n, d // dt, p_n), in_specs=in_specs,
        out_specs=pl.BlockSpec((1, 2, 1, nout, dt), lambda k, j, p: (p, 0, k, 0, j)),
        out_shape=jax.ShapeDtypeStruct((p_n, 2, k1_n, nout, d), F32),
        compiler_params=_params(3), name=name)(*ins)


def _two_sided(filt, o):
    hf, hb = filt[:, o, 0], filt[:, o, 1]
    return jnp.concatenate([hf[:1] + hb[:1], hf[1:], jnp.zeros_like(hf[:1]), hb[1:][::-1]], axis=0)


def hyena_filters(length, w1, b1, w2, b2, w3, freq, log_decay, name):
    hid = w2.shape[0]
    t = jnp.linspace(0.0, 1.0, length, dtype=F32)[:, None]
    ang = 2.0 * jnp.pi * t * jnp.arange(1, C_POS_BANDS + 1, dtype=F32)
    z = jnp.concatenate([t, jnp.cos(ang), jnp.sin(ang)], axis=-1)
    kpad = LANES - z.shape[1]
    z = jnp.pad(z, ((0, 0), (0, kpad)))
    w1p = jnp.pad(w1, ((0, kpad), (0, 0)))
    ncol = w3.shape[1]
    tn = 256

    def kern(z_ref, w1_ref, b1_ref, w2_ref, b2_ref, f_ref, w3_ref, ld_ref, o_ref):
        fr = f_ref[...]
        a = jnp.sin(fr * (_dot3(z_ref[...], w1_ref[...]) + b1_ref[...]))
        a = jnp.sin(fr * (_dot3(a, w2_ref[...]) + b2_ref[...]))
        filt = _dot3(a, w3_ref[...])
        tt = lax.broadcasted_iota(jnp.int32, filt.shape, 0).astype(F32) * (1.0 / (length - 1))
        filt = filt * jnp.exp(-jnp.exp(ld_ref[...]) * tt)
        o_ref[...] = filt / (jnp.sum(jnp.abs(filt), axis=0, keepdims=True) + 1e-6)

    full = lambda a: pl.BlockSpec(a.shape, lambda j: (0, 0))
    args = [z, w1p, b1.reshape(1, hid), w2, b2.reshape(1, hid), freq.reshape(1, hid)]
    return pl.pallas_call(
        kern, grid=(ncol // tn,),
        in_specs=[full(a) for a in args] + [pl.BlockSpec((hid, tn), lambda j: (0, j)),
                                            pl.BlockSpec((1, tn), lambda j: (0, j))],
        out_specs=pl.BlockSpec((length, tn), lambda j: (0, j)),
        out_shape=jax.ShapeDtypeStruct((length, ncol), F32),
        compiler_params=_params(1), name=name)(*args, w3, log_decay.reshape(1, ncol))


def short_conv(u, w, b, name):
    n_seq, length, c = u.shape
    dt = 256

    def kern(u_ref, w_ref, b_ref, o_ref):
        x = u_ref[0]
        r = lax.broadcasted_iota(jnp.int32, x.shape, 0)
        prev = jnp.where(r == 0, 0.0, pltpu.roll(x, 1, 0))
        nxt = jnp.where(r == length - 1, 0.0, pltpu.roll(x, length - 1, 0))
        wv = w_ref[...]
        o_ref[0] = prev * wv[0:1] + x * wv[1:2] + nxt * wv[2:3] + b_ref[...]

    return pl.pallas_call(
        kern, grid=(n_seq, c // dt),
        in_specs=[pl.BlockSpec((1, length, dt), lambda s, j: (s, 0, j)),
                  pl.BlockSpec((3, dt), lambda s, j: (0, j)),
                  pl.BlockSpec((1, dt), lambda s, j: (0, j))],
        out_specs=pl.BlockSpec((1, length, dt), lambda s, j: (s, 0, j)),
        out_shape=jax.ShapeDtypeStruct(u.shape, F32), compiler_params=_params(2), name=name)(u, w, b)


def _dft_mats_single(n, nin_data, nout):
    k = jnp.arange(n, dtype=jnp.int32)
    fr, fi = _cs(k[:, None] * k[None, :nin_data], n, -1.0)
    mf_data = _cblock(fr, fi)[None]
    gr, gi = _cs(k[:, None] * k[None, :], n, -1.0)
    mf_filt = jnp.concatenate([gr, gi], axis=0)[None]
    ir, ii = _cs(k[:nout, None] * k[None, :], n, 1.0)
    mi = _cblock(ir / n, ii / n)[None]
    return mf_data, mf_filt, mi


def _dft_mats_two_stage(n, n2):
    n1 = n // n2
    k1 = jnp.arange(n1, dtype=jnp.int32)
    t1h = jnp.arange(n1 // 2, dtype=jnp.int32)
    ar, ai = _cs((n // n1) * k1[:, None] * t1h[None, :], n, -1.0)
    a_data = _cblock(ar, ai)
    fr, fi = _cs((n // n1) * k1[:, None] * k1[None, :], n, -1.0)
    a_filt = jnp.concatenate([fr, fi], axis=0)
    br, bi = _cs((n // n1) * t1h[:, None] * k1[None, :], n, 1.0)
    a_inv = _cblock(br, bi)
    t2 = jnp.arange(n2, dtype=jnp.int32)
    kk = k1[:, None, None] + n1 * t2[None, :, None]
    mr, mi_ = _cs(kk * t2[None, None, :], n, -1.0)
    mf = _cblock(mr, mi_)
    vr, vi = _cs(jnp.swapaxes(kk, 1, 2) * t2[None, :, None], n, 1.0)
    mi = _cblock(vr / n, vi / n)
    return a_data, a_filt, a_inv, mf, mi


def _route(logits):
    lane = lax.broadcasted_iota(jnp.int32, logits.shape, 1).astype(F32)
    big = 1e9
    lg = jnp.where(lane < N_GROUPS, logits, -jnp.inf)
    mg = jnp.max(lg, axis=1, keepdims=True)
    gi = jnp.min(jnp.where(lg == mg, lane, big), axis=1, keepdims=True)
    p_g = 1.0 / jnp.sum(jnp.exp(lg - mg), axis=1, keepdims=True)
    lo = N_GROUPS + EXPERTS_PER_GROUP * gi
    le = jnp.where((lane >= lo) & (lane < lo + EXPERTS_PER_GROUP), logits, -jnp.inf)
    m1 = jnp.max(le, axis=1, keepdims=True)
    i1 = jnp.min(jnp.where(le == m1, lane, big), axis=1, keepdims=True)
    le2 = jnp.where(lane == i1, -jnp.inf, le)
    m2 = jnp.max(le2, axis=1, keepdims=True)
    i2 = jnp.min(jnp.where(le2 == m2, lane, big), axis=1, keepdims=True)
    e2 = jnp.exp(m2 - m1)
    w1 = p_g / (1.0 + e2)
    w2 = p_g * e2 / (1.0 + e2)
    return lane, i1 - N_GROUPS, i2 - N_GROUPS, w1, w2


def _rank_in_tile(lane, e1, e2):
    tm = lane.shape[0]
    picks = jnp.where(lane == e1, 1.0, 0.0) + jnp.where(lane == e2, 1.0, 0.0)
    r = lax.broadcasted_iota(jnp.int32, (tm, tm), 0)
    c = lax.broadcasted_iota(jnp.int32, (tm, tm), 1)
    earlier = jnp.where(c < r, 1.0, 0.0).astype(BF16)
    return _dot(earlier, picks.astype(BF16)), picks


def expert_mlp(xs, blk_e, n_used, w_gate, w_up, w_down):
    p_rows, d = xs.shape
    de = w_gate.shape[2]
    nblk = p_rows // MOE_BLK

    def kern(be_ref, nu_ref, x_ref, wg_ref, wu_ref, wd_ref, o_ref, wg_s, wu_s, wd_s):
        i = pl.program_id(0)
        used = i < nu_ref[0]
        fresh = jnp.logical_or(i == 0, be_ref[i] != be_ref[jnp.maximum(i - 1, 0)])

        @pl.when(jnp.logical_and(used, fresh))
        def _():
            wg_s[...] = wg_ref[0].astype(BF16)
            wu_s[...] = wu_ref[0].astype(BF16)
            wd_s[...] = wd_ref[0].astype(BF16)

        @pl.when(used)
        def _():
            x = x_ref[...].astype(BF16)
            g = _dot(x, wg_s[...])
            u = _dot(x, wu_s[...])
            a = (g * jax.nn.sigmoid(g) * u).astype(BF16)
            o_ref[...] = _dot(a, wd_s[...])

        @pl.when(jnp.logical_not(used))
        def _():
            o_ref[...] = jnp.zeros_like(o_ref)

    last = lambda i, nu: jnp.minimum(i, nu[0] - 1)
    grid_spec = pltpu.PrefetchScalarGridSpec(
        num_scalar_prefetch=2, grid=(nblk,),
        in_specs=[pl.BlockSpec((MOE_BLK, d), lambda i, be, nu: (last(i, nu), 0)),
                  pl.BlockSpec((1, d, de), lambda i, be, nu: (be[last(i, nu)], 0, 0)),
                  pl.BlockSpec((1, d, de), lambda i, be, nu: (be[last(i, nu)], 0, 0)),
                  pl.BlockSpec((1, de, d), lambda i, be, nu: (be[last(i, nu)], 0, 0))],
        out_specs=pl.BlockSpec((MOE_BLK, d), lambda i, be, nu: (i, 0)),
        scratch_shapes=[pltpu.VMEM((d, de), BF16), pltpu.VMEM((d, de), BF16), pltpu.VMEM((de, d), BF16)])
    return pl.pallas_call(
        kern, grid_spec=grid_spec, out_shape=jax.ShapeDtypeStruct((p_rows, d), F32),
        compiler_params=_params(1), name="expert_mlp")(blk_e, n_used, xs, w_gate, w_up, w_down)


def _slot_plan(experts, ranks):
    n = 2 * experts.shape[0]
    ids = jnp.arange(N_EXPERTS, dtype=jnp.int32)
    onehot = experts[..., None] == ids
    counts = jnp.sum(onehot, axis=(0, 1)).astype(jnp.int32)
    padded = (counts + MOE_BLK - 1) // MOE_BLK * MOE_BLK
    pend = jnp.cumsum(padded)
    pstart = pend - padded
    p_rows = -(-n // MOE_BLK) * MOE_BLK + N_EXPERTS * MOE_BLK
    nblk = p_rows // MOE_BLK
    blk_first = jnp.arange(nblk, dtype=jnp.int32) * MOE_BLK
    blk_e = jnp.minimum(jnp.sum(pend[None, :] <= blk_first[:, None], axis=1), N_EXPERTS - 1).astype(jnp.int32)
    n_used = (pend[-1:] // MOE_BLK).astype(jnp.int32)
    blk_ids = jnp.arange(nblk, dtype=jnp.int32)
    partly = jnp.any((blk_ids[:, None] == (pend // MOE_BLK - 1)[None, :]) & (counts % MOE_BLK != 0)[None, :], axis=1)
    zero_blk = (partly | (blk_ids >= n_used[0])).astype(jnp.int32)
    slot = ranks + jnp.sum(jnp.where(onehot, pstart, 0), axis=-1)
    return slot.reshape(n).astype(jnp.int32), blk_e, n_used, zero_blk, p_rows


def _dma_params():
    return pltpu.CompilerParams(dimension_semantics=("arbitrary",), vmem_limit_bytes=VMEM_LIMIT,
                                disable_bounds_checks=True)


def moe_dispatch(h, slot, zero_blk, p_rows, name):
    t, d = h.shape
    tm = ROW_TILE

    def body(slot_ref, zb_ref, h_ref, xs_ref, zero_ref, sem):
        @pl.when(pl.program_id(0) == 0)
        def _():
            zero_ref[...] = jnp.zeros_like(zero_ref)

            def zstart(b, c):
                @pl.when(zb_ref[b] != 0)
                def _():
                    first = pl.multiple_of(b * MOE_BLK, MOE_BLK)
                    pltpu.make_async_copy(zero_ref, xs_ref.at[pl.ds(first, MOE_BLK)], sem).start()
                return c

            def zwait(b, c):
                @pl.when(zb_ref[b] != 0)
                def _():
                    pltpu.make_async_copy(zero_ref, xs_ref.at[pl.ds(0, MOE_BLK)], sem).wait()
                return c

            lax.fori_loop(0, p_rows // MOE_BLK, zstart, 0)
            lax.fori_loop(0, p_rows // MOE_BLK, zwait, 0)

        def start(r, c):
            for k in range(2):
                pltpu.make_async_copy(h_ref.at[pl.ds(r, 1)], xs_ref.at[pl.ds(slot_ref[2 * r + k], 1)], sem).start()
            return c

        lax.fori_loop(0, tm, start, 0)
        for k in range(2):
            pltpu.make_async_copy(h_ref, xs_ref.at[pl.ds(0, tm)], sem).wait()

    return row_call(body, t, [(slot, "smem_rows"), (zero_blk, "smem"), (h, "row")], [(p_rows, d, F32, "any")],
                    scratch=[pltpu.VMEM((MOE_BLK, d), F32), pltpu.SemaphoreType.DMA(())],
                    params=_dma_params(), name=name)[0]


def kernel(x_prompt, x_sample, cache_a_k, cache_a_v, cache_b_ckv, cache_b_krope, cache_d_k, cache_d_v, c_ctx, c, mod_w, mod_b, ln_g, ln_b, a_wq, a_wk, a_wv, a_wo, a_sink, b_wq_a, b_q_norm, b_wq_b, b_wkv_a, b_kv_norm, b_wk_b, b_wv_b, b_wo, c_w_in, c_b_in, c_conv_w, c_conv_b, c_ffn_w1, c_ffn_b1, c_ffn_w2, c_ffn_b2, c_ffn_w3, c_ffn_freq, c_log_decay, c_skip, c_wo, d_wq, d_wk, d_wv, d_wo, d_rel_bias, moe_wr_g, moe_br_g, moe_wr_e, moe_br_e, moe_w_gate, moe_w_up, moe_w_down):
    bc, lc, d = x_prompt.shape
    bl, ll, _ = x_sample.shape
    past = cache_a_k.shape[2]
    gl = ll
    assert bc * lc == gl and d == A_HEADS * HEAD_DIM
    ng = 1 + bl
    t_all = ng * gl
    x = jnp.concatenate([x_prompt.reshape(gl, d), x_sample.reshape(bl * gl, d)], axis=0)
    cvec = jnp.concatenate([c_ctx[None, :], c], axis=0)
    mods = modulation_all(cvec, mod_w, mod_b)
    rc = functools.partial(row_call, group_len=gl)
    row2 = lambda v: v.reshape(1, -1)

    def post_mixer(i, att, wo, x_in):
        unused = LANES - N_GROUPS - N_EXPERTS
        wr = jnp.concatenate([moe_wr_g[i], moe_wr_e[i], jnp.zeros((d, unused), F32)], axis=1)
        br = jnp.concatenate([moe_br_g[i], moe_br_e[i], jnp.zeros((unused,), F32)])[None, :]

        def body(att_ref, x_ref, mod_ref, wo_ref, g_ref, b_ref, wr_ref, br_ref, x1_ref, h_ref, info_ref, w_ref, seen_ref):
            @pl.when(pl.program_id(0) == 0)
            def _():
                seen_ref[...] = jnp.zeros_like(seen_ref)

            m = mod_ref[0]
            o = _dot(att_ref[...].astype(BF16), wo_ref[...])
            x1 = _layer_norm(DEEPNORM_ALPHA * x_ref[...] + m[2:3] * o, g_ref[...], b_ref[...])
            x1_ref[...] = x1
            h = x1 * (1.0 + m[4:5]) + m[3:4]
            h_ref[...] = h
            lane, e1, e2, w1, w2 = _route(_dot3(h, wr_ref[...]) + br_ref[...])
            before, picks = _rank_in_tile(lane, e1, e2)
            before = before + seen_ref[...]
            r1 = jnp.sum(jnp.where(lane == e1, before, 0.0), axis=1, keepdims=True)
            r2 = jnp.sum(jnp.where(lane == e2, before, 0.0), axis=1, keepdims=True)
            seen_ref[...] += jnp.sum(picks, axis=0, keepdims=True)
            info = jnp.where(lane == 0, e1, jnp.where(lane == 1, e2, jnp.where(lane == 2, r1, jnp.where(lane == 3, r2, 0.0))))
            info_ref[...] = info.astype(jnp.int32)
            w_ref[...] = jnp.where(lane == 0, w1, jnp.where(lane == 1, w2, 0.0))

        return rc(body, t_all,
                  [(att, "row"), (x_in, "row"), (mods[i], "group"), (wo.astype(BF16), "full"),
                   (row2(ln_g[i, 0]), "full"), (row2(ln_b[i, 0]), "full"), (wr, "full"), (br, "full")],
                  [(d, F32), (d, F32), (LANES, jnp.int32), (LANES, F32)],
                  scratch=[pltpu.VMEM((1, LANES), F32)], name=f"post_mixer{i}")

    def moe(i, x1, h, info, w):
        slot, blk_e, n_used, zero_blk, p_rows = _slot_plan(info[:, 0:2], info[:, 2:4])
        xs = moe_dispatch(h, slot, zero_blk, p_rows, f"moe_dispatch{i}")
        ys = expert_mlp(xs, blk_e, n_used, moe_w_gate[i], moe_w_up[i], moe_w_down[i])
        tm = ROW_TILE

        def body(slot_ref, x_ref, w_ref, mod_ref, g_ref, b_ref, ys_ref, out_ref, buf, sem):
            def start(r, c):
                for k in range(2):
                    pltpu.make_async_copy(ys_ref.at[pl.ds(slot_ref[2 * r + k], 1)], buf.at[k, pl.ds(r, 1)], sem).start()
                return c

            lax.fori_loop(0, tm, start, 0)
            for k in range(2):
                pltpu.make_async_copy(ys_ref.at[pl.ds(0, tm)], buf.at[k], sem).wait()
            m = mod_ref[0]
            wv = w_ref[...]
            y = wv[:, 0:1] * buf[0] + wv[:, 1:2] * buf[1]
            out_ref[...] = _layer_norm(DEEPNORM_ALPHA * x_ref[...] + m[5:6] * y, g_ref[...], b_ref[...])

        return rc(body, t_all,
                  [(slot, "smem_rows"), (x1, "row"), (w, "row"), (mods[i], "group"),
                   (row2(ln_g[i, 1]), "full"), (row2(ln_b[i, 1]), "full"), (ys, "any")],
                  [(d, F32)], scratch=[pltpu.VMEM((2, tm, d), F32), pltpu.SemaphoreType.DMA(())],
                  params=_dma_params(), name=f"moe_combine{i}")[0]

    def finish_layer(i, att, wo, x_in):
        x1, h, info, w = post_mixer(i, att, wo, x_in)
        return moe(i, x1, h, info, w)

    tq = 256
    i = 0
    hd = HEAD_DIM
    kvw = A_KV_HEADS * hd
    cq, sq = _rope_tables(gl, hd, 0, 0, LANES // hd)
    wq, wk, wv = a_wq[0], a_wk[0], a_wv[0]

    def body_a(x_ref, mod_ref, wq_ref, wqr_ref, wk_ref, wkr_ref, wv_ref, c_ref, s_ref, q_ref, k_ref, v_ref):
        h = _modulate(x_ref, mod_ref, 0, 1).astype(BF16)
        cc, ss = c_ref[0], s_ref[0]
        q = _dot(h, wq_ref[...]) * _tile_lanes(cc, d) + _dot(h, wqr_ref[...]) * _tile_lanes(ss, d)
        k = _dot(h, wk_ref[...]) * _tile_lanes(cc, kvw) + _dot(h, wkr_ref[...]) * _tile_lanes(ss, kvw)
        q_ref[...] = q.astype(BF16)
        k_ref[...] = k
        v_ref[...] = _dot(h, wv_ref[...])

    q, k, v = rc(body_a, t_all,
                 [(x, "row"), (mods[i], "group"), (wq.astype(BF16), "full"), (_rot_cols(wq, hd).astype(BF16), "full"),
                  (wk.astype(BF16), "full"), (_rot_cols(wk, hd).astype(BF16), "full"), (wv.astype(BF16), "full"),
                  (cq, "pos"), (sq, "pos")],
                 [(d, BF16), (kvw, F32), (kvw, F32)], name="proj_a")
    out_a_k = k[:gl].reshape(bc, 1, lc, A_KV_HEADS, hd)
    out_a_v = v[:gl].reshape(bc, 1, lc, A_KV_HEADS, hd)
    kb, vb = k.astype(BF16), v.astype(BF16)
    scale = hd ** -0.5
    common = dict(n_heads=A_HEADS, dq=hd, dv=hd, hpb=4, kvpb=1, tq=tq, scale=scale, sink=a_sink[0])
    att_c = attention(q, _heads_major(kb[:gl], bc, lc, A_KV_HEADS, hd), _heads_major(vb[:gl], bc, lc, A_KV_HEADS, hd),
                      n_seq=bc, lq=lc, row_off=0, win=lc, name="attn_a_ctx", **common)
    win_a = 2 * tq
    nq_l = gl // tq
    qi = jnp.arange(tq)[:, None]
    ki = jnp.arange(win_a)[None, :]
    band = jnp.stack([jnp.where(jnp.abs(ki - (qi + off)) <= A_WINDOW, 0.0, NEG_INF)
                      for off in (0, A_WINDOW, 2 * A_WINDOW)]).astype(F32)[None]
    att_l = attention(q, _heads_major(kb[gl:], bl, gl, A_KV_HEADS, hd), _heads_major(vb[gl:], bl, gl, A_KV_HEADS, hd),
                      n_seq=bl, lq=gl, row_off=gl, win=win_a,
                      start_fn=lambda ii: jnp.clip(ii * tq - A_WINDOW, 0, gl - win_a),
                      kc=cache_a_k[:, 0].transpose(0, 2, 1, 3).astype(BF16),
                      vc=cache_a_v[:, 0].transpose(0, 2, 1, 3).astype(BF16),
                      bias=band, type_fn=lambda ii: jnp.where(ii == 0, 0, jnp.where(ii == nq_l - 1, 2, 1)),
                      name="attn_a_lat", **common)
    x = finish_layer(i, jnp.concatenate([att_c, att_l], axis=0), a_wo[0], x)

    i = 1
    hq = B_NOPE + B_ROPE
    qw = B_HEADS * B_QPAD
    wqb = b_wq_b[0].reshape(B_Q_RANK, B_HEADS, hq)
    wqb_rot = jnp.concatenate([jnp.zeros_like(wqb[..., :B_NOPE]),
                               _rot_cols(wqb[..., B_NOPE:].reshape(B_Q_RANK, -1), B_ROPE).reshape(B_Q_RANK, B_HEADS, B_ROPE)],
                              axis=-1)
    padq = lambda wz: jnp.pad(wz, ((0, 0), (0, 0), (0, B_QPAD - hq))).reshape(B_Q_RANK, qw).astype(BF16)
    wkv_c, wkv_r = b_wkv_a[0][:, :B_KV_RANK], b_wkv_a[0][:, B_KV_RANK:]
    cqb, sqb = _rope_tables(gl, B_ROPE, B_NOPE, B_QPAD - hq, 1)
    ckr, skr = _rope_tables(gl, B_ROPE, 0, 0, 1)

    def body_b(x_ref, mod_ref, wqa_ref, qn_ref, wqb_ref, wqbr_ref, wc_ref, kn_ref, wr_ref, wrr_ref, wkb_ref, wvb_ref,
               cq_ref, sq_ref, ck_ref, sk_ref, q_ref, ckv_ref, kr_ref, kn_out, v_out):
        h = _modulate(x_ref, mod_ref, 0, 1).astype(BF16)
        qa = _rms(_dot(h, wqa_ref[...]), qn_ref[...]).astype(BF16)
        q = (_dot(qa, wqb_ref[...]) * _tile_lanes(cq_ref[0], qw) + _dot(qa, wqbr_ref[...]) * _tile_lanes(sq_ref[0], qw))
        q_ref[...] = q.astype(BF16)
        ckv = _rms(_dot(h, wc_ref[...]), kn_ref[...])
        ckv_ref[...] = ckv
        kr_ref[...] = _dot(h, wr_ref[...]) * ck_ref[0] + _dot(h, wrr_ref[...]) * sk_ref[0]
        cb = ckv.astype(BF16)
        kn_out[...] = _dot(cb, wkb_ref[...]).astype(BF16)
        v_out[...] = _dot(cb, wvb_ref[...]).astype(BF16)

    wkb, wvb = b_wk_b[0].astype(BF16), b_wv_b[0].astype(BF16)
    q, ckv, kr, kn, vv = rc(
        body_b, t_all,
        [(x, "row"), (mods[i], "group"), (b_wq_a[0].astype(BF16), "full"), (row2(b_q_norm[0]), "full"),
         (padq(wqb), "full"), (padq(wqb_rot), "full"), (wkv_c.astype(BF16), "full"), (row2(b_kv_norm[0]), "full"),
         (wkv_r.astype(BF16), "full"), (_rot_cols(wkv_r, B_ROPE).astype(BF16), "full"), (wkb, "full"), (wvb, "full"),
         (cqb, "pos"), (sqb, "pos"), (ckr, "pos"), (skr, "pos")],
        [(qw, BF16), (B_KV_RANK, F32), (B_ROPE, F32), (B_HEADS * B_NOPE, BF16), (B_HEADS * B_VDIM, BF16)], name="proj_b")
    out_b_ckv = ckv[:gl].reshape(bc, 1, lc, B_KV_RANK)
    out_b_krope = kr[:gl].reshape(bc, 1, lc, B_ROPE)

    def body_bc(c_ref, wkb_ref, wvb_ref, kn_out, v_out):
        cb = c_ref[...].astype(BF16)
        kn_out[...] = _dot(cb, wkb_ref[...]).astype(BF16)
        v_out[...] = _dot(cb, wvb_ref[...]).astype(BF16)

    n_pc = bl * past
    kn_p, vv_p = row_call(body_bc, n_pc, [(cache_b_ckv[:, 0].reshape(n_pc, B_KV_RANK), "row"), (wkb, "full"), (wvb, "full")],
                          [(B_HEADS * B_NOPE, BF16), (B_HEADS * B_VDIM, BF16)], tm=min(ROW_TILE, n_pc), name="proj_b_past")

    def mla_keys(knope, krope, n_seq, length):
        kn4 = knope.reshape(n_seq, length, B_HEADS, B_NOPE)
        kr4 = jnp.broadcast_to(krope.astype(BF16).reshape(n_seq, length, 1, B_ROPE), (n_seq, length, B_HEADS, B_ROPE))
        z4 = jnp.zeros((n_seq, length, B_HEADS, B_QPAD - hq), BF16)
        return jnp.concatenate([kn4, kr4, z4], axis=-1).transpose(0, 2, 1, 3)

    scale = hq ** -0.5
    common = dict(n_heads=B_HEADS, dq=B_QPAD, dv=B_VDIM, hpb=2, kvpb=2, tq=tq, scale=scale)
    att_c = attention(q, mla_keys(kn[:gl], kr[:gl], bc, lc), _heads_major(vv[:gl], bc, lc, B_HEADS, B_VDIM),
                      n_seq=bc, lq=lc, row_off=0, win=lc, name="attn_b_ctx", **common)
    k_lat = jnp.concatenate([mla_keys(kn_p, cache_b_krope[:, 0].reshape(n_pc, B_ROPE), bl, past),
                             mla_keys(kn[gl:], kr[gl:], bl, gl)], axis=2)
    v_lat = jnp.concatenate([_heads_major(vv_p, bl, past, B_HEADS, B_VDIM),
                             _heads_major(vv[gl:], bl, gl, B_HEADS, B_VDIM)], axis=2)
    att_l = attention(q, k_lat, v_lat, n_seq=bl, lq=gl, row_off=gl, win=past + gl, name="attn_b_lat", **common)
    x = finish_layer(i, jnp.concatenate([att_c, att_l], axis=0), b_wo[0], x)

    i = 2
    d3 = 3 * d

    def body_c(x_ref, mod_ref, w_ref, b_ref, u_ref):
        h = _modulate(x_ref, mod_ref, 0, 1).astype(BF16)
        for j in range(3):
            u_ref[:, j * d:(j + 1) * d] = _dot(h, w_ref[:, j * d:(j + 1) * d]) + b_ref[:, j * d:(j + 1) * d]

    (u,) = rc(body_c, t_all, [(x, "row"), (mods[i], "group"), (c_w_in[0].astype(BF16), "full"), (row2(c_b_in[0]), "full")],
              [(d3, F32)], name="proj_c")
    u_c = short_conv(u[:gl].reshape(bc, lc, d3), c_conv_w[0], row2(c_conv_b[0]), "short_conv_ctx")
    u_l = short_conv(u[gl:].reshape(bl, gl, d3), c_conv_w[0], row2(c_conv_b[0]), "short_conv_lat")
    fargs = (c_ffn_w1[0], c_ffn_b1[0], c_ffn_w2[0], c_ffn_b2[0], c_ffn_w3[0], c_ffn_freq[0], c_log_decay[0])
    filt_c = hyena_filters(lc, *fargs, name="filters_ctx").reshape(lc, 2, 2, d)
    filt_l = hyena_filters(gl, *fargs, name="filters_lat").reshape(gl, 2, 2, d)

    nc = 2 * lc
    mf_data, mf_filt, mi_c = _dft_mats_single(nc, lc, lc)
    g_c = jnp.stack([_two_sided(filt_c, o) for o in range(2)])
    spec_c = spectral(g_c.reshape(2, 1, 1, nc, d), mf_filt, dt=512, name="filt_spec_ctx")
    z_c = u_c[..., :d].reshape(bc // 2, 2, 1, lc, d)
    for o in range(2):
        gate = u_c[..., (o + 1) * d:(o + 2) * d].reshape(bc // 2, 2, 1, lc, d)
        z_c = spectral(z_c, mf_data, spec_c[o], mi_c, dt=512, epi=(gate, z_c, row2(c_skip[0, o])), name=f"conv_ctx{o}")
    zc_out = z_c.reshape(gl, d)

    nl = 2 * gl
    n1 = nl // FFT_N2
    a_data, a_filt, a_inv, mf_l, mi_l = _dft_mats_two_stage(nl, FFT_N2)
    cols = FFT_N2 * d
    g_l = jnp.stack([_two_sided(filt_l, o) for o in range(2)])
    ga = mm3(a_filt, g_l.reshape(2, n1, cols), name="filt_stage_a")
    spec_l = spectral(ga.reshape(2, 2, n1, FFT_N2, d), mf_l, dt=1024, name="filt_spec_lat")
    z_l = u_l[..., :d]
    tn = 2048
    for o in range(2):
        za = mm3(a_data, z_l.reshape(bl // 2, n1, cols), name=f"conv_lat_a{o}")
        zb = spectral(za.reshape(bl // 2, 2, n1, FFT_N2, d), mf_l, spec_l[o], mi_l, dt=1024, name=f"conv_lat_c{o}")
        gate = u_l[..., (o + 1) * d:(o + 2) * d].reshape(bl // 2, n1, cols)
        skip_t = jnp.tile(row2(c_skip[0, o]), (1, tn // d))
        z_l = mm3(a_inv, zb.reshape(bl // 2, 2 * n1, cols), tn=tn,
                  epi=(gate, z_l.reshape(bl // 2, n1, cols), skip_t), name=f"conv_lat_i{o}").reshape(bl, gl, d)
    zz = jnp.concatenate([zc_out, z_l.reshape(bl * gl, d)], axis=0)
    x = finish_layer(i, zz, c_wo[0], x)

    i = 3

    def body_d(x_ref, mod_ref, wq_ref, wk_ref, wv_ref, q_ref, k_ref, v_ref):
        h = _modulate(x_ref, mod_ref, 0, 1).astype(BF16)
        q_ref[...] = _dot(h, wq_ref[...]).astype(BF16)
        k_ref[...] = _dot(h, wk_ref[...])
        v_ref[...] = _dot(h, wv_ref[...])

    q, k, v = rc(body_d, t_all,
                 [(x, "row"), (mods[i], "group"), (d_wq[0].astype(BF16), "full"), (d_wk[0].astype(BF16), "full"),
                  (d_wv[0].astype(BF16), "full")],
                 [(d, BF16), (d, F32), (d, F32)], name="proj_d")
    out_d_k = k[:gl].reshape(bc, 1, lc, D_HEADS, hd)
    out_d_v = v[:gl].reshape(bc, 1, lc, D_HEADS, hd)
    kb, vb = k.astype(BF16), v.astype(BF16)
    scale = hd ** -0.5
    common = dict(n_heads=D_HEADS, dq=hd, dv=hd, hpb=2, kvpb=2, tq=tq, scale=scale)
    att_c = attention(q, _heads_major(kb[:gl], bc, lc, D_HEADS, hd), _heads_major(vb[:gl], bc, lc, D_HEADS, hd),
                      n_seq=bc, lq=lc, row_off=0, win=lc, name="attn_d_ctx", **common)
    rows = gl // GRID_W
    kh = min(MAX_NBR_ROWS, rows)
    qrows = tq // GRID_W
    krows = qrows + kh
    win_d = krows * GRID_W
    nq_l = gl // tq
    ql = jnp.arange(tq)
    kl_ = jnp.arange(win_d)
    qr_l, qc = (ql // GRID_W)[:, None], (ql % GRID_W)[:, None]
    kr_l, kc_ = (kl_ // GRID_W)[None, :], (kl_ % GRID_W)[None, :]
    c0 = jnp.clip(qc - NBR_COLS // 2, 0, GRID_W - NBR_COLS)
    col_ok = (kc_ >= c0) & (kc_ < c0 + NBR_COLS)
    dc = jnp.clip(kc_ - qc, 1 - NBR_COLS, NBR_COLS - 1) + NBR_COLS - 1
    tabs = []
    for off, lo in ((0, jnp.zeros_like(qr_l)), (kh // 2, qr_l), (kh, jnp.full_like(qr_l, qrows))):
        ok = col_ok & (kr_l >= lo) & (kr_l < lo + kh)
        dr = jnp.clip(kr_l - off - qr_l + MAX_NBR_ROWS - 1, 0, 2 * MAX_NBR_ROWS - 2)
        tabs.append(jnp.where(ok[None], d_rel_bias[0][:, dr, dc], NEG_INF))
    nbr_bias = jnp.stack(tabs, axis=1).astype(F32)
    att_l = attention(q, _heads_major(kb[gl:], bl, gl, D_HEADS, hd), _heads_major(vb[gl:], bl, gl, D_HEADS, hd),
                      n_seq=bl, lq=gl, row_off=gl, win=win_d,
                      start_fn=lambda ii: jnp.clip(ii * qrows - kh // 2, 0, rows - krows) * GRID_W,
                      kc=cache_d_k[:, 0].transpose(0, 2, 1, 3).astype(BF16),
                      vc=cache_d_v[:, 0].transpose(0, 2, 1, 3).astype(BF16),
                      bias=nbr_bias, type_fn=lambda ii: jnp.where(ii == 0, 0, jnp.where(ii == nq_l - 1, 2, 1)),
                      name="attn_d_lat", **common)
    x = finish_layer(i, jnp.concatenate([att_c, att_l], axis=0), d_wo[0], x)

    y_prompt = x[:gl].reshape(bc, lc, d)
    y_sample = x[gl:].reshape(bl, gl, d)
    return (y_prompt, y_sample, out_a_k, out_a_v, out_b_ckv, out_b_krope, out_d_k, out_d_v)
```

```python
import functools

import jax
import jax.numpy as jnp
import numpy as np
from jax import lax
from jax.experimental import pallas as pl
from jax.experimental.pallas import tpu as pltpu

F32 = jnp.float32
BF16 = jnp.bfloat16

GRID_W = 64
HEAD_DIM = 64
ROPE_BASE = 10000.0
LN_EPS = 1e-5
RMS_EPS = 1e-6
NEG_INF = -1e30
DEPTH = 4
DEEPNORM_ALPHA = (2 * DEPTH) ** 0.25
A_HEADS = 16
A_KV_HEADS = 4
A_WINDOW = 128
B_HEADS = 16
B_Q_RANK = 384
B_KV_RANK = 256
B_NOPE = 64
B_ROPE = 32
B_VDIM = 64
B_QPAD = 128
C_POS_BANDS = 16
D_HEADS = 16
MAX_NBR_ROWS = 8
NBR_COLS = 16
N_GROUPS = 4
EXPERTS_PER_GROUP = 8
N_EXPERTS = N_GROUPS * EXPERTS_PER_GROUP
D_EXPERT = 512
MOE_BLK = 256
LANES = 128
LOG2E = 1.4426950408889634
QSCALE_64 = HEAD_DIM ** -0.5 * LOG2E
QSCALE_MLA = (B_NOPE + B_ROPE) ** -0.5 * LOG2E
FFT_N2 = 128
VMEM_LIMIT = 56 * 1024 * 1024
ROW_TILE = 512


def _params(n_axes):
    return pltpu.CompilerParams(dimension_semantics=("arbitrary",) * n_axes, vmem_limit_bytes=VMEM_LIMIT)


def _dot(a, b):
    return jnp.dot(a, b, preferred_element_type=F32)


def _split(x):
    hi = x.astype(BF16)
    lo = (x - hi.astype(F32)).astype(BF16)
    return hi, lo


def _dot3(a, b):
    ah, al = _split(a)
    bh, bl = _split(b)
    return _dot(ah, bh) + _dot(ah, bl) + _dot(al, bh)


def _layer_norm(y, g, b):
    mu = jnp.mean(y, axis=-1, keepdims=True)
    d = y - mu
    var = jnp.mean(d * d, axis=-1, keepdims=True)
    return d * lax.rsqrt(var + LN_EPS) * g + b


def _rms(y, g):
    return y * lax.rsqrt(jnp.mean(y * y, axis=-1, keepdims=True) + RMS_EPS) * g


def _tile_lanes(t, n):
    reps = n // t.shape[-1]
    return t if reps == 1 else jnp.concatenate([t] * reps, axis=-1)


def row_call(body, m_rows, ins, outs, *, tm=ROW_TILE, group_len=None, name=None, scratch=(), params=None):
    nb = None if group_len is None else group_len // tm
    n_tiles = m_rows // tm
    in_specs = []
    for a, kind in ins:
        if kind == "row":
            in_specs.append(pl.BlockSpec((tm, a.shape[1]), lambda i: (i, 0)))
        elif kind == "any":
            in_specs.append(pl.BlockSpec(memory_space=pl.ANY))
        elif kind == "smem":
            in_specs.append(pl.BlockSpec(memory_space=pltpu.SMEM))
        elif kind == "smem_rows":
            in_specs.append(pl.BlockSpec((a.shape[0] // n_tiles,), lambda i: (i,), memory_space=pltpu.SMEM))
        elif kind == "full":
            in_specs.append(pl.BlockSpec(a.shape, lambda i, nd=a.ndim: (0,) * nd))
        elif kind == "group":
            in_specs.append(pl.BlockSpec((1,) + a.shape[1:], lambda i: (i // nb, 0, 0)))
        elif kind == "pos":
            in_specs.append(pl.BlockSpec((1, tm, a.shape[2]), lambda i: (jnp.minimum(i // nb, 1), i % nb, 0)))
        else:
            raise ValueError(kind)
    out_specs, out_shape = [], []
    for o in outs:
        if len(o) == 4 and o[3] == "any":
            out_specs.append(pl.BlockSpec(memory_space=pl.ANY))
            out_shape.append(jax.ShapeDtypeStruct((o[0], o[1]), o[2]))
        elif len(o) == 4 and o[3] == "heads":
            out_specs.append(pl.BlockSpec((o[0], tm, o[1]), lambda i: (0, i, 0)))
            out_shape.append(jax.ShapeDtypeStruct((o[0], m_rows, o[1]), o[2]))
        elif len(o) == 4 and o[3] == "ctx":
            nkeep = o[0] // tm
            out_specs.append(pl.BlockSpec((tm, o[1]), lambda i, nkeep=nkeep: (jnp.minimum(i, nkeep - 1), 0)))
            out_shape.append(jax.ShapeDtypeStruct((o[0], o[1]), o[2]))
        else:
            out_specs.append(pl.BlockSpec((tm, o[0]), lambda i: (i, 0)))
            out_shape.append(jax.ShapeDtypeStruct((m_rows, o[0]), o[1]))
    return pl.pallas_call(
        body, grid=(n_tiles,), in_specs=in_specs, out_specs=out_specs, out_shape=out_shape,
        scratch_shapes=list(scratch), compiler_params=params or _params(1), name=name)(*[a for a, _ in ins])


def _modulate(x_ref, mod_ref, shift_row, scale_row):
    m = mod_ref[0]
    return x_ref[...] * (1.0 + m[scale_row:scale_row + 1]) + m[shift_row:shift_row + 1]


def modulation_all(cvec, mod_w, mod_b):
    depth, d, d6 = mod_w.shape
    g = cvec.shape[0]
    gp = -(-g // 16) * 16
    cp = jnp.zeros((gp, d), F32).at[:g].set(cvec)
    tn = 1024

    def body(c_ref, w_ref, b_ref, o_ref):
        c = c_ref[...]
        a = (c * jax.nn.sigmoid(c)).astype(BF16)
        o_ref[0] = _dot(a, w_ref[0].astype(BF16)) + b_ref[0]

    out = pl.pallas_call(
        body, grid=(depth, d6 // tn),
        in_specs=[pl.BlockSpec((gp, d), lambda l, j: (0, 0)),
                  pl.BlockSpec((1, d, tn), lambda l, j: (l, 0, j)),
                  pl.BlockSpec((1, 1, tn), lambda l, j: (l, 0, j))],
        out_specs=pl.BlockSpec((1, gp, tn), lambda l, j: (l, 0, j)),
        out_shape=jax.ShapeDtypeStruct((depth, gp, d6), F32),
        compiler_params=_params(2), name="modulation")(cp, mod_w, mod_b.reshape(depth, 1, d6))
    return out[:, :g].reshape(depth, g, 6, d)


def attention(q, kl, vl, *, n_seq, lq, row_off, n_heads, dq, dv, hpb, kvpb, tq, win,
              start_fn=None, kc=None, vc=None, bias=None, type_fn=None, sink=None, name=None):
    nq = lq // tq
    koff = row_off // lq
    rep = hpb // kvpb
    has_ctx, has_bias, has_sink = kc is not None, bias is not None, sink is not None
    bias_heads = has_bias and bias.shape[0] > 1
    off_blk = row_off // tq

    def kern(*refs):
        it = iter(refs)
        q_ref, kl_ref, vl_ref = next(it), next(it), next(it)
        kc_ref = next(it) if has_ctx else None
        vc_ref = next(it) if has_ctx else None
        b_ref = next(it) if has_bias else None
        s_ref = next(it) if has_sink else None
        o_ref = next(it)
        i = pl.program_id(2)
        hb = pl.program_id(1)
        if start_fn is None:
            start = 0
        else:
            start = pl.multiple_of(start_fn(i), 64)
        outs = []
        for j in range(hpb):
            kv = j // rep
            qj = q_ref[:, j * dq:(j + 1) * dq]
            k = kl_ref[kv, pl.ds(start, win), :]
            v = vl_ref[kv, pl.ds(start, win), :]
            s = lax.dot_general(qj, k, (((1,), (1,)), ((), ())), preferred_element_type=F32)
            if has_bias:
                s = s + b_ref[j if bias_heads else 0, 0]
            m = jnp.max(s, axis=1, keepdims=True)
            if has_ctx:
                sc = lax.dot_general(qj, kc_ref[kv], (((1,), (1,)), ((), ())), preferred_element_type=F32)
                m = jnp.maximum(m, jnp.max(sc, axis=1, keepdims=True))
            if has_sink:
                sk = s_ref[hb * hpb + j]
                m = jnp.maximum(m, sk)
            p = jnp.exp2(s - m)
            l = jnp.sum(p, axis=1, keepdims=True)
            acc = _dot(p.astype(BF16), v)
            if has_ctx:
                pc = jnp.exp2(sc - m)
                l = l + jnp.sum(pc, axis=1, keepdims=True)
                acc = acc + _dot(pc.astype(BF16), vc_ref[kv])
            if has_sink:
                l = l + jnp.exp2(sk - m)
            outs.append(acc / l)
        o_ref[...] = jnp.concatenate(outs, axis=1).astype(o_ref.dtype)

    ins = [q, kl, vl]
    in_specs = [pl.BlockSpec((tq, hpb * dq), lambda b, h, i: (off_blk + b * nq + i, h)),
                pl.BlockSpec((kvpb, lq, dq), lambda b, h, i: (h, koff + b, 0)),
                pl.BlockSpec((kvpb, lq, dv), lambda b, h, i: (h, koff + b, 0))]
    if has_ctx:
        lc = kc.shape[1] // n_seq
        ins += [kc, vc]
        in_specs += [pl.BlockSpec((kvpb, lc, dq), lambda b, h, i: (h, b, 0)),
                     pl.BlockSpec((kvpb, lc, dv), lambda b, h, i: (h, b, 0))]
    if has_bias:
        ins.append(bias * LOG2E)
        hb_blk = hpb if bias_heads else 1
        in_specs.append(pl.BlockSpec((hb_blk, 1, tq, win),
                                     lambda b, h, i: (h if bias_heads else 0, type_fn(i), 0, 0)))
    if has_sink:
        ins.append(sink.astype(F32) * LOG2E)
        in_specs.append(pl.BlockSpec(memory_space=pltpu.SMEM))
    return pl.pallas_call(
        kern, grid=(n_seq, n_heads // hpb, nq), in_specs=in_specs,
        out_specs=pl.BlockSpec((tq, hpb * dv), lambda b, h, i: (b * nq + i, h)),
        out_shape=jax.ShapeDtypeStruct((n_seq * lq, n_heads * dv), BF16),
        compiler_params=_params(3), name=name)(*ins)


def _cache_heads_major(c):
    n_seq, length, n_heads, d = c.shape
    return c.astype(BF16).transpose(2, 0, 1, 3).reshape(n_heads, n_seq * length, d)


def _store_mla_keys(ref, k_nope, k_rope):
    rows = k_nope.shape[0]
    tail = jnp.concatenate([k_rope, jnp.zeros((rows, B_QPAD - B_NOPE - B_ROPE), k_rope.dtype)], axis=1)
    for hh in range(ref.shape[0]):
        ref[hh] = jnp.concatenate([k_nope[:, hh * B_NOPE:(hh + 1) * B_NOPE], tail], axis=1).astype(ref.dtype)


def _store_heads(ref, val, width):
    for hh in range(ref.shape[0]):
        ref[hh] = val[:, hh * width:(hh + 1) * width].astype(ref.dtype)


def _rope_tables(length, dim, lead, tail, reps):
    half = dim // 2
    nf = half // 2
    t = jnp.arange(length)
    row = (t // GRID_W).astype(F32)
    col = (t % GRID_W).astype(F32)
    inv = ROPE_BASE ** (-jnp.arange(nf, dtype=F32) / nf)
    ang = jnp.concatenate([row[:, None] * inv, col[:, None] * inv], axis=-1)
    cos, sin = jnp.cos(ang), jnp.sin(ang)
    c = jnp.concatenate([jnp.ones((length, lead), F32), cos, cos, jnp.ones((length, tail), F32)] * reps, axis=1)
    s = jnp.concatenate([jnp.zeros((length, lead), F32), sin, sin, jnp.zeros((length, tail), F32)] * reps, axis=1)
    return (jnp.stack([jnp.ones_like(c), c]), jnp.stack([jnp.zeros_like(s), s]))


def _rot_cols(w, dim):
    k, n = w.shape
    wb = w.reshape(k, n // dim, dim)
    half = dim // 2
    return jnp.concatenate([-wb[..., half:], wb[..., :half]], axis=-1).reshape(k, n)


def _cs(phase, n, sign):
    ang = (2.0 * np.pi / n) * (phase % n).astype(F32)
    return jnp.cos(ang), sign * jnp.sin(ang)


def _cblock(wr, wi):
    return jnp.concatenate([jnp.concatenate([wr, -wi], axis=-1), jnp.concatenate([wi, wr], axis=-1)], axis=-2)


def mm3(a, x, *, tn=2048, epi=None, name=None):
    bt, k, n = x.shape
    m = a.shape[0]

    def kern(*refs):
        if epi is None:
            a_ref, x_ref, o_ref = refs
        else:
            a_ref, x_ref, g_ref, z_ref, s_ref, o_ref = refs
        y = _dot3(a_ref[...], x_ref[0])
        if epi is not None:
            y = g_ref[0] * (y + s_ref[...] * z_ref[0])
        o_ref[0] = y

    ins = [a, x]
    in_specs = [pl.BlockSpec((m, k), lambda b, j: (0, 0)), pl.BlockSpec((1, k, tn), lambda b, j: (b, 0, j))]
    if epi is not None:
        ins += list(epi)
        in_specs += [pl.BlockSpec((1, m, tn), lambda b, j: (b, 0, j)),
                     pl.BlockSpec((1, m, tn), lambda b, j: (b, 0, j)),
                     pl.BlockSpec((1, tn), lambda b, j: (0, 0))]
    return pl.pallas_call(
        kern, grid=(bt, n // tn), in_specs=in_specs,
        out_specs=pl.BlockSpec((1, m, tn), lambda b, j: (b, 0, j)),
        out_shape=jax.ShapeDtypeStruct((bt, m, n), F32), compiler_params=_params(2), name=name)(*ins)


def spectral(x, mf, g=None, mi=None, *, dt, epi=None, name=None):
    p_n, planes, k1_n, nin, d = x.shape
    nf = mf.shape[1] // 2
    nout = nf if g is None else mi.shape[1] // 2

    def kern(*refs):
        it = iter(refs)
        x_ref, mf_ref = next(it), next(it)
        g_ref = next(it) if g is not None else None
        mi_ref = next(it) if g is not None else None
        if epi is not None:
            gate_ref, z_ref, skip_ref = next(it), next(it), next(it)
        o_ref = next(it)
        xs = [x_ref[0, pp, 0] for pp in range(planes)]
        xin = xs[0] if planes == 1 else jnp.concatenate(xs, axis=0)
        f = _dot3(mf_ref[0], xin)
        if g is not None:
            fr, fi = f[:nf], f[nf:]
            gr, gi = g_ref[0, 0], g_ref[1, 0]
            y = jnp.concatenate([fr * gr - fi * gi, fr * gi + fi * gr], axis=0)
            f = _dot3(mi_ref[0], y)
        for pp in range(2):
            y = f[pp * nout:(pp + 1) * nout]
            if epi is not None:
                y = gate_ref[0, pp, 0] * (y + skip_ref[...] * z_ref[0, pp, 0])
            o_ref[0, pp, 0] = y

    ins = [x, mf]
    in_specs = [pl.BlockSpec((1, planes, 1, nin, dt), lambda k, j, p: (p, 0, k, 0, j)),
                pl.BlockSpec((1,) + mf.shape[1:], lambda k, j, p: (k, 0, 0))]
    if g is not None:
        ins += [g, mi]
        in_specs += [pl.BlockSpec((2, 1, nf, dt), lambda k, j, p: (0, k, 0, j)),
                     pl.BlockSpec((1,) + mi.shape[1:], lambda k, j, p: (k, 0, 0))]
    if epi is not None:
        ins += list(epi)
        in_specs += [pl.BlockSpec((1, 2, 1, nout, dt), lambda k, j, p: (p, 0, k, 0, j)),
                     pl.BlockSpec((1, 2, 1, nout, dt), lambda k, j, p: (p, 0, k, 0, j)),
                     pl.BlockSpec((1, dt), lambda k, j, p: (0, j))]
    return pl.pallas_call(
        kern, grid=(k1_n, d // dt, p_n), in_specs=in_specs,
        out_specs=pl.BlockSpec((1, 2, 1, nout, dt), lambda k, j, p: (p, 0, k, 0, j)),
        out_shape=jax.ShapeDtypeStruct((p_n, 2, k1_n, nout, d), F32),
        compiler_params=_params(3), name=name)(*ins)


def _two_sided(filt, o):
    hf, hb = filt[:, o, 0], filt[:, o, 1]
    return jnp.concatenate([hf[:1] + hb[:1], hf[1:], jnp.zeros_like(hf[:1]), hb[1:][::-1]], axis=0)


def hyena_filters(length, w1, b1, w2, b2, w3, freq, log_decay, name):
    hid = w2.shape[0]
    t = jnp.linspace(0.0, 1.0, length, dtype=F32)[:, None]
    ang = 2.0 * jnp.pi * t * jnp.arange(1, C_POS_BANDS + 1, dtype=F32)
    z = jnp.concatenate([t, jnp.cos(ang), jnp.sin(ang)], axis=-1)
    kpad = LANES - z.shape[1]
    z = jnp.pad(z, ((0, 0), (0, kpad)))
    w1p = jnp.pad(w1, ((0, kpad), (0, 0)))
    ncol = w3.shape[1]
    tn = 256

    def kern(z_ref, w1_ref, b1_ref, w2_ref, b2_ref, f_ref, w3_ref, ld_ref, o_ref):
        fr = f_ref[...]
        a = jnp.sin(fr * (_dot3(z_ref[...], w1_ref[...]) + b1_ref[...]))
        a = jnp.sin(fr * (_dot3(a, w2_ref[...]) + b2_ref[...]))
        filt = _dot3(a, w3_ref[...])
        tt = lax.broadcasted_iota(jnp.int32, filt.shape, 0).astype(F32) * (1.0 / (length - 1))
        filt = filt * jnp.exp(-jnp.exp(ld_ref[...]) * tt)
        o_ref[...] = filt / (jnp.sum(jnp.abs(filt), axis=0, keepdims=True) + 1e-6)

    full = lambda a: pl.BlockSpec(a.shape, lambda j: (0, 0))
    args = [z, w1p, b1.reshape(1, hid), w2, b2.reshape(1, hid), freq.reshape(1, hid)]
    return pl.pallas_call(
        kern, grid=(ncol // tn,),
        in_specs=[full(a) for a in args] + [pl.BlockSpec((hid, tn), lambda j: (0, j)),
                                            pl.BlockSpec((1, tn), lambda j: (0, j))],
        out_specs=pl.BlockSpec((length, tn), lambda j: (0, j)),
        out_shape=jax.ShapeDtypeStruct((length, ncol), F32),
        compiler_params=_params(1), name=name)(*args, w3, log_decay.reshape(1, ncol))


def short_conv(u, w, b, *, row_off, n_seq, length, parts, name):
    c = u.shape[1]
    dt = 256
    per = c // parts // dt
    first = row_off // length

    def kern(u_ref, w_ref, b_ref, o_ref):
        x = u_ref[...]
        r = lax.broadcasted_iota(jnp.int32, x.shape, 0)
        prev = jnp.where(r == 0, 0.0, pltpu.roll(x, 1, 0))
        nxt = jnp.where(r == length - 1, 0.0, pltpu.roll(x, length - 1, 0))
        wv = w_ref[...]
        o_ref[0, 0] = prev * wv[0:1] + x * wv[1:2] + nxt * wv[2:3] + b_ref[...]

    return pl.pallas_call(
        kern, grid=(n_seq, c // dt),
        in_specs=[pl.BlockSpec((length, dt), lambda s, j: (first + s, j)),
                  pl.BlockSpec((3, dt), lambda s, j: (0, j)),
                  pl.BlockSpec((1, dt), lambda s, j: (0, j))],
        out_specs=pl.BlockSpec((1, 1, length, dt), lambda s, j: (j // per, s, 0, j % per)),
        out_shape=jax.ShapeDtypeStruct((parts, n_seq, length, c // parts), F32),
        compiler_params=_params(2), name=name)(u, w, b)


def _dft_mats_single(n, nin_data, nout):
    k = jnp.arange(n, dtype=jnp.int32)
    fr, fi = _cs(k[:, None] * k[None, :nin_data], n, -1.0)
    mf_data = _cblock(fr, fi)[None]
    gr, gi = _cs(k[:, None] * k[None, :], n, -1.0)
    mf_filt = jnp.concatenate([gr, gi], axis=0)[None]
    ir, ii = _cs(k[:nout, None] * k[None, :], n, 1.0)
    mi = _cblock(ir / n, ii / n)[None]
    return mf_data, mf_filt, mi


def _dft_mats_two_stage(n, n2):
    n1 = n // n2
    k1 = jnp.arange(n1, dtype=jnp.int32)
    t1h = jnp.arange(n1 // 2, dtype=jnp.int32)
    ar, ai = _cs((n // n1) * k1[:, None] * t1h[None, :], n, -1.0)
    a_data = _cblock(ar, ai)
    fr, fi = _cs((n // n1) * k1[:, None] * k1[None, :], n, -1.0)
    a_filt = jnp.concatenate([fr, fi], axis=0)
    br, bi = _cs((n // n1) * t1h[:, None] * k1[None, :], n, 1.0)
    a_inv = _cblock(br, bi)
    t2 = jnp.arange(n2, dtype=jnp.int32)
    kk = k1[:, None, None] + n1 * t2[None, :, None]
    mr, mi_ = _cs(kk * t2[None, None, :], n, -1.0)
    mf = _cblock(mr, mi_)
    vr, vi = _cs(jnp.swapaxes(kk, 1, 2) * t2[None, :, None], n, 1.0)
    mi = _cblock(vr / n, vi / n)
    return a_data, a_filt, a_inv, mf, mi


def _route(logits):
    lane = lax.broadcasted_iota(jnp.int32, logits.shape, 1).astype(F32)
    big = 1e9
    lg = jnp.where(lane < N_GROUPS, logits, -jnp.inf)
    mg = jnp.max(lg, axis=1, keepdims=True)
    gi = jnp.min(jnp.where(lg == mg, lane, big), axis=1, keepdims=True)
    p_g = 1.0 / jnp.sum(jnp.exp(lg - mg), axis=1, keepdims=True)
    lo = N_GROUPS + EXPERTS_PER_GROUP * gi
    le = jnp.where((lane >= lo) & (lane < lo + EXPERTS_PER_GROUP), logits, -jnp.inf)
    m1 = jnp.max(le, axis=1, keepdims=True)
    i1 = jnp.min(jnp.where(le == m1, lane, big), axis=1, keepdims=True)
    le2 = jnp.where(lane == i1, -jnp.inf, le)
    m2 = jnp.max(le2, axis=1, keepdims=True)
    i2 = jnp.min(jnp.where(le2 == m2, lane, big), axis=1, keepdims=True)
    e2 = jnp.exp(m2 - m1)
    w1 = p_g / (1.0 + e2)
    w2 = p_g * e2 / (1.0 + e2)
    return lane, i1 - N_GROUPS, i2 - N_GROUPS, w1, w2


def _rank_in_tile(lane, e1, e2):
    tm = lane.shape[0]
    picks = jnp.where(lane == e1, 1.0, 0.0) + jnp.where(lane == e2, 1.0, 0.0)
    r = lax.broadcasted_iota(jnp.int32, (tm, tm), 0)
    c = lax.broadcasted_iota(jnp.int32, (tm, tm), 1)
    earlier = jnp.where(c < r, 1.0, 0.0).astype(BF16)
    return _dot(earlier, picks.astype(BF16)), picks


def expert_mlp(xs, blk_e, n_used, w_gate, w_up, w_down):
    p_rows, d = xs.shape
    de = w_gate.shape[2]
    nblk = p_rows // MOE_BLK

    def kern(be_ref, nu_ref, x_ref, wg_ref, wu_ref, wd_ref, o_ref, wg_s, wu_s, wd_s):
        i = pl.program_id(0)
        used = i < nu_ref[0]
        fresh = jnp.logical_or(i == 0, be_ref[i] != be_ref[jnp.maximum(i - 1, 0)])

        @pl.when(jnp.logical_and(used, fresh))
        def _():
            wg_s[...] = wg_ref[0].astype(BF16)
            wu_s[...] = wu_ref[0].astype(BF16)
            wd_s[...] = wd_ref[0].astype(BF16)

        @pl.when(used)
        def _():
            x = x_ref[...].astype(BF16)
            g = _dot(x, wg_s[...])
            u = _dot(x, wu_s[...])
            a = (g * jax.nn.sigmoid(g) * u).astype(BF16)
            o_ref[...] = _dot(a, wd_s[...])

        @pl.when(jnp.logical_not(used))
        def _():
            o_ref[...] = jnp.zeros_like(o_ref)

    last = lambda i, nu: jnp.minimum(i, nu[0] - 1)
    grid_spec = pltpu.PrefetchScalarGridSpec(
        num_scalar_prefetch=2, grid=(nblk,),
        in_specs=[pl.BlockSpec((MOE_BLK, d), lambda i, be, nu: (last(i, nu), 0)),
                  pl.BlockSpec((1, d, de), lambda i, be, nu: (be[last(i, nu)], 0, 0)),
                  pl.BlockSpec((1, d, de), lambda i, be, nu: (be[last(i, nu)], 0, 0)),
                  pl.BlockSpec((1, de, d), lambda i, be, nu: (be[last(i, nu)], 0, 0))],
        out_specs=pl.BlockSpec((MOE_BLK, d), lambda i, be, nu: (i, 0)),
        scratch_shapes=[pltpu.VMEM((d, de), BF16), pltpu.VMEM((d, de), BF16), pltpu.VMEM((de, d), BF16)])
    return pl.pallas_call(
        kern, grid_spec=grid_spec, out_shape=jax.ShapeDtypeStruct((p_rows, d), F32),
        compiler_params=_params(1), name="expert_mlp")(blk_e, n_used, xs, w_gate, w_up, w_down)


def _slot_plan(experts, ranks):
    n = 2 * experts.shape[0]
    ids = jnp.arange(N_EXPERTS, dtype=jnp.int32)
    onehot = experts[..., None] == ids
    counts = jnp.sum(onehot, axis=(0, 1)).astype(jnp.int32)
    padded = (counts + MOE_BLK - 1) // MOE_BLK * MOE_BLK
    pend = jnp.cumsum(padded)
    pstart = pend - padded
    p_rows = -(-n // MOE_BLK) * MOE_BLK + N_EXPERTS * MOE_BLK
    nblk = p_rows // MOE_BLK
    blk_first = jnp.arange(nblk, dtype=jnp.int32) * MOE_BLK
    blk_e = jnp.minimum(jnp.sum(pend[None, :] <= blk_first[:, None], axis=1), N_EXPERTS - 1).astype(jnp.int32)
    n_used = (pend[-1:] // MOE_BLK).astype(jnp.int32)
    blk_ids = jnp.arange(nblk, dtype=jnp.int32)
    partly = jnp.any((blk_ids[:, None] == (pend // MOE_BLK - 1)[None, :]) & (counts % MOE_BLK != 0)[None, :], axis=1)
    zero_blk = (partly | (blk_ids >= n_used[0])).astype(jnp.int32)
    slot = ranks + jnp.sum(jnp.where(onehot, pstart, 0), axis=-1)
    return slot.reshape(n).astype(jnp.int32), blk_e, n_used, zero_blk, p_rows


def _dma_params():
    return pltpu.CompilerParams(dimension_semantics=("arbitrary",), vmem_limit_bytes=VMEM_LIMIT,
                                disable_bounds_checks=True)


DMA_UNROLL = 8


def moe_dispatch(x1, mod, slot, zero_blk, p_rows, group_len, name):
    t, d = x1.shape
    tm = ROW_TILE

    def body(slot_ref, zb_ref, x_ref, mod_ref, xs_ref, h_ref, zero_ref, sem):
        h_ref[...] = _modulate(x_ref, mod_ref, 3, 4)

        @pl.when(pl.program_id(0) == 0)
        def _():
            zero_ref[...] = jnp.zeros_like(zero_ref)

            def zstart(b, c):
                @pl.when(zb_ref[b] != 0)
                def _():
                    first = pl.multiple_of(b * MOE_BLK, MOE_BLK)
                    pltpu.make_async_copy(zero_ref, xs_ref.at[pl.ds(first, MOE_BLK)], sem).start()
                return c

            def zwait(b, c):
                @pl.when(zb_ref[b] != 0)
                def _():
                    pltpu.make_async_copy(zero_ref, xs_ref.at[pl.ds(0, MOE_BLK)], sem).wait()
                return c

            lax.fori_loop(0, p_rows // MOE_BLK, zstart, 0)
            lax.fori_loop(0, p_rows // MOE_BLK, zwait, 0)

        def start(rr, c):
            for u in range(DMA_UNROLL):
                r = rr * DMA_UNROLL + u
                for k in range(2):
                    pltpu.make_async_copy(h_ref.at[pl.ds(r, 1)], xs_ref.at[pl.ds(slot_ref[2 * r + k], 1)], sem).start()
            return c

        lax.fori_loop(0, tm // DMA_UNROLL, start, 0)
        for k in range(2):
            pltpu.make_async_copy(h_ref, xs_ref.at[pl.ds(0, tm)], sem).wait()

    return row_call(body, t, [(slot, "smem_rows"), (zero_blk, "smem"), (x1, "row"), (mod, "group")],
                    [(p_rows, d, F32, "any")], group_len=group_len,
                    scratch=[pltpu.VMEM((tm, d), F32), pltpu.VMEM((MOE_BLK, d), F32), pltpu.SemaphoreType.DMA(())],
                    params=_dma_params(), name=name)[0]


def kernel(x_prompt, x_sample, cache_a_k, cache_a_v, cache_b_ckv, cache_b_krope, cache_d_k, cache_d_v, c_ctx, c, mod_w, mod_b, ln_g, ln_b, a_wq, a_wk, a_wv, a_wo, a_sink, b_wq_a, b_q_norm, b_wq_b, b_wkv_a, b_kv_norm, b_wk_b, b_wv_b, b_wo, c_w_in, c_b_in, c_conv_w, c_conv_b, c_ffn_w1, c_ffn_b1, c_ffn_w2, c_ffn_b2, c_ffn_w3, c_ffn_freq, c_log_decay, c_skip, c_wo, d_wq, d_wk, d_wv, d_wo, d_rel_bias, moe_wr_g, moe_br_g, moe_wr_e, moe_br_e, moe_w_gate, moe_w_up, moe_w_down):
    bc, lc, d = x_prompt.shape
    bl, ll, _ = x_sample.shape
    past = cache_a_k.shape[2]
    gl = ll
    assert bc * lc == gl and d == A_HEADS * HEAD_DIM
    ng = 1 + bl
    t_all = ng * gl
    x = jnp.concatenate([x_prompt.reshape(gl, d), x_sample.reshape(bl * gl, d)], axis=0)
    cvec = jnp.concatenate([c_ctx[None, :], c], axis=0)
    mods = modulation_all(cvec, mod_w, mod_b)
    rc = functools.partial(row_call, group_len=gl)
    row2 = lambda v: v.reshape(1, -1)
    n_ctx_tiles = gl // ROW_TILE

    def post_mixer(i, att, wo, x_in):
        unused = LANES - N_GROUPS - N_EXPERTS
        wr = jnp.concatenate([moe_wr_g[i], moe_wr_e[i], jnp.zeros((d, unused), F32)], axis=1)
        br = jnp.concatenate([moe_br_g[i], moe_br_e[i], jnp.zeros((unused,), F32)])[None, :]

        def body(att_ref, x_ref, mod_ref, wo_ref, g_ref, b_ref, wr_ref, br_ref, x1_ref, info_ref, w_ref, seen_ref):
            @pl.when(pl.program_id(0) == 0)
            def _():
                seen_ref[...] = jnp.zeros_like(seen_ref)

            m = mod_ref[0]
            o = _dot(att_ref[...].astype(BF16), wo_ref[...])
            x1 = _layer_norm(DEEPNORM_ALPHA * x_ref[...] + m[2:3] * o, g_ref[...], b_ref[...])
            x1_ref[...] = x1
            h = x1 * (1.0 + m[4:5]) + m[3:4]
            lane, e1, e2, w1, w2 = _route(_dot3(h, wr_ref[...]) + br_ref[...])
            before, picks = _rank_in_tile(lane, e1, e2)
            before = before + seen_ref[...]
            r1 = jnp.sum(jnp.where(lane == e1, before, 0.0), axis=1, keepdims=True)
            r2 = jnp.sum(jnp.where(lane == e2, before, 0.0), axis=1, keepdims=True)
            seen_ref[...] += jnp.sum(picks, axis=0, keepdims=True)
            info = jnp.where(lane == 0, e1, jnp.where(lane == 1, e2, jnp.where(lane == 2, r1, jnp.where(lane == 3, r2, 0.0))))
            info_ref[...] = info.astype(jnp.int32)
            w_ref[...] = jnp.where(lane == 0, w1, jnp.where(lane == 1, w2, 0.0))

        return rc(body, t_all,
                  [(att, "row"), (x_in, "row"), (mods[i], "group"), (wo.astype(BF16), "full"),
                   (row2(ln_g[i, 0]), "full"), (row2(ln_b[i, 0]), "full"), (wr, "full"), (br, "full")],
                  [(d, F32), (LANES, jnp.int32), (LANES, F32)],
                  scratch=[pltpu.VMEM((1, LANES), F32)], name=f"post_mixer{i}")

    def moe(i, x1, info, w):
        slot, blk_e, n_used, zero_blk, p_rows = _slot_plan(info[:, 0:2], info[:, 2:4])
        xs = moe_dispatch(x1, mods[i], slot, zero_blk, p_rows, gl, f"moe_dispatch{i}")
        ys = expert_mlp(xs, blk_e, n_used, moe_w_gate[i], moe_w_up[i], moe_w_down[i])
        tm = ROW_TILE

        def body(slot_ref, x_ref, w_ref, mod_ref, g_ref, b_ref, ys_ref, out_ref, buf, sem):
            def start(rr, c):
                for u in range(DMA_UNROLL):
                    r = rr * DMA_UNROLL + u
                    for k in range(2):
                        pltpu.make_async_copy(ys_ref.at[pl.ds(slot_ref[2 * r + k], 1)], buf.at[k, pl.ds(r, 1)], sem).start()
                return c

            lax.fori_loop(0, tm // DMA_UNROLL, start, 0)
            for k in range(2):
                pltpu.make_async_copy(ys_ref.at[pl.ds(0, tm)], buf.at[k], sem).wait()
            m = mod_ref[0]
            wv = w_ref[...]
            y = wv[:, 0:1] * buf[0] + wv[:, 1:2] * buf[1]
            out_ref[...] = _layer_norm(DEEPNORM_ALPHA * x_ref[...] + m[5:6] * y, g_ref[...], b_ref[...])

        return rc(body, t_all,
                  [(slot, "smem_rows"), (x1, "row"), (w, "row"), (mods[i], "group"),
                   (row2(ln_g[i, 1]), "full"), (row2(ln_b[i, 1]), "full"), (ys, "any")],
                  [(d, F32)], scratch=[pltpu.VMEM((2, tm, d), F32), pltpu.SemaphoreType.DMA(())],
                  params=_dma_params(), name=f"moe_combine{i}")[0]

    def finish_layer(i, att, wo, x_in):
        x1, info, w = post_mixer(i, att, wo, x_in)
        return moe(i, x1, info, w)

    tq = 256
    i = 0
    hd = HEAD_DIM
    kvw = A_KV_HEADS * hd
    cq, sq = _rope_tables(gl, hd, 0, 0, LANES // hd)
    wq, wk, wv = a_wq[0], a_wk[0], a_wv[0]

    def body_a(x_ref, mod_ref, wq_ref, wqr_ref, wk_ref, wkr_ref, wv_ref, c_ref, s_ref, q_ref, kh_ref, vh_ref, kc_ref, vc_ref):
        h = _modulate(x_ref, mod_ref, 0, 1).astype(BF16)
        cc, ss = c_ref[0], s_ref[0]
        q = _dot(h, wq_ref[...]) * _tile_lanes(cc, d) + _dot(h, wqr_ref[...]) * _tile_lanes(ss, d)
        k = _dot(h, wk_ref[...]) * _tile_lanes(cc, kvw) + _dot(h, wkr_ref[...]) * _tile_lanes(ss, kvw)
        v = _dot(h, wv_ref[...])
        q_ref[...] = (q * QSCALE_64).astype(BF16)
        _store_heads(kh_ref, k, hd)
        _store_heads(vh_ref, v, hd)

        @pl.when(pl.program_id(0) < n_ctx_tiles)
        def _():
            kc_ref[...] = k
            vc_ref[...] = v

    q, kh, vh, k_new, v_new = rc(
        body_a, t_all,
        [(x, "row"), (mods[i], "group"), (wq.astype(BF16), "full"), (_rot_cols(wq, hd).astype(BF16), "full"),
         (wk.astype(BF16), "full"), (_rot_cols(wk, hd).astype(BF16), "full"), (wv.astype(BF16), "full"),
         (cq, "pos"), (sq, "pos")],
        [(d, BF16), (A_KV_HEADS, hd, BF16, "heads"), (A_KV_HEADS, hd, BF16, "heads"),
         (gl, kvw, F32, "ctx"), (gl, kvw, F32, "ctx")], name="proj_a")
    out_a_k = k_new.reshape(bc, 1, lc, A_KV_HEADS, hd)
    out_a_v = v_new.reshape(bc, 1, lc, A_KV_HEADS, hd)
    common = dict(n_heads=A_HEADS, dq=hd, dv=hd, hpb=4, kvpb=1, tq=tq, sink=a_sink[0])
    att_c = attention(q, kh, vh, n_seq=bc, lq=lc, row_off=0, win=lc, name="attn_a_ctx", **common)
    win_a = 2 * tq
    nq_l = gl // tq
    qi = jnp.arange(tq)[:, None]
    ki = jnp.arange(win_a)[None, :]
    band = jnp.stack([jnp.where(jnp.abs(ki - (qi + off)) <= A_WINDOW, 0.0, NEG_INF)
                      for off in (0, A_WINDOW, 2 * A_WINDOW)]).astype(F32)[None]
    att_l = attention(q, kh, vh, n_seq=bl, lq=gl, row_off=gl, win=win_a,
                      start_fn=lambda ii: jnp.clip(ii * tq - A_WINDOW, 0, gl - win_a),
                      kc=_cache_heads_major(cache_a_k[:, 0]), vc=_cache_heads_major(cache_a_v[:, 0]),
                      bias=band, type_fn=lambda ii: jnp.where(ii == 0, 0, jnp.where(ii == nq_l - 1, 2, 1)),
                      name="attn_a_lat", **common)
    x = finish_layer(i, jnp.concatenate([att_c, att_l], axis=0), a_wo[0], x)

    i = 1
    hq = B_NOPE + B_ROPE
    qw = B_HEADS * B_QPAD
    wqb = b_wq_b[0].reshape(B_Q_RANK, B_HEADS, hq)
    wqb_rot = jnp.concatenate([jnp.zeros_like(wqb[..., :B_NOPE]),
                               _rot_cols(wqb[..., B_NOPE:].reshape(B_Q_RANK, -1), B_ROPE).reshape(B_Q_RANK, B_HEADS, B_ROPE)],
                              axis=-1)
    padq = lambda wz: jnp.pad(wz, ((0, 0), (0, 0), (0, B_QPAD - hq))).reshape(B_Q_RANK, qw).astype(BF16)
    wkv_c, wkv_r = b_wkv_a[0][:, :B_KV_RANK], b_wkv_a[0][:, B_KV_RANK:]
    cqb, sqb = _rope_tables(gl, B_ROPE, B_NOPE, B_QPAD - hq, 1)
    ckr, skr = _rope_tables(gl, B_ROPE, 0, 0, 1)

    def body_b(x_ref, mod_ref, wqa_ref, qn_ref, wqb_ref, wqbr_ref, wc_ref, kn_ref, wr_ref, wrr_ref, wkb_ref, wvb_ref,
               cq_ref, sq_ref, ck_ref, sk_ref, q_ref, kh_ref, vh_ref, ckv_ref, kr_ref):
        h = _modulate(x_ref, mod_ref, 0, 1).astype(BF16)
        qa = _rms(_dot(h, wqa_ref[...]), qn_ref[...]).astype(BF16)
        q = (_dot(qa, wqb_ref[...]) * _tile_lanes(cq_ref[0], qw) + _dot(qa, wqbr_ref[...]) * _tile_lanes(sq_ref[0], qw))
        q_ref[...] = (q * QSCALE_MLA).astype(BF16)
        ckv = _rms(_dot(h, wc_ref[...]), kn_ref[...])
        kr = _dot(h, wr_ref[...]) * ck_ref[0] + _dot(h, wrr_ref[...]) * sk_ref[0]
        cb = ckv.astype(BF16)
        _store_mla_keys(kh_ref, _dot(cb, wkb_ref[...]), kr)
        _store_heads(vh_ref, _dot(cb, wvb_ref[...]), B_VDIM)

        @pl.when(pl.program_id(0) < n_ctx_tiles)
        def _():
            ckv_ref[...] = ckv
            kr_ref[...] = kr

    wkb, wvb = b_wk_b[0].astype(BF16), b_wv_b[0].astype(BF16)
    q, kh, vh, ckv_new, kr_new = rc(
        body_b, t_all,
        [(x, "row"), (mods[i], "group"), (b_wq_a[0].astype(BF16), "full"), (row2(b_q_norm[0]), "full"),
         (padq(wqb), "full"), (padq(wqb_rot), "full"), (wkv_c.astype(BF16), "full"), (row2(b_kv_norm[0]), "full"),
         (wkv_r.astype(BF16), "full"), (_rot_cols(wkv_r, B_ROPE).astype(BF16), "full"), (wkb, "full"), (wvb, "full"),
         (cqb, "pos"), (sqb, "pos"), (ckr, "pos"), (skr, "pos")],
        [(qw, BF16), (B_HEADS, B_QPAD, BF16, "heads"), (B_HEADS, B_VDIM, BF16, "heads"),
         (gl, B_KV_RANK, F32, "ctx"), (gl, B_ROPE, F32, "ctx")], name="proj_b")
    out_b_ckv = ckv_new.reshape(bc, 1, lc, B_KV_RANK)
    out_b_krope = kr_new.reshape(bc, 1, lc, B_ROPE)

    def body_bc(c_ref, r_ref, wkb_ref, wvb_ref, kh_ref, vh_ref):
        cb = c_ref[...].astype(BF16)
        _store_mla_keys(kh_ref, _dot(cb, wkb_ref[...]), r_ref[...])
        _store_heads(vh_ref, _dot(cb, wvb_ref[...]), B_VDIM)

    n_pc = bl * past
    kh_p, vh_p = row_call(body_bc, n_pc,
                          [(cache_b_ckv[:, 0].reshape(n_pc, B_KV_RANK), "row"), (cache_b_krope[:, 0].reshape(n_pc, B_ROPE), "row"),
                           (wkb, "full"), (wvb, "full")],
                          [(B_HEADS, B_QPAD, BF16, "heads"), (B_HEADS, B_VDIM, BF16, "heads")],
                          tm=min(ROW_TILE, n_pc), name="proj_b_past")
    common = dict(n_heads=B_HEADS, dq=B_QPAD, dv=B_VDIM, hpb=2, kvpb=2, tq=tq)
    att_c = attention(q, kh, vh, n_seq=bc, lq=lc, row_off=0, win=lc, name="attn_b_ctx", **common)
    att_l = attention(q, kh, vh, n_seq=bl, lq=gl, row_off=gl, win=gl, kc=kh_p, vc=vh_p, name="attn_b_lat", **common)
    x = finish_layer(i, jnp.concatenate([att_c, att_l], axis=0), b_wo[0], x)

    i = 2
    d3 = 3 * d

    def body_c(x_ref, mod_ref, w_ref, b_ref, u_ref):
        h = _modulate(x_ref, mod_ref, 0, 1).astype(BF16)
        for j in range(3):
            u_ref[:, j * d:(j + 1) * d] = _dot(h, w_ref[:, j * d:(j + 1) * d]) + b_ref[:, j * d:(j + 1) * d]

    (u,) = rc(body_c, t_all, [(x, "row"), (mods[i], "group"), (c_w_in[0].astype(BF16), "full"), (row2(c_b_in[0]), "full")],
              [(d3, F32)], name="proj_c")
    u_c = short_conv(u, c_conv_w[0], row2(c_conv_b[0]), row_off=0, n_seq=bc, length=lc, parts=3, name="short_conv_ctx")
    u_l = short_conv(u, c_conv_w[0], row2(c_conv_b[0]), row_off=gl, n_seq=bl, length=gl, parts=3, name="short_conv_lat")
    fargs = (c_ffn_w1[0], c_ffn_b1[0], c_ffn_w2[0], c_ffn_b2[0], c_ffn_w3[0], c_ffn_freq[0], c_log_decay[0])
    filt_c = hyena_filters(lc, *fargs, name="filters_ctx").reshape(lc, 2, 2, d)
    filt_l = hyena_filters(gl, *fargs, name="filters_lat").reshape(gl, 2, 2, d)

    nc = 2 * lc
    mf_data, mf_filt, mi_c = _dft_mats_single(nc, lc, lc)
    g_c = jnp.stack([_two_sided(filt_c, o) for o in range(2)])
    spec_c = spectral(g_c.reshape(2, 1, 1, nc, d), mf_filt, dt=512, name="filt_spec_ctx")
    z_c = u_c[0].reshape(bc // 2, 2, 1, lc, d)
    for o in range(2):
        gate = u_c[o + 1].reshape(bc // 2, 2, 1, lc, d)
        z_c = spectral(z_c, mf_data, spec_c[o], mi_c, dt=512, epi=(gate, z_c, row2(c_skip[0, o])), name=f"conv_ctx{o}")
    zc_out = z_c.reshape(gl, d)

    nl = 2 * gl
    n1 = nl // FFT_N2
    a_data, a_filt, a_inv, mf_l, mi_l = _dft_mats_two_stage(nl, FFT_N2)
    cols = FFT_N2 * d
    g_l = jnp.stack([_two_sided(filt_l, o) for o in range(2)])
    ga = mm3(a_filt, g_l.reshape(2, n1, cols), name="filt_stage_a")
    spec_l = spectral(ga.reshape(2, 2, n1, FFT_N2, d), mf_l, dt=1024, name="filt_spec_lat")
    z_l = u_l[0]
    tn = 2048
    for o in range(2):
        za = mm3(a_data, z_l.reshape(bl // 2, n1, cols), name=f"conv_lat_a{o}")
        zb = spectral(za.reshape(bl // 2, 2, n1, FFT_N2, d), mf_l, spec_l[o], mi_l, dt=1024, name=f"conv_lat_c{o}")
        gate = u_l[o + 1].reshape(bl // 2, n1, cols)
        skip_t = jnp.tile(row2(c_skip[0, o]), (1, tn // d))
        z_l = mm3(a_inv, zb.reshape(bl // 2, 2 * n1, cols), tn=tn,
                  epi=(gate, z_l.reshape(bl // 2, n1, cols), skip_t), name=f"conv_lat_i{o}").reshape(bl, gl, d)
    zz = jnp.concatenate([zc_out, z_l.reshape(bl * gl, d)], axis=0)
    x = finish_layer(i, zz, c_wo[0], x)

    i = 3

    def body_d(x_ref, mod_ref, wq_ref, wk_ref, wv_ref, q_ref, kh_ref, vh_ref, kc_ref, vc_ref):
        h = _modulate(x_ref, mod_ref, 0, 1).astype(BF16)
        q_ref[...] = (_dot(h, wq_ref[...]) * QSCALE_64).astype(BF16)
        k = _dot(h, wk_ref[...])
        v = _dot(h, wv_ref[...])
        _store_heads(kh_ref, k, hd)
        _store_heads(vh_ref, v, hd)

        @pl.when(pl.program_id(0) < n_ctx_tiles)
        def _():
            kc_ref[...] = k
            vc_ref[...] = v

    q, k_heads, v_heads, k_new, v_new = rc(
        body_d, t_all,
        [(x, "row"), (mods[i], "group"), (d_wq[0].astype(BF16), "full"), (d_wk[0].astype(BF16), "full"),
         (d_wv[0].astype(BF16), "full")],
        [(d, BF16), (D_HEADS, hd, BF16, "heads"), (D_HEADS, hd, BF16, "heads"), (gl, d, F32, "ctx"), (gl, d, F32, "ctx")],
        name="proj_d")
    out_d_k = k_new.reshape(bc, 1, lc, D_HEADS, hd)
    out_d_v = v_new.reshape(bc, 1, lc, D_HEADS, hd)
    common = dict(n_heads=D_HEADS, dq=hd, dv=hd, hpb=2, kvpb=2, tq=tq)
    att_c = attention(q, k_heads, v_heads, n_seq=bc, lq=lc, row_off=0, win=lc, name="attn_d_ctx", **common)
    rows = gl // GRID_W
    kh = min(MAX_NBR_ROWS, rows)
    qrows = tq // GRID_W
    krows = qrows + kh
    win_d = krows * GRID_W
    nq_l = gl // tq
    qr_l, kr_l = jnp.arange(qrows)[:, None], jnp.arange(krows)[None, :]
    qc, kc_ = jnp.arange(GRID_W)[:, None], jnp.arange(GRID_W)[None, :]
    c0 = jnp.clip(qc - NBR_COLS // 2, 0, GRID_W - NBR_COLS)
    col_ok = (kc_ >= c0) & (kc_ < c0 + NBR_COLS)
    dc = jnp.clip(kc_ - qc, 1 - NBR_COLS, NBR_COLS - 1) + NBR_COLS - 1
    dc_hot = (dc[..., None] == jnp.arange(2 * NBR_COLS - 1)).astype(F32)
    by_col = jnp.einsum("hrc,xyc->hrxy", d_rel_bias[0], dc_hot, precision=lax.Precision.HIGHEST)
    tabs = []
    for off, lo in ((0, jnp.zeros_like(qr_l)), (kh // 2, qr_l), (kh, jnp.full_like(qr_l, qrows))):
        row_ok = (kr_l >= lo) & (kr_l < lo + kh)
        dr = jnp.clip(kr_l - off - qr_l + MAX_NBR_ROWS - 1, 0, 2 * MAX_NBR_ROWS - 2)
        dr_hot = (dr[..., None] == jnp.arange(2 * MAX_NBR_ROWS - 1)).astype(F32)
        tab = jnp.einsum("qkr,hrxy->hqxky", dr_hot, by_col, precision=lax.Precision.HIGHEST)
        ok = row_ok[:, None, :, None] & col_ok[None, :, None, :]
        tabs.append(jnp.where(ok[None], tab, NEG_INF).reshape(D_HEADS, tq, win_d))
    nbr_bias = jnp.stack(tabs, axis=1).astype(F32)
    att_l = attention(q, k_heads, v_heads, n_seq=bl, lq=gl, row_off=gl, win=win_d,
                      start_fn=lambda ii: jnp.clip(ii * qrows - kh // 2, 0, rows - krows) * GRID_W,
                      kc=_cache_heads_major(cache_d_k[:, 0]), vc=_cache_heads_major(cache_d_v[:, 0]),
                      bias=nbr_bias, type_fn=lambda ii: jnp.where(ii == 0, 0, jnp.where(ii == nq_l - 1, 2, 1)),
                      name="attn_d_lat", **common)
    x = finish_layer(i, jnp.concatenate([att_c, att_l], axis=0), d_wo[0], x)

    y_prompt = x[:gl].reshape(bc, lc, d)
    y_sample = x[gl:].reshape(bl, gl, d)
    return (y_prompt, y_sample, out_a_k, out_a_v, out_b_ckv, out_b_krope, out_d_k, out_d_v)
```

```python
import functools

import jax
import jax.numpy as jnp
import numpy as np
from jax import lax
from jax.experimental import pallas as pl
from jax.experimental.pallas import tpu as pltpu

F32 = jnp.float32
BF16 = jnp.bfloat16

GRID_W = 64
HEAD_DIM = 64
ROPE_BASE = 10000.0
LN_EPS = 1e-5
RMS_EPS = 1e-6
NEG_INF = -1e30
DEPTH = 4
DEEPNORM_ALPHA = (2 * DEPTH) ** 0.25
A_HEADS = 16
A_KV_HEADS = 4
A_WINDOW = 128
B_HEADS = 16
B_Q_RANK = 384
B_KV_RANK = 256
B_NOPE = 64
B_ROPE = 32
B_VDIM = 64
B_QPAD = 128
C_POS_BANDS = 16
D_HEADS = 16
MAX_NBR_ROWS = 8
NBR_COLS = 16
N_GROUPS = 4
EXPERTS_PER_GROUP = 8
N_EXPERTS = N_GROUPS * EXPERTS_PER_GROUP
D_EXPERT = 512
MOE_BLK = 512
LANES = 128
LOG2E = 1.4426950408889634
QSCALE_64 = HEAD_DIM ** -0.5 * LOG2E
QSCALE_MLA = (B_NOPE + B_ROPE) ** -0.5 * LOG2E
FFT_N2 = 128
VMEM_LIMIT = 56 * 1024 * 1024
ROW_TILE = 512


def _params(n_axes):
    return pltpu.CompilerParams(dimension_semantics=("arbitrary",) * n_axes, vmem_limit_bytes=VMEM_LIMIT)


def _dot(a, b):
    return jnp.dot(a, b, preferred_element_type=F32)


def _split(x):
    hi = x.astype(BF16)
    lo = (x - hi.astype(F32)).astype(BF16)
    return hi, lo


def _dot3(a, b):
    ah, al = _split(a)
    bh, bl = _split(b)
    return _dot(ah, bh) + _dot(ah, bl) + _dot(al, bh)


def _layer_norm(y, g, b):
    mu = jnp.mean(y, axis=-1, keepdims=True)
    d = y - mu
    var = jnp.mean(d * d, axis=-1, keepdims=True)
    return d * lax.rsqrt(var + LN_EPS) * g + b


def _rms(y, g):
    return y * lax.rsqrt(jnp.mean(y * y, axis=-1, keepdims=True) + RMS_EPS) * g


def _tile_lanes(t, n):
    reps = n // t.shape[-1]
    return t if reps == 1 else jnp.concatenate([t] * reps, axis=-1)


def row_call(body, m_rows, ins, outs, *, tm=ROW_TILE, group_len=None, name=None, scratch=(), params=None):
    nb = None if group_len is None else group_len // tm
    n_tiles = m_rows // tm
    in_specs = []
    for a, kind in ins:
        if kind == "row":
            in_specs.append(pl.BlockSpec((tm, a.shape[1]), lambda i: (i, 0)))
        elif kind == "row_head":
            nh = a.shape[0] // tm
            in_specs.append(pl.BlockSpec((tm, a.shape[1]), lambda i, nh=nh: (jnp.minimum(i, nh - 1), 0)))
        elif kind == "row_tail":
            nh = n_tiles - a.shape[0] // tm
            in_specs.append(pl.BlockSpec((tm, a.shape[1]), lambda i, nh=nh: (jnp.maximum(i - nh, 0), 0)))
        elif kind == "any":
            in_specs.append(pl.BlockSpec(memory_space=pl.ANY))
        elif kind == "smem":
            in_specs.append(pl.BlockSpec(memory_space=pltpu.SMEM))
        elif kind == "smem_rows":
            in_specs.append(pl.BlockSpec((a.shape[0] // n_tiles,), lambda i: (i,), memory_space=pltpu.SMEM))
        elif kind == "full":
            in_specs.append(pl.BlockSpec(a.shape, lambda i, nd=a.ndim: (0,) * nd))
        elif kind == "group":
            in_specs.append(pl.BlockSpec((1,) + a.shape[1:], lambda i: (i // nb, 0, 0)))
        elif kind == "pos":
            in_specs.append(pl.BlockSpec((1, tm, a.shape[2]), lambda i: (jnp.minimum(i // nb, 1), i % nb, 0)))
        else:
            raise ValueError(kind)
    out_specs, out_shape = [], []
    for o in outs:
        if len(o) == 4 and o[3] == "any":
            out_specs.append(pl.BlockSpec(memory_space=pl.ANY))
            out_shape.append(jax.ShapeDtypeStruct((o[0], o[1]), o[2]))
        elif len(o) == 4 and o[3] == "heads":
            out_specs.append(pl.BlockSpec((o[0], tm, o[1]), lambda i: (0, i, 0)))
            out_shape.append(jax.ShapeDtypeStruct((o[0], m_rows, o[1]), o[2]))
        elif len(o) == 4 and o[3] == "ctx":
            nkeep = o[0] // tm
            out_specs.append(pl.BlockSpec((tm, o[1]), lambda i, nkeep=nkeep: (jnp.minimum(i, nkeep - 1), 0)))
            out_shape.append(jax.ShapeDtypeStruct((o[0], o[1]), o[2]))
        else:
            out_specs.append(pl.BlockSpec((tm, o[0]), lambda i: (i, 0)))
            out_shape.append(jax.ShapeDtypeStruct((m_rows, o[0]), o[1]))
    return pl.pallas_call(
        body, grid=(n_tiles,), in_specs=in_specs, out_specs=out_specs, out_shape=out_shape,
        scratch_shapes=list(scratch), compiler_params=params or _params(1), name=name)(*[a for a, _ in ins])


def _modulate(x_ref, mod_ref, shift_row, scale_row):
    m = mod_ref[0]
    return x_ref[...] * (1.0 + m[scale_row:scale_row + 1]) + m[shift_row:shift_row + 1]


def modulation_all(cvec, mod_w, mod_b):
    depth, d, d6 = mod_w.shape
    g = cvec.shape[0]
    gp = -(-g // 16) * 16
    cp = jnp.zeros((gp, d), F32).at[:g].set(cvec)
    tn = 1024

    def body(c_ref, w_ref, b_ref, o_ref):
        c = c_ref[...]
        a = (c * jax.nn.sigmoid(c)).astype(BF16)
        o_ref[0] = _dot(a, w_ref[0].astype(BF16)) + b_ref[0]

    out = pl.pallas_call(
        body, grid=(depth, d6 // tn),
        in_specs=[pl.BlockSpec((gp, d), lambda l, j: (0, 0)),
                  pl.BlockSpec((1, d, tn), lambda l, j: (l, 0, j)),
                  pl.BlockSpec((1, 1, tn), lambda l, j: (l, 0, j))],
        out_specs=pl.BlockSpec((1, gp, tn), lambda l, j: (l, 0, j)),
        out_shape=jax.ShapeDtypeStruct((depth, gp, d6), F32),
        compiler_params=_params(2), name="modulation")(cp, mod_w, mod_b.reshape(depth, 1, d6))
    return out[:, :g].reshape(depth, g, 6, d)


def attention(q, kl, vl, *, n_seq, lq, row_off, n_heads, dq, dv, hpb, kvpb, tq, win,
              start_fn=None, kc=None, vc=None, bias=None, type_fn=None, sink=None, name=None):
    nq = lq // tq
    koff = row_off // lq
    rep = hpb // kvpb
    has_ctx, has_bias, has_sink = kc is not None, bias is not None, sink is not None
    bias_heads = has_bias and bias.shape[0] > 1
    off_blk = row_off // tq

    def kern(*refs):
        it = iter(refs)
        q_ref, kl_ref, vl_ref = next(it), next(it), next(it)
        kc_ref = next(it) if has_ctx else None
        vc_ref = next(it) if has_ctx else None
        b_ref = next(it) if has_bias else None
        s_ref = next(it) if has_sink else None
        o_ref = next(it)
        i = pl.program_id(2)
        hb = pl.program_id(1)
        if start_fn is None:
            start = 0
        else:
            start = pl.multiple_of(start_fn(i), 64)
        outs = []
        for j in range(hpb):
            kv = j // rep
            qj = q_ref[:, j * dq:(j + 1) * dq]
            k = kl_ref[kv, pl.ds(start, win), :]
            v = vl_ref[kv, pl.ds(start, win), :]
            s = lax.dot_general(qj, k, (((1,), (1,)), ((), ())), preferred_element_type=F32)
            if has_bias:
                s = s + b_ref[j if bias_heads else 0, 0]
            m = jnp.max(s, axis=1, keepdims=True)
            if has_ctx:
                sc = lax.dot_general(qj, kc_ref[kv], (((1,), (1,)), ((), ())), preferred_element_type=F32)
                m = jnp.maximum(m, jnp.max(sc, axis=1, keepdims=True))
            if has_sink:
                sk = s_ref[hb * hpb + j]
                m = jnp.maximum(m, sk)
            p = jnp.exp2(s - m)
            l = jnp.sum(p, axis=1, keepdims=True)
            acc = _dot(p.astype(BF16), v)
            if has_ctx:
                pc = jnp.exp2(sc - m)
                l = l + jnp.sum(pc, axis=1, keepdims=True)
                acc = acc + _dot(pc.astype(BF16), vc_ref[kv])
            if has_sink:
                l = l + jnp.exp2(sk - m)
            outs.append(acc / l)
        o_ref[...] = jnp.concatenate(outs, axis=1).astype(o_ref.dtype)

    ins = [q, kl, vl]
    in_specs = [pl.BlockSpec((tq, hpb * dq), lambda b, h, i: (off_blk + b * nq + i, h)),
                pl.BlockSpec((kvpb, lq, dq), lambda b, h, i: (h, koff + b, 0)),
                pl.BlockSpec((kvpb, lq, dv), lambda b, h, i: (h, koff + b, 0))]
    if has_ctx:
        lc = kc.shape[1] // n_seq
        ins += [kc, vc]
        in_specs += [pl.BlockSpec((kvpb, lc, dq), lambda b, h, i: (h, b, 0)),
                     pl.BlockSpec((kvpb, lc, dv), lambda b, h, i: (h, b, 0))]
    if has_bias:
        ins.append(bias * LOG2E)
        hb_blk = hpb if bias_heads else 1
        in_specs.append(pl.BlockSpec((hb_blk, 1, tq, win),
                                     lambda b, h, i: (h if bias_heads else 0, type_fn(i), 0, 0)))
    if has_sink:
        ins.append(sink.astype(F32) * LOG2E)
        in_specs.append(pl.BlockSpec(memory_space=pltpu.SMEM))
    return pl.pallas_call(
        kern, grid=(n_seq, n_heads // hpb, nq), in_specs=in_specs,
        out_specs=pl.BlockSpec((tq, hpb * dv), lambda b, h, i: (b * nq + i, h)),
        out_shape=jax.ShapeDtypeStruct((n_seq * lq, n_heads * dv), BF16),
        compiler_params=_params(3), name=name)(*ins)


def _cache_heads_major(c):
    n_seq, length, n_heads, d = c.shape
    return c.astype(BF16).transpose(2, 0, 1, 3).reshape(n_heads, n_seq * length, d)


def _store_mla_keys(ref, k_nope, k_rope):
    rows = k_nope.shape[0]
    tail = jnp.concatenate([k_rope, jnp.zeros((rows, B_QPAD - B_NOPE - B_ROPE), k_rope.dtype)], axis=1)
    for hh in range(ref.shape[0]):
        ref[hh] = jnp.concatenate([k_nope[:, hh * B_NOPE:(hh + 1) * B_NOPE], tail], axis=1).astype(ref.dtype)


def _store_heads(ref, val, width):
    for hh in range(ref.shape[0]):
        ref[hh] = val[:, hh * width:(hh + 1) * width].astype(ref.dtype)


def _rope_tables(length, dim, lead, tail, reps):
    half = dim // 2
    nf = half // 2
    t = jnp.arange(length)
    row = (t // GRID_W).astype(F32)
    col = (t % GRID_W).astype(F32)
    inv = ROPE_BASE ** (-jnp.arange(nf, dtype=F32) / nf)
    ang = jnp.concatenate([row[:, None] * inv, col[:, None] * inv], axis=-1)
    cos, sin = jnp.cos(ang), jnp.sin(ang)
    c = jnp.concatenate([jnp.ones((length, lead), F32), cos, cos, jnp.ones((length, tail), F32)] * reps, axis=1)
    s = jnp.concatenate([jnp.zeros((length, lead), F32), sin, sin, jnp.zeros((length, tail), F32)] * reps, axis=1)
    return (jnp.stack([jnp.ones_like(c), c]), jnp.stack([jnp.zeros_like(s), s]))


def _rot_cols(w, dim):
    k, n = w.shape
    wb = w.reshape(k, n // dim, dim)
    half = dim // 2
    return jnp.concatenate([-wb[..., half:], wb[..., :half]], axis=-1).reshape(k, n)


def _cs(phase, n, sign):
    ang = (2.0 * np.pi / n) * (phase % n).astype(F32)
    return jnp.cos(ang), sign * jnp.sin(ang)


def _cblock(wr, wi):
    return jnp.concatenate([jnp.concatenate([wr, -wi], axis=-1), jnp.concatenate([wi, wr], axis=-1)], axis=-2)


SUB = 8


def slow_stage(a, x, *, epi=None, name=None):
    p_n, k, s_n, d = x.shape
    m = a.shape[0]
    x5 = x.reshape(p_n, k, s_n // SUB, SUB, d)

    def kern(*refs):
        if epi is None:
            a_ref, x_ref, o_ref = refs
        else:
            a_ref, x_ref, g_ref, z_ref, s_ref, o_ref = refs
        xt = pltpu.einshape("ksd->skd", x_ref[...])
        av = a_ref[...]
        y = jnp.stack([_dot3(av, xt[s]) for s in range(SUB)], axis=0)
        y = pltpu.einshape("smd->msd", y)
        if epi is not None:
            y = g_ref[...] * (y + s_ref[...][None] * z_ref[...])
        o_ref[...] = y

    blk = lambda rows: pl.BlockSpec((None, rows, None, SUB, d), lambda p, t: (p, 0, t, 0, 0))
    ins = [a, x5]
    in_specs = [pl.BlockSpec((m, k), lambda p, t: (0, 0)), blk(k)]
    if epi is not None:
        gate, z, skip = epi
        ins += [gate.reshape(p_n, m, s_n // SUB, SUB, d), z.reshape(p_n, m, s_n // SUB, SUB, d), skip]
        in_specs += [blk(m), blk(m), pl.BlockSpec((1, d), lambda p, t: (0, 0))]
    out = pl.pallas_call(
        kern, grid=(p_n, s_n // SUB), in_specs=in_specs, out_specs=blk(m),
        out_shape=jax.ShapeDtypeStruct((p_n, m, s_n // SUB, SUB, d), F32),
        compiler_params=_params(2), name=name)(*ins)
    return out.reshape(p_n, m, s_n, d)


def spectral(x, mf, g=None, mi=None, *, dt, g_index=0, epi=None, name=None):
    p_n, planes, k1_n, nin, d = x.shape
    nf = mf.shape[1] // 2
    nout = nf if g is None else mi.shape[1] // 2

    def kern(*refs):
        it = iter(refs)
        x_ref, mf_ref = next(it), next(it)
        g_ref = next(it) if g is not None else None
        mi_ref = next(it) if g is not None else None
        if epi is not None:
            gate_ref, z_ref, skip_ref = next(it), next(it), next(it)
        o_ref = next(it)
        xs = [x_ref[0, pp, 0] for pp in range(planes)]
        xin = xs[0] if planes == 1 else jnp.concatenate(xs, axis=0)
        f = _dot3(mf_ref[0], xin)
        if g is not None:
            fr, fi = f[:nf], f[nf:]
            gr, gi = g_ref[0, 0, 0], g_ref[0, 1, 0]
            y = jnp.concatenate([fr * gr - fi * gi, fr * gi + fi * gr], axis=0)
            f = _dot3(mi_ref[0], y)
        for pp in range(2):
            y = f[pp * nout:(pp + 1) * nout]
            if epi is not None:
                y = gate_ref[0, pp, 0] * (y + skip_ref[...] * z_ref[0, pp, 0])
            o_ref[0, pp, 0] = y

    ins = [x, mf]
    in_specs = [pl.BlockSpec((1, planes, 1, nin, dt), lambda k, j, p: (p, 0, k, 0, j)),
                pl.BlockSpec((1,) + mf.shape[1:], lambda k, j, p: (k, 0, 0))]
    if g is not None:
        ins += [g, mi]
        in_specs += [pl.BlockSpec((1, 2, 1, nf, dt), lambda k, j, p: (g_index, 0, k, 0, j)),
                     pl.BlockSpec((1,) + mi.shape[1:], lambda k, j, p: (k, 0, 0))]
    if epi is not None:
        ins += list(epi)
        in_specs += [pl.BlockSpec((1, 2, 1, nout, dt), lambda k, j, p: (p, 0, k, 0, j)),
                     pl.BlockSpec((1, 2, 1, nout, dt), lambda k, j, p: (p, 0, k, 0, j)),
                     pl.BlockSpec((1, dt), lambda k, j, p: (0, j))]
    return pl.pallas_call(
        kern, grid=(k1_n, d // dt, p_n), in_specs=in_specs,
        out_specs=pl.BlockSpec((1, 2, 1, nout, dt), lambda k, j, p: (p, 0, k, 0, j)),
        out_shape=jax.ShapeDtypeStruct((p_n, 2, k1_n, nout, d), F32),
        compiler_params=_params(3), name=name)(*ins)


def _two_sided(filt, o):
    hf, hb = filt[:, o, 0], filt[:, o, 1]
    return jnp.concatenate([hf[:1] + hb[:1], hf[1:], jnp.zeros_like(hf[:1]), hb[1:][::-1]], axis=0)


def hyena_filters(length, w1, b1, w2, b2, w3, freq, log_decay, name):
    hid = w2.shape[0]
    t = jnp.linspace(0.0, 1.0, length, dtype=F32)[:, None]
    ang = 2.0 * jnp.pi * t * jnp.arange(1, C_POS_BANDS + 1, dtype=F32)
    z = jnp.concatenate([t, jnp.cos(ang), jnp.sin(ang)], axis=-1)
    kpad = LANES - z.shape[1]
    z = jnp.pad(z, ((0, 0), (0, kpad)))
    w1p = jnp.pad(w1, ((0, kpad), (0, 0)))
    ncol = w3.shape[1]
    tn = 256

    def kern(z_ref, w1_ref, b1_ref, w2_ref, b2_ref, f_ref, w3_ref, ld_ref, o_ref):
        fr = f_ref[...]
        a = jnp.sin(fr * (_dot3(z_ref[...], w1_ref[...]) + b1_ref[...]))
        a = jnp.sin(fr * (_dot3(a, w2_ref[...]) + b2_ref[...]))
        filt = _dot3(a, w3_ref[...])
        tt = lax.broadcasted_iota(jnp.int32, filt.shape, 0).astype(F32) * (1.0 / (length - 1))
        filt = filt * jnp.exp(-jnp.exp(ld_ref[...]) * tt)
        o_ref[...] = filt / (jnp.sum(jnp.abs(filt), axis=0, keepdims=True) + 1e-6)

    full = lambda a: pl.BlockSpec(a.shape, lambda j: (0, 0))
    args = [z, w1p, b1.reshape(1, hid), w2, b2.reshape(1, hid), freq.reshape(1, hid)]
    return pl.pallas_call(
        kern, grid=(ncol // tn,),
        in_specs=[full(a) for a in args] + [pl.BlockSpec((hid, tn), lambda j: (0, j)),
                                            pl.BlockSpec((1, tn), lambda j: (0, j))],
        out_specs=pl.BlockSpec((length, tn), lambda j: (0, j)),
        out_shape=jax.ShapeDtypeStruct((length, ncol), F32),
        compiler_params=_params(1), name=name)(*args, w3, log_decay.reshape(1, ncol))


def short_conv(u, w, b, *, row_off, n_seq, length, parts, name):
    c = u.shape[1]
    dt = 256
    per = c // parts // dt
    first = row_off // length

    def kern(u_ref, w_ref, b_ref, *o_refs):
        x = u_ref[...]
        r = lax.broadcasted_iota(jnp.int32, x.shape, 0)
        prev = jnp.where(r == 0, 0.0, pltpu.roll(x, 1, 0))
        nxt = jnp.where(r == length - 1, 0.0, pltpu.roll(x, length - 1, 0))
        wv = w_ref[...]
        y = prev * wv[0:1] + x * wv[1:2] + nxt * wv[2:3] + b_ref[...]
        part = pl.program_id(1) // per
        for k, o_ref in enumerate(o_refs):
            @pl.when(part == k)
            def _(o_ref=o_ref):
                o_ref[0] = y

    out_spec = lambda k: pl.BlockSpec((1, length, dt), lambda s, j: (s, 0, jnp.clip(j - k * per, 0, per - 1)))
    return pl.pallas_call(
        kern, grid=(n_seq, c // dt),
        in_specs=[pl.BlockSpec((length, dt), lambda s, j: (first + s, j)),
                  pl.BlockSpec((3, dt), lambda s, j: (0, j)),
                  pl.BlockSpec((1, dt), lambda s, j: (0, j))],
        out_specs=[out_spec(k) for k in range(parts)],
        out_shape=[jax.ShapeDtypeStruct((n_seq, length, c // parts), F32)] * parts,
        compiler_params=_params(2), name=name)(u, w, b)


def _dft_mats_single(n, nin_data, nout):
    k = jnp.arange(n, dtype=jnp.int32)
    fr, fi = _cs(k[:, None] * k[None, :nin_data], n, -1.0)
    mf_data = _cblock(fr, fi)[None]
    gr, gi = _cs(k[:, None] * k[None, :], n, -1.0)
    mf_filt = jnp.concatenate([gr, gi], axis=0)[None]
    ir, ii = _cs(k[:nout, None] * k[None, :], n, 1.0)
    mi = _cblock(ir / n, ii / n)[None]
    return mf_data, mf_filt, mi


def _dft_mats_two_stage(n, n2):
    n1 = n // n2
    k1 = jnp.arange(n1, dtype=jnp.int32)
    t1h = jnp.arange(n1 // 2, dtype=jnp.int32)
    ar, ai = _cs((n // n1) * k1[:, None] * t1h[None, :], n, -1.0)
    a_data = _cblock(ar, ai)
    fr, fi = _cs((n // n1) * k1[:, None] * k1[None, :], n, -1.0)
    a_filt = jnp.concatenate([fr, fi], axis=0)
    br, bi = _cs((n // n1) * t1h[:, None] * k1[None, :], n, 1.0)
    a_inv = _cblock(br, bi)
    t2 = jnp.arange(n2, dtype=jnp.int32)
    kk = k1[:, None, None] + n1 * t2[None, :, None]
    mr, mi_ = _cs(kk * t2[None, None, :], n, -1.0)
    mf = _cblock(mr, mi_)
    vr, vi = _cs(jnp.swapaxes(kk, 1, 2) * t2[None, :, None], n, 1.0)
    mi = _cblock(vr / n, vi / n)
    return a_data, a_filt, a_inv, mf, mi


def _route(logits):
    lane = lax.broadcasted_iota(jnp.int32, logits.shape, 1).astype(F32)
    big = 1e9
    lg = jnp.where(lane < N_GROUPS, logits, -jnp.inf)
    mg = jnp.max(lg, axis=1, keepdims=True)
    gi = jnp.min(jnp.where(lg == mg, lane, big), axis=1, keepdims=True)
    p_g = 1.0 / jnp.sum(jnp.exp(lg - mg), axis=1, keepdims=True)
    lo = N_GROUPS + EXPERTS_PER_GROUP * gi
    le = jnp.where((lane >= lo) & (lane < lo + EXPERTS_PER_GROUP), logits, -jnp.inf)
    m1 = jnp.max(le, axis=1, keepdims=True)
    i1 = jnp.min(jnp.where(le == m1, lane, big), axis=1, keepdims=True)
    le2 = jnp.where(lane == i1, -jnp.inf, le)
    m2 = jnp.max(le2, axis=1, keepdims=True)
    i2 = jnp.min(jnp.where(le2 == m2, lane, big), axis=1, keepdims=True)
    e2 = jnp.exp(m2 - m1)
    w1 = p_g / (1.0 + e2)
    w2 = p_g * e2 / (1.0 + e2)
    return lane, i1 - N_GROUPS, i2 - N_GROUPS, w1, w2


def _rank_in_tile(lane, e1, e2):
    tm = lane.shape[0]
    picks = jnp.where(lane == e1, 1.0, 0.0) + jnp.where(lane == e2, 1.0, 0.0)
    r = lax.broadcasted_iota(jnp.int32, (tm, tm), 0)
    c = lax.broadcasted_iota(jnp.int32, (tm, tm), 1)
    earlier = jnp.where(c < r, 1.0, 0.0).astype(BF16)
    return _dot(earlier, picks.astype(BF16)), picks


def expert_mlp(xs, blk_e, n_used, w_gate, w_up, w_down):
    p_rows, d = xs.shape
    de = w_gate.shape[2]
    nblk = p_rows // MOE_BLK

    def kern(be_ref, nu_ref, x_ref, wg_ref, wu_ref, wd_ref, o_ref, wg_s, wu_s, wd_s):
        i = pl.program_id(0)
        used = i < nu_ref[0]
        fresh = jnp.logical_or(i == 0, be_ref[i] != be_ref[jnp.maximum(i - 1, 0)])

        @pl.when(jnp.logical_and(used, fresh))
        def _():
            wg_s[...] = wg_ref[0].astype(BF16)
            wu_s[...] = wu_ref[0].astype(BF16)
            wd_s[...] = wd_ref[0].astype(BF16)

        @pl.when(used)
        def _():
            x = x_ref[...].astype(BF16)
            g = _dot(x, wg_s[...])
            u = _dot(x, wu_s[...])
            a = (g * jax.nn.sigmoid(g) * u).astype(BF16)
            o_ref[...] = _dot(a, wd_s[...])

        @pl.when(jnp.logical_not(used))
        def _():
            o_ref[...] = jnp.zeros_like(o_ref)

    last = lambda i, nu: jnp.minimum(i, nu[0] - 1)
    grid_spec = pltpu.PrefetchScalarGridSpec(
        num_scalar_prefetch=2, grid=(nblk,),
        in_specs=[pl.BlockSpec((MOE_BLK, d), lambda i, be, nu: (last(i, nu), 0)),
                  pl.BlockSpec((1, d, de), lambda i, be, nu: (be[last(i, nu)], 0, 0)),
                  pl.BlockSpec((1, d, de), lambda i, be, nu: (be[last(i, nu)], 0, 0)),
                  pl.BlockSpec((1, de, d), lambda i, be, nu: (be[last(i, nu)], 0, 0))],
        out_specs=pl.BlockSpec((MOE_BLK, d), lambda i, be, nu: (i, 0)),
        scratch_shapes=[pltpu.VMEM((d, de), BF16), pltpu.VMEM((d, de), BF16), pltpu.VMEM((de, d), BF16)])
    return pl.pallas_call(
        kern, grid_spec=grid_spec, out_shape=jax.ShapeDtypeStruct((p_rows, d), F32),
        compiler_params=_params(1), name="expert_mlp")(blk_e, n_used, xs, w_gate, w_up, w_down)


def _slot_plan(experts, ranks):
    n = 2 * experts.shape[0]
    ids = jnp.arange(N_EXPERTS, dtype=jnp.int32)
    onehot = experts[..., None] == ids
    counts = jnp.sum(onehot, axis=(0, 1)).astype(jnp.int32)
    padded = (counts + MOE_BLK - 1) // MOE_BLK * MOE_BLK
    pend = jnp.cumsum(padded)
    pstart = pend - padded
    p_rows = -(-n // MOE_BLK) * MOE_BLK + N_EXPERTS * MOE_BLK
    nblk = p_rows // MOE_BLK
    blk_first = jnp.arange(nblk, dtype=jnp.int32) * MOE_BLK
    blk_e = jnp.minimum(jnp.sum(pend[None, :] <= blk_first[:, None], axis=1), N_EXPERTS - 1).astype(jnp.int32)
    n_used = (pend[-1:] // MOE_BLK).astype(jnp.int32)
    blk_ids = jnp.arange(nblk, dtype=jnp.int32)
    partly = jnp.any((blk_ids[:, None] == (pend // MOE_BLK - 1)[None, :]) & (counts % MOE_BLK != 0)[None, :], axis=1)
    zero_blk = (partly | (blk_ids >= n_used[0])).astype(jnp.int32)
    slot = ranks + jnp.sum(jnp.where(onehot, pstart, 0), axis=-1)
    return slot.reshape(n).astype(jnp.int32), blk_e, n_used, zero_blk, p_rows


def _dma_params():
    return pltpu.CompilerParams(dimension_semantics=("arbitrary",), vmem_limit_bytes=VMEM_LIMIT,
                                disable_bounds_checks=True)


DMA_UNROLL = 8


def moe_dispatch(x1, mod, slot, zero_blk, p_rows, group_len, name):
    t, d = x1.shape
    tm = ROW_TILE

    def body(slot_ref, zb_ref, x_ref, mod_ref, xs_ref, h_ref, zero_ref, sem):
        h_ref[...] = _modulate(x_ref, mod_ref, 3, 4)

        @pl.when(pl.program_id(0) == 0)
        def _():
            zero_ref[...] = jnp.zeros_like(zero_ref)

            def zstart(b, c):
                @pl.when(zb_ref[b] != 0)
                def _():
                    first = pl.multiple_of(b * MOE_BLK, MOE_BLK)
                    pltpu.make_async_copy(zero_ref, xs_ref.at[pl.ds(first, MOE_BLK)], sem).start()
                return c

            def zwait(b, c):
                @pl.when(zb_ref[b] != 0)
                def _():
                    pltpu.make_async_copy(zero_ref, xs_ref.at[pl.ds(0, MOE_BLK)], sem).wait()
                return c

            lax.fori_loop(0, p_rows // MOE_BLK, zstart, 0)
            lax.fori_loop(0, p_rows // MOE_BLK, zwait, 0)

        def start(rr, c):
            for u in range(DMA_UNROLL):
                r = rr * DMA_UNROLL + u
                for k in range(2):
                    pltpu.make_async_copy(h_ref.at[pl.ds(r, 1)], xs_ref.at[pl.ds(slot_ref[2 * r + k], 1)], sem).start()
            return c

        lax.fori_loop(0, tm // DMA_UNROLL, start, 0)
        for k in range(2):
            pltpu.make_async_copy(h_ref, xs_ref.at[pl.ds(0, tm)], sem).wait()

    return row_call(body, t, [(slot, "smem_rows"), (zero_blk, "smem"), (x1, "row"), (mod, "group")],
                    [(p_rows, d, F32, "any")], group_len=group_len,
                    scratch=[pltpu.VMEM((tm, d), F32), pltpu.VMEM((MOE_BLK, d), F32), pltpu.SemaphoreType.DMA(())],
                    params=_dma_params(), name=name)[0]


def kernel(x_prompt, x_sample, cache_a_k, cache_a_v, cache_b_ckv, cache_b_krope, cache_d_k, cache_d_v, c_ctx, c, mod_w, mod_b, ln_g, ln_b, a_wq, a_wk, a_wv, a_wo, a_sink, b_wq_a, b_q_norm, b_wq_b, b_wkv_a, b_kv_norm, b_wk_b, b_wv_b, b_wo, c_w_in, c_b_in, c_conv_w, c_conv_b, c_ffn_w1, c_ffn_b1, c_ffn_w2, c_ffn_b2, c_ffn_w3, c_ffn_freq, c_log_decay, c_skip, c_wo, d_wq, d_wk, d_wv, d_wo, d_rel_bias, moe_wr_g, moe_br_g, moe_wr_e, moe_br_e, moe_w_gate, moe_w_up, moe_w_down):
    bc, lc, d = x_prompt.shape
    bl, ll, _ = x_sample.shape
    past = cache_a_k.shape[2]
    gl = ll
    assert bc * lc == gl and d == A_HEADS * HEAD_DIM
    ng = 1 + bl
    t_all = ng * gl
    x = jnp.concatenate([x_prompt.reshape(gl, d), x_sample.reshape(bl * gl, d)], axis=0)
    cvec = jnp.concatenate([c_ctx[None, :], c], axis=0)
    mods = modulation_all(cvec, mod_w, mod_b)
    rc = functools.partial(row_call, group_len=gl)
    row2 = lambda v: v.reshape(1, -1)
    n_ctx_tiles = gl // ROW_TILE

    def post_mixer(i, att_ctx, att_lat, wo, x_in):
        unused = LANES - N_GROUPS - N_EXPERTS
        wr = jnp.concatenate([moe_wr_g[i], moe_wr_e[i], jnp.zeros((d, unused), F32)], axis=1)
        br = jnp.concatenate([moe_br_g[i], moe_br_e[i], jnp.zeros((unused,), F32)])[None, :]

        def body(attc_ref, attl_ref, x_ref, mod_ref, wo_ref, g_ref, b_ref, wr_ref, br_ref, x1_ref, info_ref, w_ref, seen_ref):
            @pl.when(pl.program_id(0) == 0)
            def _():
                seen_ref[...] = jnp.zeros_like(seen_ref)

            m = mod_ref[0]
            att = jnp.where(pl.program_id(0) < n_ctx_tiles, attc_ref[...], attl_ref[...])
            o = _dot(att.astype(BF16), wo_ref[...])
            x1 = _layer_norm(DEEPNORM_ALPHA * x_ref[...] + m[2:3] * o, g_ref[...], b_ref[...])
            x1_ref[...] = x1
            h = x1 * (1.0 + m[4:5]) + m[3:4]
            lane, e1, e2, w1, w2 = _route(_dot3(h, wr_ref[...]) + br_ref[...])
            before, picks = _rank_in_tile(lane, e1, e2)
            before = before + seen_ref[...]
            r1 = jnp.sum(jnp.where(lane == e1, before, 0.0), axis=1, keepdims=True)
            r2 = jnp.sum(jnp.where(lane == e2, before, 0.0), axis=1, keepdims=True)
            seen_ref[...] += jnp.sum(picks, axis=0, keepdims=True)
            info = jnp.where(lane == 0, e1, jnp.where(lane == 1, e2, jnp.where(lane == 2, r1, jnp.where(lane == 3, r2, 0.0))))
            info_ref[...] = info.astype(jnp.int32)
            w_ref[...] = jnp.where(lane == 0, w1, jnp.where(lane == 1, w2, 0.0))

        return rc(body, t_all,
                  [(att_ctx, "row_head"), (att_lat, "row_tail"), (x_in, "row"), (mods[i], "group"), (wo.astype(BF16), "full"),
                   (row2(ln_g[i, 0]), "full"), (row2(ln_b[i, 0]), "full"), (wr, "full"), (br, "full")],
                  [(d, F32), (LANES, jnp.int32), (LANES, F32)],
                  scratch=[pltpu.VMEM((1, LANES), F32)], name=f"post_mixer{i}")

    def moe(i, x1, info, w):
        slot, blk_e, n_used, zero_blk, p_rows = _slot_plan(info[:, 0:2], info[:, 2:4])
        xs = moe_dispatch(x1, mods[i], slot, zero_blk, p_rows, gl, f"moe_dispatch{i}")
        ys = expert_mlp(xs, blk_e, n_used, moe_w_gate[i], moe_w_up[i], moe_w_down[i])
        tm = ROW_TILE

        def body(slot_ref, x_ref, w_ref, mod_ref, g_ref, b_ref, ys_ref, out_ref, buf, sem):
            def start(rr, c):
                for u in range(DMA_UNROLL):
                    r = rr * DMA_UNROLL + u
                    for k in range(2):
                        pltpu.make_async_copy(ys_ref.at[pl.ds(slot_ref[2 * r + k], 1)], buf.at[k, pl.ds(r, 1)], sem).start()
                return c

            lax.fori_loop(0, tm // DMA_UNROLL, start, 0)
            for k in range(2):
                pltpu.make_async_copy(ys_ref.at[pl.ds(0, tm)], buf.at[k], sem).wait()
            m = mod_ref[0]
            wv = w_ref[...]
            y = wv[:, 0:1] * buf[0] + wv[:, 1:2] * buf[1]
            out_ref[...] = _layer_norm(DEEPNORM_ALPHA * x_ref[...] + m[5:6] * y, g_ref[...], b_ref[...])

        return rc(body, t_all,
                  [(slot, "smem_rows"), (x1, "row"), (w, "row"), (mods[i], "group"),
                   (row2(ln_g[i, 1]), "full"), (row2(ln_b[i, 1]), "full"), (ys, "any")],
                  [(d, F32)], scratch=[pltpu.VMEM((2, tm, d), F32), pltpu.SemaphoreType.DMA(())],
                  params=_dma_params(), name=f"moe_combine{i}")[0]

    def finish_layer(i, att_ctx, att_lat, wo, x_in):
        x1, info, w = post_mixer(i, att_ctx, att_lat, wo, x_in)
        return moe(i, x1, info, w)

    tq = 256
    i = 0
    hd = HEAD_DIM
    kvw = A_KV_HEADS * hd
    cq, sq = _rope_tables(gl, hd, 0, 0, LANES // hd)
    wq, wk, wv = a_wq[0], a_wk[0], a_wv[0]

    def body_a(x_ref, mod_ref, wq_ref, wqr_ref, wk_ref, wkr_ref, wv_ref, c_ref, s_ref, q_ref, kh_ref, vh_ref, kc_ref, vc_ref):
        h = _modulate(x_ref, mod_ref, 0, 1).astype(BF16)
        cc, ss = c_ref[0], s_ref[0]
        q = _dot(h, wq_ref[...]) * _tile_lanes(cc, d) + _dot(h, wqr_ref[...]) * _tile_lanes(ss, d)
        k = _dot(h, wk_ref[...]) * _tile_lanes(cc, kvw) + _dot(h, wkr_ref[...]) * _tile_lanes(ss, kvw)
        v = _dot(h, wv_ref[...])
        q_ref[...] = (q * QSCALE_64).astype(BF16)
        _store_heads(kh_ref, k, hd)
        _store_heads(vh_ref, v, hd)

        @pl.when(pl.program_id(0) < n_ctx_tiles)
        def _():
            kc_ref[...] = k
            vc_ref[...] = v

    q, kh, vh, k_new, v_new = rc(
        body_a, t_all,
        [(x, "row"), (mods[i], "group"), (wq.astype(BF16), "full"), (_rot_cols(wq, hd).astype(BF16), "full"),
         (wk.astype(BF16), "full"), (_rot_cols(wk, hd).astype(BF16), "full"), (wv.astype(BF16), "full"),
         (cq, "pos"), (sq, "pos")],
        [(d, BF16), (A_KV_HEADS, hd, BF16, "heads"), (A_KV_HEADS, hd, BF16, "heads"),
         (gl, kvw, F32, "ctx"), (gl, kvw, F32, "ctx")], name="proj_a")
    out_a_k = k_new.reshape(bc, 1, lc, A_KV_HEADS, hd)
    out_a_v = v_new.reshape(bc, 1, lc, A_KV_HEADS, hd)
    common = dict(n_heads=A_HEADS, dq=hd, dv=hd, hpb=4, kvpb=1, tq=tq, sink=a_sink[0])
    att_c = attention(q, kh, vh, n_seq=bc, lq=lc, row_off=0, win=lc, name="attn_a_ctx", **common)
    win_a = 2 * tq
    nq_l = gl // tq
    qi = jnp.arange(tq)[:, None]
    ki = jnp.arange(win_a)[None, :]
    band = jnp.stack([jnp.where(jnp.abs(ki - (qi + off)) <= A_WINDOW, 0.0, NEG_INF)
                      for off in (0, A_WINDOW, 2 * A_WINDOW)]).astype(F32)[None]
    att_l = attention(q, kh, vh, n_seq=bl, lq=gl, row_off=gl, win=win_a,
                      start_fn=lambda ii: jnp.clip(ii * tq - A_WINDOW, 0, gl - win_a),
                      kc=_cache_heads_major(cache_a_k[:, 0]), vc=_cache_heads_major(cache_a_v[:, 0]),
                      bias=band, type_fn=lambda ii: jnp.where(ii == 0, 0, jnp.where(ii == nq_l - 1, 2, 1)),
                      name="attn_a_lat", **common)
    x = finish_layer(i, att_c, att_l, a_wo[0], x)

    i = 1
    hq = B_NOPE + B_ROPE
    qw = B_HEADS * B_QPAD
    wqb = b_wq_b[0].reshape(B_Q_RANK, B_HEADS, hq)
    wqb_rot = jnp.concatenate([jnp.zeros_like(wqb[..., :B_NOPE]),
                               _rot_cols(wqb[..., B_NOPE:].reshape(B_Q_RANK, -1), B_ROPE).reshape(B_Q_RANK, B_HEADS, B_ROPE)],
                              axis=-1)
    padq = lambda wz: jnp.pad(wz, ((0, 0), (0, 0), (0, B_QPAD - hq))).reshape(B_Q_RANK, qw).astype(BF16)
    wkv_c, wkv_r = b_wkv_a[0][:, :B_KV_RANK], b_wkv_a[0][:, B_KV_RANK:]
    cqb, sqb = _rope_tables(gl, B_ROPE, B_NOPE, B_QPAD - hq, 1)
    ckr, skr = _rope_tables(gl, B_ROPE, 0, 0, 1)

    def body_b(x_ref, mod_ref, wqa_ref, qn_ref, wqb_ref, wqbr_ref, wc_ref, kn_ref, wr_ref, wrr_ref, wkb_ref, wvb_ref,
               cq_ref, sq_ref, ck_ref, sk_ref, q_ref, kh_ref, vh_ref, ckv_ref, kr_ref):
        h = _modulate(x_ref, mod_ref, 0, 1).astype(BF16)
        qa = _rms(_dot(h, wqa_ref[...]), qn_ref[...]).astype(BF16)
        q = (_dot(qa, wqb_ref[...]) * _tile_lanes(cq_ref[0], qw) + _dot(qa, wqbr_ref[...]) * _tile_lanes(sq_ref[0], qw))
        q_ref[...] = (q * QSCALE_MLA).astype(BF16)
        ckv = _rms(_dot(h, wc_ref[...]), kn_ref[...])
        kr = _dot(h, wr_ref[...]) * ck_ref[0] + _dot(h, wrr_ref[...]) * sk_ref[0]
        cb = ckv.astype(BF16)
        _store_mla_keys(kh_ref, _dot(cb, wkb_ref[...]), kr)
        _store_heads(vh_ref, _dot(cb, wvb_ref[...]), B_VDIM)

        @pl.when(pl.program_id(0) < n_ctx_tiles)
        def _():
            ckv_ref[...] = ckv
            kr_ref[...] = kr

    wkb, wvb = b_wk_b[0].astype(BF16), b_wv_b[0].astype(BF16)
    q, kh, vh, ckv_new, kr_new = rc(
        body_b, t_all,
        [(x, "row"), (mods[i], "group"), (b_wq_a[0].astype(BF16), "full"), (row2(b_q_norm[0]), "full"),
         (padq(wqb), "full"), (padq(wqb_rot), "full"), (wkv_c.astype(BF16), "full"), (row2(b_kv_norm[0]), "full"),
         (wkv_r.astype(BF16), "full"), (_rot_cols(wkv_r, B_ROPE).astype(BF16), "full"), (wkb, "full"), (wvb, "full"),
         (cqb, "pos"), (sqb, "pos"), (ckr, "pos"), (skr, "pos")],
        [(qw, BF16), (B_HEADS, B_QPAD, BF16, "heads"), (B_HEADS, B_VDIM, BF16, "heads"),
         (gl, B_KV_RANK, F32, "ctx"), (gl, B_ROPE, F32, "ctx")], name="proj_b")
    out_b_ckv = ckv_new.reshape(bc, 1, lc, B_KV_RANK)
    out_b_krope = kr_new.reshape(bc, 1, lc, B_ROPE)

    def body_bc(c_ref, r_ref, wkb_ref, wvb_ref, kh_ref, vh_ref):
        cb = c_ref[...].astype(BF16)
        _store_mla_keys(kh_ref, _dot(cb, wkb_ref[...]), r_ref[...])
        _store_heads(vh_ref, _dot(cb, wvb_ref[...]), B_VDIM)

    n_pc = bl * past
    kh_p, vh_p = row_call(body_bc, n_pc,
                          [(cache_b_ckv[:, 0].reshape(n_pc, B_KV_RANK), "row"), (cache_b_krope[:, 0].reshape(n_pc, B_ROPE), "row"),
                           (wkb, "full"), (wvb, "full")],
                          [(B_HEADS, B_QPAD, BF16, "heads"), (B_HEADS, B_VDIM, BF16, "heads")],
                          tm=min(ROW_TILE, n_pc), name="proj_b_past")
    common = dict(n_heads=B_HEADS, dq=B_QPAD, dv=B_VDIM, hpb=2, kvpb=2, tq=tq)
    att_c = attention(q, kh, vh, n_seq=bc, lq=lc, row_off=0, win=lc, name="attn_b_ctx", **common)
    att_l = attention(q, kh, vh, n_seq=bl, lq=gl, row_off=gl, win=gl, kc=kh_p, vc=vh_p, name="attn_b_lat", **common)
    x = finish_layer(i, att_c, att_l, b_wo[0], x)

    i = 2
    d3 = 3 * d

    def body_c(x_ref, mod_ref, w_ref, b_ref, u_ref):
        h = _modulate(x_ref, mod_ref, 0, 1).astype(BF16)
        for j in range(3):
            u_ref[:, j * d:(j + 1) * d] = _dot(h, w_ref[:, j * d:(j + 1) * d]) + b_ref[:, j * d:(j + 1) * d]

    (u,) = rc(body_c, t_all, [(x, "row"), (mods[i], "group"), (c_w_in[0].astype(BF16), "full"), (row2(c_b_in[0]), "full")],
              [(d3, F32)], name="proj_c")
    u_c = short_conv(u, c_conv_w[0], row2(c_conv_b[0]), row_off=0, n_seq=bc, length=lc, parts=3, name="short_conv_ctx")
    u_l = short_conv(u, c_conv_w[0], row2(c_conv_b[0]), row_off=gl, n_seq=bl, length=gl, parts=3, name="short_conv_lat")
    fargs = (c_ffn_w1[0], c_ffn_b1[0], c_ffn_w2[0], c_ffn_b2[0], c_ffn_w3[0], c_ffn_freq[0], c_log_decay[0])
    filt_c = hyena_filters(lc, *fargs, name="filters_ctx").reshape(lc, 2, 2, d)
    filt_l = hyena_filters(gl, *fargs, name="filters_lat").reshape(gl, 2, 2, d)

    nc = 2 * lc
    mf_data, mf_filt, mi_c = _dft_mats_single(nc, lc, lc)
    g_c = jnp.stack([_two_sided(filt_c, o) for o in range(2)])
    spec_c = spectral(g_c.reshape(2, 1, 1, nc, d), mf_filt, dt=512, name="filt_spec_ctx")
    z_c = u_c[0].reshape(bc // 2, 2, 1, lc, d)
    for o in range(2):
        gate = u_c[o + 1].reshape(bc // 2, 2, 1, lc, d)
        z_c = spectral(z_c, mf_data, spec_c, mi_c, dt=512, g_index=o, epi=(gate, z_c, row2(c_skip[0, o])),
                       name=f"conv_ctx{o}")
    zc_out = z_c.reshape(gl, d)

    nl = 2 * gl
    n1 = nl // FFT_N2
    a_data, a_filt, a_inv, mf_l, mi_l = _dft_mats_two_stage(nl, FFT_N2)
    g_l = jnp.stack([_two_sided(filt_l, o) for o in range(2)])
    ga = slow_stage(a_filt, g_l.reshape(2, n1, FFT_N2, d), name="filt_stage_a")
    spec_l = spectral(ga.reshape(2, 2, n1, FFT_N2, d), mf_l, dt=1024, name="filt_spec_lat")
    z_l = u_l[0].reshape(bl // 2, n1, FFT_N2, d)
    for o in range(2):
        za = slow_stage(a_data, z_l, name=f"conv_lat_a{o}")
        zb = spectral(za.reshape(bl // 2, 2, n1, FFT_N2, d), mf_l, spec_l, mi_l, dt=1024, g_index=o,
                      name=f"conv_lat_c{o}")
        gate = u_l[o + 1].reshape(bl // 2, n1, FFT_N2, d)
        z_l = slow_stage(a_inv, zb.reshape(bl // 2, 2 * n1, FFT_N2, d), epi=(gate, z_l, row2(c_skip[0, o])),
                         name=f"conv_lat_i{o}")
    x = finish_layer(i, zc_out, z_l.reshape(bl * gl, d), c_wo[0], x)

    i = 3

    def body_d(x_ref, mod_ref, wq_ref, wk_ref, wv_ref, q_ref, kh_ref, vh_ref, kc_ref, vc_ref):
        h = _modulate(x_ref, mod_ref, 0, 1).astype(BF16)
        q_ref[...] = (_dot(h, wq_ref[...]) * QSCALE_64).astype(BF16)
        k = _dot(h, wk_ref[...])
        v = _dot(h, wv_ref[...])
        _store_heads(kh_ref, k, hd)
        _store_heads(vh_ref, v, hd)

        @pl.when(pl.program_id(0) < n_ctx_tiles)
        def _():
            kc_ref[...] = k
            vc_ref[...] = v

    q, k_heads, v_heads, k_new, v_new = rc(
        body_d, t_all,
        [(x, "row"), (mods[i], "group"), (d_wq[0].astype(BF16), "full"), (d_wk[0].astype(BF16), "full"),
         (d_wv[0].astype(BF16), "full")],
        [(d, BF16), (D_HEADS, hd, BF16, "heads"), (D_HEADS, hd, BF16, "heads"), (gl, d, F32, "ctx"), (gl, d, F32, "ctx")],
        name="proj_d")
    out_d_k = k_new.reshape(bc, 1, lc, D_HEADS, hd)
    out_d_v = v_new.reshape(bc, 1, lc, D_HEADS, hd)
    common = dict(n_heads=D_HEADS, dq=hd, dv=hd, hpb=2, kvpb=2, tq=tq)
    att_c = attention(q, k_heads, v_heads, n_seq=bc, lq=lc, row_off=0, win=lc, name="attn_d_ctx", **common)
    rows = gl // GRID_W
    kh = min(MAX_NBR_ROWS, rows)
    qrows = tq // GRID_W
    krows = qrows + kh
    win_d = krows * GRID_W
    nq_l = gl // tq
    qr_l, kr_l = jnp.arange(qrows)[:, None], jnp.arange(krows)[None, :]
    qc, kc_ = jnp.arange(GRID_W)[:, None], jnp.arange(GRID_W)[None, :]
    c0 = jnp.clip(qc - NBR_COLS // 2, 0, GRID_W - NBR_COLS)
    col_ok = (kc_ >= c0) & (kc_ < c0 + NBR_COLS)
    dc = jnp.clip(kc_ - qc, 1 - NBR_COLS, NBR_COLS - 1) + NBR_COLS - 1
    dc_hot = (dc[..., None] == jnp.arange(2 * NBR_COLS - 1)).astype(F32)
    by_col = jnp.einsum("hrc,xyc->hrxy", d_rel_bias[0], dc_hot, precision=lax.Precision.HIGHEST)
    tabs = []
    for off, lo in ((0, jnp.zeros_like(qr_l)), (kh // 2, qr_l), (kh, jnp.full_like(qr_l, qrows))):
        row_ok = (kr_l >= lo) & (kr_l < lo + kh)
        dr = jnp.clip(kr_l - off - qr_l + MAX_NBR_ROWS - 1, 0, 2 * MAX_NBR_ROWS - 2)
        dr_hot = (dr[..., None] == jnp.arange(2 * MAX_NBR_ROWS - 1)).astype(F32)
        tab = jnp.einsum("qkr,hrxy->hqxky", dr_hot, by_col, precision=lax.Precision.HIGHEST)
        ok = row_ok[:, None, :, None] & col_ok[None, :, None, :]
        tabs.append(jnp.where(ok[None], tab, NEG_INF).reshape(D_HEADS, tq, win_d))
    nbr_bias = jnp.stack(tabs, axis=1).astype(F32)
    att_l = attention(q, k_heads, v_heads, n_seq=bl, lq=gl, row_off=gl, win=win_d,
                      start_fn=lambda ii: jnp.clip(ii * qrows - kh // 2, 0, rows - krows) * GRID_W,
                      kc=_cache_heads_major(cache_d_k[:, 0]), vc=_cache_heads_major(cache_d_v[:, 0]),
                      bias=nbr_bias, type_fn=lambda ii: jnp.where(ii == 0, 0, jnp.where(ii == nq_l - 1, 2, 1)),
                      name="attn_d_lat", **common)
    x = finish_layer(i, att_c, att_l, d_wo[0], x)

    y_prompt = x[:gl].reshape(bc, lc, d)
    y_sample = x[gl:].reshape(bl, gl, d)
    return (y_prompt, y_sample, out_a_k, out_a_v, out_b_ckv, out_b_krope, out_d_k, out_d_v)
```

```python
import functools

import jax
import jax.numpy as jnp
import numpy as np
from jax import lax
from jax.experimental import pallas as pl
from jax.experimental.pallas import tpu as pltpu

F32 = jnp.float32
BF16 = jnp.bfloat16

GRID_W = 64
HEAD_DIM = 64
ROPE_BASE = 10000.0
LN_EPS = 1e-5
RMS_EPS = 1e-6
NEG_INF = -1e30
DEPTH = 4
DEEPNORM_ALPHA = (2 * DEPTH) ** 0.25
A_HEADS = 16
A_KV_HEADS = 4
A_WINDOW = 128
B_HEADS = 16
B_Q_RANK = 384
B_KV_RANK = 256
B_NOPE = 64
B_ROPE = 32
B_VDIM = 64
B_QPAD = 128
C_POS_BANDS = 16
D_HEADS = 16
MAX_NBR_ROWS = 8
NBR_COLS = 16
N_GROUPS = 4
EXPERTS_PER_GROUP = 8
N_EXPERTS = N_GROUPS * EXPERTS_PER_GROUP
D_EXPERT = 512
MOE_BLK = 512
LANES = 128
V_LANES = 128
LOG2E = 1.4426950408889634
QSCALE_64 = HEAD_DIM ** -0.5 * LOG2E
QSCALE_MLA = (B_NOPE + B_ROPE) ** -0.5 * LOG2E
FFT_N2 = 128
VMEM_LIMIT = 56 * 1024 * 1024
ROW_TILE = 512


def _params(n_axes):
    return pltpu.CompilerParams(dimension_semantics=("arbitrary",) * n_axes, vmem_limit_bytes=VMEM_LIMIT)


def _dot(a, b):
    return jnp.dot(a, b, preferred_element_type=F32)


def _split(x):
    hi = x.astype(BF16)
    lo = (x - hi.astype(F32)).astype(BF16)
    return hi, lo


def _dot3(a, b):
    ah, al = _split(a)
    bh, bl = _split(b)
    return _dot(ah, bh) + _dot(ah, bl) + _dot(al, bh)


def _layer_norm(y, g, b):
    mu = jnp.mean(y, axis=-1, keepdims=True)
    d = y - mu
    var = jnp.mean(d * d, axis=-1, keepdims=True)
    return d * lax.rsqrt(var + LN_EPS) * g + b


def _rms(y, g):
    return y * lax.rsqrt(jnp.mean(y * y, axis=-1, keepdims=True) + RMS_EPS) * g


def _tile_lanes(t, n):
    reps = n // t.shape[-1]
    return t if reps == 1 else jnp.concatenate([t] * reps, axis=-1)


def row_call(body, m_rows, ins, outs, *, tm=ROW_TILE, group_len=None, name=None, scratch=(), params=None):
    nb = None if group_len is None else group_len // tm
    n_tiles = m_rows // tm
    in_specs = []
    for a, kind in ins:
        if kind == "row":
            in_specs.append(pl.BlockSpec((tm, a.shape[1]), lambda i: (i, 0)))
        elif kind == "row_head":
            nh = a.shape[0] // tm
            in_specs.append(pl.BlockSpec((tm, a.shape[1]), lambda i, nh=nh: (jnp.minimum(i, nh - 1), 0)))
        elif kind == "row_tail":
            nh = n_tiles - a.shape[0] // tm
            in_specs.append(pl.BlockSpec((tm, a.shape[1]), lambda i, nh=nh: (jnp.maximum(i - nh, 0), 0)))
        elif kind == "any":
            in_specs.append(pl.BlockSpec(memory_space=pl.ANY))
        elif kind == "smem":
            in_specs.append(pl.BlockSpec(memory_space=pltpu.SMEM))
        elif kind == "smem_rows":
            in_specs.append(pl.BlockSpec((a.shape[0] // n_tiles,), lambda i: (i,), memory_space=pltpu.SMEM))
        elif kind == "full":
            in_specs.append(pl.BlockSpec(a.shape, lambda i, nd=a.ndim: (0,) * nd))
        elif kind == "group":
            in_specs.append(pl.BlockSpec((1,) + a.shape[1:], lambda i: (i // nb, 0, 0)))
        elif kind == "pos":
            in_specs.append(pl.BlockSpec((1, tm, a.shape[2]), lambda i: (jnp.minimum(i // nb, 1), i % nb, 0)))
        else:
            raise ValueError(kind)
    out_specs, out_shape = [], []
    for o in outs:
        if len(o) == 4 and o[3] == "any":
            out_specs.append(pl.BlockSpec(memory_space=pl.ANY))
            out_shape.append(jax.ShapeDtypeStruct((o[0], o[1]), o[2]))
        elif len(o) == 4 and o[3] == "heads":
            out_specs.append(pl.BlockSpec((o[0], tm, o[1]), lambda i: (0, i, 0)))
            out_shape.append(jax.ShapeDtypeStruct((o[0], m_rows, o[1]), o[2]))
        elif len(o) == 4 and o[3] == "ctx":
            nkeep = o[0] // tm
            out_specs.append(pl.BlockSpec((tm, o[1]), lambda i, nkeep=nkeep: (jnp.minimum(i, nkeep - 1), 0)))
            out_shape.append(jax.ShapeDtypeStruct((o[0], o[1]), o[2]))
        else:
            out_specs.append(pl.BlockSpec((tm, o[0]), lambda i: (i, 0)))
            out_shape.append(jax.ShapeDtypeStruct((m_rows, o[0]), o[1]))
    return pl.pallas_call(
        body, grid=(n_tiles,), in_specs=in_specs, out_specs=out_specs, out_shape=out_shape,
        scratch_shapes=list(scratch), compiler_params=params or _params(1), name=name)(*[a for a, _ in ins])


def _modulate(x_ref, mod_ref, shift_row, scale_row):
    m = mod_ref[0]
    return x_ref[...] * (1.0 + m[scale_row:scale_row + 1]) + m[shift_row:shift_row + 1]


def modulation_all(cvec, mod_w, mod_b):
    depth, d, d6 = mod_w.shape
    g = cvec.shape[0]
    gp = -(-g // 16) * 16
    cp = jnp.zeros((gp, d), F32).at[:g].set(cvec)
    tn = 1024

    def body(c_ref, w_ref, b_ref, o_ref):
        c = c_ref[...]
        a = (c * jax.nn.sigmoid(c)).astype(BF16)
        o_ref[0] = _dot(a, w_ref[0].astype(BF16)) + b_ref[0]

    out = pl.pallas_call(
        body, grid=(depth, d6 // tn),
        in_specs=[pl.BlockSpec((gp, d), lambda l, j: (0, 0)),
                  pl.BlockSpec((1, d, tn), lambda l, j: (l, 0, j)),
                  pl.BlockSpec((1, 1, tn), lambda l, j: (l, 0, j))],
        out_specs=pl.BlockSpec((1, gp, tn), lambda l, j: (l, 0, j)),
        out_shape=jax.ShapeDtypeStruct((depth, gp, d6), F32),
        compiler_params=_params(2), name="modulation")(cp, mod_w, mod_b.reshape(depth, 1, d6))
    return out[:, :g].reshape(depth, g, 6, d)


def attention(q, kl, vl, *, n_seq, lq, row_off, n_heads, dq, dv, hpb, kvpb, tq, win,
              start_fn=None, kc=None, vc=None, bias=None, type_fn=None, sink=None, name=None):
    nq = lq // tq
    koff = row_off // lq
    rep = hpb // kvpb
    has_ctx, has_bias, has_sink = kc is not None, bias is not None, sink is not None
    bias_heads = has_bias and bias.shape[0] > 1
    off_blk = row_off // tq

    def kern(*refs):
        it = iter(refs)
        q_ref, kl_ref, vl_ref = next(it), next(it), next(it)
        kc_ref = next(it) if has_ctx else None
        vc_ref = next(it) if has_ctx else None
        b_ref = next(it) if has_bias else None
        s_ref = next(it) if has_sink else None
        o_ref = next(it)
        i = pl.program_id(2)
        hb = pl.program_id(1)
        if start_fn is None:
            start = 0
        else:
            start = pl.multiple_of(start_fn(i), 64)
        nt = (((1,), (1,)), ((), ()))
        scores = []
        for j in range(hpb):
            kv = j // rep
            qj = q_ref[:, j * dq:(j + 1) * dq]
            s = lax.dot_general(qj, kl_ref[kv, pl.ds(start, win), :], nt, preferred_element_type=F32)
            if has_bias:
                s = s + b_ref[j if bias_heads else 0, 0]
            sc = lax.dot_general(qj, kc_ref[kv], nt, preferred_element_type=F32) if has_ctx else None
            scores.append((s, sc))
        outs = []
        for j in range(hpb):
            kv = j // rep
            s, sc = scores[j]
            m = jnp.max(s, axis=1, keepdims=True)
            if has_ctx:
                m = jnp.maximum(m, jnp.max(sc, axis=1, keepdims=True))
            if has_sink:
                sk = s_ref[hb * hpb + j]
                m = jnp.maximum(m, sk)
            acc = _dot(jnp.exp2(s - m).astype(BF16), vl_ref[kv, pl.ds(start, win), :])
            if has_ctx:
                acc = acc + _dot(jnp.exp2(sc - m).astype(BF16), vc_ref[kv])
            l = acc[:, dv:dv + 1]
            if has_sink:
                l = l + jnp.exp2(sk - m)
            outs.append(acc[:, :dv] / l)
        o_ref[...] = jnp.concatenate(outs, axis=1).astype(o_ref.dtype)

    ins = [q, kl, vl]
    in_specs = [pl.BlockSpec((tq, hpb * dq), lambda b, h, i: (off_blk + b * nq + i, h)),
                pl.BlockSpec((kvpb, lq, dq), lambda b, h, i: (h, koff + b, 0)),
                pl.BlockSpec((kvpb, lq, V_LANES), lambda b, h, i: (h, koff + b, 0))]
    if has_ctx:
        lc = kc.shape[1] // n_seq
        ins += [kc, vc]
        in_specs += [pl.BlockSpec((kvpb, lc, dq), lambda b, h, i: (h, b, 0)),
                     pl.BlockSpec((kvpb, lc, V_LANES), lambda b, h, i: (h, b, 0))]
    if has_bias:
        ins.append(bias * LOG2E)
        hb_blk = hpb if bias_heads else 1
        in_specs.append(pl.BlockSpec((hb_blk, 1, tq, win),
                                     lambda b, h, i: (h if bias_heads else 0, type_fn(i), 0, 0)))
    if has_sink:
        ins.append(sink.astype(F32) * LOG2E)
        in_specs.append(pl.BlockSpec(memory_space=pltpu.SMEM))
    return pl.pallas_call(
        kern, grid=(n_seq, n_heads // hpb, nq), in_specs=in_specs,
        out_specs=pl.BlockSpec((tq, hpb * dv), lambda b, h, i: (b * nq + i, h)),
        out_shape=jax.ShapeDtypeStruct((n_seq * lq, n_heads * dv), BF16),
        compiler_params=_params(3), name=name)(*ins)


def _cache_heads_major(c, values=False):
    n_seq, length, n_heads, d = c.shape
    out = c.astype(BF16).transpose(2, 0, 1, 3).reshape(n_heads, n_seq * length, d)
    if values:
        rows = n_seq * length
        out = jnp.concatenate([out, jnp.ones((n_heads, rows, 1), BF16),
                               jnp.zeros((n_heads, rows, V_LANES - d - 1), BF16)], axis=-1)
    return out


def _store_value_heads(ref, val, width):
    rows = val.shape[0]
    lane = lax.broadcasted_iota(jnp.int32, (rows, V_LANES - width), 1)
    tail = jnp.where(lane == 0, 1.0, 0.0).astype(ref.dtype)
    for hh in range(ref.shape[0]):
        ref[hh] = jnp.concatenate([val[:, hh * width:(hh + 1) * width].astype(ref.dtype), tail], axis=1)


def _store_mla_keys(ref, k_nope, k_rope):
    rows = k_nope.shape[0]
    tail = jnp.concatenate([k_rope, jnp.zeros((rows, B_QPAD - B_NOPE - B_ROPE), k_rope.dtype)], axis=1)
    for hh in range(ref.shape[0]):
        ref[hh] = jnp.concatenate([k_nope[:, hh * B_NOPE:(hh + 1) * B_NOPE], tail], axis=1).astype(ref.dtype)


def _store_heads(ref, val, width):
    for hh in range(ref.shape[0]):
        ref[hh] = val[:, hh * width:(hh + 1) * width].astype(ref.dtype)


def _rope_tables(length, dim, lead, tail, reps):
    half = dim // 2
    nf = half // 2
    t = jnp.arange(length)
    row = (t // GRID_W).astype(F32)
    col = (t % GRID_W).astype(F32)
    inv = ROPE_BASE ** (-jnp.arange(nf, dtype=F32) / nf)
    ang = jnp.concatenate([row[:, None] * inv, col[:, None] * inv], axis=-1)
    cos, sin = jnp.cos(ang), jnp.sin(ang)
    c = jnp.concatenate([jnp.ones((length, lead), F32), cos, cos, jnp.ones((length, tail), F32)] * reps, axis=1)
    s = jnp.concatenate([jnp.zeros((length, lead), F32), sin, sin, jnp.zeros((length, tail), F32)] * reps, axis=1)
    return (jnp.stack([jnp.ones_like(c), c]), jnp.stack([jnp.zeros_like(s), s]))


def _rot_cols(w, dim):
    k, n = w.shape
    wb = w.reshape(k, n // dim, dim)
    half = dim // 2
    return jnp.concatenate([-wb[..., half:], wb[..., :half]], axis=-1).reshape(k, n)


def _cs(phase, n, sign):
    ang = (2.0 * np.pi / n) * (phase % n).astype(F32)
    return jnp.cos(ang), sign * jnp.sin(ang)


def _cblock(wr, wi):
    return jnp.concatenate([jnp.concatenate([wr, -wi], axis=-1), jnp.concatenate([wi, wr], axis=-1)], axis=-2)


SUB = 8


def slow_stage(a, x, *, epi=None, name=None):
    p_n, k, s_n, d = x.shape
    m = a.shape[0]
    x5 = x.reshape(p_n, k, s_n // SUB, SUB, d)

    def kern(*refs):
        if epi is None:
            a_ref, x_ref, o_ref = refs
        else:
            a_ref, x_ref, g_ref, z_ref, s_ref, o_ref = refs
        xt = pltpu.einshape("ksd->skd", x_ref[...])
        av = a_ref[...]
        y = jnp.stack([_dot3(av, xt[s]) for s in range(SUB)], axis=0)
        y = pltpu.einshape("smd->msd", y)
        if epi is not None:
            y = g_ref[...] * (y + s_ref[...][None] * z_ref[...])
        o_ref[...] = y

    blk = lambda rows: pl.BlockSpec((None, rows, None, SUB, d), lambda p, t: (p, 0, t, 0, 0))
    ins = [a, x5]
    in_specs = [pl.BlockSpec((m, k), lambda p, t: (0, 0)), blk(k)]
    if epi is not None:
        gate, z, skip = epi
        ins += [gate.reshape(p_n, m, s_n // SUB, SUB, d), z.reshape(p_n, m, s_n // SUB, SUB, d), skip]
        in_specs += [blk(m), blk(m), pl.BlockSpec((1, d), lambda p, t: (0, 0))]
    out = pl.pallas_call(
        kern, grid=(p_n, s_n // SUB), in_specs=in_specs, out_specs=blk(m),
        out_shape=jax.ShapeDtypeStruct((p_n, m, s_n // SUB, SUB, d), F32),
        compiler_params=_params(2), name=name)(*ins)
    return out.reshape(p_n, m, s_n, d)


def spectral(x, mf, g=None, mi=None, *, dt, g_index=0, epi=None, name=None):
    p_n, planes, k1_n, nin, d = x.shape
    nf = mf.shape[1] // 2
    nout = nf if g is None else mi.shape[1] // 2

    def kern(*refs):
        it = iter(refs)
        x_ref, mf_ref = next(it), next(it)
        g_ref = next(it) if g is not None else None
        mi_ref = next(it) if g is not None else None
        if epi is not None:
            gate_ref, z_ref, skip_ref = next(it), next(it), next(it)
        o_ref = next(it)
        xs = [x_ref[0, pp, 0] for pp in range(planes)]
        xin = xs[0] if planes == 1 else jnp.concatenate(xs, axis=0)
        f = _dot3(mf_ref[0], xin)
        if g is not None:
            fr, fi = f[:nf], f[nf:]
            gr, gi = g_ref[0, 0, 0], g_ref[0, 1, 0]
            y = jnp.concatenate([fr * gr - fi * gi, fr * gi + fi * gr], axis=0)
            f = _dot3(mi_ref[0], y)
        for pp in range(2):
            y = f[pp * nout:(pp + 1) * nout]
            if epi is not None:
                y = gate_ref[0, pp, 0] * (y + skip_ref[...] * z_ref[0, pp, 0])
            o_ref[0, pp, 0] = y

    ins = [x, mf]
    in_specs = [pl.BlockSpec((1, planes, 1, nin, dt), lambda k, j, p: (p, 0, k, 0, j)),
                pl.BlockSpec((1,) + mf.shape[1:], lambda k, j, p: (k, 0, 0))]
    if g is not None:
        ins += [g, mi]
        in_specs += [pl.BlockSpec((1, 2, 1, nf, dt), lambda k, j, p: (g_index, 0, k, 0, j)),
                     pl.BlockSpec((1,) + mi.shape[1:], lambda k, j, p: (k, 0, 0))]
    if epi is not None:
        ins += list(epi)
        in_specs += [pl.BlockSpec((1, 2, 1, nout, dt), lambda k, j, p: (p, 0, k, 0, j)),
                     pl.BlockSpec((1, 2, 1, nout, dt), lambda k, j, p: (p, 0, k, 0, j)),
                     pl.BlockSpec((1, dt), lambda k, j, p: (0, j))]
    return pl.pallas_call(
        kern, grid=(k1_n, d // dt, p_n), in_specs=in_specs,
        out_specs=pl.BlockSpec((1, 2, 1, nout, dt), lambda k, j, p: (p, 0, k, 0, j)),
        out_shape=jax.ShapeDtypeStruct((p_n, 2, k1_n, nout, d), F32),
        compiler_params=_params(3), name=name)(*ins)


def hyena_two_sided_filters(length, w1, b1, w2, b2, w3, freq, log_decay, name):
    hid = w2.shape[0]
    d = w3.shape[1] // 4
    t = jnp.linspace(0.0, 1.0, length, dtype=F32)[:, None]
    ang = 2.0 * jnp.pi * t * jnp.arange(1, C_POS_BANDS + 1, dtype=F32)
    z = jnp.concatenate([t, jnp.cos(ang), jnp.sin(ang)], axis=-1)
    kpad = LANES - z.shape[1]
    z = jnp.pad(z, ((0, 0), (0, kpad)))
    z2 = jnp.concatenate([z, z[::-1]], axis=0)
    w1p = jnp.pad(w1, ((0, kpad), (0, 0)))
    full = lambda a: pl.BlockSpec(a.shape, lambda *_: (0,) * a.ndim)

    def ffn_kern(z_ref, w1_ref, b1_ref, w2_ref, b2_ref, f_ref, a_ref):
        fr = f_ref[...]
        a = jnp.sin(fr * (_dot3(z_ref[...], w1_ref[...]) + b1_ref[...]))
        a_ref[...] = jnp.sin(fr * (_dot3(a, w2_ref[...]) + b2_ref[...]))

    args = [z2, w1p, b1.reshape(1, hid), w2, b2.reshape(1, hid), freq.reshape(1, hid)]
    act = pl.pallas_call(
        ffn_kern, grid=(1,), in_specs=[full(a) for a in args], out_specs=pl.BlockSpec((2 * length, hid), lambda i: (0, 0)),
        out_shape=jax.ShapeDtypeStruct((2 * length, hid), F32), compiler_params=_params(1), name=name + "_ffn")(*args)

    tn = 256
    per = d // tn

    def kern(af_ref, ar_ref, w3f_ref, w3b_ref, ldf_ref, ldb_ref, o_ref):
        row = lax.broadcasted_iota(jnp.int32, (length, tn), 0)
        pos = row.astype(F32) * (1.0 / (length - 1))
        pos_rev = (length - 1 - row).astype(F32) * (1.0 / (length - 1))

        def filt(a_ref, w_ref, ld_ref, tt):
            f = _dot3(a_ref[...], w_ref[...]) * jnp.exp(-jnp.exp(ld_ref[...]) * tt)
            return f / (jnp.sum(jnp.abs(f), axis=0, keepdims=True) + 1e-6)

        hf = filt(af_ref, w3f_ref, ldf_ref, pos)
        hb_rev = filt(ar_ref, w3b_ref, ldb_ref, pos_rev)
        o_ref[0, 0] = hf + jnp.where(row == 0, hb_rev[length - 1:length], 0.0)
        o_ref[0, 1] = jnp.where(row == 0, 0.0, pltpu.roll(hb_rev, 1, 0))

    col = lambda direction: (lambda o, j: (0, (2 * o + direction) * per + j))
    ld = log_decay.reshape(1, 4 * d)
    g = pl.pallas_call(
        kern, grid=(2, per),
        in_specs=[pl.BlockSpec((length, hid), lambda o, j: (0, 0)), pl.BlockSpec((length, hid), lambda o, j: (1, 0)),
                  pl.BlockSpec((hid, tn), col(0)), pl.BlockSpec((hid, tn), col(1)),
                  pl.BlockSpec((1, tn), col(0)), pl.BlockSpec((1, tn), col(1))],
        out_specs=pl.BlockSpec((1, 2, length, tn), lambda o, j: (o, 0, 0, j)),
        out_shape=jax.ShapeDtypeStruct((2, 2, length, d), F32),
        compiler_params=_params(2), name=name)(act, act, w3, w3, ld, ld)
    return g.reshape(2, 2 * length, d)


def short_conv(u, w, b, *, row_off, n_seq, length, parts, name):
    c = u.shape[1]
    dt = 256
    per = c // parts // dt
    first = row_off // length

    def kern(u_ref, w_ref, b_ref, *o_refs):
        x = u_ref[...]
        r = lax.broadcasted_iota(jnp.int32, x.shape, 0)
        prev = jnp.where(r == 0, 0.0, pltpu.roll(x, 1, 0))
        nxt = jnp.where(r == length - 1, 0.0, pltpu.roll(x, length - 1, 0))
        wv = w_ref[...]
        y = prev * wv[0:1] + x * wv[1:2] + nxt * wv[2:3] + b_ref[...]
        part = pl.program_id(1) // per
        for k, o_ref in enumerate(o_refs):
            @pl.when(part == k)
            def _(o_ref=o_ref):
                o_ref[0] = y

    out_spec = lambda k: pl.BlockSpec((1, length, dt), lambda s, j: (s, 0, jnp.clip(j - k * per, 0, per - 1)))
    return pl.pallas_call(
        kern, grid=(n_seq, c // dt),
        in_specs=[pl.BlockSpec((length, dt), lambda s, j: (first + s, j)),
                  pl.BlockSpec((3, dt), lambda s, j: (0, j)),
                  pl.BlockSpec((1, dt), lambda s, j: (0, j))],
        out_specs=[out_spec(k) for k in range(parts)],
        out_shape=[jax.ShapeDtypeStruct((n_seq, length, c // parts), F32)] * parts,
        compiler_params=_params(2), name=name)(u, w, b)


def _dft_mats_single(n, nin_data, nout):
    k = jnp.arange(n, dtype=jnp.int32)
    fr, fi = _cs(k[:, None] * k[None, :nin_data], n, -1.0)
    mf_data = _cblock(fr, fi)[None]
    gr, gi = _cs(k[:, None] * k[None, :], n, -1.0)
    mf_filt = jnp.concatenate([gr, gi], axis=0)[None]
    ir, ii = _cs(k[:nout, None] * k[None, :], n, 1.0)
    mi = _cblock(ir / n, ii / n)[None]
    return mf_data, mf_filt, mi


def _dft_mats_two_stage(n, n2):
    n1 = n // n2
    k1 = jnp.arange(n1, dtype=jnp.int32)
    t1h = jnp.arange(n1 // 2, dtype=jnp.int32)
    ar, ai = _cs((n // n1) * k1[:, None] * t1h[None, :], n, -1.0)
    a_data = _cblock(ar, ai)
    fr, fi = _cs((n // n1) * k1[:, None] * k1[None, :], n, -1.0)
    a_filt = jnp.concatenate([fr, fi], axis=0)
    br, bi = _cs((n // n1) * t1h[:, None] * k1[None, :], n, 1.0)
    a_inv = _cblock(br, bi)
    t2 = jnp.arange(n2, dtype=jnp.int32)
    kk = k1[:, None, None] + n1 * t2[None, :, None]
    mr, mi_ = _cs(kk * t2[None, None, :], n, -1.0)
    mf = _cblock(mr, mi_)
    vr, vi = _cs(jnp.swapaxes(kk, 1, 2) * t2[None, :, None], n, 1.0)
    mi = _cblock(vr / n, vi / n)
    return a_data, a_filt, a_inv, mf, mi


def _route(logits):
    lane = lax.broadcasted_iota(jnp.int32, logits.shape, 1).astype(F32)
    big = 1e9
    lg = jnp.where(lane < N_GROUPS, logits, -jnp.inf)
    mg = jnp.max(lg, axis=1, keepdims=True)
    gi = jnp.min(jnp.where(lg == mg, lane, big), axis=1, keepdims=True)
    p_g = 1.0 / jnp.sum(jnp.exp(lg - mg), axis=1, keepdims=True)
    lo = N_GROUPS + EXPERTS_PER_GROUP * gi
    le = jnp.where((lane >= lo) & (lane < lo + EXPERTS_PER_GROUP), logits, -jnp.inf)
    m1 = jnp.max(le, axis=1, keepdims=True)
    i1 = jnp.min(jnp.where(le == m1, lane, big), axis=1, keepdims=True)
    le2 = jnp.where(lane == i1, -jnp.inf, le)
    m2 = jnp.max(le2, axis=1, keepdims=True)
    i2 = jnp.min(jnp.where(le2 == m2, lane, big), axis=1, keepdims=True)
    e2 = jnp.exp(m2 - m1)
    w1 = p_g / (1.0 + e2)
    w2 = p_g * e2 / (1.0 + e2)
    return lane, i1 - N_GROUPS, i2 - N_GROUPS, w1, w2


def _rank_in_tile(lane, e1, e2):
    tm = lane.shape[0]
    picks = jnp.where(lane == e1, 1.0, 0.0) + jnp.where(lane == e2, 1.0, 0.0)
    r = lax.broadcasted_iota(jnp.int32, (tm, tm), 0)
    c = lax.broadcasted_iota(jnp.int32, (tm, tm), 1)
    earlier = jnp.where(c < r, 1.0, 0.0).astype(BF16)
    return _dot(earlier, picks.astype(BF16)), picks


def expert_mlp(xs, blk_e, n_used, w_gate, w_up, w_down):
    p_rows, d = xs.shape
    de = w_gate.shape[2]
    nblk = p_rows // MOE_BLK

    def kern(be_ref, nu_ref, x_ref, wg_ref, wu_ref, wd_ref, o_ref, wg_s, wu_s, wd_s):
        i = pl.program_id(0)
        used = i < nu_ref[0]
        fresh = jnp.logical_or(i == 0, be_ref[i] != be_ref[jnp.maximum(i - 1, 0)])

        @pl.when(jnp.logical_and(used, fresh))
        def _():
            wg_s[...] = wg_ref[0].astype(BF16)
            wu_s[...] = wu_ref[0].astype(BF16)
            wd_s[...] = wd_ref[0].astype(BF16)

        @pl.when(used)
        def _():
            x = x_ref[...].astype(BF16)
            g = _dot(x, wg_s[...])
            u = _dot(x, wu_s[...])
            a = (g * jax.nn.sigmoid(g) * u).astype(BF16)
            o_ref[...] = _dot(a, wd_s[...])

        @pl.when(jnp.logical_not(used))
        def _():
            o_ref[...] = jnp.zeros_like(o_ref)

    last = lambda i, nu: jnp.minimum(i, nu[0] - 1)
    grid_spec = pltpu.PrefetchScalarGridSpec(
        num_scalar_prefetch=2, grid=(nblk,),
        in_specs=[pl.BlockSpec((MOE_BLK, d), lambda i, be, nu: (last(i, nu), 0)),
                  pl.BlockSpec((1, d, de), lambda i, be, nu: (be[last(i, nu)], 0, 0)),
                  pl.BlockSpec((1, d, de), lambda i, be, nu: (be[last(i, nu)], 0, 0)),
                  pl.BlockSpec((1, de, d), lambda i, be, nu: (be[last(i, nu)], 0, 0))],
        out_specs=pl.BlockSpec((MOE_BLK, d), lambda i, be, nu: (i, 0)),
        scratch_shapes=[pltpu.VMEM((d, de), BF16), pltpu.VMEM((d, de), BF16), pltpu.VMEM((de, d), BF16)])
    return pl.pallas_call(
        kern, grid_spec=grid_spec, out_shape=jax.ShapeDtypeStruct((p_rows, d), F32),
        compiler_params=_params(1), name="expert_mlp")(blk_e, n_used, xs, w_gate, w_up, w_down)


def _slot_plan(experts, ranks):
    n = 2 * experts.shape[0]
    ids = jnp.arange(N_EXPERTS, dtype=jnp.int32)
    onehot = experts[..., None] == ids
    counts = jnp.sum(onehot, axis=(0, 1)).astype(jnp.int32)
    padded = (counts + MOE_BLK - 1) // MOE_BLK * MOE_BLK
    pend = jnp.cumsum(padded)
    pstart = pend - padded
    p_rows = -(-n // MOE_BLK) * MOE_BLK + N_EXPERTS * MOE_BLK
    nblk = p_rows // MOE_BLK
    blk_first = jnp.arange(nblk, dtype=jnp.int32) * MOE_BLK
    blk_e = jnp.minimum(jnp.sum(pend[None, :] <= blk_first[:, None], axis=1), N_EXPERTS - 1).astype(jnp.int32)
    n_used = (pend[-1:] // MOE_BLK).astype(jnp.int32)
    blk_ids = jnp.arange(nblk, dtype=jnp.int32)
    partly = jnp.any((blk_ids[:, None] == (pend // MOE_BLK - 1)[None, :]) & (counts % MOE_BLK != 0)[None, :], axis=1)
    zero_blk = (partly | (blk_ids >= n_used[0])).astype(jnp.int32)
    slot = ranks + jnp.sum(jnp.where(onehot, pstart, 0), axis=-1)
    return slot.reshape(n).astype(jnp.int32), blk_e, n_used, zero_blk, p_rows


def _dma_params():
    return pltpu.CompilerParams(dimension_semantics=("arbitrary",), vmem_limit_bytes=VMEM_LIMIT,
                                disable_bounds_checks=True)


DMA_UNROLL = 8


def moe_dispatch(x1, mod, slot, zero_blk, p_rows, group_len, name):
    t, d = x1.shape
    tm = ROW_TILE

    def body(slot_ref, zb_ref, x_ref, mod_ref, xs_ref, h_ref, zero_ref, sem):
        h_ref[...] = _modulate(x_ref, mod_ref, 3, 4)

        @pl.when(pl.program_id(0) == 0)
        def _():
            zero_ref[...] = jnp.zeros_like(zero_ref)

            def zstart(b, c):
                @pl.when(zb_ref[b] != 0)
                def _():
                    first = pl.multiple_of(b * MOE_BLK, MOE_BLK)
                    pltpu.make_async_copy(zero_ref, xs_ref.at[pl.ds(first, MOE_BLK)], sem).start()
                return c

            def zwait(b, c):
                @pl.when(zb_ref[b] != 0)
                def _():
                    pltpu.make_async_copy(zero_ref, xs_ref.at[pl.ds(0, MOE_BLK)], sem).wait()
                return c

            lax.fori_loop(0, p_rows // MOE_BLK, zstart, 0)
            lax.fori_loop(0, p_rows // MOE_BLK, zwait, 0)

        def start(rr, c):
            for u in range(DMA_UNROLL):
                r = rr * DMA_UNROLL + u
                for k in range(2):
                    pltpu.make_async_copy(h_ref.at[pl.ds(r, 1)], xs_ref.at[pl.ds(slot_ref[2 * r + k], 1)], sem).start()
            return c

        lax.fori_loop(0, tm // DMA_UNROLL, start, 0)
        for k in range(2):
            pltpu.make_async_copy(h_ref, xs_ref.at[pl.ds(0, tm)], sem).wait()

    return row_call(body, t, [(slot, "smem_rows"), (zero_blk, "smem"), (x1, "row"), (mod, "group")],
                    [(p_rows, d, F32, "any")], group_len=group_len,
                    scratch=[pltpu.VMEM((tm, d), F32), pltpu.VMEM((MOE_BLK, d), F32), pltpu.SemaphoreType.DMA(())],
                    params=_dma_params(), name=name)[0]


def kernel(x_prompt, x_sample, cache_a_k, cache_a_v, cache_b_ckv, cache_b_krope, cache_d_k, cache_d_v, c_ctx, c, mod_w, mod_b, ln_g, ln_b, a_wq, a_wk, a_wv, a_wo, a_sink, b_wq_a, b_q_norm, b_wq_b, b_wkv_a, b_kv_norm, b_wk_b, b_wv_b, b_wo, c_w_in, c_b_in, c_conv_w, c_conv_b, c_ffn_w1, c_ffn_b1, c_ffn_w2, c_ffn_b2, c_ffn_w3, c_ffn_freq, c_log_decay, c_skip, c_wo, d_wq, d_wk, d_wv, d_wo, d_rel_bias, moe_wr_g, moe_br_g, moe_wr_e, moe_br_e, moe_w_gate, moe_w_up, moe_w_down):
    bc, lc, d = x_prompt.shape
    bl, ll, _ = x_sample.shape
    past = cache_a_k.shape[2]
    gl = ll
    assert bc * lc == gl and d == A_HEADS * HEAD_DIM
    ng = 1 + bl
    t_all = ng * gl
    x = jnp.concatenate([x_prompt.reshape(gl, d), x_sample.reshape(bl * gl, d)], axis=0)
    cvec = jnp.concatenate([c_ctx[None, :], c], axis=0)
    mods = modulation_all(cvec, mod_w, mod_b)
    rc = functools.partial(row_call, group_len=gl)
    row2 = lambda v: v.reshape(1, -1)
    n_ctx_tiles = gl // ROW_TILE

    def post_mixer(i, att_ctx, att_lat, wo, x_in):
        unused = LANES - N_GROUPS - N_EXPERTS
        wr = jnp.concatenate([moe_wr_g[i], moe_wr_e[i], jnp.zeros((d, unused), F32)], axis=1)
        br = jnp.concatenate([moe_br_g[i], moe_br_e[i], jnp.zeros((unused,), F32)])[None, :]

        def body(attc_ref, attl_ref, x_ref, mod_ref, wo_ref, g_ref, b_ref, wr_ref, br_ref, x1_ref, info_ref, w_ref, seen_ref):
            @pl.when(pl.program_id(0) == 0)
            def _():
                seen_ref[...] = jnp.zeros_like(seen_ref)

            m = mod_ref[0]
            att = jnp.where(pl.program_id(0) < n_ctx_tiles, attc_ref[...], attl_ref[...])
            o = _dot(att.astype(BF16), wo_ref[...])
            x1 = _layer_norm(DEEPNORM_ALPHA * x_ref[...] + m[2:3] * o, g_ref[...], b_ref[...])
            x1_ref[...] = x1
            h = x1 * (1.0 + m[4:5]) + m[3:4]
            lane, e1, e2, w1, w2 = _route(_dot3(h, wr_ref[...]) + br_ref[...])
            before, picks = _rank_in_tile(lane, e1, e2)
            before = before + seen_ref[...]
            r1 = jnp.sum(jnp.where(lane == e1, before, 0.0), axis=1, keepdims=True)
            r2 = jnp.sum(jnp.where(lane == e2, before, 0.0), axis=1, keepdims=True)
            seen_ref[...] += jnp.sum(picks, axis=0, keepdims=True)
            info = jnp.where(lane == 0, e1, jnp.where(lane == 1, e2, jnp.where(lane == 2, r1, jnp.where(lane == 3, r2, 0.0))))
            info_ref[...] = info.astype(jnp.int32)
            w_ref[...] = jnp.where(lane == 0, w1, jnp.where(lane == 1, w2, 0.0))

        return rc(body, t_all,
                  [(att_ctx, "row_head"), (att_lat, "row_tail"), (x_in, "row"), (mods[i], "group"), (wo.astype(BF16), "full"),
                   (row2(ln_g[i, 0]), "full"), (row2(ln_b[i, 0]), "full"), (wr, "full"), (br, "full")],
                  [(d, F32), (LANES, jnp.int32), (LANES, F32)],
                  scratch=[pltpu.VMEM((1, LANES), F32)], name=f"post_mixer{i}")

    def moe(i, x1, info, w):
        slot, blk_e, n_used, zero_blk, p_rows = _slot_plan(info[:, 0:2], info[:, 2:4])
        xs = moe_dispatch(x1, mods[i], slot, zero_blk, p_rows, gl, f"moe_dispatch{i}")
        ys = expert_mlp(xs, blk_e, n_used, moe_w_gate[i], moe_w_up[i], moe_w_down[i])
        tm = ROW_TILE

        def body(slot_ref, x_ref, w_ref, mod_ref, g_ref, b_ref, ys_ref, out_ref, buf, sem):
            def start(rr, c):
                for u in range(DMA_UNROLL):
                    r = rr * DMA_UNROLL + u
                    for k in range(2):
                        pltpu.make_async_copy(ys_ref.at[pl.ds(slot_ref[2 * r + k], 1)], buf.at[k, pl.ds(r, 1)], sem).start()
                return c

            lax.fori_loop(0, tm // DMA_UNROLL, start, 0)
            for k in range(2):
                pltpu.make_async_copy(ys_ref.at[pl.ds(0, tm)], buf.at[k], sem).wait()
            m = mod_ref[0]
            wv = w_ref[...]
            y = wv[:, 0:1] * buf[0] + wv[:, 1:2] * buf[1]
            out_ref[...] = _layer_norm(DEEPNORM_ALPHA * x_ref[...] + m[5:6] * y, g_ref[...], b_ref[...])

        return rc(body, t_all,
                  [(slot, "smem_rows"), (x1, "row"), (w, "row"), (mods[i], "group"),
                   (row2(ln_g[i, 1]), "full"), (row2(ln_b[i, 1]), "full"), (ys, "any")],
                  [(d, F32)], scratch=[pltpu.VMEM((2, tm, d), F32), pltpu.SemaphoreType.DMA(())],
                  params=_dma_params(), name=f"moe_combine{i}")[0]

    def finish_layer(i, att_ctx, att_lat, wo, x_in):
        x1, info, w = post_mixer(i, att_ctx, att_lat, wo, x_in)
        return moe(i, x1, info, w)

    tq = 256
    i = 0
    hd = HEAD_DIM
    kvw = A_KV_HEADS * hd
    cq, sq = _rope_tables(gl, hd, 0, 0, LANES // hd)
    wq, wk, wv = a_wq[0], a_wk[0], a_wv[0]

    def body_a(x_ref, mod_ref, wq_ref, wqr_ref, wk_ref, wkr_ref, wv_ref, c_ref, s_ref, q_ref, kh_ref, vh_ref, kc_ref, vc_ref):
        h = _modulate(x_ref, mod_ref, 0, 1).astype(BF16)
        cc, ss = c_ref[0], s_ref[0]
        q = _dot(h, wq_ref[...]) * _tile_lanes(cc, d) + _dot(h, wqr_ref[...]) * _tile_lanes(ss, d)
        k = _dot(h, wk_ref[...]) * _tile_lanes(cc, kvw) + _dot(h, wkr_ref[...]) * _tile_lanes(ss, kvw)
        v = _dot(h, wv_ref[...])
        q_ref[...] = (q * QSCALE_64).astype(BF16)
        _store_heads(kh_ref, k, hd)
        _store_value_heads(vh_ref, v, hd)

        @pl.when(pl.program_id(0) < n_ctx_tiles)
        def _():
            kc_ref[...] = k
            vc_ref[...] = v

    q, kh, vh, k_new, v_new = rc(
        body_a, t_all,
        [(x, "row"), (mods[i], "group"), (wq.astype(BF16), "full"), (_rot_cols(wq, hd).astype(BF16), "full"),
         (wk.astype(BF16), "full"), (_rot_cols(wk, hd).astype(BF16), "full"), (wv.astype(BF16), "full"),
         (cq, "pos"), (sq, "pos")],
        [(d, BF16), (A_KV_HEADS, hd, BF16, "heads"), (A_KV_HEADS, V_LANES, BF16, "heads"),
         (gl, kvw, F32, "ctx"), (gl, kvw, F32, "ctx")], name="proj_a")
    out_a_k = k_new.reshape(bc, 1, lc, A_KV_HEADS, hd)
    out_a_v = v_new.reshape(bc, 1, lc, A_KV_HEADS, hd)
    common = dict(n_heads=A_HEADS, dq=hd, dv=hd, hpb=4, kvpb=1, tq=tq, sink=a_sink[0])
    att_c = attention(q, kh, vh, n_seq=bc, lq=lc, row_off=0, win=lc, name="attn_a_ctx", **common)
    win_a = 2 * tq
    nq_l = gl // tq
    qi = jnp.arange(tq)[:, None]
    ki = jnp.arange(win_a)[None, :]
    band = jnp.stack([jnp.where(jnp.abs(ki - (qi + off)) <= A_WINDOW, 0.0, NEG_INF)
                      for off in (0, A_WINDOW, 2 * A_WINDOW)]).astype(F32)[None]
    att_l = attention(q, kh, vh, n_seq=bl, lq=gl, row_off=gl, win=win_a,
                      start_fn=lambda ii: jnp.clip(ii * tq - A_WINDOW, 0, gl - win_a),
                      kc=_cache_heads_major(cache_a_k[:, 0]), vc=_cache_heads_major(cache_a_v[:, 0], values=True),
                      bias=band, type_fn=lambda ii: jnp.where(ii == 0, 0, jnp.where(ii == nq_l - 1, 2, 1)),
                      name="attn_a_lat", **common)
    x = finish_layer(i, att_c, att_l, a_wo[0], x)

    i = 1
    hq = B_NOPE + B_ROPE
    qw = B_HEADS * B_QPAD
    wqb = b_wq_b[0].reshape(B_Q_RANK, B_HEADS, hq)
    wqb_rot = jnp.concatenate([jnp.zeros_like(wqb[..., :B_NOPE]),
                               _rot_cols(wqb[..., B_NOPE:].reshape(B_Q_RANK, -1), B_ROPE).reshape(B_Q_RANK, B_HEADS, B_ROPE)],
                              axis=-1)
    padq = lambda wz: jnp.pad(wz, ((0, 0), (0, 0), (0, B_QPAD - hq))).reshape(B_Q_RANK, qw).astype(BF16)
    wkv_c, wkv_r = b_wkv_a[0][:, :B_KV_RANK], b_wkv_a[0][:, B_KV_RANK:]
    cqb, sqb = _rope_tables(gl, B_ROPE, B_NOPE, B_QPAD - hq, 1)
    ckr, skr = _rope_tables(gl, B_ROPE, 0, 0, 1)

    def body_b(x_ref, mod_ref, wqa_ref, qn_ref, wqb_ref, wqbr_ref, wc_ref, kn_ref, wr_ref, wrr_ref, wkb_ref, wvb_ref,
               cq_ref, sq_ref, ck_ref, sk_ref, q_ref, kh_ref, vh_ref, ckv_ref, kr_ref):
        h = _modulate(x_ref, mod_ref, 0, 1).astype(BF16)
        qa = _rms(_dot(h, wqa_ref[...]), qn_ref[...]).astype(BF16)
        q = (_dot(qa, wqb_ref[...]) * _tile_lanes(cq_ref[0], qw) + _dot(qa, wqbr_ref[...]) * _tile_lanes(sq_ref[0], qw))
        q_ref[...] = (q * QSCALE_MLA).astype(BF16)
        ckv = _rms(_dot(h, wc_ref[...]), kn_ref[...])
        kr = _dot(h, wr_ref[...]) * ck_ref[0] + _dot(h, wrr_ref[...]) * sk_ref[0]
        cb = ckv.astype(BF16)
        _store_mla_keys(kh_ref, _dot(cb, wkb_ref[...]), kr)
        _store_value_heads(vh_ref, _dot(cb, wvb_ref[...]), B_VDIM)

        @pl.when(pl.program_id(0) < n_ctx_tiles)
        def _():
            ckv_ref[...] = ckv
            kr_ref[...] = kr

    wkb, wvb = b_wk_b[0].astype(BF16), b_wv_b[0].astype(BF16)
    q, kh, vh, ckv_new, kr_new = rc(
        body_b, t_all,
        [(x, "row"), (mods[i], "group"), (b_wq_a[0].astype(BF16), "full"), (row2(b_q_norm[0]), "full"),
         (padq(wqb), "full"), (padq(wqb_rot), "full"), (wkv_c.astype(BF16), "full"), (row2(b_kv_norm[0]), "full"),
         (wkv_r.astype(BF16), "full"), (_rot_cols(wkv_r, B_ROPE).astype(BF16), "full"), (wkb, "full"), (wvb, "full"),
         (cqb, "pos"), (sqb, "pos"), (ckr, "pos"), (skr, "pos")],
        [(qw, BF16), (B_HEADS, B_QPAD, BF16, "heads"), (B_HEADS, V_LANES, BF16, "heads"),
         (gl, B_KV_RANK, F32, "ctx"), (gl, B_ROPE, F32, "ctx")], name="proj_b")
    out_b_ckv = ckv_new.reshape(bc, 1, lc, B_KV_RANK)
    out_b_krope = kr_new.reshape(bc, 1, lc, B_ROPE)

    def body_bc(c_ref, r_ref, wkb_ref, wvb_ref, kh_ref, vh_ref):
        cb = c_ref[...].astype(BF16)
        _store_mla_keys(kh_ref, _dot(cb, wkb_ref[...]), r_ref[...])
        _store_value_heads(vh_ref, _dot(cb, wvb_ref[...]), B_VDIM)

    n_pc = bl * past
    kh_p, vh_p = row_call(body_bc, n_pc,
                          [(cache_b_ckv[:, 0].reshape(n_pc, B_KV_RANK), "row"), (cache_b_krope[:, 0].reshape(n_pc, B_ROPE), "row"),
                           (wkb, "full"), (wvb, "full")],
                          [(B_HEADS, B_QPAD, BF16, "heads"), (B_HEADS, V_LANES, BF16, "heads")],
                          tm=min(ROW_TILE, n_pc), name="proj_b_past")
    common = dict(n_heads=B_HEADS, dq=B_QPAD, dv=B_VDIM, hpb=2, kvpb=2, tq=tq)
    att_c = attention(q, kh, vh, n_seq=bc, lq=lc, row_off=0, win=lc, name="attn_b_ctx", **common)
    att_l = attention(q, kh, vh, n_seq=bl, lq=gl, row_off=gl, win=gl, kc=kh_p, vc=vh_p, name="attn_b_lat", **common)
    x = finish_layer(i, att_c, att_l, b_wo[0], x)

    i = 2
    d3 = 3 * d

    def body_c(x_ref, mod_ref, w_ref, b_ref, u_ref):
        h = _modulate(x_ref, mod_ref, 0, 1).astype(BF16)
        for j in range(3):
            u_ref[:, j * d:(j + 1) * d] = _dot(h, w_ref[:, j * d:(j + 1) * d]) + b_ref[:, j * d:(j + 1) * d]

    (u,) = rc(body_c, t_all, [(x, "row"), (mods[i], "group"), (c_w_in[0].astype(BF16), "full"), (row2(c_b_in[0]), "full")],
              [(d3, F32)], name="proj_c")
    u_c = short_conv(u, c_conv_w[0], row2(c_conv_b[0]), row_off=0, n_seq=bc, length=lc, parts=3, name="short_conv_ctx")
    u_l = short_conv(u, c_conv_w[0], row2(c_conv_b[0]), row_off=gl, n_seq=bl, length=gl, parts=3, name="short_conv_lat")
    fargs = (c_ffn_w1[0], c_ffn_b1[0], c_ffn_w2[0], c_ffn_b2[0], c_ffn_w3[0], c_ffn_freq[0], c_log_decay[0])
    g_c = hyena_two_sided_filters(lc, *fargs, name="filters_ctx")
    g_l = hyena_two_sided_filters(gl, *fargs, name="filters_lat")

    nc = 2 * lc
    mf_data, mf_filt, mi_c = _dft_mats_single(nc, lc, lc)
    spec_c = spectral(g_c.reshape(2, 1, 1, nc, d), mf_filt, dt=512, name="filt_spec_ctx")
    z_c = u_c[0].reshape(bc // 2, 2, 1, lc, d)
    for o in range(2):
        gate = u_c[o + 1].reshape(bc // 2, 2, 1, lc, d)
        z_c = spectral(z_c, mf_data, spec_c, mi_c, dt=512, g_index=o, epi=(gate, z_c, row2(c_skip[0, o])),
                       name=f"conv_ctx{o}")
    zc_out = z_c.reshape(gl, d)

    nl = 2 * gl
    n1 = nl // FFT_N2
    a_data, a_filt, a_inv, mf_l, mi_l = _dft_mats_two_stage(nl, FFT_N2)
    ga = slow_stage(a_filt, g_l.reshape(2, n1, FFT_N2, d), name="filt_stage_a")
    spec_l = spectral(ga.reshape(2, 2, n1, FFT_N2, d), mf_l, dt=1024, name="filt_spec_lat")
    z_l = u_l[0].reshape(bl // 2, n1, FFT_N2, d)
    for o in range(2):
        za = slow_stage(a_data, z_l, name=f"conv_lat_a{o}")
        zb = spectral(za.reshape(bl // 2, 2, n1, FFT_N2, d), mf_l, spec_l, mi_l, dt=1024, g_index=o,
                      name=f"conv_lat_c{o}")
        gate = u_l[o + 1].reshape(bl // 2, n1, FFT_N2, d)
        z_l = slow_stage(a_inv, zb.reshape(bl // 2, 2 * n1, FFT_N2, d), epi=(gate, z_l, row2(c_skip[0, o])),
                         name=f"conv_lat_i{o}")
    x = finish_layer(i, zc_out, z_l.reshape(bl * gl, d), c_wo[0], x)

    i = 3

    def body_d(x_ref, mod_ref, wq_ref, wk_ref, wv_ref, q_ref, kh_ref, vh_ref, kc_ref, vc_ref):
        h = _modulate(x_ref, mod_ref, 0, 1).astype(BF16)
        q_ref[...] = (_dot(h, wq_ref[...]) * QSCALE_64).astype(BF16)
        k = _dot(h, wk_ref[...])
        v = _dot(h, wv_ref[...])
        _store_heads(kh_ref, k, hd)
        _store_value_heads(vh_ref, v, hd)

        @pl.when(pl.program_id(0) < n_ctx_tiles)
        def _():
            kc_ref[...] = k
            vc_ref[...] = v

    q, k_heads, v_heads, k_new, v_new = rc(
        body_d, t_all,
        [(x, "row"), (mods[i], "group"), (d_wq[0].astype(BF16), "full"), (d_wk[0].astype(BF16), "full"),
         (d_wv[0].astype(BF16), "full")],
        [(d, BF16), (D_HEADS, hd, BF16, "heads"), (D_HEADS, V_LANES, BF16, "heads"), (gl, d, F32, "ctx"), (gl, d, F32, "ctx")],
        name="proj_d")
    out_d_k = k_new.reshape(bc, 1, lc, D_HEADS, hd)
    out_d_v = v_new.reshape(bc, 1, lc, D_HEADS, hd)
    common = dict(n_heads=D_HEADS, dq=hd, dv=hd, hpb=2, kvpb=2, tq=tq)
    att_c = attention(q, k_heads, v_heads, n_seq=bc, lq=lc, row_off=0, win=lc, name="attn_d_ctx", **common)
    rows = gl // GRID_W
    kh = min(MAX_NBR_ROWS, rows)
    qrows = tq // GRID_W
    krows = qrows + kh
    win_d = krows * GRID_W
    nq_l = gl // tq
    qr_l, kr_l = jnp.arange(qrows)[:, None], jnp.arange(krows)[None, :]
    qc, kc_ = jnp.arange(GRID_W)[:, None], jnp.arange(GRID_W)[None, :]
    c0 = jnp.clip(qc - NBR_COLS // 2, 0, GRID_W - NBR_COLS)
    col_ok = (kc_ >= c0) & (kc_ < c0 + NBR_COLS)
    dc = jnp.clip(kc_ - qc, 1 - NBR_COLS, NBR_COLS - 1) + NBR_COLS - 1
    dc_hot = (dc[..., None] == jnp.arange(2 * NBR_COLS - 1)).astype(F32)
    by_col = jnp.einsum("hrc,xyc->hrxy", d_rel_bias[0], dc_hot, precision=lax.Precision.HIGHEST)
    tabs = []
    for off, lo in ((0, jnp.zeros_like(qr_l)), (kh // 2, qr_l), (kh, jnp.full_like(qr_l, qrows))):
        row_ok = (kr_l >= lo) & (kr_l < lo + kh)
        dr = jnp.clip(kr_l - off - qr_l + MAX_NBR_ROWS - 1, 0, 2 * MAX_NBR_ROWS - 2)
        dr_hot = (dr[..., None] == jnp.arange(2 * MAX_NBR_ROWS - 1)).astype(F32)
        tab = jnp.einsum("qkr,hrxy->hqxky", dr_hot, by_col, precision=lax.Precision.HIGHEST)
        ok = row_ok[:, None, :, None] & col_ok[None, :, None, :]
        tabs.append(jnp.where(ok[None], tab, NEG_INF).reshape(D_HEADS, tq, win_d))
    nbr_bias = jnp.stack(tabs, axis=1).astype(F32)
    att_l = attention(q, k_heads, v_heads, n_seq=bl, lq=gl, row_off=gl, win=win_d,
                      start_fn=lambda ii: jnp.clip(ii * qrows - kh // 2, 0, rows - krows) * GRID_W,
                      kc=_cache_heads_major(cache_d_k[:, 0]), vc=_cache_heads_major(cache_d_v[:, 0], values=True),
                      bias=nbr_bias, type_fn=lambda ii: jnp.where(ii == 0, 0, jnp.where(ii == nq_l - 1, 2, 1)),
                      name="attn_d_lat", **common)
    x = finish_layer(i, att_c, att_l, d_wo[0], x)

    y_prompt = x[:gl].reshape(bc, lc, d)
    y_sample = x[gl:].reshape(bl, gl, d)
    return (y_prompt, y_sample, out_a_k, out_a_v, out_b_ckv, out_b_krope, out_d_k, out_d_v)
```

```python
import functools

import jax
import jax.numpy as jnp
import numpy as np
from jax import lax
from jax.experimental import pallas as pl
from jax.experimental.pallas import tpu as pltpu

F32 = jnp.float32
BF16 = jnp.bfloat16

GRID_W = 64
HEAD_DIM = 64
ROPE_BASE = 10000.0
LN_EPS = 1e-5
RMS_EPS = 1e-6
NEG_INF = -1e30
DEPTH = 4
DEEPNORM_ALPHA = (2 * DEPTH) ** 0.25
A_HEADS = 16
A_KV_HEADS = 4
A_WINDOW = 128
B_HEADS = 16
B_Q_RANK = 384
B_KV_RANK = 256
B_NOPE = 64
B_ROPE = 32
B_VDIM = 64
B_QPAD = 128
C_POS_BANDS = 16
D_HEADS = 16
MAX_NBR_ROWS = 8
NBR_COLS = 16
N_GROUPS = 4
EXPERTS_PER_GROUP = 8
N_EXPERTS = N_GROUPS * EXPERTS_PER_GROUP
D_EXPERT = 512
MOE_BLK = 512
LANES = 128
V_LANES = 128
LOG2E = 1.4426950408889634
QSCALE_64 = HEAD_DIM ** -0.5 * LOG2E
QSCALE_MLA = (B_NOPE + B_ROPE) ** -0.5 * LOG2E
FFT_N2 = 128
VMEM_LIMIT = 56 * 1024 * 1024
ROW_TILE = 512


def _params(n_axes):
    return pltpu.CompilerParams(dimension_semantics=("arbitrary",) * n_axes, vmem_limit_bytes=VMEM_LIMIT)


def _dot(a, b):
    return jnp.dot(a, b, preferred_element_type=F32)


def _split(x):
    hi = x.astype(BF16)
    lo = (x - hi.astype(F32)).astype(BF16)
    return hi, lo


def _dot3(a, b):
    ah, al = _split(a)
    bh, bl = _split(b)
    return _dot(ah, bh) + _dot(ah, bl) + _dot(al, bh)


def _layer_norm(y, g, b):
    mu = jnp.mean(y, axis=-1, keepdims=True)
    d = y - mu
    var = jnp.mean(d * d, axis=-1, keepdims=True)
    return d * lax.rsqrt(var + LN_EPS) * g + b


def _rms(y, g):
    return y * lax.rsqrt(jnp.mean(y * y, axis=-1, keepdims=True) + RMS_EPS) * g


def _tile_lanes(t, n):
    reps = n // t.shape[-1]
    return t if reps == 1 else jnp.concatenate([t] * reps, axis=-1)


def row_call(body, m_rows, ins, outs, *, tm=ROW_TILE, group_len=None, name=None, scratch=(), params=None):
    nb = None if group_len is None else group_len // tm
    n_tiles = m_rows // tm
    in_specs = []
    for a, kind in ins:
        if kind == "row":
            in_specs.append(pl.BlockSpec((tm, a.shape[1]), lambda i: (i, 0)))
        elif kind == "row_head":
            nh = a.shape[0] // tm
            in_specs.append(pl.BlockSpec((tm, a.shape[1]), lambda i, nh=nh: (jnp.minimum(i, nh - 1), 0)))
        elif kind == "row_tail":
            nh = n_tiles - a.shape[0] // tm
            in_specs.append(pl.BlockSpec((tm, a.shape[1]), lambda i, nh=nh: (jnp.maximum(i - nh, 0), 0)))
        elif kind == "any":
            in_specs.append(pl.BlockSpec(memory_space=pl.ANY))
        elif kind == "smem":
            in_specs.append(pl.BlockSpec(memory_space=pltpu.SMEM))
        elif kind == "smem_rows":
            in_specs.append(pl.BlockSpec((a.shape[0] // n_tiles,), lambda i: (i,), memory_space=pltpu.SMEM))
        elif kind == "full":
            in_specs.append(pl.BlockSpec(a.shape, lambda i, nd=a.ndim: (0,) * nd))
        elif kind == "group":
            in_specs.append(pl.BlockSpec((1,) + a.shape[1:], lambda i: (i // nb, 0, 0)))
        elif kind == "pos":
            in_specs.append(pl.BlockSpec((1, tm, a.shape[2]), lambda i: (jnp.minimum(i // nb, 1), i % nb, 0)))
        else:
            raise ValueError(kind)
    out_specs, out_shape = [], []
    for o in outs:
        if len(o) == 4 and o[3] == "any":
            out_specs.append(pl.BlockSpec(memory_space=pl.ANY))
            out_shape.append(jax.ShapeDtypeStruct((o[0], o[1]), o[2]))
        elif len(o) == 4 and o[3] == "heads":
            out_specs.append(pl.BlockSpec((o[0], tm, o[1]), lambda i: (0, i, 0)))
            out_shape.append(jax.ShapeDtypeStruct((o[0], m_rows, o[1]), o[2]))
        elif len(o) == 4 and o[3] == "ctx":
            nkeep = o[0] // tm
            out_specs.append(pl.BlockSpec((tm, o[1]), lambda i, nkeep=nkeep: (jnp.minimum(i, nkeep - 1), 0)))
            out_shape.append(jax.ShapeDtypeStruct((o[0], o[1]), o[2]))
        else:
            out_specs.append(pl.BlockSpec((tm, o[0]), lambda i: (i, 0)))
            out_shape.append(jax.ShapeDtypeStruct((m_rows, o[0]), o[1]))
    return pl.pallas_call(
        body, grid=(n_tiles,), in_specs=in_specs, out_specs=out_specs, out_shape=out_shape,
        scratch_shapes=list(scratch), compiler_params=params or _params(1), name=name)(*[a for a, _ in ins])


def _modulate(x_ref, mod_ref, shift_row, scale_row):
    m = mod_ref[0]
    return x_ref[...] * (1.0 + m[scale_row:scale_row + 1]) + m[shift_row:shift_row + 1]


def modulation_all(cvec, mod_w, mod_b):
    depth, d, d6 = mod_w.shape
    g = cvec.shape[0]
    gp = -(-g // 16) * 16
    cp = jnp.zeros((gp, d), F32).at[:g].set(cvec)
    tn = 1024

    def body(c_ref, w_ref, b_ref, o_ref):
        c = c_ref[...]
        a = (c * jax.nn.sigmoid(c)).astype(BF16)
        o_ref[0] = _dot(a, w_ref[0].astype(BF16)) + b_ref[0]

    out = pl.pallas_call(
        body, grid=(depth, d6 // tn),
        in_specs=[pl.BlockSpec((gp, d), lambda l, j: (0, 0)),
                  pl.BlockSpec((1, d, tn), lambda l, j: (l, 0, j)),
                  pl.BlockSpec((1, 1, tn), lambda l, j: (l, 0, j))],
        out_specs=pl.BlockSpec((1, gp, tn), lambda l, j: (l, 0, j)),
        out_shape=jax.ShapeDtypeStruct((depth, gp, d6), F32),
        compiler_params=_params(2), name="modulation")(cp, mod_w, mod_b.reshape(depth, 1, d6))
    return out[:, :g].reshape(depth, g, 6, d)


def attention(q, kl, vl, *, n_seq, lq, row_off, n_heads, dq, dv, hpb, kvpb, tq, win,
              start_fn=None, kc=None, vc=None, bias=None, type_fn=None, sink=None, name=None):
    nq = lq // tq
    koff = row_off // lq
    rep = hpb // kvpb
    has_ctx, has_bias, has_sink = kc is not None, bias is not None, sink is not None
    bias_heads = has_bias and bias.shape[0] > 1
    off_blk = row_off // tq

    def kern(*refs):
        it = iter(refs)
        q_ref, kl_ref, vl_ref = next(it), next(it), next(it)
        kc_ref = next(it) if has_ctx else None
        vc_ref = next(it) if has_ctx else None
        b_ref = next(it) if has_bias else None
        s_ref = next(it) if has_sink else None
        o_ref = next(it)
        i = pl.program_id(2)
        hb = pl.program_id(1)
        if start_fn is None:
            start = 0
        else:
            start = pl.multiple_of(start_fn(i), 64)
        nt = (((1,), (1,)), ((), ()))
        scores = []
        for j in range(hpb):
            kv = j // rep
            qj = q_ref[:, j * dq:(j + 1) * dq]
            s = lax.dot_general(qj, kl_ref[kv, pl.ds(start, win), :], nt, preferred_element_type=F32)
            if has_bias:
                s = s + b_ref[j if bias_heads else 0, 0]
            sc = lax.dot_general(qj, kc_ref[kv], nt, preferred_element_type=F32) if has_ctx else None
            scores.append((s, sc))
        outs = []
        for j in range(hpb):
            kv = j // rep
            s, sc = scores[j]
            m = jnp.max(s, axis=1, keepdims=True)
            if has_ctx:
                m = jnp.maximum(m, jnp.max(sc, axis=1, keepdims=True))
            if has_sink:
                sk = s_ref[hb * hpb + j]
                m = jnp.maximum(m, sk)
            acc = _dot(jnp.exp2(s - m).astype(BF16), vl_ref[kv, pl.ds(start, win), :])
            if has_ctx:
                acc = acc + _dot(jnp.exp2(sc - m).astype(BF16), vc_ref[kv])
            l = acc[:, dv:dv + 1]
            if has_sink:
                l = l + jnp.exp2(sk - m)
            outs.append(acc[:, :dv] / l)
        o_ref[...] = jnp.concatenate(outs, axis=1).astype(o_ref.dtype)

    ins = [q, kl, vl]
    in_specs = [pl.BlockSpec((tq, hpb * dq), lambda b, h, i: (off_blk + b * nq + i, h)),
                pl.BlockSpec((kvpb, lq, dq), lambda b, h, i: (h, koff + b, 0)),
                pl.BlockSpec((kvpb, lq, V_LANES), lambda b, h, i: (h, koff + b, 0))]
    if has_ctx:
        lc = kc.shape[1] // n_seq
        ins += [kc, vc]
        in_specs += [pl.BlockSpec((kvpb, lc, dq), lambda b, h, i: (h, b, 0)),
                     pl.BlockSpec((kvpb, lc, V_LANES), lambda b, h, i: (h, b, 0))]
    if has_bias:
        ins.append(bias * LOG2E)
        hb_blk = hpb if bias_heads else 1
        in_specs.append(pl.BlockSpec((hb_blk, 1, tq, win),
                                     lambda b, h, i: (h if bias_heads else 0, type_fn(i), 0, 0)))
    if has_sink:
        ins.append(sink.astype(F32) * LOG2E)
        in_specs.append(pl.BlockSpec(memory_space=pltpu.SMEM))
    return pl.pallas_call(
        kern, grid=(n_seq, n_heads // hpb, nq), in_specs=in_specs,
        out_specs=pl.BlockSpec((tq, hpb * dv), lambda b, h, i: (b * nq + i, h)),
        out_shape=jax.ShapeDtypeStruct((n_seq * lq, n_heads * dv), BF16),
        compiler_params=_params(3), name=name)(*ins)


def _cache_heads_major(c, values=False):
    n_seq, length, n_heads, d = c.shape
    out = c.astype(BF16).transpose(2, 0, 1, 3).reshape(n_heads, n_seq * length, d)
    if values:
        rows = n_seq * length
        out = jnp.concatenate([out, jnp.ones((n_heads, rows, 1), BF16),
                               jnp.zeros((n_heads, rows, V_LANES - d - 1), BF16)], axis=-1)
    return out


def _store_value_heads(ref, val, width):
    rows = val.shape[0]
    lane = lax.broadcasted_iota(jnp.int32, (rows, V_LANES - width), 1)
    tail = jnp.where(lane == 0, 1.0, 0.0).astype(ref.dtype)
    for hh in range(ref.shape[0]):
        ref[hh] = jnp.concatenate([val[:, hh * width:(hh + 1) * width].astype(ref.dtype), tail], axis=1)


def _store_mla_keys(ref, k_nope, k_rope):
    rows = k_nope.shape[0]
    tail = jnp.concatenate([k_rope, jnp.zeros((rows, B_QPAD - B_NOPE - B_ROPE), k_rope.dtype)], axis=1)
    for hh in range(ref.shape[0]):
        ref[hh] = jnp.concatenate([k_nope[:, hh * B_NOPE:(hh + 1) * B_NOPE], tail], axis=1).astype(ref.dtype)


def _store_heads(ref, val, width):
    for hh in range(ref.shape[0]):
        ref[hh] = val[:, hh * width:(hh + 1) * width].astype(ref.dtype)


def _rope_tables(length, dim, lead, tail, reps):
    half = dim // 2
    nf = half // 2
    t = jnp.arange(length)
    row = (t // GRID_W).astype(F32)
    col = (t % GRID_W).astype(F32)
    inv = ROPE_BASE ** (-jnp.arange(nf, dtype=F32) / nf)
    ang = jnp.concatenate([row[:, None] * inv, col[:, None] * inv], axis=-1)
    cos, sin = jnp.cos(ang), jnp.sin(ang)
    c = jnp.concatenate([jnp.ones((length, lead), F32), cos, cos, jnp.ones((length, tail), F32)] * reps, axis=1)
    s = jnp.concatenate([jnp.zeros((length, lead), F32), sin, sin, jnp.zeros((length, tail), F32)] * reps, axis=1)
    return (jnp.stack([jnp.ones_like(c), c]), jnp.stack([jnp.zeros_like(s), s]))


def _rot_cols(w, dim):
    k, n = w.shape
    wb = w.reshape(k, n // dim, dim)
    half = dim // 2
    return jnp.concatenate([-wb[..., half:], wb[..., :half]], axis=-1).reshape(k, n)


def _cs(phase, n, sign):
    ang = (2.0 * np.pi / n) * (phase % n).astype(F32)
    return jnp.cos(ang), sign * jnp.sin(ang)


def _cblock(wr, wi):
    return jnp.concatenate([jnp.concatenate([wr, -wi], axis=-1), jnp.concatenate([wi, wr], axis=-1)], axis=-2)


SUB = 8


def slow_stage(a, x, *, epi=None, name=None):
    p_n, k, s_n, d = x.shape
    m = a.shape[0]
    x5 = x.reshape(p_n, k, s_n // SUB, SUB, d)

    def kern(*refs):
        if epi is None:
            a_ref, x_ref, o_ref = refs
        else:
            a_ref, x_ref, g_ref, z_ref, s_ref, o_ref = refs
        xt = pltpu.einshape("ksd->skd", x_ref[...])
        av = a_ref[...]
        y = jnp.stack([_dot3(av, xt[s]) for s in range(SUB)], axis=0)
        y = pltpu.einshape("smd->msd", y)
        if epi is not None:
            y = g_ref[...] * (y + s_ref[...][None] * z_ref[...])
        o_ref[...] = y

    blk = lambda rows: pl.BlockSpec((None, rows, None, SUB, d), lambda p, t: (p, 0, t, 0, 0))
    ins = [a, x5]
    in_specs = [pl.BlockSpec((m, k), lambda p, t: (0, 0)), blk(k)]
    if epi is not None:
        gate, z, skip = epi
        ins += [gate.reshape(p_n, m, s_n // SUB, SUB, d), z.reshape(p_n, m, s_n // SUB, SUB, d), skip]
        in_specs += [blk(m), blk(m), pl.BlockSpec((1, d), lambda p, t: (0, 0))]
    out = pl.pallas_call(
        kern, grid=(p_n, s_n // SUB), in_specs=in_specs, out_specs=blk(m),
        out_shape=jax.ShapeDtypeStruct((p_n, m, s_n // SUB, SUB, d), F32),
        compiler_params=_params(2), name=name)(*ins)
    return out.reshape(p_n, m, s_n, d)


def spectral(x, mf, g=None, mi=None, *, dt, g_index=0, epi=None, name=None):
    p_n, planes, k1_n, nin, d = x.shape
    nf = mf.shape[1] // 2
    nout = nf if g is None else mi.shape[1] // 2

    def kern(*refs):
        it = iter(refs)
        x_ref, mf_ref = next(it), next(it)
        g_ref = next(it) if g is not None else None
        mi_ref = next(it) if g is not None else None
        if epi is not None:
            gate_ref, z_ref, skip_ref = next(it), next(it), next(it)
        o_ref = next(it)
        xs = [x_ref[0, pp, 0] for pp in range(planes)]
        xin = xs[0] if planes == 1 else jnp.concatenate(xs, axis=0)
        f = _dot3(mf_ref[0], xin)
        if g is not None:
            fr, fi = f[:nf], f[nf:]
            gr, gi = g_ref[0, 0, 0], g_ref[0, 1, 0]
            y = jnp.concatenate([fr * gr - fi * gi, fr * gi + fi * gr], axis=0)
            f = _dot3(mi_ref[0], y)
        for pp in range(2):
            y = f[pp * nout:(pp + 1) * nout]
            if epi is not None:
                y = gate_ref[0, pp, 0] * (y + skip_ref[...] * z_ref[0, pp, 0])
            o_ref[0, pp, 0] = y

    ins = [x, mf]
    in_specs = [pl.BlockSpec((1, planes, 1, nin, dt), lambda k, j, p: (p, 0, k, 0, j)),
                pl.BlockSpec((1,) + mf.shape[1:], lambda k, j, p: (k, 0, 0))]
    if g is not None:
        ins += [g, mi]
        in_specs += [pl.BlockSpec((1, 2, 1, nf, dt), lambda k, j, p: (g_index, 0, k, 0, j)),
                     pl.BlockSpec((1,) + mi.shape[1:], lambda k, j, p: (k, 0, 0))]
    if epi is not None:
        ins += list(epi)
        in_specs += [pl.BlockSpec((1, 2, 1, nout, dt), lambda k, j, p: (p, 0, k, 0, j)),
                     pl.BlockSpec((1, 2, 1, nout, dt), lambda k, j, p: (p, 0, k, 0, j)),
                     pl.BlockSpec((1, dt), lambda k, j, p: (0, j))]
    return pl.pallas_call(
        kern, grid=(k1_n, d // dt, p_n), in_specs=in_specs,
        out_specs=pl.BlockSpec((1, 2, 1, nout, dt), lambda k, j, p: (p, 0, k, 0, j)),
        out_shape=jax.ShapeDtypeStruct((p_n, 2, k1_n, nout, d), F32),
        compiler_params=_params(3), name=name)(*ins)


def hyena_two_sided_filters(length, w1, b1, w2, b2, w3, freq, log_decay, name):
    hid = w2.shape[0]
    d = w3.shape[1] // 4
    t = jnp.linspace(0.0, 1.0, length, dtype=F32)[:, None]
    ang = 2.0 * jnp.pi * t * jnp.arange(1, C_POS_BANDS + 1, dtype=F32)
    z = jnp.concatenate([t, jnp.cos(ang), jnp.sin(ang)], axis=-1)
    kpad = LANES - z.shape[1]
    z = jnp.pad(z, ((0, 0), (0, kpad)))
    z2 = jnp.concatenate([z, z[::-1]], axis=0)
    w1p = jnp.pad(w1, ((0, kpad), (0, 0)))
    full = lambda a: pl.BlockSpec(a.shape, lambda *_: (0,) * a.ndim)

    def ffn_kern(z_ref, w1_ref, b1_ref, w2_ref, b2_ref, f_ref, a_ref):
        fr = f_ref[...]
        a = jnp.sin(fr * (_dot3(z_ref[...], w1_ref[...]) + b1_ref[...]))
        a_ref[...] = jnp.sin(fr * (_dot3(a, w2_ref[...]) + b2_ref[...]))

    args = [z2, w1p, b1.reshape(1, hid), w2, b2.reshape(1, hid), freq.reshape(1, hid)]
    act = pl.pallas_call(
        ffn_kern, grid=(1,), in_specs=[full(a) for a in args], out_specs=pl.BlockSpec((2 * length, hid), lambda i: (0, 0)),
        out_shape=jax.ShapeDtypeStruct((2 * length, hid), F32), compiler_params=_params(1), name=name + "_ffn")(*args)

    tn = 256
    per = d // tn

    def kern(af_ref, ar_ref, w3f_ref, w3b_ref, ldf_ref, ldb_ref, o_ref):
        row = lax.broadcasted_iota(jnp.int32, (length, tn), 0)
        pos = row.astype(F32) * (1.0 / (length - 1))
        pos_rev = (length - 1 - row).astype(F32) * (1.0 / (length - 1))

        def filt(a_ref, w_ref, ld_ref, tt):
            f = _dot3(a_ref[...], w_ref[...]) * jnp.exp(-jnp.exp(ld_ref[...]) * tt)
            return f / (jnp.sum(jnp.abs(f), axis=0, keepdims=True) + 1e-6)

        hf = filt(af_ref, w3f_ref, ldf_ref, pos)
        hb_rev = filt(ar_ref, w3b_ref, ldb_ref, pos_rev)
        o_ref[0, 0] = hf + jnp.where(row == 0, hb_rev[length - 1:length], 0.0)
        o_ref[0, 1] = jnp.where(row == 0, 0.0, pltpu.roll(hb_rev, 1, 0))

    col = lambda direction: (lambda o, j: (0, (2 * o + direction) * per + j))
    ld = log_decay.reshape(1, 4 * d)
    g = pl.pallas_call(
        kern, grid=(2, per),
        in_specs=[pl.BlockSpec((length, hid), lambda o, j: (0, 0)), pl.BlockSpec((length, hid), lambda o, j: (1, 0)),
                  pl.BlockSpec((hid, tn), col(0)), pl.BlockSpec((hid, tn), col(1)),
                  pl.BlockSpec((1, tn), col(0)), pl.BlockSpec((1, tn), col(1))],
        out_specs=pl.BlockSpec((1, 2, length, tn), lambda o, j: (o, 0, 0, j)),
        out_shape=jax.ShapeDtypeStruct((2, 2, length, d), F32),
        compiler_params=_params(2), name=name)(act, act, w3, w3, ld, ld)
    return g.reshape(2, 2 * length, d)


def short_conv(u, w, b, *, row_off, n_seq, length, parts, name):
    c = u.shape[1]
    dt = 256
    per = c // parts // dt
    first = row_off // length

    def kern(u_ref, w_ref, b_ref, *o_refs):
        x = u_ref[...]
        r = lax.broadcasted_iota(jnp.int32, x.shape, 0)
        prev = jnp.where(r == 0, 0.0, pltpu.roll(x, 1, 0))
        nxt = jnp.where(r == length - 1, 0.0, pltpu.roll(x, length - 1, 0))
        wv = w_ref[...]
        y = prev * wv[0:1] + x * wv[1:2] + nxt * wv[2:3] + b_ref[...]
        part = pl.program_id(1) // per
        for k, o_ref in enumerate(o_refs):
            @pl.when(part == k)
            def _(o_ref=o_ref):
                o_ref[0] = y

    out_spec = lambda k: pl.BlockSpec((1, length, dt), lambda s, j: (s, 0, jnp.clip(j - k * per, 0, per - 1)))
    return pl.pallas_call(
        kern, grid=(n_seq, c // dt),
        in_specs=[pl.BlockSpec((length, dt), lambda s, j: (first + s, j)),
                  pl.BlockSpec((3, dt), lambda s, j: (0, j)),
                  pl.BlockSpec((1, dt), lambda s, j: (0, j))],
        out_specs=[out_spec(k) for k in range(parts)],
        out_shape=[jax.ShapeDtypeStruct((n_seq, length, c // parts), F32)] * parts,
        compiler_params=_params(2), name=name)(u, w, b)


def _dft_mats_single(n, nin_data, nout):
    k = jnp.arange(n, dtype=jnp.int32)
    fr, fi = _cs(k[:, None] * k[None, :nin_data], n, -1.0)
    mf_data = _cblock(fr, fi)[None]
    gr, gi = _cs(k[:, None] * k[None, :], n, -1.0)
    mf_filt = jnp.concatenate([gr, gi], axis=0)[None]
    ir, ii = _cs(k[:nout, None] * k[None, :], n, 1.0)
    mi = _cblock(ir / n, ii / n)[None]
    return mf_data, mf_filt, mi


def _dft_mats_two_stage(n, n2):
    n1 = n // n2
    k1 = jnp.arange(n1, dtype=jnp.int32)
    t1h = jnp.arange(n1 // 2, dtype=jnp.int32)
    ar, ai = _cs((n // n1) * k1[:, None] * t1h[None, :], n, -1.0)
    a_data = _cblock(ar, ai)
    fr, fi = _cs((n // n1) * k1[:, None] * k1[None, :], n, -1.0)
    a_filt = jnp.concatenate([fr, fi], axis=0)
    br, bi = _cs((n // n1) * t1h[:, None] * k1[None, :], n, 1.0)
    a_inv = _cblock(br, bi)
    t2 = jnp.arange(n2, dtype=jnp.int32)
    kk = k1[:, None, None] + n1 * t2[None, :, None]
    mr, mi_ = _cs(kk * t2[None, None, :], n, -1.0)
    mf = _cblock(mr, mi_)
    vr, vi = _cs(jnp.swapaxes(kk, 1, 2) * t2[None, :, None], n, 1.0)
    mi = _cblock(vr / n, vi / n)
    return a_data, a_filt, a_inv, mf, mi


def _route(logits):
    lane = lax.broadcasted_iota(jnp.int32, logits.shape, 1).astype(F32)
    big = 1e9
    lg = jnp.where(lane < N_GROUPS, logits, -jnp.inf)
    mg = jnp.max(lg, axis=1, keepdims=True)
    gi = jnp.min(jnp.where(lg == mg, lane, big), axis=1, keepdims=True)
    p_g = 1.0 / jnp.sum(jnp.exp(lg - mg), axis=1, keepdims=True)
    lo = N_GROUPS + EXPERTS_PER_GROUP * gi
    le = jnp.where((lane >= lo) & (lane < lo + EXPERTS_PER_GROUP), logits, -jnp.inf)
    m1 = jnp.max(le, axis=1, keepdims=True)
    i1 = jnp.min(jnp.where(le == m1, lane, big), axis=1, keepdims=True)
    le2 = jnp.where(lane == i1, -jnp.inf, le)
    m2 = jnp.max(le2, axis=1, keepdims=True)
    i2 = jnp.min(jnp.where(le2 == m2, lane, big), axis=1, keepdims=True)
    e2 = jnp.exp(m2 - m1)
    w1 = p_g / (1.0 + e2)
    w2 = p_g * e2 / (1.0 + e2)
    return lane, i1 - N_GROUPS, i2 - N_GROUPS, w1, w2


def _rank_in_tile(lane, e1, e2):
    tm = lane.shape[0]
    picks = jnp.where(lane == e1, 1.0, 0.0) + jnp.where(lane == e2, 1.0, 0.0)
    r = lax.broadcasted_iota(jnp.int32, (tm, tm), 0)
    c = lax.broadcasted_iota(jnp.int32, (tm, tm), 1)
    earlier = jnp.where(c < r, 1.0, 0.0).astype(BF16)
    return _dot(earlier, picks.astype(BF16)), picks


def expert_mlp(xs, blk_e, n_used, w_gate, w_up, w_down, layer):
    p_rows, d = xs.shape
    de = w_gate.shape[3]
    nblk = p_rows // MOE_BLK

    def kern(be_ref, nu_ref, x_ref, wg_ref, wu_ref, wd_ref, o_ref, wg_s, wu_s, wd_s):
        i = pl.program_id(0)
        used = i < nu_ref[0]
        fresh = jnp.logical_or(i == 0, be_ref[i] != be_ref[jnp.maximum(i - 1, 0)])

        @pl.when(jnp.logical_and(used, fresh))
        def _():
            wg_s[...] = wg_ref[0, 0].astype(BF16)
            wu_s[...] = wu_ref[0, 0].astype(BF16)
            wd_s[...] = wd_ref[0, 0].astype(BF16)

        @pl.when(used)
        def _():
            x = x_ref[...].astype(BF16)
            g = _dot(x, wg_s[...])
            u = _dot(x, wu_s[...])
            a = (g * jax.nn.sigmoid(g) * u).astype(BF16)
            o_ref[...] = _dot(a, wd_s[...])

        @pl.when(jnp.logical_not(used))
        def _():
            o_ref[...] = jnp.zeros_like(o_ref)

    last = lambda i, nu: jnp.minimum(i, nu[0] - 1)
    grid_spec = pltpu.PrefetchScalarGridSpec(
        num_scalar_prefetch=2, grid=(nblk,),
        in_specs=[pl.BlockSpec((MOE_BLK, d), lambda i, be, nu: (last(i, nu), 0)),
                  pl.BlockSpec((1, 1, d, de), lambda i, be, nu: (layer, be[last(i, nu)], 0, 0)),
                  pl.BlockSpec((1, 1, d, de), lambda i, be, nu: (layer, be[last(i, nu)], 0, 0)),
                  pl.BlockSpec((1, 1, de, d), lambda i, be, nu: (layer, be[last(i, nu)], 0, 0))],
        out_specs=pl.BlockSpec((MOE_BLK, d), lambda i, be, nu: (i, 0)),
        scratch_shapes=[pltpu.VMEM((d, de), BF16), pltpu.VMEM((d, de), BF16), pltpu.VMEM((de, d), BF16)])
    return pl.pallas_call(
        kern, grid_spec=grid_spec, out_shape=jax.ShapeDtypeStruct((p_rows, d), F32),
        compiler_params=_params(1), name="expert_mlp")(blk_e, n_used, xs, w_gate, w_up, w_down)


def _slot_plan(experts, ranks):
    n = 2 * experts.shape[0]
    ids = jnp.arange(N_EXPERTS, dtype=jnp.int32)
    onehot = experts[..., None] == ids
    counts = jnp.sum(onehot, axis=(0, 1)).astype(jnp.int32)
    padded = (counts + MOE_BLK - 1) // MOE_BLK * MOE_BLK
    pend = jnp.cumsum(padded)
    pstart = pend - padded
    p_rows = -(-n // MOE_BLK) * MOE_BLK + N_EXPERTS * MOE_BLK
    nblk = p_rows // MOE_BLK
    blk_first = jnp.arange(nblk, dtype=jnp.int32) * MOE_BLK
    blk_e = jnp.minimum(jnp.sum(pend[None, :] <= blk_first[:, None], axis=1), N_EXPERTS - 1).astype(jnp.int32)
    n_used = (pend[-1:] // MOE_BLK).astype(jnp.int32)
    blk_ids = jnp.arange(nblk, dtype=jnp.int32)
    partly = jnp.any((blk_ids[:, None] == (pend // MOE_BLK - 1)[None, :]) & (counts % MOE_BLK != 0)[None, :], axis=1)
    zero_blk = (partly | (blk_ids >= n_used[0])).astype(jnp.int32)
    slot = ranks + jnp.sum(jnp.where(onehot, pstart, 0), axis=-1)
    return slot.reshape(n).astype(jnp.int32), blk_e, n_used, zero_blk, p_rows


def _dma_params():
    return pltpu.CompilerParams(dimension_semantics=("arbitrary",), vmem_limit_bytes=VMEM_LIMIT,
                                disable_bounds_checks=True)


DMA_UNROLL = 8


def moe_dispatch(x1, mod, slot, zero_blk, p_rows, group_len, name):
    t, d = x1.shape
    tm = ROW_TILE

    def body(slot_ref, zb_ref, x_ref, mod_ref, xs_ref, h_ref, zero_ref, sem):
        h_ref[...] = _modulate(x_ref, mod_ref, 3, 4)

        @pl.when(pl.program_id(0) == 0)
        def _():
            zero_ref[...] = jnp.zeros_like(zero_ref)

            def zstart(b, c):
                @pl.when(zb_ref[b] != 0)
                def _():
                    first = pl.multiple_of(b * MOE_BLK, MOE_BLK)
                    pltpu.make_async_copy(zero_ref, xs_ref.at[pl.ds(first, MOE_BLK)], sem).start()
                return c

            def zwait(b, c):
                @pl.when(zb_ref[b] != 0)
                def _():
                    pltpu.make_async_copy(zero_ref, xs_ref.at[pl.ds(0, MOE_BLK)], sem).wait()
                return c

            lax.fori_loop(0, p_rows // MOE_BLK, zstart, 0)
            lax.fori_loop(0, p_rows // MOE_BLK, zwait, 0)

        def start(rr, c):
            for u in range(DMA_UNROLL):
                r = rr * DMA_UNROLL + u
                for k in range(2):
                    pltpu.make_async_copy(h_ref.at[pl.ds(r, 1)], xs_ref.at[pl.ds(slot_ref[2 * r + k], 1)], sem).start()
            return c

        lax.fori_loop(0, tm // DMA_UNROLL, start, 0)
        for k in range(2):
            pltpu.make_async_copy(h_ref, xs_ref.at[pl.ds(0, tm)], sem).wait()

    return row_call(body, t, [(slot, "smem_rows"), (zero_blk, "smem"), (x1, "row"), (mod, "group")],
                    [(p_rows, d, F32, "any")], group_len=group_len,
                    scratch=[pltpu.VMEM((tm, d), F32), pltpu.VMEM((MOE_BLK, d), F32), pltpu.SemaphoreType.DMA(())],
                    params=_dma_params(), name=name)[0]


def kernel(x_prompt, x_sample, cache_a_k, cache_a_v, cache_b_ckv, cache_b_krope, cache_d_k, cache_d_v, c_ctx, c, mod_w, mod_b, ln_g, ln_b, a_wq, a_wk, a_wv, a_wo, a_sink, b_wq_a, b_q_norm, b_wq_b, b_wkv_a, b_kv_norm, b_wk_b, b_wv_b, b_wo, c_w_in, c_b_in, c_conv_w, c_conv_b, c_ffn_w1, c_ffn_b1, c_ffn_w2, c_ffn_b2, c_ffn_w3, c_ffn_freq, c_log_decay, c_skip, c_wo, d_wq, d_wk, d_wv, d_wo, d_rel_bias, moe_wr_g, moe_br_g, moe_wr_e, moe_br_e, moe_w_gate, moe_w_up, moe_w_down):
    bc, lc, d = x_prompt.shape
    bl, ll, _ = x_sample.shape
    past = cache_a_k.shape[2]
    gl = ll
    assert bc * lc == gl and d == A_HEADS * HEAD_DIM
    ng = 1 + bl
    t_all = ng * gl
    x = jnp.concatenate([x_prompt.reshape(gl, d), x_sample.reshape(bl * gl, d)], axis=0)
    cvec = jnp.concatenate([c_ctx[None, :], c], axis=0)
    mods = modulation_all(cvec, mod_w, mod_b)
    rc = functools.partial(row_call, group_len=gl)
    row2 = lambda v: v.reshape(1, -1)
    n_ctx_tiles = gl // ROW_TILE

    def post_mixer(i, att_ctx, att_lat, wo, x_in):
        unused = LANES - N_GROUPS - N_EXPERTS
        wr = jnp.concatenate([moe_wr_g[i], moe_wr_e[i], jnp.zeros((d, unused), F32)], axis=1)
        br = jnp.concatenate([moe_br_g[i], moe_br_e[i], jnp.zeros((unused,), F32)])[None, :]

        def body(attc_ref, attl_ref, x_ref, mod_ref, wo_ref, g_ref, b_ref, wr_ref, br_ref, x1_ref, info_ref, w_ref, seen_ref):
            @pl.when(pl.program_id(0) == 0)
            def _():
                seen_ref[...] = jnp.zeros_like(seen_ref)

            m = mod_ref[0]
            att = jnp.where(pl.program_id(0) < n_ctx_tiles, attc_ref[...], attl_ref[...])
            o = _dot(att.astype(BF16), wo_ref[...])
            x1 = _layer_norm(DEEPNORM_ALPHA * x_ref[...] + m[2:3] * o, g_ref[...], b_ref[...])
            x1_ref[...] = x1
            h = x1 * (1.0 + m[4:5]) + m[3:4]
            lane, e1, e2, w1, w2 = _route(_dot3(h, wr_ref[...]) + br_ref[...])
            before, picks = _rank_in_tile(lane, e1, e2)
            before = before + seen_ref[...]
            r1 = jnp.sum(jnp.where(lane == e1, before, 0.0), axis=1, keepdims=True)
            r2 = jnp.sum(jnp.where(lane == e2, before, 0.0), axis=1, keepdims=True)
            seen_ref[...] += jnp.sum(picks, axis=0, keepdims=True)
            info = jnp.where(lane == 0, e1, jnp.where(lane == 1, e2, jnp.where(lane == 2, r1, jnp.where(lane == 3, r2, 0.0))))
            info_ref[...] = info.astype(jnp.int32)
            w_ref[...] = jnp.where(lane == 0, w1, jnp.where(lane == 1, w2, 0.0))

        return rc(body, t_all,
                  [(att_ctx, "row_head"), (att_lat, "row_tail"), (x_in, "row"), (mods[i], "group"), (wo.astype(BF16), "full"),
                   (row2(ln_g[i, 0]), "full"), (row2(ln_b[i, 0]), "full"), (wr, "full"), (br, "full")],
                  [(d, F32), (LANES, jnp.int32), (LANES, F32)],
                  scratch=[pltpu.VMEM((1, LANES), F32)], name=f"post_mixer{i}")

    def moe(i, x1, info, w):
        slot, blk_e, n_used, zero_blk, p_rows = _slot_plan(info[:, 0:2], info[:, 2:4])
        xs = moe_dispatch(x1, mods[i], slot, zero_blk, p_rows, gl, f"moe_dispatch{i}")
        ys = expert_mlp(xs, blk_e, n_used, moe_w_gate, moe_w_up, moe_w_down, i)
        tm = ROW_TILE

        def body(slot_ref, x_ref, w_ref, mod_ref, g_ref, b_ref, ys_ref, out_ref, buf, sem):
            def start(rr, c):
                for u in range(DMA_UNROLL):
                    r = rr * DMA_UNROLL + u
                    for k in range(2):
                        pltpu.make_async_copy(ys_ref.at[pl.ds(slot_ref[2 * r + k], 1)], buf.at[k, pl.ds(r, 1)], sem).start()
                return c

            lax.fori_loop(0, tm // DMA_UNROLL, start, 0)
            for k in range(2):
                pltpu.make_async_copy(ys_ref.at[pl.ds(0, tm)], buf.at[k], sem).wait()
            m = mod_ref[0]
            wv = w_ref[...]
            y = wv[:, 0:1] * buf[0] + wv[:, 1:2] * buf[1]
            out_ref[...] = _layer_norm(DEEPNORM_ALPHA * x_ref[...] + m[5:6] * y, g_ref[...], b_ref[...])

        return rc(body, t_all,
                  [(slot, "smem_rows"), (x1, "row"), (w, "row"), (mods[i], "group"),
                   (row2(ln_g[i, 1]), "full"), (row2(ln_b[i, 1]), "full"), (ys, "any")],
                  [(d, F32)], scratch=[pltpu.VMEM((2, tm, d), F32), pltpu.SemaphoreType.DMA(())],
                  params=_dma_params(), name=f"moe_combine{i}")[0]

    def finish_layer(i, att_ctx, att_lat, wo, x_in):
        x1, info, w = post_mixer(i, att_ctx, att_lat, wo, x_in)
        return moe(i, x1, info, w)

    tq = 256
    i = 0
    hd = HEAD_DIM
    kvw = A_KV_HEADS * hd
    cq, sq = _rope_tables(gl, hd, 0, 0, LANES // hd)
    wq, wk, wv = a_wq[0], a_wk[0], a_wv[0]

    def body_a(x_ref, mod_ref, wq_ref, wqr_ref, wk_ref, wkr_ref, wv_ref, c_ref, s_ref, q_ref, kh_ref, vh_ref, kc_ref, vc_ref):
        h = _modulate(x_ref, mod_ref, 0, 1).astype(BF16)
        cc, ss = c_ref[0], s_ref[0]
        q = _dot(h, wq_ref[...]) * _tile_lanes(cc, d) + _dot(h, wqr_ref[...]) * _tile_lanes(ss, d)
        k = _dot(h, wk_ref[...]) * _tile_lanes(cc, kvw) + _dot(h, wkr_ref[...]) * _tile_lanes(ss, kvw)
        v = _dot(h, wv_ref[...])
        q_ref[...] = (q * QSCALE_64).astype(BF16)
        _store_heads(kh_ref, k, hd)
        _store_value_heads(vh_ref, v, hd)

        @pl.when(pl.program_id(0) < n_ctx_tiles)
        def _():
            kc_ref[...] = k
            vc_ref[...] = v

    q, kh, vh, k_new, v_new = rc(
        body_a, t_all,
        [(x, "row"), (mods[i], "group"), (wq.astype(BF16), "full"), (_rot_cols(wq, hd).astype(BF16), "full"),
         (wk.astype(BF16), "full"), (_rot_cols(wk, hd).astype(BF16), "full"), (wv.astype(BF16), "full"),
         (cq, "pos"), (sq, "pos")],
        [(d, BF16), (A_KV_HEADS, hd, BF16, "heads"), (A_KV_HEADS, V_LANES, BF16, "heads"),
         (gl, kvw, F32, "ctx"), (gl, kvw, F32, "ctx")], name="proj_a")
    out_a_k = k_new.reshape(bc, 1, lc, A_KV_HEADS, hd)
    out_a_v = v_new.reshape(bc, 1, lc, A_KV_HEADS, hd)
    common = dict(n_heads=A_HEADS, dq=hd, dv=hd, hpb=8, kvpb=2, tq=tq, sink=a_sink[0])
    att_c = attention(q, kh, vh, n_seq=bc, lq=lc, row_off=0, win=lc, name="attn_a_ctx", **common)
    win_a = 2 * tq
    nq_l = gl // tq
    qi = jnp.arange(tq)[:, None]
    ki = jnp.arange(win_a)[None, :]
    band = jnp.stack([jnp.where(jnp.abs(ki - (qi + off)) <= A_WINDOW, 0.0, NEG_INF)
                      for off in (0, A_WINDOW, 2 * A_WINDOW)]).astype(F32)[None]
    att_l = attention(q, kh, vh, n_seq=bl, lq=gl, row_off=gl, win=win_a,
                      start_fn=lambda ii: jnp.clip(ii * tq - A_WINDOW, 0, gl - win_a),
                      kc=_cache_heads_major(cache_a_k[:, 0]), vc=_cache_heads_major(cache_a_v[:, 0], values=True),
                      bias=band, type_fn=lambda ii: jnp.where(ii == 0, 0, jnp.where(ii == nq_l - 1, 2, 1)),
                      name="attn_a_lat", **common)
    x = finish_layer(i, att_c, att_l, a_wo[0], x)

    i = 1
    hq = B_NOPE + B_ROPE
    qw = B_HEADS * B_QPAD
    wqb = b_wq_b[0].reshape(B_Q_RANK, B_HEADS, hq)
    wqb_rot = jnp.concatenate([jnp.zeros_like(wqb[..., :B_NOPE]),
                               _rot_cols(wqb[..., B_NOPE:].reshape(B_Q_RANK, -1), B_ROPE).reshape(B_Q_RANK, B_HEADS, B_ROPE)],
                              axis=-1)
    padq = lambda wz: jnp.pad(wz, ((0, 0), (0, 0), (0, B_QPAD - hq))).reshape(B_Q_RANK, qw).astype(BF16)
    wkv_c, wkv_r = b_wkv_a[0][:, :B_KV_RANK], b_wkv_a[0][:, B_KV_RANK:]
    cqb, sqb = _rope_tables(gl, B_ROPE, B_NOPE, B_QPAD - hq, 1)
    ckr, skr = _rope_tables(gl, B_ROPE, 0, 0, 1)

    def body_b(x_ref, mod_ref, wqa_ref, qn_ref, wqb_ref, wqbr_ref, wc_ref, kn_ref, wr_ref, wrr_ref, wkb_ref, wvb_ref,
               cq_ref, sq_ref, ck_ref, sk_ref, q_ref, kh_ref, vh_ref, ckv_ref, kr_ref):
        h = _modulate(x_ref, mod_ref, 0, 1).astype(BF16)
        qa = _rms(_dot(h, wqa_ref[...]), qn_ref[...]).astype(BF16)
        q = (_dot(qa, wqb_ref[...]) * _tile_lanes(cq_ref[0], qw) + _dot(qa, wqbr_ref[...]) * _tile_lanes(sq_ref[0], qw))
        q_ref[...] = (q * QSCALE_MLA).astype(BF16)
        ckv = _rms(_dot(h, wc_ref[...]), kn_ref[...])
        kr = _dot(h, wr_ref[...]) * ck_ref[0] + _dot(h, wrr_ref[...]) * sk_ref[0]
        cb = ckv.astype(BF16)
        _store_mla_keys(kh_ref, _dot(cb, wkb_ref[...]), kr)
        _store_value_heads(vh_ref, _dot(cb, wvb_ref[...]), B_VDIM)

        @pl.when(pl.program_id(0) < n_ctx_tiles)
        def _():
            ckv_ref[...] = ckv
            kr_ref[...] = kr

    wkb, wvb = b_wk_b[0].astype(BF16), b_wv_b[0].astype(BF16)
    q, kh, vh, ckv_new, kr_new = rc(
        body_b, t_all,
        [(x, "row"), (mods[i], "group"), (b_wq_a[0].astype(BF16), "full"), (row2(b_q_norm[0]), "full"),
         (padq(wqb), "full"), (padq(wqb_rot), "full"), (wkv_c.astype(BF16), "full"), (row2(b_kv_norm[0]), "full"),
         (wkv_r.astype(BF16), "full"), (_rot_cols(wkv_r, B_ROPE).astype(BF16), "full"), (wkb, "full"), (wvb, "full"),
         (cqb, "pos"), (sqb, "pos"), (ckr, "pos"), (skr, "pos")],
        [(qw, BF16), (B_HEADS, B_QPAD, BF16, "heads"), (B_HEADS, V_LANES, BF16, "heads"),
         (gl, B_KV_RANK, F32, "ctx"), (gl, B_ROPE, F32, "ctx")], name="proj_b")
    out_b_ckv = ckv_new.reshape(bc, 1, lc, B_KV_RANK)
    out_b_krope = kr_new.reshape(bc, 1, lc, B_ROPE)

    def body_bc(c_ref, r_ref, wkb_ref, wvb_ref, kh_ref, vh_ref):
        cb = c_ref[...].astype(BF16)
        _store_mla_keys(kh_ref, _dot(cb, wkb_ref[...]), r_ref[...])
        _store_value_heads(vh_ref, _dot(cb, wvb_ref[...]), B_VDIM)

    n_pc = bl * past
    kh_p, vh_p = row_call(body_bc, n_pc,
                          [(cache_b_ckv[:, 0].reshape(n_pc, B_KV_RANK), "row"), (cache_b_krope[:, 0].reshape(n_pc, B_ROPE), "row"),
                           (wkb, "full"), (wvb, "full")],
                          [(B_HEADS, B_QPAD, BF16, "heads"), (B_HEADS, V_LANES, BF16, "heads")],
                          tm=min(ROW_TILE, n_pc), name="proj_b_past")
    common = dict(n_heads=B_HEADS, dq=B_QPAD, dv=B_VDIM, hpb=2, kvpb=2)
    att_c = attention(q, kh, vh, n_seq=bc, lq=lc, row_off=0, win=lc, tq=tq, name="attn_b_ctx", **common)
    att_l = attention(q, kh, vh, n_seq=bl, lq=gl, row_off=gl, win=gl, kc=kh_p, vc=vh_p, tq=min(2 * tq, gl),
                      name="attn_b_lat", **common)
    x = finish_layer(i, att_c, att_l, b_wo[0], x)

    i = 2
    d3 = 3 * d

    def body_c(x_ref, mod_ref, w_ref, b_ref, u_ref):
        h = _modulate(x_ref, mod_ref, 0, 1).astype(BF16)
        for j in range(3):
            u_ref[:, j * d:(j + 1) * d] = _dot(h, w_ref[:, j * d:(j + 1) * d]) + b_ref[:, j * d:(j + 1) * d]

    (u,) = rc(body_c, t_all, [(x, "row"), (mods[i], "group"), (c_w_in[0].astype(BF16), "full"), (row2(c_b_in[0]), "full")],
              [(d3, F32)], name="proj_c")
    u_c = short_conv(u, c_conv_w[0], row2(c_conv_b[0]), row_off=0, n_seq=bc, length=lc, parts=3, name="short_conv_ctx")
    u_l = short_conv(u, c_conv_w[0], row2(c_conv_b[0]), row_off=gl, n_seq=bl, length=gl, parts=3, name="short_conv_lat")
    fargs = (c_ffn_w1[0], c_ffn_b1[0], c_ffn_w2[0], c_ffn_b2[0], c_ffn_w3[0], c_ffn_freq[0], c_log_decay[0])
    g_c = hyena_two_sided_filters(lc, *fargs, name="filters_ctx")
    g_l = hyena_two_sided_filters(gl, *fargs, name="filters_lat")

    nc = 2 * lc
    mf_data, mf_filt, mi_c = _dft_mats_single(nc, lc, lc)
    spec_c = spectral(g_c.reshape(2, 1, 1, nc, d), mf_filt, dt=512, name="filt_spec_ctx")
    z_c = u_c[0].reshape(bc // 2, 2, 1, lc, d)
    for o in range(2):
        gate = u_c[o + 1].reshape(bc // 2, 2, 1, lc, d)
        z_c = spectral(z_c, mf_data, spec_c, mi_c, dt=512, g_index=o, epi=(gate, z_c, row2(c_skip[0, o])),
                       name=f"conv_ctx{o}")
    zc_out = z_c.reshape(gl, d)

    nl = 2 * gl
    n1 = nl // FFT_N2
    a_data, a_filt, a_inv, mf_l, mi_l = _dft_mats_two_stage(nl, FFT_N2)
    ga = slow_stage(a_filt, g_l.reshape(2, n1, FFT_N2, d), name="filt_stage_a")
    spec_l = spectral(ga.reshape(2, 2, n1, FFT_N2, d), mf_l, dt=1024, name="filt_spec_lat")
    z_l = u_l[0].reshape(bl // 2, n1, FFT_N2, d)
    for o in range(2):
        za = slow_stage(a_data, z_l, name=f"conv_lat_a{o}")
        zb = spectral(za.reshape(bl // 2, 2, n1, FFT_N2, d), mf_l, spec_l, mi_l, dt=1024, g_index=o,
                      name=f"conv_lat_c{o}")
        gate = u_l[o + 1].reshape(bl // 2, n1, FFT_N2, d)
        z_l = slow_stage(a_inv, zb.reshape(bl // 2, 2 * n1, FFT_N2, d), epi=(gate, z_l, row2(c_skip[0, o])),
                         name=f"conv_lat_i{o}")
    x = finish_layer(i, zc_out, z_l.reshape(bl * gl, d), c_wo[0], x)

    i = 3

    def body_d(x_ref, mod_ref, wq_ref, wk_ref, wv_ref, q_ref, kh_ref, vh_ref, kc_ref, vc_ref):
        h = _modulate(x_ref, mod_ref, 0, 1).astype(BF16)
        q_ref[...] = (_dot(h, wq_ref[...]) * QSCALE_64).astype(BF16)
        k = _dot(h, wk_ref[...])
        v = _dot(h, wv_ref[...])
        _store_heads(kh_ref, k, hd)
        _store_value_heads(vh_ref, v, hd)

        @pl.when(pl.program_id(0) < n_ctx_tiles)
        def _():
            kc_ref[...] = k
            vc_ref[...] = v

    q, k_heads, v_heads, k_new, v_new = rc(
        body_d, t_all,
        [(x, "row"), (mods[i], "group"), (d_wq[0].astype(BF16), "full"), (d_wk[0].astype(BF16), "full"),
         (d_wv[0].astype(BF16), "full")],
        [(d, BF16), (D_HEADS, hd, BF16, "heads"), (D_HEADS, V_LANES, BF16, "heads"), (gl, d, F32, "ctx"), (gl, d, F32, "ctx")],
        name="proj_d")
    out_d_k = k_new.reshape(bc, 1, lc, D_HEADS, hd)
    out_d_v = v_new.reshape(bc, 1, lc, D_HEADS, hd)
    common = dict(n_heads=D_HEADS, dq=hd, dv=hd, hpb=4, kvpb=4, tq=tq)
    att_c = attention(q, k_heads, v_heads, n_seq=bc, lq=lc, row_off=0, win=lc, name="attn_d_ctx", **common)
    rows = gl // GRID_W
    kh = min(MAX_NBR_ROWS, rows)
    qrows = tq // GRID_W
    krows = qrows + kh
    win_d = krows * GRID_W
    nq_l = gl // tq
    qr_l, kr_l = jnp.arange(qrows)[:, None], jnp.arange(krows)[None, :]
    qc, kc_ = jnp.arange(GRID_W)[:, None], jnp.arange(GRID_W)[None, :]
    c0 = jnp.clip(qc - NBR_COLS // 2, 0, GRID_W - NBR_COLS)
    col_ok = (kc_ >= c0) & (kc_ < c0 + NBR_COLS)
    dc = jnp.clip(kc_ - qc, 1 - NBR_COLS, NBR_COLS - 1) + NBR_COLS - 1
    dc_hot = (dc[..., None] == jnp.arange(2 * NBR_COLS - 1)).astype(F32)
    by_col = jnp.einsum("hrc,xyc->hrxy", d_rel_bias[0], dc_hot, precision=lax.Precision.HIGHEST)
    tabs = []
    for off, lo in ((0, jnp.zeros_like(qr_l)), (kh // 2, qr_l), (kh, jnp.full_like(qr_l, qrows))):
        row_ok = (kr_l >= lo) & (kr_l < lo + kh)
        dr = jnp.clip(kr_l - off - qr_l + MAX_NBR_ROWS - 1, 0, 2 * MAX_NBR_ROWS - 2)
        dr_hot = (dr[..., None] == jnp.arange(2 * MAX_NBR_ROWS - 1)).astype(F32)
        tab = jnp.einsum("qkr,hrxy->hqxky", dr_hot, by_col, precision=lax.Precision.HIGHEST)
        ok = row_ok[:, None, :, None] & col_ok[None, :, None, :]
        tabs.append(jnp.where(ok[None], tab, NEG_INF).reshape(D_HEADS, tq, win_d))
    nbr_bias = jnp.stack(tabs, axis=1).astype(F32)
    att_l = attention(q, k_heads, v_heads, n_seq=bl, lq=gl, row_off=gl, win=win_d,
                      start_fn=lambda ii: jnp.clip(ii * qrows - kh // 2, 0, rows - krows) * GRID_W,
                      kc=_cache_heads_major(cache_d_k[:, 0]), vc=_cache_heads_major(cache_d_v[:, 0], values=True),
                      bias=nbr_bias, type_fn=lambda ii: jnp.where(ii == 0, 0, jnp.where(ii == nq_l - 1, 2, 1)),
                      name="attn_d_lat", **common)
    x = finish_layer(i, att_c, att_l, d_wo[0], x)

    y_prompt = x[:gl].reshape(bc, lc, d)
    y_sample = x[gl:].reshape(bl, gl, d)
    return (y_prompt, y_sample, out_a_k, out_a_v, out_b_ckv, out_b_krope, out_d_k, out_d_v)
```

```python
import functools

import jax
import jax.numpy as jnp
import numpy as np
from jax import lax
from jax.experimental import pallas as pl
from jax.experimental.pallas import tpu as pltpu

F32 = jnp.float32
BF16 = jnp.bfloat16

GRID_W = 64
HEAD_DIM = 64
ROPE_BASE = 10000.0
LN_EPS = 1e-5
RMS_EPS = 1e-6
NEG_INF = -1e30
DEPTH = 4
DEEPNORM_ALPHA = (2 * DEPTH) ** 0.25
A_HEADS = 16
A_KV_HEADS = 4
A_WINDOW = 128
B_HEADS = 16
B_Q_RANK = 384
B_KV_RANK = 256
B_NOPE = 64
B_ROPE = 32
B_VDIM = 64
B_QPAD = 128
C_POS_BANDS = 16
D_HEADS = 16
MAX_NBR_ROWS = 8
NBR_COLS = 16
N_GROUPS = 4
EXPERTS_PER_GROUP = 8
N_EXPERTS = N_GROUPS * EXPERTS_PER_GROUP
D_EXPERT = 512
MOE_BLK = 512
LANES = 128
V_LANES = 128
LOG2E = 1.4426950408889634
QSCALE_64 = HEAD_DIM ** -0.5 * LOG2E
QSCALE_MLA = (B_NOPE + B_ROPE) ** -0.5 * LOG2E
FFT_N2 = 128
VMEM_LIMIT = 56 * 1024 * 1024
ROW_TILE = 512


def _params(n_axes):
    return pltpu.CompilerParams(dimension_semantics=("arbitrary",) * n_axes, vmem_limit_bytes=VMEM_LIMIT)


def _dot(a, b):
    return jnp.dot(a, b, preferred_element_type=F32)


def _split(x):
    hi = x.astype(BF16)
    lo = (x - hi.astype(F32)).astype(BF16)
    return hi, lo


def _dot3(a, b):
    ah, al = _split(a)
    bh, bl = _split(b)
    return _dot(ah, bh) + _dot(ah, bl) + _dot(al, bh)


def _layer_norm(y, g, b):
    mu = jnp.mean(y, axis=-1, keepdims=True)
    d = y - mu
    var = jnp.mean(d * d, axis=-1, keepdims=True)
    return d * lax.rsqrt(var + LN_EPS) * g + b


def _rms(y, g):
    return y * lax.rsqrt(jnp.mean(y * y, axis=-1, keepdims=True) + RMS_EPS) * g


def _tile_lanes(t, n):
    reps = n // t.shape[-1]
    return t if reps == 1 else jnp.concatenate([t] * reps, axis=-1)


def row_call(body, m_rows, ins, outs, *, tm=ROW_TILE, group_len=None, name=None, scratch=(), params=None):
    nb = None if group_len is None else group_len // tm
    n_tiles = m_rows // tm
    in_specs = []
    for a, kind in ins:
        if kind == "row":
            in_specs.append(pl.BlockSpec((tm, a.shape[1]), lambda i: (i, 0)))
        elif kind == "row_head":
            nh = a.shape[0] // tm
            in_specs.append(pl.BlockSpec((tm, a.shape[1]), lambda i, nh=nh: (jnp.minimum(i, nh - 1), 0)))
        elif kind == "row_tail":
            nh = n_tiles - a.shape[0] // tm
            in_specs.append(pl.BlockSpec((tm, a.shape[1]), lambda i, nh=nh: (jnp.maximum(i - nh, 0), 0)))
        elif kind == "any":
            in_specs.append(pl.BlockSpec(memory_space=pl.ANY))
        elif kind == "smem":
            in_specs.append(pl.BlockSpec(memory_space=pltpu.SMEM))
        elif kind == "smem_rows":
            in_specs.append(pl.BlockSpec((a.shape[0] // n_tiles,), lambda i: (i,), memory_space=pltpu.SMEM))
        elif kind == "full":
            in_specs.append(pl.BlockSpec(a.shape, lambda i, nd=a.ndim: (0,) * nd))
        elif kind == "group":
            in_specs.append(pl.BlockSpec((1,) + a.shape[1:], lambda i: (i // nb, 0, 0)))
        elif kind == "pos":
            in_specs.append(pl.BlockSpec((1, tm, a.shape[2]), lambda i: (jnp.minimum(i // nb, 1), i % nb, 0)))
        else:
            raise ValueError(kind)
    out_specs, out_shape = [], []
    for o in outs:
        if len(o) == 4 and o[3] == "any":
            out_specs.append(pl.BlockSpec(memory_space=pl.ANY))
            out_shape.append(jax.ShapeDtypeStruct((o[0], o[1]), o[2]))
        elif len(o) == 4 and o[3] == "heads":
            out_specs.append(pl.BlockSpec((o[0], tm, o[1]), lambda i: (0, i, 0)))
            out_shape.append(jax.ShapeDtypeStruct((o[0], m_rows, o[1]), o[2]))
        elif len(o) == 4 and o[3] == "ctx":
            nkeep = o[0] // tm
            out_specs.append(pl.BlockSpec((tm, o[1]), lambda i, nkeep=nkeep: (jnp.minimum(i, nkeep - 1), 0)))
            out_shape.append(jax.ShapeDtypeStruct((o[0], o[1]), o[2]))
        else:
            out_specs.append(pl.BlockSpec((tm, o[0]), lambda i: (i, 0)))
            out_shape.append(jax.ShapeDtypeStruct((m_rows, o[0]), o[1]))
    return pl.pallas_call(
        body, grid=(n_tiles,), in_specs=in_specs, out_specs=out_specs, out_shape=out_shape,
        scratch_shapes=list(scratch), compiler_params=params or _params(1), name=name)(*[a for a, _ in ins])


def _modulate(x_ref, mod_ref, shift_row, scale_row):
    m = mod_ref[0]
    return x_ref[...] * (1.0 + m[scale_row:scale_row + 1]) + m[shift_row:shift_row + 1]


def modulation_all(cvec, mod_w, mod_b):
    depth, d, d6 = mod_w.shape
    g = cvec.shape[0]
    gp = -(-g // 16) * 16
    cp = jnp.zeros((gp, d), F32).at[:g].set(cvec)
    tn = 1024

    def body(c_ref, w_ref, b_ref, o_ref):
        c = c_ref[...]
        a = (c * jax.nn.sigmoid(c)).astype(BF16)
        o_ref[0] = _dot(a, w_ref[0].astype(BF16)) + b_ref[0]

    out = pl.pallas_call(
        body, grid=(depth, d6 // tn),
        in_specs=[pl.BlockSpec((gp, d), lambda l, j: (0, 0)),
                  pl.BlockSpec((1, d, tn), lambda l, j: (l, 0, j)),
                  pl.BlockSpec((1, 1, tn), lambda l, j: (l, 0, j))],
        out_specs=pl.BlockSpec((1, gp, tn), lambda l, j: (l, 0, j)),
        out_shape=jax.ShapeDtypeStruct((depth, gp, d6), F32),
        compiler_params=_params(2), name="modulation")(cp, mod_w, mod_b.reshape(depth, 1, d6))
    return out[:, :g].reshape(depth, g, 6, d)


def attention(q, kl, vl, *, n_seq, lq, row_off, n_heads, dq, dv, hpb, kvpb, tq, win,
              start_fn=None, kc=None, vc=None, bias=None, type_fn=None, sink=None, name=None):
    nq = lq // tq
    koff = row_off // lq
    rep = hpb // kvpb
    has_ctx, has_bias, has_sink = kc is not None, bias is not None, sink is not None
    bias_heads = has_bias and bias.shape[0] > 1
    off_blk = row_off // tq

    def kern(*refs):
        it = iter(refs)
        q_ref, kl_ref, vl_ref = next(it), next(it), next(it)
        kc_ref = next(it) if has_ctx else None
        vc_ref = next(it) if has_ctx else None
        b_ref = next(it) if has_bias else None
        s_ref = next(it) if has_sink else None
        o_ref = next(it)
        i = pl.program_id(2)
        hb = pl.program_id(1)
        if start_fn is None:
            start = 0
        else:
            start = pl.multiple_of(start_fn(i), 64)
        nt = (((1,), (1,)), ((), ()))
        scores = []
        for j in range(hpb):
            kv = j // rep
            qj = q_ref[:, j * dq:(j + 1) * dq]
            s = lax.dot_general(qj, kl_ref[kv, pl.ds(start, win), :], nt, preferred_element_type=F32)
            if has_bias:
                s = s + b_ref[j if bias_heads else 0, 0]
            sc = lax.dot_general(qj, kc_ref[kv], nt, preferred_element_type=F32) if has_ctx else None
            scores.append((s, sc))
        outs = []
        for j in range(hpb):
            kv = j // rep
            s, sc = scores[j]
            m = jnp.max(s, axis=1, keepdims=True)
            if has_ctx:
                m = jnp.maximum(m, jnp.max(sc, axis=1, keepdims=True))
            if has_sink:
                sk = s_ref[hb * hpb + j]
                m = jnp.maximum(m, sk)
            acc = _dot(jnp.exp2((s - m).astype(BF16)), vl_ref[kv, pl.ds(start, win), :])
            if has_ctx:
                acc = acc + _dot(jnp.exp2((sc - m).astype(BF16)), vc_ref[kv])
            l = acc[:, dv:dv + 1]
            if has_sink:
                l = l + jnp.exp2(sk - m)
            outs.append(acc[:, :dv] / l)
        o_ref[...] = jnp.concatenate(outs, axis=1).astype(o_ref.dtype)

    ins = [q, kl, vl]
    in_specs = [pl.BlockSpec((tq, hpb * dq), lambda b, h, i: (off_blk + b * nq + i, h)),
                pl.BlockSpec((kvpb, lq, dq), lambda b, h, i: (h, koff + b, 0)),
                pl.BlockSpec((kvpb, lq, V_LANES), lambda b, h, i: (h, koff + b, 0))]
    if has_ctx:
        lc = kc.shape[1] // n_seq
        ins += [kc, vc]
        in_specs += [pl.BlockSpec((kvpb, lc, dq), lambda b, h, i: (h, b, 0)),
                     pl.BlockSpec((kvpb, lc, V_LANES), lambda b, h, i: (h, b, 0))]
    if has_bias:
        ins.append(bias * LOG2E)
        hb_blk = hpb if bias_heads else 1
        in_specs.append(pl.BlockSpec((hb_blk, 1, tq, win),
                                     lambda b, h, i: (h if bias_heads else 0, type_fn(i), 0, 0)))
    if has_sink:
        ins.append(sink.astype(F32) * LOG2E)
        in_specs.append(pl.BlockSpec(memory_space=pltpu.SMEM))
    return pl.pallas_call(
        kern, grid=(n_seq, n_heads // hpb, nq), in_specs=in_specs,
        out_specs=pl.BlockSpec((tq, hpb * dv), lambda b, h, i: (b * nq + i, h)),
        out_shape=jax.ShapeDtypeStruct((n_seq * lq, n_heads * dv), BF16),
        compiler_params=_params(3), name=name)(*ins)


def _cache_heads_major(c, values=False):
    n_seq, length, n_heads, d = c.shape
    out = c.astype(BF16).transpose(2, 0, 1, 3).reshape(n_heads, n_seq * length, d)
    if values:
        rows = n_seq * length
        out = jnp.concatenate([out, jnp.ones((n_heads, rows, 1), BF16),
                               jnp.zeros((n_heads, rows, V_LANES - d - 1), BF16)], axis=-1)
    return out


def _store_value_heads(ref, val, width):
    rows = val.shape[0]
    lane = lax.broadcasted_iota(jnp.int32, (rows, V_LANES - width), 1)
    tail = jnp.where(lane == 0, 1.0, 0.0).astype(ref.dtype)
    for hh in range(ref.shape[0]):
        ref[hh] = jnp.concatenate([val[:, hh * width:(hh + 1) * width].astype(ref.dtype), tail], axis=1)


def _store_mla_keys(ref, k_nope, k_rope):
    rows = k_nope.shape[0]
    tail = jnp.concatenate([k_rope, jnp.zeros((rows, B_QPAD - B_NOPE - B_ROPE), k_rope.dtype)], axis=1)
    for hh in range(ref.shape[0]):
        ref[hh] = jnp.concatenate([k_nope[:, hh * B_NOPE:(hh + 1) * B_NOPE], tail], axis=1).astype(ref.dtype)


def _store_heads(ref, val, width):
    for hh in range(ref.shape[0]):
        ref[hh] = val[:, hh * width:(hh + 1) * width].astype(ref.dtype)


def _rope_tables(length, dim, lead, tail, reps):
    half = dim // 2
    nf = half // 2
    t = jnp.arange(length)
    row = (t // GRID_W).astype(F32)
    col = (t % GRID_W).astype(F32)
    inv = ROPE_BASE ** (-jnp.arange(nf, dtype=F32) / nf)
    ang = jnp.concatenate([row[:, None] * inv, col[:, None] * inv], axis=-1)
    cos, sin = jnp.cos(ang), jnp.sin(ang)
    c = jnp.concatenate([jnp.ones((length, lead), F32), cos, cos, jnp.ones((length, tail), F32)] * reps, axis=1)
    s = jnp.concatenate([jnp.zeros((length, lead), F32), sin, sin, jnp.zeros((length, tail), F32)] * reps, axis=1)
    return (jnp.stack([jnp.ones_like(c), c]), jnp.stack([jnp.zeros_like(s), s]))


def _rot_cols(w, dim):
    k, n = w.shape
    wb = w.reshape(k, n // dim, dim)
    half = dim // 2
    return jnp.concatenate([-wb[..., half:], wb[..., :half]], axis=-1).reshape(k, n)


def _cs(phase, n, sign):
    ang = (2.0 * np.pi / n) * (phase % n).astype(F32)
    return jnp.cos(ang), sign * jnp.sin(ang)


def _cblock(wr, wi):
    return jnp.concatenate([jnp.concatenate([wr, -wi], axis=-1), jnp.concatenate([wi, wr], axis=-1)], axis=-2)


SUB = 8


def slow_stage(a, x, *, epi=None, name=None):
    p_n, k, s_n, d = x.shape
    m = a.shape[0]
    x5 = x.reshape(p_n, k, s_n // SUB, SUB, d)

    def kern(*refs):
        if epi is None:
            a_ref, x_ref, o_ref = refs
        else:
            a_ref, x_ref, g_ref, z_ref, s_ref, o_ref = refs
        xt = pltpu.einshape("ksd->skd", x_ref[...])
        av = a_ref[...]
        y = jnp.stack([_dot3(av, xt[s]) for s in range(SUB)], axis=0)
        y = pltpu.einshape("smd->msd", y)
        if epi is not None:
            y = g_ref[...] * (y + s_ref[...][None] * z_ref[...])
        o_ref[...] = y

    blk = lambda rows: pl.BlockSpec((None, rows, None, SUB, d), lambda p, t: (p, 0, t, 0, 0))
    ins = [a, x5]
    in_specs = [pl.BlockSpec((m, k), lambda p, t: (0, 0)), blk(k)]
    if epi is not None:
        gate, z, skip = epi
        ins += [gate.reshape(p_n, m, s_n // SUB, SUB, d), z.reshape(p_n, m, s_n // SUB, SUB, d), skip]
        in_specs += [blk(m), blk(m), pl.BlockSpec((1, d), lambda p, t: (0, 0))]
    out = pl.pallas_call(
        kern, grid=(p_n, s_n // SUB), in_specs=in_specs, out_specs=blk(m),
        out_shape=jax.ShapeDtypeStruct((p_n, m, s_n // SUB, SUB, d), F32),
        compiler_params=_params(2), name=name)(*ins)
    return out.reshape(p_n, m, s_n, d)


def spectral(x, mf, g=None, mi=None, *, dt, g_index=0, epi=None, name=None):
    p_n, planes, k1_n, nin, d = x.shape
    nf = mf.shape[1] // 2
    nout = nf if g is None else mi.shape[1] // 2

    def kern(*refs):
        it = iter(refs)
        x_ref, mf_ref = next(it), next(it)
        g_ref = next(it) if g is not None else None
        mi_ref = next(it) if g is not None else None
        if epi is not None:
            gate_ref, z_ref, skip_ref = next(it), next(it), next(it)
        o_ref = next(it)
        xs = [x_ref[0, pp, 0] for pp in range(planes)]
        xin = xs[0] if planes == 1 else jnp.concatenate(xs, axis=0)
        f = _dot3(mf_ref[0], xin)
        if g is not None:
            fr, fi = f[:nf], f[nf:]
            gr, gi = g_ref[0, 0, 0], g_ref[0, 1, 0]
            y = jnp.concatenate([fr * gr - fi * gi, fr * gi + fi * gr], axis=0)
            f = _dot3(mi_ref[0], y)
        for pp in range(2):
            y = f[pp * nout:(pp + 1) * nout]
            if epi is not None:
                y = gate_ref[0, pp, 0] * (y + skip_ref[...] * z_ref[0, pp, 0])
            o_ref[0, pp, 0] = y

    ins = [x, mf]
    in_specs = [pl.BlockSpec((1, planes, 1, nin, dt), lambda k, j, p: (p, 0, k, 0, j)),
                pl.BlockSpec((1,) + mf.shape[1:], lambda k, j, p: (k, 0, 0))]
    if g is not None:
        ins += [g, mi]
        in_specs += [pl.BlockSpec((1, 2, 1, nf, dt), lambda k, j, p: (g_index, 0, k, 0, j)),
                     pl.BlockSpec((1,) + mi.shape[1:], lambda k, j, p: (k, 0, 0))]
    if epi is not None:
        ins += list(epi)
        in_specs += [pl.BlockSpec((1, 2, 1, nout, dt), lambda k, j, p: (p, 0, k, 0, j)),
                     pl.BlockSpec((1, 2, 1, nout, dt), lambda k, j, p: (p, 0, k, 0, j)),
                     pl.BlockSpec((1, dt), lambda k, j, p: (0, j))]
    return pl.pallas_call(
        kern, grid=(k1_n, d // dt, p_n), in_specs=in_specs,
        out_specs=pl.BlockSpec((1, 2, 1, nout, dt), lambda k, j, p: (p, 0, k, 0, j)),
        out_shape=jax.ShapeDtypeStruct((p_n, 2, k1_n, nout, d), F32),
        compiler_params=_params(3), name=name)(*ins)


def hyena_two_sided_filters(length, w1, b1, w2, b2, w3, freq, log_decay, name):
    hid = w2.shape[0]
    d = w3.shape[1] // 4
    t = jnp.linspace(0.0, 1.0, length, dtype=F32)[:, None]
    ang = 2.0 * jnp.pi * t * jnp.arange(1, C_POS_BANDS + 1, dtype=F32)
    z = jnp.concatenate([t, jnp.cos(ang), jnp.sin(ang)], axis=-1)
    kpad = LANES - z.shape[1]
    z = jnp.pad(z, ((0, 0), (0, kpad)))
    z2 = jnp.concatenate([z, z[::-1]], axis=0)
    w1p = jnp.pad(w1, ((0, kpad), (0, 0)))
    full = lambda a: pl.BlockSpec(a.shape, lambda *_: (0,) * a.ndim)

    def ffn_kern(z_ref, w1_ref, b1_ref, w2_ref, b2_ref, f_ref, a_ref):
        fr = f_ref[...]
        a = jnp.sin(fr * (_dot3(z_ref[...], w1_ref[...]) + b1_ref[...]))
        a_ref[...] = jnp.sin(fr * (_dot3(a, w2_ref[...]) + b2_ref[...]))

    args = [z2, w1p, b1.reshape(1, hid), w2, b2.reshape(1, hid), freq.reshape(1, hid)]
    act = pl.pallas_call(
        ffn_kern, grid=(1,), in_specs=[full(a) for a in args], out_specs=pl.BlockSpec((2 * length, hid), lambda i: (0, 0)),
        out_shape=jax.ShapeDtypeStruct((2 * length, hid), F32), compiler_params=_params(1), name=name + "_ffn")(*args)

    tn = 256
    per = d // tn

    def kern(af_ref, ar_ref, w3f_ref, w3b_ref, ldf_ref, ldb_ref, o_ref):
        row = lax.broadcasted_iota(jnp.int32, (length, tn), 0)
        pos = row.astype(F32) * (1.0 / (length - 1))
        pos_rev = (length - 1 - row).astype(F32) * (1.0 / (length - 1))

        def filt(a_ref, w_ref, ld_ref, tt):
            f = _dot3(a_ref[...], w_ref[...]) * jnp.exp(-jnp.exp(ld_ref[...]) * tt)
            return f / (jnp.sum(jnp.abs(f), axis=0, keepdims=True) + 1e-6)

        hf = filt(af_ref, w3f_ref, ldf_ref, pos)
        hb_rev = filt(ar_ref, w3b_ref, ldb_ref, pos_rev)
        o_ref[0, 0] = hf + jnp.where(row == 0, hb_rev[length - 1:length], 0.0)
        o_ref[0, 1] = jnp.where(row == 0, 0.0, pltpu.roll(hb_rev, 1, 0))

    col = lambda direction: (lambda o, j: (0, (2 * o + direction) * per + j))
    ld = log_decay.reshape(1, 4 * d)
    g = pl.pallas_call(
        kern, grid=(2, per),
        in_specs=[pl.BlockSpec((length, hid), lambda o, j: (0, 0)), pl.BlockSpec((length, hid), lambda o, j: (1, 0)),
                  pl.BlockSpec((hid, tn), col(0)), pl.BlockSpec((hid, tn), col(1)),
                  pl.BlockSpec((1, tn), col(0)), pl.BlockSpec((1, tn), col(1))],
        out_specs=pl.BlockSpec((1, 2, length, tn), lambda o, j: (o, 0, 0, j)),
        out_shape=jax.ShapeDtypeStruct((2, 2, length, d), F32),
        compiler_params=_params(2), name=name)(act, act, w3, w3, ld, ld)
    return g.reshape(2, 2 * length, d)


def short_conv(u, w, b, *, row_off, n_seq, length, parts, name):
    c = u.shape[1]
    dt = 256
    per = c // parts // dt
    first = row_off // length

    def kern(u_ref, w_ref, b_ref, *o_refs):
        x = u_ref[...]
        r = lax.broadcasted_iota(jnp.int32, x.shape, 0)
        prev = jnp.where(r == 0, 0.0, pltpu.roll(x, 1, 0))
        nxt = jnp.where(r == length - 1, 0.0, pltpu.roll(x, length - 1, 0))
        wv = w_ref[...]
        y = prev * wv[0:1] + x * wv[1:2] + nxt * wv[2:3] + b_ref[...]
        part = pl.program_id(1) // per
        for k, o_ref in enumerate(o_refs):
            @pl.when(part == k)
            def _(o_ref=o_ref):
                o_ref[0] = y

    out_spec = lambda k: pl.BlockSpec((1, length, dt), lambda s, j: (s, 0, jnp.clip(j - k * per, 0, per - 1)))
    return pl.pallas_call(
        kern, grid=(n_seq, c // dt),
        in_specs=[pl.BlockSpec((length, dt), lambda s, j: (first + s, j)),
                  pl.BlockSpec((3, dt), lambda s, j: (0, j)),
                  pl.BlockSpec((1, dt), lambda s, j: (0, j))],
        out_specs=[out_spec(k) for k in range(parts)],
        out_shape=[jax.ShapeDtypeStruct((n_seq, length, c // parts), F32)] * parts,
        compiler_params=_params(2), name=name)(u, w, b)


def _dft_mats_single(n, nin_data, nout):
    k = jnp.arange(n, dtype=jnp.int32)
    fr, fi = _cs(k[:, None] * k[None, :nin_data], n, -1.0)
    mf_data = _cblock(fr, fi)[None]
    gr, gi = _cs(k[:, None] * k[None, :], n, -1.0)
    mf_filt = jnp.concatenate([gr, gi], axis=0)[None]
    ir, ii = _cs(k[:nout, None] * k[None, :], n, 1.0)
    mi = _cblock(ir / n, ii / n)[None]
    return mf_data, mf_filt, mi


def _dft_mats_two_stage(n, n2):
    n1 = n // n2
    k1 = jnp.arange(n1, dtype=jnp.int32)
    t1h = jnp.arange(n1 // 2, dtype=jnp.int32)
    ar, ai = _cs((n // n1) * k1[:, None] * t1h[None, :], n, -1.0)
    a_data = _cblock(ar, ai)
    fr, fi = _cs((n // n1) * k1[:, None] * k1[None, :], n, -1.0)
    a_filt = jnp.concatenate([fr, fi], axis=0)
    br, bi = _cs((n // n1) * t1h[:, None] * k1[None, :], n, 1.0)
    a_inv = _cblock(br, bi)
    t2 = jnp.arange(n2, dtype=jnp.int32)
    kk = k1[:, None, None] + n1 * t2[None, :, None]
    mr, mi_ = _cs(kk * t2[None, None, :], n, -1.0)
    mf = _cblock(mr, mi_)
    vr, vi = _cs(jnp.swapaxes(kk, 1, 2) * t2[None, :, None], n, 1.0)
    mi = _cblock(vr / n, vi / n)
    return a_data, a_filt, a_inv, mf, mi


def _route(logits):
    lane = lax.broadcasted_iota(jnp.int32, logits.shape, 1).astype(F32)
    big = 1e9
    lg = jnp.where(lane < N_GROUPS, logits, -jnp.inf)
    mg = jnp.max(lg, axis=1, keepdims=True)
    gi = jnp.min(jnp.where(lg == mg, lane, big), axis=1, keepdims=True)
    p_g = 1.0 / jnp.sum(jnp.exp(lg - mg), axis=1, keepdims=True)
    lo = N_GROUPS + EXPERTS_PER_GROUP * gi
    le = jnp.where((lane >= lo) & (lane < lo + EXPERTS_PER_GROUP), logits, -jnp.inf)
    m1 = jnp.max(le, axis=1, keepdims=True)
    i1 = jnp.min(jnp.where(le == m1, lane, big), axis=1, keepdims=True)
    le2 = jnp.where(lane == i1, -jnp.inf, le)
    m2 = jnp.max(le2, axis=1, keepdims=True)
    i2 = jnp.min(jnp.where(le2 == m2, lane, big), axis=1, keepdims=True)
    e2 = jnp.exp(m2 - m1)
    w1 = p_g / (1.0 + e2)
    w2 = p_g * e2 / (1.0 + e2)
    return lane, i1 - N_GROUPS, i2 - N_GROUPS, w1, w2


def _rank_in_tile(lane, e1, e2):
    tm = lane.shape[0]
    picks = jnp.where(lane == e1, 1.0, 0.0) + jnp.where(lane == e2, 1.0, 0.0)
    r = lax.broadcasted_iota(jnp.int32, (tm, tm), 0)
    c = lax.broadcasted_iota(jnp.int32, (tm, tm), 1)
    earlier = jnp.where(c < r, 1.0, 0.0).astype(BF16)
    return _dot(earlier, picks.astype(BF16)), picks


def _pack_bf16_pairs(y):
    c = y.shape[1] // 2
    hi = lax.bitcast_convert_type(y[:, :c].astype(BF16).astype(F32), jnp.uint32)
    lo = lax.bitcast_convert_type(y[:, c:].astype(BF16).astype(F32), jnp.uint32)
    return hi | (lo >> 16)


def _unpack_bf16_pairs(w):
    hi = lax.bitcast_convert_type(w & jnp.uint32(0xFFFF0000), F32)
    lo = lax.bitcast_convert_type(w << 16, F32)
    return hi, lo


def expert_mlp(xs, blk_e, n_used, w_gate, w_up, w_down, layer):
    p_rows, d = xs.shape
    de = w_gate.shape[3]
    nblk = p_rows // MOE_BLK

    def kern(be_ref, nu_ref, x_ref, wg_ref, wu_ref, wd_ref, o_ref, wg_s, wu_s, wd_s):
        i = pl.program_id(0)
        used = i < nu_ref[0]
        fresh = jnp.logical_or(i == 0, be_ref[i] != be_ref[jnp.maximum(i - 1, 0)])

        @pl.when(jnp.logical_and(used, fresh))
        def _():
            wg_s[...] = wg_ref[0, 0].astype(BF16)
            wu_s[...] = wu_ref[0, 0].astype(BF16)
            wd_s[...] = wd_ref[0, 0].astype(BF16)

        @pl.when(used)
        def _():
            x = x_ref[...].astype(BF16)
            g = _dot(x, wg_s[...])
            u = _dot(x, wu_s[...])
            a = (g * jax.nn.sigmoid(g) * u).astype(BF16)
            o_ref[...] = _pack_bf16_pairs(_dot(a, wd_s[...]))

        @pl.when(jnp.logical_not(used))
        def _():
            o_ref[...] = jnp.zeros_like(o_ref)

    last = lambda i, nu: jnp.minimum(i, nu[0] - 1)
    grid_spec = pltpu.PrefetchScalarGridSpec(
        num_scalar_prefetch=2, grid=(nblk,),
        in_specs=[pl.BlockSpec((MOE_BLK, d), lambda i, be, nu: (last(i, nu), 0)),
                  pl.BlockSpec((1, 1, d, de), lambda i, be, nu: (layer, be[last(i, nu)], 0, 0)),
                  pl.BlockSpec((1, 1, d, de), lambda i, be, nu: (layer, be[last(i, nu)], 0, 0)),
                  pl.BlockSpec((1, 1, de, d), lambda i, be, nu: (layer, be[last(i, nu)], 0, 0))],
        out_specs=pl.BlockSpec((MOE_BLK, d // 2), lambda i, be, nu: (i, 0)),
        scratch_shapes=[pltpu.VMEM((d, de), BF16), pltpu.VMEM((d, de), BF16), pltpu.VMEM((de, d), BF16)])
    return pl.pallas_call(
        kern, grid_spec=grid_spec, out_shape=jax.ShapeDtypeStruct((p_rows, d // 2), jnp.uint32),
        compiler_params=_params(1), name="expert_mlp")(blk_e, n_used, xs, w_gate, w_up, w_down)


def _slot_plan(experts, ranks):
    n = 2 * experts.shape[0]
    ids = jnp.arange(N_EXPERTS, dtype=jnp.int32)
    onehot = experts[..., None] == ids
    counts = jnp.sum(onehot, axis=(0, 1)).astype(jnp.int32)
    padded = (counts + MOE_BLK - 1) // MOE_BLK * MOE_BLK
    pend = jnp.cumsum(padded)
    pstart = pend - padded
    p_rows = -(-n // MOE_BLK) * MOE_BLK + N_EXPERTS * MOE_BLK
    nblk = p_rows // MOE_BLK
    blk_first = jnp.arange(nblk, dtype=jnp.int32) * MOE_BLK
    blk_e = jnp.minimum(jnp.sum(pend[None, :] <= blk_first[:, None], axis=1), N_EXPERTS - 1).astype(jnp.int32)
    n_used = (pend[-1:] // MOE_BLK).astype(jnp.int32)
    blk_ids = jnp.arange(nblk, dtype=jnp.int32)
    partly = jnp.any((blk_ids[:, None] == (pend // MOE_BLK - 1)[None, :]) & (counts % MOE_BLK != 0)[None, :], axis=1)
    zero_blk = (partly | (blk_ids >= n_used[0])).astype(jnp.int32)
    slot = ranks + jnp.sum(jnp.where(onehot, pstart, 0), axis=-1)
    return slot.reshape(n).astype(jnp.int32), blk_e, n_used, zero_blk, p_rows


def _dma_params():
    return pltpu.CompilerParams(dimension_semantics=("arbitrary",), vmem_limit_bytes=VMEM_LIMIT,
                                disable_bounds_checks=True)


DMA_UNROLL = 8


def moe_dispatch(x1, mod, slot, zero_blk, p_rows, group_len, name):
    t, d = x1.shape
    tm = ROW_TILE

    def body(slot_ref, zb_ref, x_ref, mod_ref, xs_ref, h_ref, zero_ref, sem):
        h_ref[...] = _modulate(x_ref, mod_ref, 3, 4)

        @pl.when(pl.program_id(0) == 0)
        def _():
            zero_ref[...] = jnp.zeros_like(zero_ref)

            def zstart(b, c):
                @pl.when(zb_ref[b] != 0)
                def _():
                    first = pl.multiple_of(b * MOE_BLK, MOE_BLK)
                    pltpu.make_async_copy(zero_ref, xs_ref.at[pl.ds(first, MOE_BLK)], sem).start()
                return c

            def zwait(b, c):
                @pl.when(zb_ref[b] != 0)
                def _():
                    pltpu.make_async_copy(zero_ref, xs_ref.at[pl.ds(0, MOE_BLK)], sem).wait()
                return c

            lax.fori_loop(0, p_rows // MOE_BLK, zstart, 0)
            lax.fori_loop(0, p_rows // MOE_BLK, zwait, 0)

        def start(rr, c):
            for u in range(DMA_UNROLL):
                r = rr * DMA_UNROLL + u
                for k in range(2):
                    pltpu.make_async_copy(h_ref.at[pl.ds(r, 1)], xs_ref.at[pl.ds(slot_ref[2 * r + k], 1)], sem).start()
            return c

        lax.fori_loop(0, tm // DMA_UNROLL, start, 0)
        for k in range(2):
            pltpu.make_async_copy(h_ref, xs_ref.at[pl.ds(0, tm)], sem).wait()

    return row_call(body, t, [(slot, "smem_rows"), (zero_blk, "smem"), (x1, "row"), (mod, "group")],
                    [(p_rows, d, F32, "any")], group_len=group_len,
                    scratch=[pltpu.VMEM((tm, d), F32), pltpu.VMEM((MOE_BLK, d), F32), pltpu.SemaphoreType.DMA(())],
                    params=_dma_params(), name=name)[0]


def kernel(x_prompt, x_sample, cache_a_k, cache_a_v, cache_b_ckv, cache_b_krope, cache_d_k, cache_d_v, c_ctx, c, mod_w, mod_b, ln_g, ln_b, a_wq, a_wk, a_wv, a_wo, a_sink, b_wq_a, b_q_norm, b_wq_b, b_wkv_a, b_kv_norm, b_wk_b, b_wv_b, b_wo, c_w_in, c_b_in, c_conv_w, c_conv_b, c_ffn_w1, c_ffn_b1, c_ffn_w2, c_ffn_b2, c_ffn_w3, c_ffn_freq, c_log_decay, c_skip, c_wo, d_wq, d_wk, d_wv, d_wo, d_rel_bias, moe_wr_g, moe_br_g, moe_wr_e, moe_br_e, moe_w_gate, moe_w_up, moe_w_down):
    bc, lc, d = x_prompt.shape
    bl, ll, _ = x_sample.shape
    past = cache_a_k.shape[2]
    gl = ll
    assert bc * lc == gl and d == A_HEADS * HEAD_DIM
    ng = 1 + bl
    t_all = ng * gl
    x = jnp.concatenate([x_prompt.reshape(gl, d), x_sample.reshape(bl * gl, d)], axis=0)
    cvec = jnp.concatenate([c_ctx[None, :], c], axis=0)
    mods = modulation_all(cvec, mod_w, mod_b)
    rc = functools.partial(row_call, group_len=gl)
    row2 = lambda v: v.reshape(1, -1)
    n_ctx_tiles = gl // ROW_TILE

    def post_mixer(i, att_ctx, att_lat, wo, x_in):
        unused = LANES - N_GROUPS - N_EXPERTS
        wr = jnp.concatenate([moe_wr_g[i], moe_wr_e[i], jnp.zeros((d, unused), F32)], axis=1)
        br = jnp.concatenate([moe_br_g[i], moe_br_e[i], jnp.zeros((unused,), F32)])[None, :]

        def body(attc_ref, attl_ref, x_ref, mod_ref, wo_ref, g_ref, b_ref, wr_ref, br_ref, x1_ref, info_ref, w_ref, seen_ref):
            @pl.when(pl.program_id(0) == 0)
            def _():
                seen_ref[...] = jnp.zeros_like(seen_ref)

            m = mod_ref[0]
            att = jnp.where(pl.program_id(0) < n_ctx_tiles, attc_ref[...], attl_ref[...])
            o = _dot(att.astype(BF16), wo_ref[...])
            x1 = _layer_norm(DEEPNORM_ALPHA * x_ref[...] + m[2:3] * o, g_ref[...], b_ref[...])
            x1_ref[...] = x1
            h = x1 * (1.0 + m[4:5]) + m[3:4]
            lane, e1, e2, w1, w2 = _route(_dot3(h, wr_ref[...]) + br_ref[...])
            before, picks = _rank_in_tile(lane, e1, e2)
            before = before + seen_ref[...]
            r1 = jnp.sum(jnp.where(lane == e1, before, 0.0), axis=1, keepdims=True)
            r2 = jnp.sum(jnp.where(lane == e2, before, 0.0), axis=1, keepdims=True)
            seen_ref[...] += jnp.sum(picks, axis=0, keepdims=True)
            info = jnp.where(lane == 0, e1, jnp.where(lane == 1, e2, jnp.where(lane == 2, r1, jnp.where(lane == 3, r2, 0.0))))
            info_ref[...] = info.astype(jnp.int32)
            w_ref[...] = jnp.where(lane == 0, w1, jnp.where(lane == 1, w2, 0.0))

        return rc(body, t_all,
                  [(att_ctx, "row_head"), (att_lat, "row_tail"), (x_in, "row"), (mods[i], "group"), (wo.astype(BF16), "full"),
                   (row2(ln_g[i, 0]), "full"), (row2(ln_b[i, 0]), "full"), (wr, "full"), (br, "full")],
                  [(d, F32), (LANES, jnp.int32), (LANES, F32)],
                  scratch=[pltpu.VMEM((1, LANES), F32)], name=f"post_mixer{i}")

    def moe(i, x1, info, w):
        slot, blk_e, n_used, zero_blk, p_rows = _slot_plan(info[:, 0:2], info[:, 2:4])
        xs = moe_dispatch(x1, mods[i], slot, zero_blk, p_rows, gl, f"moe_dispatch{i}")
        ys = expert_mlp(xs, blk_e, n_used, moe_w_gate, moe_w_up, moe_w_down, i)
        tm = ROW_TILE

        def body(slot_ref, x_ref, w_ref, mod_ref, g_ref, b_ref, ys_ref, out_ref, buf, sem):
            def start(rr, c):
                for u in range(DMA_UNROLL):
                    r = rr * DMA_UNROLL + u
                    for k in range(2):
                        pltpu.make_async_copy(ys_ref.at[pl.ds(slot_ref[2 * r + k], 1)], buf.at[k, pl.ds(r, 1)], sem).start()
                return c

            lax.fori_loop(0, tm // DMA_UNROLL, start, 0)
            for k in range(2):
                pltpu.make_async_copy(ys_ref.at[pl.ds(0, tm)], buf.at[k], sem).wait()
            m = mod_ref[0]
            wv = w_ref[...]
            a_hi, a_lo = _unpack_bf16_pairs(buf[0])
            b_hi, b_lo = _unpack_bf16_pairs(buf[1])
            w0, w1 = wv[:, 0:1], wv[:, 1:2]
            y = jnp.concatenate([w0 * a_hi + w1 * b_hi, w0 * a_lo + w1 * b_lo], axis=1)
            out_ref[...] = _layer_norm(DEEPNORM_ALPHA * x_ref[...] + m[5:6] * y, g_ref[...], b_ref[...])

        return rc(body, t_all,
                  [(slot, "smem_rows"), (x1, "row"), (w, "row"), (mods[i], "group"),
                   (row2(ln_g[i, 1]), "full"), (row2(ln_b[i, 1]), "full"), (ys, "any")],
                  [(d, F32)], scratch=[pltpu.VMEM((2, tm, d // 2), jnp.uint32), pltpu.SemaphoreType.DMA(())],
                  params=_dma_params(), name=f"moe_combine{i}")[0]

    def finish_layer(i, att_ctx, att_lat, wo, x_in):
        x1, info, w = post_mixer(i, att_ctx, att_lat, wo, x_in)
        return moe(i, x1, info, w)

    tq = 256
    i = 0
    hd = HEAD_DIM
    kvw = A_KV_HEADS * hd
    cq, sq = _rope_tables(gl, hd, 0, 0, LANES // hd)
    wq, wk, wv = a_wq[0], a_wk[0], a_wv[0]

    def body_a(x_ref, mod_ref, wq_ref, wqr_ref, wk_ref, wkr_ref, wv_ref, c_ref, s_ref, q_ref, kh_ref, vh_ref, kc_ref, vc_ref):
        h = _modulate(x_ref, mod_ref, 0, 1).astype(BF16)
        cc, ss = c_ref[0], s_ref[0]
        q = _dot(h, wq_ref[...]) * _tile_lanes(cc, d) + _dot(h, wqr_ref[...]) * _tile_lanes(ss, d)
        k = _dot(h, wk_ref[...]) * _tile_lanes(cc, kvw) + _dot(h, wkr_ref[...]) * _tile_lanes(ss, kvw)
        v = _dot(h, wv_ref[...])
        q_ref[...] = (q * QSCALE_64).astype(BF16)
        _store_heads(kh_ref, k, hd)
        _store_value_heads(vh_ref, v, hd)

        @pl.when(pl.program_id(0) < n_ctx_tiles)
        def _():
            kc_ref[...] = k
            vc_ref[...] = v

    q, kh, vh, k_new, v_new = rc(
        body_a, t_all,
        [(x, "row"), (mods[i], "group"), (wq.astype(BF16), "full"), (_rot_cols(wq, hd).astype(BF16), "full"),
         (wk.astype(BF16), "full"), (_rot_cols(wk, hd).astype(BF16), "full"), (wv.astype(BF16), "full"),
         (cq, "pos"), (sq, "pos")],
        [(d, BF16), (A_KV_HEADS, hd, BF16, "heads"), (A_KV_HEADS, V_LANES, BF16, "heads"),
         (gl, kvw, F32, "ctx"), (gl, kvw, F32, "ctx")], name="proj_a")
    out_a_k = k_new.reshape(bc, 1, lc, A_KV_HEADS, hd)
    out_a_v = v_new.reshape(bc, 1, lc, A_KV_HEADS, hd)
    common = dict(n_heads=A_HEADS, dq=hd, dv=hd, hpb=8, kvpb=2, tq=tq, sink=a_sink[0])
    att_c = attention(q, kh, vh, n_seq=bc, lq=lc, row_off=0, win=lc, name="attn_a_ctx", **common)
    win_a = 2 * tq
    nq_l = gl // tq
    qi = jnp.arange(tq)[:, None]
    ki = jnp.arange(win_a)[None, :]
    band = jnp.stack([jnp.where(jnp.abs(ki - (qi + off)) <= A_WINDOW, 0.0, NEG_INF)
                      for off in (0, A_WINDOW, 2 * A_WINDOW)]).astype(F32)[None]
    att_l = attention(q, kh, vh, n_seq=bl, lq=gl, row_off=gl, win=win_a,
                      start_fn=lambda ii: jnp.clip(ii * tq - A_WINDOW, 0, gl - win_a),
                      kc=_cache_heads_major(cache_a_k[:, 0]), vc=_cache_heads_major(cache_a_v[:, 0], values=True),
                      bias=band, type_fn=lambda ii: jnp.where(ii == 0, 0, jnp.where(ii == nq_l - 1, 2, 1)),
                      name="attn_a_lat", **common)
    x = finish_layer(i, att_c, att_l, a_wo[0], x)

    i = 1
    hq = B_NOPE + B_ROPE
    qw = B_HEADS * B_QPAD
    wqb = b_wq_b[0].reshape(B_Q_RANK, B_HEADS, hq)
    wqb_rot = jnp.concatenate([jnp.zeros_like(wqb[..., :B_NOPE]),
                               _rot_cols(wqb[..., B_NOPE:].reshape(B_Q_RANK, -1), B_ROPE).reshape(B_Q_RANK, B_HEADS, B_ROPE)],
                              axis=-1)
    padq = lambda wz: jnp.pad(wz, ((0, 0), (0, 0), (0, B_QPAD - hq))).reshape(B_Q_RANK, qw).astype(BF16)
    wkv_c, wkv_r = b_wkv_a[0][:, :B_KV_RANK], b_wkv_a[0][:, B_KV_RANK:]
    cqb, sqb = _rope_tables(gl, B_ROPE, B_NOPE, B_QPAD - hq, 1)
    ckr, skr = _rope_tables(gl, B_ROPE, 0, 0, 1)

    def body_b(x_ref, mod_ref, wqa_ref, qn_ref, wqb_ref, wqbr_ref, wc_ref, kn_ref, wr_ref, wrr_ref, wkb_ref, wvb_ref,
               cq_ref, sq_ref, ck_ref, sk_ref, q_ref, kh_ref, vh_ref, ckv_ref, kr_ref):
        h = _modulate(x_ref, mod_ref, 0, 1).astype(BF16)
        qa = _rms(_dot(h, wqa_ref[...]), qn_ref[...]).astype(BF16)
        q = (_dot(qa, wqb_ref[...]) * _tile_lanes(cq_ref[0], qw) + _dot(qa, wqbr_ref[...]) * _tile_lanes(sq_ref[0], qw))
        q_ref[...] = (q * QSCALE_MLA).astype(BF16)
        ckv = _rms(_dot(h, wc_ref[...]), kn_ref[...])
        kr = _dot(h, wr_ref[...]) * ck_ref[0] + _dot(h, wrr_ref[...]) * sk_ref[0]
        cb = ckv.astype(BF16)
        _store_mla_keys(kh_ref, _dot(cb, wkb_ref[...]), kr)
        _store_value_heads(vh_ref, _dot(cb, wvb_ref[...]), B_VDIM)

        @pl.when(pl.program_id(0) < n_ctx_tiles)
        def _():
            ckv_ref[...] = ckv
            kr_ref[...] = kr

    wkb, wvb = b_wk_b[0].astype(BF16), b_wv_b[0].astype(BF16)
    q, kh, vh, ckv_new, kr_new = rc(
        body_b, t_all,
        [(x, "row"), (mods[i], "group"), (b_wq_a[0].astype(BF16), "full"), (row2(b_q_norm[0]), "full"),
         (padq(wqb), "full"), (padq(wqb_rot), "full"), (wkv_c.astype(BF16), "full"), (row2(b_kv_norm[0]), "full"),
         (wkv_r.astype(BF16), "full"), (_rot_cols(wkv_r, B_ROPE).astype(BF16), "full"), (wkb, "full"), (wvb, "full"),
         (cqb, "pos"), (sqb, "pos"), (ckr, "pos"), (skr, "pos")],
        [(qw, BF16), (B_HEADS, B_QPAD, BF16, "heads"), (B_HEADS, V_LANES, BF16, "heads"),
         (gl, B_KV_RANK, F32, "ctx"), (gl, B_ROPE, F32, "ctx")], name="proj_b")
    out_b_ckv = ckv_new.reshape(bc, 1, lc, B_KV_RANK)
    out_b_krope = kr_new.reshape(bc, 1, lc, B_ROPE)

    def body_bc(c_ref, r_ref, wkb_ref, wvb_ref, kh_ref, vh_ref):
        cb = c_ref[...].astype(BF16)
        _store_mla_keys(kh_ref, _dot(cb, wkb_ref[...]), r_ref[...])
        _store_value_heads(vh_ref, _dot(cb, wvb_ref[...]), B_VDIM)

    n_pc = bl * past
    kh_p, vh_p = row_call(body_bc, n_pc,
                          [(cache_b_ckv[:, 0].reshape(n_pc, B_KV_RANK), "row"), (cache_b_krope[:, 0].reshape(n_pc, B_ROPE), "row"),
                           (wkb, "full"), (wvb, "full")],
                          [(B_HEADS, B_QPAD, BF16, "heads"), (B_HEADS, V_LANES, BF16, "heads")],
                          tm=min(ROW_TILE, n_pc), name="proj_b_past")
    common = dict(n_heads=B_HEADS, dq=B_QPAD, dv=B_VDIM, hpb=2, kvpb=2)
    att_c = attention(q, kh, vh, n_seq=bc, lq=lc, row_off=0, win=lc, tq=tq, name="attn_b_ctx", **common)
    att_l = attention(q, kh, vh, n_seq=bl, lq=gl, row_off=gl, win=gl, kc=kh_p, vc=vh_p, tq=min(2 * tq, gl),
                      name="attn_b_lat", **common)
    x = finish_layer(i, att_c, att_l, b_wo[0], x)

    i = 2
    d3 = 3 * d

    def body_c(x_ref, mod_ref, w_ref, b_ref, u_ref):
        h = _modulate(x_ref, mod_ref, 0, 1).astype(BF16)
        for j in range(3):
            u_ref[:, j * d:(j + 1) * d] = _dot(h, w_ref[:, j * d:(j + 1) * d]) + b_ref[:, j * d:(j + 1) * d]

    (u,) = rc(body_c, t_all, [(x, "row"), (mods[i], "group"), (c_w_in[0].astype(BF16), "full"), (row2(c_b_in[0]), "full")],
              [(d3, F32)], name="proj_c")
    u_c = short_conv(u, c_conv_w[0], row2(c_conv_b[0]), row_off=0, n_seq=bc, length=lc, parts=3, name="short_conv_ctx")
    u_l = short_conv(u, c_conv_w[0], row2(c_conv_b[0]), row_off=gl, n_seq=bl, length=gl, parts=3, name="short_conv_lat")
    fargs = (c_ffn_w1[0], c_ffn_b1[0], c_ffn_w2[0], c_ffn_b2[0], c_ffn_w3[0], c_ffn_freq[0], c_log_decay[0])
    g_c = hyena_two_sided_filters(lc, *fargs, name="filters_ctx")
    g_l = hyena_two_sided_filters(gl, *fargs, name="filters_lat")

    nc = 2 * lc
    mf_data, mf_filt, mi_c = _dft_mats_single(nc, lc, lc)
    spec_c = spectral(g_c.reshape(2, 1, 1, nc, d), mf_filt, dt=512, name="filt_spec_ctx")
    z_c = u_c[0].reshape(bc // 2, 2, 1, lc, d)
    for o in range(2):
        gate = u_c[o + 1].reshape(bc // 2, 2, 1, lc, d)
        z_c = spectral(z_c, mf_data, spec_c, mi_c, dt=512, g_index=o, epi=(gate, z_c, row2(c_skip[0, o])),
                       name=f"conv_ctx{o}")
    zc_out = z_c.reshape(gl, d)

    nl = 2 * gl
    n1 = nl // FFT_N2
    a_data, a_filt, a_inv, mf_l, mi_l = _dft_mats_two_stage(nl, FFT_N2)
    ga = slow_stage(a_filt, g_l.reshape(2, n1, FFT_N2, d), name="filt_stage_a")
    spec_l = spectral(ga.reshape(2, 2, n1, FFT_N2, d), mf_l, dt=1024, name="filt_spec_lat")
    z_l = u_l[0].reshape(bl // 2, n1, FFT_N2, d)
    for o in range(2):
        za = slow_stage(a_data, z_l, name=f"conv_lat_a{o}")
        zb = spectral(za.reshape(bl // 2, 2, n1, FFT_N2, d), mf_l, spec_l, mi_l, dt=1024, g_index=o,
                      name=f"conv_lat_c{o}")
        gate = u_l[o + 1].reshape(bl // 2, n1, FFT_N2, d)
        z_l = slow_stage(a_inv, zb.reshape(bl // 2, 2 * n1, FFT_N2, d), epi=(gate, z_l, row2(c_skip[0, o])),
                         name=f"conv_lat_i{o}")
    x = finish_layer(i, zc_out, z_l.reshape(bl * gl, d), c_wo[0], x)

    i = 3

    def body_d(x_ref, mod_ref, wq_ref, wk_ref, wv_ref, q_ref, kh_ref, vh_ref, kc_ref, vc_ref):
        h = _modulate(x_ref, mod_ref, 0, 1).astype(BF16)
        q_ref[...] = (_dot(h, wq_ref[...]) * QSCALE_64).astype(BF16)
        k = _dot(h, wk_ref[...])
        v = _dot(h, wv_ref[...])
        _store_heads(kh_ref, k, hd)
        _store_value_heads(vh_ref, v, hd)

        @pl.when(pl.program_id(0) < n_ctx_tiles)
        def _():
            kc_ref[...] = k
            vc_ref[...] = v

    q, k_heads, v_heads, k_new, v_new = rc(
        body_d, t_all,
        [(x, "row"), (mods[i], "group"), (d_wq[0].astype(BF16), "full"), (d_wk[0].astype(BF16), "full"),
         (d_wv[0].astype(BF16), "full")],
        [(d, BF16), (D_HEADS, hd, BF16, "heads"), (D_HEADS, V_LANES, BF16, "heads"), (gl, d, F32, "ctx"), (gl, d, F32, "ctx")],
        name="proj_d")
    out_d_k = k_new.reshape(bc, 1, lc, D_HEADS, hd)
    out_d_v = v_new.reshape(bc, 1, lc, D_HEADS, hd)
    common = dict(n_heads=D_HEADS, dq=hd, dv=hd, hpb=4, kvpb=4, tq=tq)
    att_c = attention(q, k_heads, v_heads, n_seq=bc, lq=lc, row_off=0, win=lc, name="attn_d_ctx", **common)
    rows = gl // GRID_W
    kh = min(MAX_NBR_ROWS, rows)
    qrows = tq // GRID_W
    krows = qrows + kh
    win_d = krows * GRID_W
    nq_l = gl // tq
    qr_l, kr_l = jnp.arange(qrows)[:, None], jnp.arange(krows)[None, :]
    qc, kc_ = jnp.arange(GRID_W)[:, None], jnp.arange(GRID_W)[None, :]
    c0 = jnp.clip(qc - NBR_COLS // 2, 0, GRID_W - NBR_COLS)
    col_ok = (kc_ >= c0) & (kc_ < c0 + NBR_COLS)
    dc = jnp.clip(kc_ - qc, 1 - NBR_COLS, NBR_COLS - 1) + NBR_COLS - 1
    dc_hot = (dc[..., None] == jnp.arange(2 * NBR_COLS - 1)).astype(F32)
    by_col = jnp.einsum("hrc,xyc->hrxy", d_rel_bias[0], dc_hot, precision=lax.Precision.HIGHEST)
    tabs = []
    for off, lo in ((0, jnp.zeros_like(qr_l)), (kh // 2, qr_l), (kh, jnp.full_like(qr_l, qrows))):
        row_ok = (kr_l >= lo) & (kr_l < lo + kh)
        dr = jnp.clip(kr_l - off - qr_l + MAX_NBR_ROWS - 1, 0, 2 * MAX_NBR_ROWS - 2)
        dr_hot = (dr[..., None] == jnp.arange(2 * MAX_NBR_ROWS - 1)).astype(F32)
        tab = jnp.einsum("qkr,hrxy->hqxky", dr_hot, by_col, precision=lax.Precision.HIGHEST)
        ok = row_ok[:, None, :, None] & col_ok[None, :, None, :]
        tabs.append(jnp.where(ok[None], tab, NEG_INF).reshape(D_HEADS, tq, win_d))
    nbr_bias = jnp.stack(tabs, axis=1).astype(F32)
    att_l = attention(q, k_heads, v_heads, n_seq=bl, lq=gl, row_off=gl, win=win_d,
                      start_fn=lambda ii: jnp.clip(ii * qrows - kh // 2, 0, rows - krows) * GRID_W,
                      kc=_cache_heads_major(cache_d_k[:, 0]), vc=_cache_heads_major(cache_d_v[:, 0], values=True),
                      bias=nbr_bias, type_fn=lambda ii: jnp.where(ii == 0, 0, jnp.where(ii == nq_l - 1, 2, 1)),
                      name="attn_d_lat", **common)
    x = finish_layer(i, att_c, att_l, d_wo[0], x)

    y_prompt = x[:gl].reshape(bc, lc, d)
    y_sample = x[gl:].reshape(bl, gl, d)
    return (y_prompt, y_sample, out_a_k, out_a_v, out_b_ckv, out_b_krope, out_d_k, out_d_v)
```

```python
import functools

import jax
import jax.numpy as jnp
import numpy as np
from jax import lax
from jax.experimental import pallas as pl
from jax.experimental.pallas import tpu as pltpu

F32 = jnp.float32
BF16 = jnp.bfloat16

GRID_W = 64
HEAD_DIM = 64
ROPE_BASE = 10000.0
LN_EPS = 1e-5
RMS_EPS = 1e-6
NEG_INF = -1e30
DEPTH = 4
DEEPNORM_ALPHA = (2 * DEPTH) ** 0.25
A_HEADS = 16
A_KV_HEADS = 4
A_WINDOW = 128
B_HEADS = 16
B_Q_RANK = 384
B_KV_RANK = 256
B_NOPE = 64
B_ROPE = 32
B_VDIM = 64
B_QPAD = 128
C_POS_BANDS = 16
D_HEADS = 16
MAX_NBR_ROWS = 8
NBR_COLS = 16
N_GROUPS = 4
EXPERTS_PER_GROUP = 8
N_EXPERTS = N_GROUPS * EXPERTS_PER_GROUP
D_EXPERT = 512
MOE_BLK = 512
LANES = 128
V_LANES = 128
LOG2E = 1.4426950408889634
QSCALE_64 = HEAD_DIM ** -0.5 * LOG2E
QSCALE_MLA = (B_NOPE + B_ROPE) ** -0.5 * LOG2E
FFT_N2 = 128
VMEM_LIMIT = 56 * 1024 * 1024
ROW_TILE = 512


def _params(n_axes):
    return pltpu.CompilerParams(dimension_semantics=("arbitrary",) * n_axes, vmem_limit_bytes=VMEM_LIMIT)


def _dot(a, b):
    return jnp.dot(a, b, preferred_element_type=F32)


def _split(x):
    hi = x.astype(BF16)
    lo = (x - hi.astype(F32)).astype(BF16)
    return hi, lo


def _dot3(a, b):
    ah, al = _split(a)
    bh, bl = _split(b)
    return _dot(ah, bh) + _dot(ah, bl) + _dot(al, bh)


def _layer_norm(y, g, b):
    mu = jnp.mean(y, axis=-1, keepdims=True)
    d = y - mu
    var = jnp.mean(d * d, axis=-1, keepdims=True)
    return d * lax.rsqrt(var + LN_EPS) * g + b


def _rms(y, g):
    return y * lax.rsqrt(jnp.mean(y * y, axis=-1, keepdims=True) + RMS_EPS) * g


def _tile_lanes(t, n):
    reps = n // t.shape[-1]
    return t if reps == 1 else jnp.concatenate([t] * reps, axis=-1)


def row_call(body, m_rows, ins, outs, *, tm=ROW_TILE, group_len=None, name=None, scratch=(), params=None):
    nb = None if group_len is None else group_len // tm
    n_tiles = m_rows // tm
    in_specs = []
    for a, kind in ins:
        if kind == "row":
            in_specs.append(pl.BlockSpec((tm, a.shape[1]), lambda i: (i, 0)))
        elif kind == "row_head":
            nh = a.shape[0] // tm
            in_specs.append(pl.BlockSpec((tm, a.shape[1]), lambda i, nh=nh: (jnp.minimum(i, nh - 1), 0)))
        elif kind == "row_tail":
            nh = n_tiles - a.shape[0] // tm
            in_specs.append(pl.BlockSpec((tm, a.shape[1]), lambda i, nh=nh: (jnp.maximum(i - nh, 0), 0)))
        elif kind == "any":
            in_specs.append(pl.BlockSpec(memory_space=pl.ANY))
        elif kind == "smem":
            in_specs.append(pl.BlockSpec(memory_space=pltpu.SMEM))
        elif kind == "smem_rows":
            in_specs.append(pl.BlockSpec((a.shape[0] // n_tiles,), lambda i: (i,), memory_space=pltpu.SMEM))
        elif kind == "full":
            in_specs.append(pl.BlockSpec(a.shape, lambda i, nd=a.ndim: (0,) * nd))
        elif kind == "group":
            in_specs.append(pl.BlockSpec((1,) + a.shape[1:], lambda i: (i // nb, 0, 0)))
        elif kind == "pos":
            in_specs.append(pl.BlockSpec((1, tm, a.shape[2]), lambda i: (jnp.minimum(i // nb, 1), i % nb, 0)))
        else:
            raise ValueError(kind)
    out_specs, out_shape = [], []
    for o in outs:
        if len(o) == 4 and o[3] == "any":
            out_specs.append(pl.BlockSpec(memory_space=pl.ANY))
            out_shape.append(jax.ShapeDtypeStruct((o[0], o[1]), o[2]))
        elif len(o) == 4 and o[3] == "heads":
            out_specs.append(pl.BlockSpec((o[0], tm, o[1]), lambda i: (0, i, 0)))
            out_shape.append(jax.ShapeDtypeStruct((o[0], m_rows, o[1]), o[2]))
        elif len(o) == 4 and o[3] == "tail":
            nh = n_tiles - o[0] // tm
            out_specs.append(pl.BlockSpec((tm, o[1]), lambda i, nh=nh: (jnp.maximum(i - nh, 0), 0)))
            out_shape.append(jax.ShapeDtypeStruct((o[0], o[1]), o[2]))
        elif len(o) == 4 and o[3] == "ctx":
            nkeep = o[0] // tm
            out_specs.append(pl.BlockSpec((tm, o[1]), lambda i, nkeep=nkeep: (jnp.minimum(i, nkeep - 1), 0)))
            out_shape.append(jax.ShapeDtypeStruct((o[0], o[1]), o[2]))
        else:
            out_specs.append(pl.BlockSpec((tm, o[0]), lambda i: (i, 0)))
            out_shape.append(jax.ShapeDtypeStruct((m_rows, o[0]), o[1]))
    return pl.pallas_call(
        body, grid=(n_tiles,), in_specs=in_specs, out_specs=out_specs, out_shape=out_shape,
        scratch_shapes=list(scratch), compiler_params=params or _params(1), name=name)(*[a for a, _ in ins])


def _modulate(x_ref, mod_ref, shift_row, scale_row):
    m = mod_ref[0]
    return x_ref[...] * (1.0 + m[scale_row:scale_row + 1]) + m[shift_row:shift_row + 1]


def modulation_all(cvec, mod_w, mod_b):
    depth, d, d6 = mod_w.shape
    g = cvec.shape[0]
    gp = -(-g // 16) * 16
    cp = jnp.zeros((gp, d), F32).at[:g].set(cvec)
    tn = 1024

    def body(c_ref, w_ref, b_ref, o_ref):
        c = c_ref[...]
        a = (c * jax.nn.sigmoid(c)).astype(BF16)
        o_ref[0] = _dot(a, w_ref[0].astype(BF16)) + b_ref[0]

    out = pl.pallas_call(
        body, grid=(depth, d6 // tn),
        in_specs=[pl.BlockSpec((gp, d), lambda l, j: (0, 0)),
                  pl.BlockSpec((1, d, tn), lambda l, j: (l, 0, j)),
                  pl.BlockSpec((1, 1, tn), lambda l, j: (l, 0, j))],
        out_specs=pl.BlockSpec((1, gp, tn), lambda l, j: (l, 0, j)),
        out_shape=jax.ShapeDtypeStruct((depth, gp, d6), F32),
        compiler_params=_params(2), name="modulation")(cp, mod_w, mod_b.reshape(depth, 1, d6))
    return out[:, :g].reshape(depth, g, 6, d)


def attention(q, kl, vl, *, n_seq, lq, row_off, n_heads, dq, dv, hpb, kvpb, tq, win,
              start_fn=None, kc=None, vc=None, bias=None, type_fn=None, sink=None, name=None):
    nq = lq // tq
    koff = row_off // lq
    rep = hpb // kvpb
    has_ctx, has_bias, has_sink = kc is not None, bias is not None, sink is not None
    bias_heads = has_bias and bias.shape[0] > 1
    off_blk = row_off // tq

    def kern(*refs):
        it = iter(refs)
        q_ref, kl_ref, vl_ref = next(it), next(it), next(it)
        kc_ref = next(it) if has_ctx else None
        vc_ref = next(it) if has_ctx else None
        b_ref = next(it) if has_bias else None
        s_ref = next(it) if has_sink else None
        o_ref = next(it)
        i = pl.program_id(2)
        hb = pl.program_id(1)
        if start_fn is None:
            start = 0
        else:
            start = pl.multiple_of(start_fn(i), 64)
        nt = (((1,), (1,)), ((), ()))
        scores = []
        for j in range(hpb):
            kv = j // rep
            qj = q_ref[:, j * dq:(j + 1) * dq]
            s = lax.dot_general(qj, kl_ref[kv, pl.ds(start, win), :], nt, preferred_element_type=F32)
            if has_bias:
                s = s + b_ref[j if bias_heads else 0, 0]
            m = jnp.max(s, axis=1, keepdims=True)
            sc = None
            if has_ctx:
                sc = lax.dot_general(qj, kc_ref[kv], nt, preferred_element_type=F32)
                m = jnp.maximum(m, jnp.max(sc, axis=1, keepdims=True))
            scores.append((s, sc, m))
        outs = []
        for j in range(hpb):
            kv = j // rep
            s, sc, m = scores[j]
            if has_sink:
                sk = s_ref[hb * hpb + j]
                m = jnp.maximum(m, sk)
            acc = _dot(jnp.exp2((s - m).astype(BF16)), vl_ref[kv, pl.ds(start, win), :])
            if has_ctx:
                acc = acc + _dot(jnp.exp2((sc - m).astype(BF16)), vc_ref[kv])
            l = acc[:, dv:dv + 1]
            if has_sink:
                l = l + jnp.exp2(sk - m)
            outs.append(acc[:, :dv] / l)
        o_ref[...] = jnp.concatenate(outs, axis=1).astype(o_ref.dtype)

    ins = [q, kl, vl]
    in_specs = [pl.BlockSpec((tq, hpb * dq), lambda b, h, i: (off_blk + b * nq + i, h)),
                pl.BlockSpec((kvpb, lq, dq), lambda b, h, i: (h, koff + b, 0)),
                pl.BlockSpec((kvpb, lq, V_LANES), lambda b, h, i: (h, koff + b, 0))]
    if has_ctx:
        lc = kc.shape[1] // n_seq
        ins += [kc, vc]
        in_specs += [pl.BlockSpec((kvpb, lc, dq), lambda b, h, i: (h, b, 0)),
                     pl.BlockSpec((kvpb, lc, V_LANES), lambda b, h, i: (h, b, 0))]
    if has_bias:
        ins.append(bias * LOG2E)
        hb_blk = hpb if bias_heads else 1
        in_specs.append(pl.BlockSpec((hb_blk, 1, tq, win),
                                     lambda b, h, i: (h if bias_heads else 0, type_fn(i), 0, 0)))
    if has_sink:
        ins.append(sink.astype(F32) * LOG2E)
        in_specs.append(pl.BlockSpec(memory_space=pltpu.SMEM))
    return pl.pallas_call(
        kern, grid=(n_seq, n_heads // hpb, nq), in_specs=in_specs,
        out_specs=pl.BlockSpec((tq, hpb * dv), lambda b, h, i: (b * nq + i, h)),
        out_shape=jax.ShapeDtypeStruct((n_seq * lq, n_heads * dv), BF16),
        compiler_params=_params(3), name=name)(*ins)


def _cache_heads_major(c, values=False):
    n_seq, length, n_heads, d = c.shape
    out = c.astype(BF16).transpose(2, 0, 1, 3).reshape(n_heads, n_seq * length, d)
    if values:
        rows = n_seq * length
        out = jnp.concatenate([out, jnp.ones((n_heads, rows, 1), BF16),
                               jnp.zeros((n_heads, rows, V_LANES - d - 1), BF16)], axis=-1)
    return out


def _store_value_heads(ref, val, width):
    rows = val.shape[0]
    lane = lax.broadcasted_iota(jnp.int32, (rows, V_LANES - width), 1)
    tail = jnp.where(lane == 0, 1.0, 0.0).astype(ref.dtype)
    for hh in range(ref.shape[0]):
        ref[hh] = jnp.concatenate([val[:, hh * width:(hh + 1) * width].astype(ref.dtype), tail], axis=1)


def _store_mla_keys(ref, k_nope, k_rope):
    rows = k_nope.shape[0]
    tail = jnp.concatenate([k_rope, jnp.zeros((rows, B_QPAD - B_NOPE - B_ROPE), k_rope.dtype)], axis=1)
    for hh in range(ref.shape[0]):
        ref[hh] = jnp.concatenate([k_nope[:, hh * B_NOPE:(hh + 1) * B_NOPE], tail], axis=1).astype(ref.dtype)


def _store_heads(ref, val, width):
    for hh in range(ref.shape[0]):
        ref[hh] = val[:, hh * width:(hh + 1) * width].astype(ref.dtype)


def _rope_tables(length, dim, lead, tail, reps):
    half = dim // 2
    nf = half // 2
    t = jnp.arange(length)
    row = (t // GRID_W).astype(F32)
    col = (t % GRID_W).astype(F32)
    inv = ROPE_BASE ** (-jnp.arange(nf, dtype=F32) / nf)
    ang = jnp.concatenate([row[:, None] * inv, col[:, None] * inv], axis=-1)
    cos, sin = jnp.cos(ang), jnp.sin(ang)
    c = jnp.concatenate([jnp.ones((length, lead), F32), cos, cos, jnp.ones((length, tail), F32)] * reps, axis=1)
    s = jnp.concatenate([jnp.zeros((length, lead), F32), sin, sin, jnp.zeros((length, tail), F32)] * reps, axis=1)
    return (jnp.stack([jnp.ones_like(c), c]), jnp.stack([jnp.zeros_like(s), s]))


def _rot_cols(w, dim):
    k, n = w.shape
    wb = w.reshape(k, n // dim, dim)
    half = dim // 2
    return jnp.concatenate([-wb[..., half:], wb[..., :half]], axis=-1).reshape(k, n)


def _cs(phase, n, sign):
    ang = (2.0 * np.pi / n) * (phase % n).astype(F32)
    return jnp.cos(ang), sign * jnp.sin(ang)


def _cblock(wr, wi):
    return jnp.concatenate([jnp.concatenate([wr, -wi], axis=-1), jnp.concatenate([wi, wr], axis=-1)], axis=-2)


SUB = 8


def slow_stage(a, x, *, epi=None, name=None):
    p_n, k, s_n, d = x.shape
    m = a.shape[0]
    x5 = x.reshape(p_n, k, s_n // SUB, SUB, d)

    def kern(*refs):
        if epi is None:
            a_ref, x_ref, o_ref = refs
        else:
            a_ref, x_ref, g_ref, z_ref, s_ref, o_ref = refs
        xt = pltpu.einshape("ksd->skd", x_ref[...])
        av = a_ref[...]
        y = jnp.stack([_dot3(av, xt[s]) for s in range(SUB)], axis=0)
        y = pltpu.einshape("smd->msd", y)
        if epi is not None:
            y = g_ref[...] * (y + s_ref[...][None] * z_ref[...])
        o_ref[...] = y

    blk = lambda rows: pl.BlockSpec((None, rows, None, SUB, d), lambda p, t: (p, 0, t, 0, 0))
    ins = [a, x5]
    in_specs = [pl.BlockSpec((m, k), lambda p, t: (0, 0)), blk(k)]
    if epi is not None:
        gate, z, skip = epi
        ins += [gate.reshape(p_n, m, s_n // SUB, SUB, d), z.reshape(p_n, m, s_n // SUB, SUB, d), skip]
        in_specs += [blk(m), blk(m), pl.BlockSpec((1, d), lambda p, t: (0, 0))]
    out = pl.pallas_call(
        kern, grid=(p_n, s_n // SUB), in_specs=in_specs, out_specs=blk(m),
        out_shape=jax.ShapeDtypeStruct((p_n, m, s_n // SUB, SUB, d), F32),
        compiler_params=_params(2), name=name)(*ins)
    return out.reshape(p_n, m, s_n, d)


def spectral(x, mf, g=None, mi=None, *, dt, g_index=0, epi=None, name=None):
    p_n, planes, k1_n, nin, d = x.shape
    nf = mf.shape[1] // 2
    nout = nf if g is None else mi.shape[1] // 2

    def kern(*refs):
        it = iter(refs)
        x_ref, mf_ref = next(it), next(it)
        g_ref = next(it) if g is not None else None
        mi_ref = next(it) if g is not None else None
        if epi is not None:
            gate_ref, z_ref, skip_ref = next(it), next(it), next(it)
        o_ref = next(it)
        xs = [x_ref[0, pp, 0] for pp in range(planes)]
        xin = xs[0] if planes == 1 else jnp.concatenate(xs, axis=0)
        f = _dot3(mf_ref[0], xin)
        if g is not None:
            fr, fi = f[:nf], f[nf:]
            gr, gi = g_ref[0, 0, 0], g_ref[0, 1, 0]
            y = jnp.concatenate([fr * gr - fi * gi, fr * gi + fi * gr], axis=0)
            f = _dot3(mi_ref[0], y)
        for pp in range(2):
            y = f[pp * nout:(pp + 1) * nout]
            if epi is not None:
                y = gate_ref[0, pp, 0] * (y + skip_ref[...] * z_ref[0, pp, 0])
            o_ref[0, pp, 0] = y

    ins = [x, mf]
    in_specs = [pl.BlockSpec((1, planes, 1, nin, dt), lambda k, j, p: (p, 0, k, 0, j)),
                pl.BlockSpec((1,) + mf.shape[1:], lambda k, j, p: (k, 0, 0))]
    if g is not None:
        ins += [g, mi]
        in_specs += [pl.BlockSpec((1, 2, 1, nf, dt), lambda k, j, p: (g_index, 0, k, 0, j)),
                     pl.BlockSpec((1,) + mi.shape[1:], lambda k, j, p: (k, 0, 0))]
    if epi is not None:
        ins += list(epi)
        in_specs += [pl.BlockSpec((1, 2, 1, nout, dt), lambda k, j, p: (p, 0, k, 0, j)),
                     pl.BlockSpec((1, 2, 1, nout, dt), lambda k, j, p: (p, 0, k, 0, j)),
                     pl.BlockSpec((1, dt), lambda k, j, p: (0, j))]
    return pl.pallas_call(
        kern, grid=(k1_n, d // dt, p_n), in_specs=in_specs,
        out_specs=pl.BlockSpec((1, 2, 1, nout, dt), lambda k, j, p: (p, 0, k, 0, j)),
        out_shape=jax.ShapeDtypeStruct((p_n, 2, k1_n, nout, d), F32),
        compiler_params=_params(3), name=name)(*ins)


def hyena_two_sided_filters(length, w1, b1, w2, b2, w3, freq, log_decay, name):
    hid = w2.shape[0]
    d = w3.shape[1] // 4
    t = jnp.linspace(0.0, 1.0, length, dtype=F32)[:, None]
    ang = 2.0 * jnp.pi * t * jnp.arange(1, C_POS_BANDS + 1, dtype=F32)
    z = jnp.concatenate([t, jnp.cos(ang), jnp.sin(ang)], axis=-1)
    kpad = LANES - z.shape[1]
    z = jnp.pad(z, ((0, 0), (0, kpad)))
    z2 = jnp.concatenate([z, z[::-1]], axis=0)
    w1p = jnp.pad(w1, ((0, kpad), (0, 0)))
    full = lambda a: pl.BlockSpec(a.shape, lambda *_: (0,) * a.ndim)

    def ffn_kern(z_ref, w1_ref, b1_ref, w2_ref, b2_ref, f_ref, a_ref):
        fr = f_ref[...]
        a = jnp.sin(fr * (_dot3(z_ref[...], w1_ref[...]) + b1_ref[...]))
        a_ref[...] = jnp.sin(fr * (_dot3(a, w2_ref[...]) + b2_ref[...]))

    args = [z2, w1p, b1.reshape(1, hid), w2, b2.reshape(1, hid), freq.reshape(1, hid)]
    act = pl.pallas_call(
        ffn_kern, grid=(1,), in_specs=[full(a) for a in args], out_specs=pl.BlockSpec((2 * length, hid), lambda i: (0, 0)),
        out_shape=jax.ShapeDtypeStruct((2 * length, hid), F32), compiler_params=_params(1), name=name + "_ffn")(*args)

    tn = 256
    per = d // tn

    def kern(af_ref, ar_ref, w3f_ref, w3b_ref, ldf_ref, ldb_ref, o_ref):
        row = lax.broadcasted_iota(jnp.int32, (length, tn), 0)
        pos = row.astype(F32) * (1.0 / (length - 1))
        pos_rev = (length - 1 - row).astype(F32) * (1.0 / (length - 1))

        def filt(a_ref, w_ref, ld_ref, tt):
            f = _dot3(a_ref[...], w_ref[...]) * jnp.exp(-jnp.exp(ld_ref[...]) * tt)
            return f / (jnp.sum(jnp.abs(f), axis=0, keepdims=True) + 1e-6)

        hf = filt(af_ref, w3f_ref, ldf_ref, pos)
        hb_rev = filt(ar_ref, w3b_ref, ldb_ref, pos_rev)
        o_ref[0, 0] = hf + jnp.where(row == 0, hb_rev[length - 1:length], 0.0)
        o_ref[0, 1] = jnp.where(row == 0, 0.0, pltpu.roll(hb_rev, 1, 0))

    col = lambda direction: (lambda o, j: (0, (2 * o + direction) * per + j))
    ld = log_decay.reshape(1, 4 * d)
    g = pl.pallas_call(
        kern, grid=(2, per),
        in_specs=[pl.BlockSpec((length, hid), lambda o, j: (0, 0)), pl.BlockSpec((length, hid), lambda o, j: (1, 0)),
                  pl.BlockSpec((hid, tn), col(0)), pl.BlockSpec((hid, tn), col(1)),
                  pl.BlockSpec((1, tn), col(0)), pl.BlockSpec((1, tn), col(1))],
        out_specs=pl.BlockSpec((1, 2, length, tn), lambda o, j: (o, 0, 0, j)),
        out_shape=jax.ShapeDtypeStruct((2, 2, length, d), F32),
        compiler_params=_params(2), name=name)(act, act, w3, w3, ld, ld)
    return g.reshape(2, 2 * length, d)


def short_conv(u, w, b, *, row_off, n_seq, length, parts, name):
    c = u.shape[1]
    dt = 256
    per = c // parts // dt
    first = row_off // length

    def kern(u_ref, w_ref, b_ref, *o_refs):
        x = u_ref[...]
        r = lax.broadcasted_iota(jnp.int32, x.shape, 0)
        prev = jnp.where(r == 0, 0.0, pltpu.roll(x, 1, 0))
        nxt = jnp.where(r == length - 1, 0.0, pltpu.roll(x, length - 1, 0))
        wv = w_ref[...]
        y = prev * wv[0:1] + x * wv[1:2] + nxt * wv[2:3] + b_ref[...]
        part = pl.program_id(1) // per
        for k, o_ref in enumerate(o_refs):
            @pl.when(part == k)
            def _(o_ref=o_ref):
                o_ref[0] = y

    out_spec = lambda k: pl.BlockSpec((1, length, dt), lambda s, j: (s, 0, jnp.clip(j - k * per, 0, per - 1)))
    return pl.pallas_call(
        kern, grid=(n_seq, c // dt),
        in_specs=[pl.BlockSpec((length, dt), lambda s, j: (first + s, j)),
                  pl.BlockSpec((3, dt), lambda s, j: (0, j)),
                  pl.BlockSpec((1, dt), lambda s, j: (0, j))],
        out_specs=[out_spec(k) for k in range(parts)],
        out_shape=[jax.ShapeDtypeStruct((n_seq, length, c // parts), F32)] * parts,
        compiler_params=_params(2), name=name)(u, w, b)


def _dft_mats_single(n, nin_data, nout):
    k = jnp.arange(n, dtype=jnp.int32)
    fr, fi = _cs(k[:, None] * k[None, :nin_data], n, -1.0)
    mf_data = _cblock(fr, fi)[None]
    gr, gi = _cs(k[:, None] * k[None, :], n, -1.0)
    mf_filt = jnp.concatenate([gr, gi], axis=0)[None]
    ir, ii = _cs(k[:nout, None] * k[None, :], n, 1.0)
    mi = _cblock(ir / n, ii / n)[None]
    return mf_data, mf_filt, mi


def _dft_mats_two_stage(n, n2):
    n1 = n // n2
    k1 = jnp.arange(n1, dtype=jnp.int32)
    t1h = jnp.arange(n1 // 2, dtype=jnp.int32)
    ar, ai = _cs((n // n1) * k1[:, None] * t1h[None, :], n, -1.0)
    a_data = _cblock(ar, ai)
    fr, fi = _cs((n // n1) * k1[:, None] * k1[None, :], n, -1.0)
    a_filt = jnp.concatenate([fr, fi], axis=0)
    br, bi = _cs((n // n1) * t1h[:, None] * k1[None, :], n, 1.0)
    a_inv = _cblock(br, bi)
    t2 = jnp.arange(n2, dtype=jnp.int32)
    kk = k1[:, None, None] + n1 * t2[None, :, None]
    mr, mi_ = _cs(kk * t2[None, None, :], n, -1.0)
    mf = _cblock(mr, mi_)
    vr, vi = _cs(jnp.swapaxes(kk, 1, 2) * t2[None, :, None], n, 1.0)
    mi = _cblock(vr / n, vi / n)
    return a_data, a_filt, a_inv, mf, mi


def _route(logits):
    lane = lax.broadcasted_iota(jnp.int32, logits.shape, 1).astype(F32)
    big = 1e9
    lg = jnp.where(lane < N_GROUPS, logits, -jnp.inf)
    mg = jnp.max(lg, axis=1, keepdims=True)
    gi = jnp.min(jnp.where(lg == mg, lane, big), axis=1, keepdims=True)
    p_g = 1.0 / jnp.sum(jnp.exp(lg - mg), axis=1, keepdims=True)
    lo = N_GROUPS + EXPERTS_PER_GROUP * gi
    le = jnp.where((lane >= lo) & (lane < lo + EXPERTS_PER_GROUP), logits, -jnp.inf)
    m1 = jnp.max(le, axis=1, keepdims=True)
    i1 = jnp.min(jnp.where(le == m1, lane, big), axis=1, keepdims=True)
    le2 = jnp.where(lane == i1, -jnp.inf, le)
    m2 = jnp.max(le2, axis=1, keepdims=True)
    i2 = jnp.min(jnp.where(le2 == m2, lane, big), axis=1, keepdims=True)
    e2 = jnp.exp(m2 - m1)
    w1 = p_g / (1.0 + e2)
    w2 = p_g * e2 / (1.0 + e2)
    return lane, i1 - N_GROUPS, i2 - N_GROUPS, w1, w2


def _rank_in_tile(lane, e1, e2):
    tm = lane.shape[0]
    picks = jnp.where(lane == e1, 1.0, 0.0) + jnp.where(lane == e2, 1.0, 0.0)
    r = lax.broadcasted_iota(jnp.int32, (tm, tm), 0)
    c = lax.broadcasted_iota(jnp.int32, (tm, tm), 1)
    earlier = jnp.where(c < r, 1.0, 0.0).astype(BF16)
    return _dot(earlier, picks.astype(BF16)), picks


def _pack_bf16_pairs(y):
    c = y.shape[1] // 2
    hi = lax.bitcast_convert_type(y[:, :c].astype(BF16).astype(F32), jnp.uint32)
    lo = lax.bitcast_convert_type(y[:, c:].astype(BF16).astype(F32), jnp.uint32)
    return hi | (lo >> 16)


def _unpack_bf16_pairs(w):
    hi = lax.bitcast_convert_type(w & jnp.uint32(0xFFFF0000), F32)
    lo = lax.bitcast_convert_type(w << 16, F32)
    return hi, lo


def expert_mlp(xs, blk_e, n_used, w_gate, w_up, w_down, layer):
    p_rows, d = xs.shape
    de = w_gate.shape[3]
    nblk = p_rows // MOE_BLK

    def kern(be_ref, nu_ref, x_ref, wg_ref, wu_ref, wd_ref, o_ref, wg_s, wu_s, wd_s):
        i = pl.program_id(0)
        used = i < nu_ref[0]
        fresh = jnp.logical_or(i == 0, be_ref[i] != be_ref[jnp.maximum(i - 1, 0)])

        @pl.when(jnp.logical_and(used, fresh))
        def _():
            wg_s[...] = wg_ref[0, 0].astype(BF16)
            wu_s[...] = wu_ref[0, 0].astype(BF16)
            wd_s[...] = wd_ref[0, 0].astype(BF16)

        @pl.when(used)
        def _():
            x = x_ref[...].astype(BF16)
            g = _dot(x, wg_s[...])
            u = _dot(x, wu_s[...])
            a = (g * jax.nn.sigmoid(g) * u).astype(BF16)
            o_ref[...] = _pack_bf16_pairs(_dot(a, wd_s[...]))

        @pl.when(jnp.logical_not(used))
        def _():
            o_ref[...] = jnp.zeros_like(o_ref)

    last = lambda i, nu: jnp.minimum(i, nu[0] - 1)
    grid_spec = pltpu.PrefetchScalarGridSpec(
        num_scalar_prefetch=2, grid=(nblk,),
        in_specs=[pl.BlockSpec((MOE_BLK, d), lambda i, be, nu: (last(i, nu), 0)),
                  pl.BlockSpec((1, 1, d, de), lambda i, be, nu: (layer, be[last(i, nu)], 0, 0)),
                  pl.BlockSpec((1, 1, d, de), lambda i, be, nu: (layer, be[last(i, nu)], 0, 0)),
                  pl.BlockSpec((1, 1, de, d), lambda i, be, nu: (layer, be[last(i, nu)], 0, 0))],
        out_specs=pl.BlockSpec((MOE_BLK, d // 2), lambda i, be, nu: (i, 0)),
        scratch_shapes=[pltpu.VMEM((d, de), BF16), pltpu.VMEM((d, de), BF16), pltpu.VMEM((de, d), BF16)])
    return pl.pallas_call(
        kern, grid_spec=grid_spec, out_shape=jax.ShapeDtypeStruct((p_rows, d // 2), jnp.uint32),
        compiler_params=_params(1), name="expert_mlp")(blk_e, n_used, xs, w_gate, w_up, w_down)


def _slot_plan(experts, ranks):
    n = 2 * experts.shape[0]
    ids = jnp.arange(N_EXPERTS, dtype=jnp.int32)
    onehot = experts[..., None] == ids
    counts = jnp.sum(onehot, axis=(0, 1)).astype(jnp.int32)
    padded = (counts + MOE_BLK - 1) // MOE_BLK * MOE_BLK
    pend = jnp.cumsum(padded)
    pstart = pend - padded
    p_rows = -(-n // MOE_BLK) * MOE_BLK + N_EXPERTS * MOE_BLK
    nblk = p_rows // MOE_BLK
    blk_first = jnp.arange(nblk, dtype=jnp.int32) * MOE_BLK
    blk_e = jnp.minimum(jnp.sum(pend[None, :] <= blk_first[:, None], axis=1), N_EXPERTS - 1).astype(jnp.int32)
    n_used = (pend[-1:] // MOE_BLK).astype(jnp.int32)
    blk_ids = jnp.arange(nblk, dtype=jnp.int32)
    partly = jnp.any((blk_ids[:, None] == (pend // MOE_BLK - 1)[None, :]) & (counts % MOE_BLK != 0)[None, :], axis=1)
    zero_blk = (partly | (blk_ids >= n_used[0])).astype(jnp.int32)
    slot = ranks + jnp.sum(jnp.where(onehot, pstart, 0), axis=-1)
    return slot.reshape(n).astype(jnp.int32), blk_e, n_used, zero_blk, p_rows


def _dma_params():
    return pltpu.CompilerParams(dimension_semantics=("arbitrary",), vmem_limit_bytes=VMEM_LIMIT,
                                disable_bounds_checks=True)


DMA_UNROLL = 8


def moe_dispatch(x1, mod, slot, zero_blk, p_rows, group_len, name):
    t, d = x1.shape
    tm = ROW_TILE

    def body(slot_ref, zb_ref, x_ref, mod_ref, xs_ref, h_ref, zero_ref, sem):
        h_ref[...] = _modulate(x_ref, mod_ref, 3, 4)

        @pl.when(pl.program_id(0) == 0)
        def _():
            zero_ref[...] = jnp.zeros_like(zero_ref)

            def zstart(b, c):
                @pl.when(zb_ref[b] != 0)
                def _():
                    first = pl.multiple_of(b * MOE_BLK, MOE_BLK)
                    pltpu.make_async_copy(zero_ref, xs_ref.at[pl.ds(first, MOE_BLK)], sem).start()
                return c

            def zwait(b, c):
                @pl.when(zb_ref[b] != 0)
                def _():
                    pltpu.make_async_copy(zero_ref, xs_ref.at[pl.ds(0, MOE_BLK)], sem).wait()
                return c

            lax.fori_loop(0, p_rows // MOE_BLK, zstart, 0)
            lax.fori_loop(0, p_rows // MOE_BLK, zwait, 0)

        def start(rr, c):
            for u in range(DMA_UNROLL):
                r = rr * DMA_UNROLL + u
                for k in range(2):
                    pltpu.make_async_copy(h_ref.at[pl.ds(r, 1)], xs_ref.at[pl.ds(slot_ref[2 * r + k], 1)], sem).start()
            return c

        lax.fori_loop(0, tm // DMA_UNROLL, start, 0)
        for k in range(2):
            pltpu.make_async_copy(h_ref, xs_ref.at[pl.ds(0, tm)], sem).wait()

    return row_call(body, t, [(slot, "smem_rows"), (zero_blk, "smem"), (x1, "row"), (mod, "group")],
                    [(p_rows, d, F32, "any")], group_len=group_len,
                    scratch=[pltpu.VMEM((tm, d), F32), pltpu.VMEM((MOE_BLK, d), F32), pltpu.SemaphoreType.DMA(())],
                    params=_dma_params(), name=name)[0]


def kernel(x_prompt, x_sample, cache_a_k, cache_a_v, cache_b_ckv, cache_b_krope, cache_d_k, cache_d_v, c_ctx, c, mod_w, mod_b, ln_g, ln_b, a_wq, a_wk, a_wv, a_wo, a_sink, b_wq_a, b_q_norm, b_wq_b, b_wkv_a, b_kv_norm, b_wk_b, b_wv_b, b_wo, c_w_in, c_b_in, c_conv_w, c_conv_b, c_ffn_w1, c_ffn_b1, c_ffn_w2, c_ffn_b2, c_ffn_w3, c_ffn_freq, c_log_decay, c_skip, c_wo, d_wq, d_wk, d_wv, d_wo, d_rel_bias, moe_wr_g, moe_br_g, moe_wr_e, moe_br_e, moe_w_gate, moe_w_up, moe_w_down):
    bc, lc, d = x_prompt.shape
    bl, ll, _ = x_sample.shape
    past = cache_a_k.shape[2]
    gl = ll
    assert bc * lc == gl and d == A_HEADS * HEAD_DIM
    ng = 1 + bl
    t_all = ng * gl
    x_ctx, x_lat = x_prompt.reshape(gl, d), x_sample.reshape(bl * gl, d)
    cvec = jnp.concatenate([c_ctx[None, :], c], axis=0)
    mods = modulation_all(cvec, mod_w, mod_b)
    rc = functools.partial(row_call, group_len=gl)
    row2 = lambda v: v.reshape(1, -1)
    n_ctx_tiles = gl // ROW_TILE

    def post_mixer(i, att_ctx, att_lat, wo, x_in):
        unused = LANES - N_GROUPS - N_EXPERTS
        wr = jnp.concatenate([moe_wr_g[i], moe_wr_e[i], jnp.zeros((d, unused), F32)], axis=1)
        br = jnp.concatenate([moe_br_g[i], moe_br_e[i], jnp.zeros((unused,), F32)])[None, :]

        x_split = isinstance(x_in, tuple)

        def body(attc_ref, attl_ref, *rest):
            is_ctx = pl.program_id(0) < n_ctx_tiles
            if x_split:
                x_old = jnp.where(is_ctx, rest[0][...], rest[1][...])
                rest = rest[2:]
            else:
                x_old = rest[0][...]
                rest = rest[1:]
            mod_ref, wo_ref, g_ref, b_ref, wr_ref, br_ref, x1_ref, info_ref, w_ref, seen_ref = rest

            @pl.when(pl.program_id(0) == 0)
            def _():
                seen_ref[...] = jnp.zeros_like(seen_ref)

            m = mod_ref[0]
            att = jnp.where(is_ctx, attc_ref[...], attl_ref[...])
            o = _dot(att.astype(BF16), wo_ref[...])
            x1 = _layer_norm(DEEPNORM_ALPHA * x_old + m[2:3] * o, g_ref[...], b_ref[...])
            x1_ref[...] = x1
            h = x1 * (1.0 + m[4:5]) + m[3:4]
            lane, e1, e2, w1, w2 = _route(_dot3(h, wr_ref[...]) + br_ref[...])
            before, picks = _rank_in_tile(lane, e1, e2)
            before = before + seen_ref[...]
            r1 = jnp.sum(jnp.where(lane == e1, before, 0.0), axis=1, keepdims=True)
            r2 = jnp.sum(jnp.where(lane == e2, before, 0.0), axis=1, keepdims=True)
            seen_ref[...] += jnp.sum(picks, axis=0, keepdims=True)
            info = jnp.where(lane == 0, e1, jnp.where(lane == 1, e2, jnp.where(lane == 2, r1, jnp.where(lane == 3, r2, 0.0))))
            info_ref[...] = info.astype(jnp.int32)
            w_ref[...] = jnp.where(lane == 0, w1, jnp.where(lane == 1, w2, 0.0))

        return rc(body, t_all,
                  [(att_ctx, "row_head"), (att_lat, "row_tail")]
                  + ([(x_in[0], "row_head"), (x_in[1], "row_tail")] if x_split else [(x_in, "row")])
                  + [(mods[i], "group"), (wo.astype(BF16), "full"),
                   (row2(ln_g[i, 0]), "full"), (row2(ln_b[i, 0]), "full"), (wr, "full"), (br, "full")],
                  [(d, F32), (LANES, jnp.int32), (LANES, F32)],
                  scratch=[pltpu.VMEM((1, LANES), F32)], name=f"post_mixer{i}")

    def moe(i, x1, info, w):
        slot, blk_e, n_used, zero_blk, p_rows = _slot_plan(info[:, 0:2], info[:, 2:4])
        xs = moe_dispatch(x1, mods[i], slot, zero_blk, p_rows, gl, f"moe_dispatch{i}")
        ys = expert_mlp(xs, blk_e, n_used, moe_w_gate, moe_w_up, moe_w_down, i)
        tm = ROW_TILE

        split_out = i == DEPTH - 1

        def body(slot_ref, x_ref, w_ref, mod_ref, g_ref, b_ref, ys_ref, *rest):
            out_refs, (buf, sem) = rest[:-2], rest[-2:]

            def start(rr, c):
                for u in range(DMA_UNROLL):
                    r = rr * DMA_UNROLL + u
                    for k in range(2):
                        pltpu.make_async_copy(ys_ref.at[pl.ds(slot_ref[2 * r + k], 1)], buf.at[k, pl.ds(r, 1)], sem).start()
                return c

            lax.fori_loop(0, tm // DMA_UNROLL, start, 0)
            for k in range(2):
                pltpu.make_async_copy(ys_ref.at[pl.ds(0, tm)], buf.at[k], sem).wait()
            m = mod_ref[0]
            wv = w_ref[...]
            a_hi, a_lo = _unpack_bf16_pairs(buf[0])
            b_hi, b_lo = _unpack_bf16_pairs(buf[1])
            w0, w1 = wv[:, 0:1], wv[:, 1:2]
            y = jnp.concatenate([w0 * a_hi + w1 * b_hi, w0 * a_lo + w1 * b_lo], axis=1)
            x2 = _layer_norm(DEEPNORM_ALPHA * x_ref[...] + m[5:6] * y, g_ref[...], b_ref[...])
            if not split_out:
                out_refs[0][...] = x2
            else:
                is_ctx = pl.program_id(0) < n_ctx_tiles

                @pl.when(is_ctx)
                def _():
                    out_refs[0][...] = x2

                @pl.when(jnp.logical_not(is_ctx))
                def _():
                    out_refs[1][...] = x2

        outs = [(gl, d, F32, "ctx"), (t_all - gl, d, F32, "tail")] if split_out else [(d, F32)]
        res = rc(body, t_all,
                 [(slot, "smem_rows"), (x1, "row"), (w, "row"), (mods[i], "group"),
                  (row2(ln_g[i, 1]), "full"), (row2(ln_b[i, 1]), "full"), (ys, "any")],
                 outs, scratch=[pltpu.VMEM((2, tm, d // 2), jnp.uint32), pltpu.SemaphoreType.DMA(())],
                 params=_dma_params(), name=f"moe_combine{i}")
        return res if split_out else res[0]

    def finish_layer(i, att_ctx, att_lat, wo, x_in):
        x1, info, w = post_mixer(i, att_ctx, att_lat, wo, x_in)
        return moe(i, x1, info, w)

    tq = 256
    i = 0
    hd = HEAD_DIM
    kvw = A_KV_HEADS * hd
    cq, sq = _rope_tables(gl, hd, 0, 0, LANES // hd)
    wq, wk, wv = a_wq[0], a_wk[0], a_wv[0]

    def body_a(xc_ref, xl_ref, mod_ref, wq_ref, wqr_ref, wk_ref, wkr_ref, wv_ref, c_ref, s_ref, q_ref, kh_ref, vh_ref, kc_ref, vc_ref):
        x_in = jnp.where(pl.program_id(0) < n_ctx_tiles, xc_ref[...], xl_ref[...])
        m = mod_ref[0]
        h = (x_in * (1.0 + m[1:2]) + m[0:1]).astype(BF16)
        cc, ss = c_ref[0], s_ref[0]
        q = _dot(h, wq_ref[...]) * _tile_lanes(cc, d) + _dot(h, wqr_ref[...]) * _tile_lanes(ss, d)
        k = _dot(h, wk_ref[...]) * _tile_lanes(cc, kvw) + _dot(h, wkr_ref[...]) * _tile_lanes(ss, kvw)
        v = _dot(h, wv_ref[...])
        q_ref[...] = (q * QSCALE_64).astype(BF16)
        _store_heads(kh_ref, k, hd)
        _store_value_heads(vh_ref, v, hd)

        @pl.when(pl.program_id(0) < n_ctx_tiles)
        def _():
            kc_ref[...] = k
            vc_ref[...] = v

    q, kh, vh, k_new, v_new = rc(
        body_a, t_all,
        [(x_ctx, "row_head"), (x_lat, "row_tail"), (mods[i], "group"), (wq.astype(BF16), "full"),
         (_rot_cols(wq, hd).astype(BF16), "full"),
         (wk.astype(BF16), "full"), (_rot_cols(wk, hd).astype(BF16), "full"), (wv.astype(BF16), "full"),
         (cq, "pos"), (sq, "pos")],
        [(d, BF16), (A_KV_HEADS, hd, BF16, "heads"), (A_KV_HEADS, V_LANES, BF16, "heads"),
         (gl, kvw, F32, "ctx"), (gl, kvw, F32, "ctx")], name="proj_a")
    out_a_k = k_new.reshape(bc, 1, lc, A_KV_HEADS, hd)
    out_a_v = v_new.reshape(bc, 1, lc, A_KV_HEADS, hd)
    common = dict(n_heads=A_HEADS, dq=hd, dv=hd, hpb=8, kvpb=2, tq=tq, sink=a_sink[0])
    att_c = attention(q, kh, vh, n_seq=bc, lq=lc, row_off=0, win=lc, name="attn_a_ctx", **common)
    win_a = 2 * tq
    nq_l = gl // tq
    qi = jnp.arange(tq)[:, None]
    ki = jnp.arange(win_a)[None, :]
    band = jnp.stack([jnp.where(jnp.abs(ki - (qi + off)) <= A_WINDOW, 0.0, NEG_INF)
                      for off in (0, A_WINDOW, 2 * A_WINDOW)]).astype(F32)[None]
    att_l = attention(q, kh, vh, n_seq=bl, lq=gl, row_off=gl, win=win_a,
                      start_fn=lambda ii: jnp.clip(ii * tq - A_WINDOW, 0, gl - win_a),
                      kc=_cache_heads_major(cache_a_k[:, 0]), vc=_cache_heads_major(cache_a_v[:, 0], values=True),
                      bias=band, type_fn=lambda ii: jnp.where(ii == 0, 0, jnp.where(ii == nq_l - 1, 2, 1)),
                      name="attn_a_lat", **common)
    x = finish_layer(i, att_c, att_l, a_wo[0], (x_ctx, x_lat))

    i = 1
    hq = B_NOPE + B_ROPE
    qw = B_HEADS * B_QPAD
    wqb = b_wq_b[0].reshape(B_Q_RANK, B_HEADS, hq)
    wqb_rot = jnp.concatenate([jnp.zeros_like(wqb[..., :B_NOPE]),
                               _rot_cols(wqb[..., B_NOPE:].reshape(B_Q_RANK, -1), B_ROPE).reshape(B_Q_RANK, B_HEADS, B_ROPE)],
                              axis=-1)
    padq = lambda wz: jnp.pad(wz, ((0, 0), (0, 0), (0, B_QPAD - hq))).reshape(B_Q_RANK, qw).astype(BF16)
    wkv_c, wkv_r = b_wkv_a[0][:, :B_KV_RANK], b_wkv_a[0][:, B_KV_RANK:]
    cqb, sqb = _rope_tables(gl, B_ROPE, B_NOPE, B_QPAD - hq, 1)
    ckr, skr = _rope_tables(gl, B_ROPE, 0, 0, 1)

    def body_b(x_ref, mod_ref, wqa_ref, qn_ref, wqb_ref, wqbr_ref, wc_ref, kn_ref, wr_ref, wrr_ref, wkb_ref, wvb_ref,
               cq_ref, sq_ref, ck_ref, sk_ref, q_ref, kh_ref, vh_ref, ckv_ref, kr_ref):
        h = _modulate(x_ref, mod_ref, 0, 1).astype(BF16)
        qa = _rms(_dot(h, wqa_ref[...]), qn_ref[...]).astype(BF16)
        q = (_dot(qa, wqb_ref[...]) * _tile_lanes(cq_ref[0], qw) + _dot(qa, wqbr_ref[...]) * _tile_lanes(sq_ref[0], qw))
        q_ref[...] = (q * QSCALE_MLA).astype(BF16)
        ckv = _rms(_dot(h, wc_ref[...]), kn_ref[...])
        kr = _dot(h, wr_ref[...]) * ck_ref[0] + _dot(h, wrr_ref[...]) * sk_ref[0]
        cb = ckv.astype(BF16)
        _store_mla_keys(kh_ref, _dot(cb, wkb_ref[...]), kr)
        _store_value_heads(vh_ref, _dot(cb, wvb_ref[...]), B_VDIM)

        @pl.when(pl.program_id(0) < n_ctx_tiles)
        def _():
            ckv_ref[...] = ckv
            kr_ref[...] = kr

    wkb, wvb = b_wk_b[0].astype(BF16), b_wv_b[0].astype(BF16)
    q, kh, vh, ckv_new, kr_new = rc(
        body_b, t_all,
        [(x, "row"), (mods[i], "group"), (b_wq_a[0].astype(BF16), "full"), (row2(b_q_norm[0]), "full"),
         (padq(wqb), "full"), (padq(wqb_rot), "full"), (wkv_c.astype(BF16), "full"), (row2(b_kv_norm[0]), "full"),
         (wkv_r.astype(BF16), "full"), (_rot_cols(wkv_r, B_ROPE).astype(BF16), "full"), (wkb, "full"), (wvb, "full"),
         (cqb, "pos"), (sqb, "pos"), (ckr, "pos"), (skr, "pos")],
        [(qw, BF16), (B_HEADS, B_QPAD, BF16, "heads"), (B_HEADS, V_LANES, BF16, "heads"),
         (gl, B_KV_RANK, F32, "ctx"), (gl, B_ROPE, F32, "ctx")], name="proj_b")
    out_b_ckv = ckv_new.reshape(bc, 1, lc, B_KV_RANK)
    out_b_krope = kr_new.reshape(bc, 1, lc, B_ROPE)

    def body_bc(c_ref, r_ref, wkb_ref, wvb_ref, kh_ref, vh_ref):
        cb = c_ref[...].astype(BF16)
        _store_mla_keys(kh_ref, _dot(cb, wkb_ref[...]), r_ref[...])
        _store_value_heads(vh_ref, _dot(cb, wvb_ref[...]), B_VDIM)

    n_pc = bl * past
    kh_p, vh_p = row_call(body_bc, n_pc,
                          [(cache_b_ckv[:, 0].reshape(n_pc, B_KV_RANK), "row"), (cache_b_krope[:, 0].reshape(n_pc, B_ROPE), "row"),
                           (wkb, "full"), (wvb, "full")],
                          [(B_HEADS, B_QPAD, BF16, "heads"), (B_HEADS, V_LANES, BF16, "heads")],
                          tm=min(ROW_TILE, n_pc), name="proj_b_past")
    common = dict(n_heads=B_HEADS, dq=B_QPAD, dv=B_VDIM, hpb=2, kvpb=2)
    att_c = attention(q, kh, vh, n_seq=bc, lq=lc, row_off=0, win=lc, tq=tq, name="attn_b_ctx", **common)
    att_l = attention(q, kh, vh, n_seq=bl, lq=gl, row_off=gl, win=gl, kc=kh_p, vc=vh_p, tq=min(2 * tq, gl),
                      name="attn_b_lat", **common)
    x = finish_layer(i, att_c, att_l, b_wo[0], x)

    i = 2
    d3 = 3 * d

    def body_c(x_ref, mod_ref, w_ref, b_ref, u_ref):
        h = _modulate(x_ref, mod_ref, 0, 1).astype(BF16)
        for j in range(3):
            u_ref[:, j * d:(j + 1) * d] = _dot(h, w_ref[:, j * d:(j + 1) * d]) + b_ref[:, j * d:(j + 1) * d]

    (u,) = rc(body_c, t_all, [(x, "row"), (mods[i], "group"), (c_w_in[0].astype(BF16), "full"), (row2(c_b_in[0]), "full")],
              [(d3, F32)], name="proj_c")
    u_c = short_conv(u, c_conv_w[0], row2(c_conv_b[0]), row_off=0, n_seq=bc, length=lc, parts=3, name="short_conv_ctx")
    u_l = short_conv(u, c_conv_w[0], row2(c_conv_b[0]), row_off=gl, n_seq=bl, length=gl, parts=3, name="short_conv_lat")
    fargs = (c_ffn_w1[0], c_ffn_b1[0], c_ffn_w2[0], c_ffn_b2[0], c_ffn_w3[0], c_ffn_freq[0], c_log_decay[0])
    g_c = hyena_two_sided_filters(lc, *fargs, name="filters_ctx")
    g_l = hyena_two_sided_filters(gl, *fargs, name="filters_lat")

    nc = 2 * lc
    mf_data, mf_filt, mi_c = _dft_mats_single(nc, lc, lc)
    spec_c = spectral(g_c.reshape(2, 1, 1, nc, d), mf_filt, dt=512, name="filt_spec_ctx")
    z_c = u_c[0].reshape(bc // 2, 2, 1, lc, d)
    for o in range(2):
        gate = u_c[o + 1].reshape(bc // 2, 2, 1, lc, d)
        z_c = spectral(z_c, mf_data, spec_c, mi_c, dt=512, g_index=o, epi=(gate, z_c, row2(c_skip[0, o])),
                       name=f"conv_ctx{o}")
    zc_out = z_c.reshape(gl, d)

    nl = 2 * gl
    n1 = nl // FFT_N2
    a_data, a_filt, a_inv, mf_l, mi_l = _dft_mats_two_stage(nl, FFT_N2)
    ga = slow_stage(a_filt, g_l.reshape(2, n1, FFT_N2, d), name="filt_stage_a")
    spec_l = spectral(ga.reshape(2, 2, n1, FFT_N2, d), mf_l, dt=1024, name="filt_spec_lat")
    z_l = u_l[0].reshape(bl // 2, n1, FFT_N2, d)
    for o in range(2):
        za = slow_stage(a_data, z_l, name=f"conv_lat_a{o}")
        zb = spectral(za.reshape(bl // 2, 2, n1, FFT_N2, d), mf_l, spec_l, mi_l, dt=1024, g_index=o,
                      name=f"conv_lat_c{o}")
        gate = u_l[o + 1].reshape(bl // 2, n1, FFT_N2, d)
        z_l = slow_stage(a_inv, zb.reshape(bl // 2, 2 * n1, FFT_N2, d), epi=(gate, z_l, row2(c_skip[0, o])),
                         name=f"conv_lat_i{o}")
    x = finish_layer(i, zc_out, z_l.reshape(bl * gl, d), c_wo[0], x)

    i = 3

    def body_d(x_ref, mod_ref, wq_ref, wk_ref, wv_ref, q_ref, kh_ref, vh_ref, kc_ref, vc_ref):
        h = _modulate(x_ref, mod_ref, 0, 1).astype(BF16)
        q_ref[...] = (_dot(h, wq_ref[...]) * QSCALE_64).astype(BF16)
        k = _dot(h, wk_ref[...])
        v = _dot(h, wv_ref[...])
        _store_heads(kh_ref, k, hd)
        _store_value_heads(vh_ref, v, hd)

        @pl.when(pl.program_id(0) < n_ctx_tiles)
        def _():
            kc_ref[...] = k
            vc_ref[...] = v

    q, k_heads, v_heads, k_new, v_new = rc(
        body_d, t_all,
        [(x, "row"), (mods[i], "group"), (d_wq[0].astype(BF16), "full"), (d_wk[0].astype(BF16), "full"),
         (d_wv[0].astype(BF16), "full")],
        [(d, BF16), (D_HEADS, hd, BF16, "heads"), (D_HEADS, V_LANES, BF16, "heads"), (gl, d, F32, "ctx"), (gl, d, F32, "ctx")],
        name="proj_d")
    out_d_k = k_new.reshape(bc, 1, lc, D_HEADS, hd)
    out_d_v = v_new.reshape(bc, 1, lc, D_HEADS, hd)
    common = dict(n_heads=D_HEADS, dq=hd, dv=hd, hpb=4, kvpb=4, tq=tq)
    att_c = attention(q, k_heads, v_heads, n_seq=bc, lq=lc, row_off=0, win=lc, name="attn_d_ctx", **common)
    rows = gl // GRID_W
    kh = min(MAX_NBR_ROWS, rows)
    qrows = tq // GRID_W
    krows = qrows + kh
    win_d = krows * GRID_W
    nq_l = gl // tq
    qr_l, kr_l = jnp.arange(qrows)[:, None], jnp.arange(krows)[None, :]
    qc, kc_ = jnp.arange(GRID_W)[:, None], jnp.arange(GRID_W)[None, :]
    c0 = jnp.clip(qc - NBR_COLS // 2, 0, GRID_W - NBR_COLS)
    col_ok = (kc_ >= c0) & (kc_ < c0 + NBR_COLS)
    dc = jnp.clip(kc_ - qc, 1 - NBR_COLS, NBR_COLS - 1) + NBR_COLS - 1
    dc_hot = (dc[..., None] == jnp.arange(2 * NBR_COLS - 1)).astype(F32)
    by_col = jnp.einsum("hrc,xyc->hrxy", d_rel_bias[0], dc_hot, precision=lax.Precision.HIGHEST)
    tabs = []
    for off, lo in ((0, jnp.zeros_like(qr_l)), (kh // 2, qr_l), (kh, jnp.full_like(qr_l, qrows))):
        row_ok = (kr_l >= lo) & (kr_l < lo + kh)
        dr = jnp.clip(kr_l - off - qr_l + MAX_NBR_ROWS - 1, 0, 2 * MAX_NBR_ROWS - 2)
        dr_hot = (dr[..., None] == jnp.arange(2 * MAX_NBR_ROWS - 1)).astype(F32)
        tab = jnp.einsum("qkr,hrxy->hqxky", dr_hot, by_col, precision=lax.Precision.HIGHEST)
        ok = row_ok[:, None, :, None] & col_ok[None, :, None, :]
        tabs.append(jnp.where(ok[None], tab, NEG_INF).reshape(D_HEADS, tq, win_d))
    nbr_bias = jnp.stack(tabs, axis=1).astype(F32)
    att_l = attention(q, k_heads, v_heads, n_seq=bl, lq=gl, row_off=gl, win=win_d,
                      start_fn=lambda ii: jnp.clip(ii * qrows - kh // 2, 0, rows - krows) * GRID_W,
                      kc=_cache_heads_major(cache_d_k[:, 0]), vc=_cache_heads_major(cache_d_v[:, 0], values=True),
                      bias=nbr_bias, type_fn=lambda ii: jnp.where(ii == 0, 0, jnp.where(ii == nq_l - 1, 2, 1)),
                      name="attn_d_lat", **common)
    y_ctx, y_lat = finish_layer(i, att_c, att_l, d_wo[0], x)

    y_prompt = y_ctx.reshape(bc, lc, d)
    y_sample = y_lat.reshape(bl, gl, d)
    return (y_prompt, y_sample, out_a_k, out_a_v, out_b_ckv, out_b_krope, out_d_k, out_d_v)
```

```python
import functools

import jax
import jax.numpy as jnp
import numpy as np
from jax import lax
from jax.experimental import pallas as pl
from jax.experimental.pallas import tpu as pltpu

F32 = jnp.float32
BF16 = jnp.bfloat16

GRID_W = 64
HEAD_DIM = 64
ROPE_BASE = 10000.0
LN_EPS = 1e-5
RMS_EPS = 1e-6
NEG_INF = -1e30
DEPTH = 4
DEEPNORM_ALPHA = (2 * DEPTH) ** 0.25
A_HEADS = 16
A_KV_HEADS = 4
A_WINDOW = 128
B_HEADS = 16
B_Q_RANK = 384
B_KV_RANK = 256
B_NOPE = 64
B_ROPE = 32
B_VDIM = 64
B_QPAD = 128
C_POS_BANDS = 16
D_HEADS = 16
MAX_NBR_ROWS = 8
NBR_COLS = 16
N_GROUPS = 4
EXPERTS_PER_GROUP = 8
N_EXPERTS = N_GROUPS * EXPERTS_PER_GROUP
D_EXPERT = 512
MOE_BLK = 512
LANES = 128
V_LANES = 128
LOG2E = 1.4426950408889634
QSCALE_64 = HEAD_DIM ** -0.5 * LOG2E
QSCALE_MLA = (B_NOPE + B_ROPE) ** -0.5 * LOG2E
FFT_N2 = 128
VMEM_LIMIT = 56 * 1024 * 1024
ROW_TILE = 512


def _params(n_axes):
    return pltpu.CompilerParams(dimension_semantics=("arbitrary",) * n_axes, vmem_limit_bytes=VMEM_LIMIT)


def _dot(a, b):
    return jnp.dot(a, b, preferred_element_type=F32)


def _split(x):
    hi = x.astype(BF16)
    lo = (x - hi.astype(F32)).astype(BF16)
    return hi, lo


def _dot3(a, b):
    ah, al = _split(a)
    bh, bl = _split(b)
    return _dot(ah, bh) + _dot(ah, bl) + _dot(al, bh)


def _layer_norm(y, g, b):
    mu = jnp.mean(y, axis=-1, keepdims=True)
    d = y - mu
    var = jnp.mean(d * d, axis=-1, keepdims=True)
    return d * lax.rsqrt(var + LN_EPS) * g + b


def _rms(y, g):
    return y * lax.rsqrt(jnp.mean(y * y, axis=-1, keepdims=True) + RMS_EPS) * g


def _tile_lanes(t, n):
    reps = n // t.shape[-1]
    return t if reps == 1 else jnp.concatenate([t] * reps, axis=-1)


def row_call(body, m_rows, ins, outs, *, tm=ROW_TILE, group_len=None, name=None, scratch=(), params=None):
    nb = None if group_len is None else group_len // tm
    n_tiles = m_rows // tm
    in_specs = []
    for a, kind in ins:
        if kind == "row":
            in_specs.append(pl.BlockSpec((tm, a.shape[1]), lambda i: (i, 0)))
        elif kind == "row_head":
            nh = a.shape[0] // tm
            in_specs.append(pl.BlockSpec((tm, a.shape[1]), lambda i, nh=nh: (jnp.minimum(i, nh - 1), 0)))
        elif kind == "row_tail":
            nh = n_tiles - a.shape[0] // tm
            in_specs.append(pl.BlockSpec((tm, a.shape[1]), lambda i, nh=nh: (jnp.maximum(i - nh, 0), 0)))
        elif kind == "any":
            in_specs.append(pl.BlockSpec(memory_space=pl.ANY))
        elif kind == "smem":
            in_specs.append(pl.BlockSpec(memory_space=pltpu.SMEM))
        elif kind == "smem_rows":
            in_specs.append(pl.BlockSpec((a.shape[0] // n_tiles,), lambda i: (i,), memory_space=pltpu.SMEM))
        elif kind == "smem_rows_next":
            in_specs.append(pl.BlockSpec((a.shape[0] // n_tiles,), lambda i: (jnp.minimum(i + 1, n_tiles - 1),),
                                         memory_space=pltpu.SMEM))
        elif kind == "full":
            in_specs.append(pl.BlockSpec(a.shape, lambda i, nd=a.ndim: (0,) * nd))
        elif kind == "group":
            in_specs.append(pl.BlockSpec((1,) + a.shape[1:], lambda i: (i // nb, 0, 0)))
        elif kind == "pos":
            in_specs.append(pl.BlockSpec((1, tm, a.shape[2]), lambda i: (jnp.minimum(i // nb, 1), i % nb, 0)))
        else:
            raise ValueError(kind)
    out_specs, out_shape = [], []
    for o in outs:
        if len(o) == 4 and o[3] == "any":
            out_specs.append(pl.BlockSpec(memory_space=pl.ANY))
            out_shape.append(jax.ShapeDtypeStruct((o[0], o[1]), o[2]))
        elif len(o) == 4 and o[3] == "heads":
            out_specs.append(pl.BlockSpec((o[0], tm, o[1]), lambda i: (0, i, 0)))
            out_shape.append(jax.ShapeDtypeStruct((o[0], m_rows, o[1]), o[2]))
        elif len(o) == 4 and o[3] == "tail":
            nh = n_tiles - o[0] // tm
            out_specs.append(pl.BlockSpec((tm, o[1]), lambda i, nh=nh: (jnp.maximum(i - nh, 0), 0)))
            out_shape.append(jax.ShapeDtypeStruct((o[0], o[1]), o[2]))
        elif len(o) == 4 and o[3] == "ctx":
            nkeep = o[0] // tm
            out_specs.append(pl.BlockSpec((tm, o[1]), lambda i, nkeep=nkeep: (jnp.minimum(i, nkeep - 1), 0)))
            out_shape.append(jax.ShapeDtypeStruct((o[0], o[1]), o[2]))
        else:
            out_specs.append(pl.BlockSpec((tm, o[0]), lambda i: (i, 0)))
            out_shape.append(jax.ShapeDtypeStruct((m_rows, o[0]), o[1]))
    return pl.pallas_call(
        body, grid=(n_tiles,), in_specs=in_specs, out_specs=out_specs, out_shape=out_shape,
        scratch_shapes=list(scratch), compiler_params=params or _params(1), name=name)(*[a for a, _ in ins])


def _modulate(x_ref, mod_ref, shift_row, scale_row):
    m = mod_ref[0]
    return x_ref[...] * (1.0 + m[scale_row:scale_row + 1]) + m[shift_row:shift_row + 1]


def modulation_all(cvec, mod_w, mod_b):
    depth, d, d6 = mod_w.shape
    g = cvec.shape[0]
    gp = -(-g // 16) * 16
    cp = jnp.zeros((gp, d), F32).at[:g].set(cvec)
    tn = 1024

    def body(c_ref, w_ref, b_ref, o_ref):
        c = c_ref[...]
        a = (c * jax.nn.sigmoid(c)).astype(BF16)
        o_ref[0] = _dot(a, w_ref[0].astype(BF16)) + b_ref[0]

    out = pl.pallas_call(
        body, grid=(depth, d6 // tn),
        in_specs=[pl.BlockSpec((gp, d), lambda l, j: (0, 0)),
                  pl.BlockSpec((1, d, tn), lambda l, j: (l, 0, j)),
                  pl.BlockSpec((1, 1, tn), lambda l, j: (l, 0, j))],
        out_specs=pl.BlockSpec((1, gp, tn), lambda l, j: (l, 0, j)),
        out_shape=jax.ShapeDtypeStruct((depth, gp, d6), F32),
        compiler_params=_params(2), name="modulation")(cp, mod_w, mod_b.reshape(depth, 1, d6))
    return out[:, :g].reshape(depth, g, 6, d)


def attention(q, kl, vl, *, n_seq, lq, row_off, n_heads, dq, dv, hpb, kvpb, tq, win,
              start_fn=None, kc=None, vc=None, bias=None, type_fn=None, sink=None, name=None):
    nq = lq // tq
    koff = row_off // lq
    rep = hpb // kvpb
    has_ctx, has_bias, has_sink = kc is not None, bias is not None, sink is not None
    bias_heads = has_bias and bias.shape[0] > 1
    off_blk = row_off // tq

    def kern(*refs):
        it = iter(refs)
        q_ref, kl_ref, vl_ref = next(it), next(it), next(it)
        kc_ref = next(it) if has_ctx else None
        vc_ref = next(it) if has_ctx else None
        b_ref = next(it) if has_bias else None
        s_ref = next(it) if has_sink else None
        o_ref = next(it)
        i = pl.program_id(2)
        hb = pl.program_id(1)
        if start_fn is None:
            start = 0
        else:
            start = pl.multiple_of(start_fn(i), 64)
        nt = (((1,), (1,)), ((), ()))
        scores = []
        for j in range(hpb):
            kv = j // rep
            qj = q_ref[:, j * dq:(j + 1) * dq]
            s = lax.dot_general(qj, kl_ref[kv, pl.ds(start, win), :], nt, preferred_element_type=F32)
            if has_bias:
                s = s + b_ref[j if bias_heads else 0, 0]
            m = jnp.max(s, axis=1, keepdims=True)
            sc = None
            if has_ctx:
                sc = lax.dot_general(qj, kc_ref[kv], nt, preferred_element_type=F32)
                m = jnp.maximum(m, jnp.max(sc, axis=1, keepdims=True))
            scores.append((s, sc, m))
        outs = []
        for j in range(hpb):
            kv = j // rep
            s, sc, m = scores[j]
            if has_sink:
                sk = s_ref[hb * hpb + j]
                m = jnp.maximum(m, sk)
            acc = _dot(jnp.exp2((s - m).astype(BF16)), vl_ref[kv, pl.ds(start, win), :])
            if has_ctx:
                acc = acc + _dot(jnp.exp2((sc - m).astype(BF16)), vc_ref[kv])
            l = acc[:, dv:dv + 1]
            if has_sink:
                l = l + jnp.exp2(sk - m)
            outs.append(acc[:, :dv] / l)
        o_ref[...] = jnp.concatenate(outs, axis=1).astype(o_ref.dtype)

    ins = [q, kl, vl]
    in_specs = [pl.BlockSpec((tq, hpb * dq), lambda b, h, i: (off_blk + b * nq + i, h)),
                pl.BlockSpec((kvpb, lq, dq), lambda b, h, i: (h, koff + b, 0)),
                pl.BlockSpec((kvpb, lq, V_LANES), lambda b, h, i: (h, koff + b, 0))]
    if has_ctx:
        lc = kc.shape[1] // n_seq
        ins += [kc, vc]
        in_specs += [pl.BlockSpec((kvpb, lc, dq), lambda b, h, i: (h, b, 0)),
                     pl.BlockSpec((kvpb, lc, V_LANES), lambda b, h, i: (h, b, 0))]
    if has_bias:
        ins.append(bias * LOG2E)
        hb_blk = hpb if bias_heads else 1
        in_specs.append(pl.BlockSpec((hb_blk, 1, tq, win),
                                     lambda b, h, i: (h if bias_heads else 0, type_fn(i), 0, 0)))
    if has_sink:
        ins.append(sink.astype(F32) * LOG2E)
        in_specs.append(pl.BlockSpec(memory_space=pltpu.SMEM))
    return pl.pallas_call(
        kern, grid=(n_seq, n_heads // hpb, nq), in_specs=in_specs,
        out_specs=pl.BlockSpec((tq, hpb * dv), lambda b, h, i: (b * nq + i, h)),
        out_shape=jax.ShapeDtypeStruct((n_seq * lq, n_heads * dv), BF16),
        compiler_params=_params(3), name=name)(*ins)


def _cache_heads_major(c, values=False):
    n_seq, length, n_heads, d = c.shape
    out = c.astype(BF16).transpose(2, 0, 1, 3).reshape(n_heads, n_seq * length, d)
    if values:
        rows = n_seq * length
        out = jnp.concatenate([out, jnp.ones((n_heads, rows, 1), BF16),
                               jnp.zeros((n_heads, rows, V_LANES - d - 1), BF16)], axis=-1)
    return out


def _store_value_heads(ref, val, width):
    rows = val.shape[0]
    lane = lax.broadcasted_iota(jnp.int32, (rows, V_LANES - width), 1)
    tail = jnp.where(lane == 0, 1.0, 0.0).astype(ref.dtype)
    for hh in range(ref.shape[0]):
        ref[hh] = jnp.concatenate([val[:, hh * width:(hh + 1) * width].astype(ref.dtype), tail], axis=1)


def _store_mla_keys(ref, k_nope, k_rope):
    rows = k_nope.shape[0]
    tail = jnp.concatenate([k_rope, jnp.zeros((rows, B_QPAD - B_NOPE - B_ROPE), k_rope.dtype)], axis=1)
    for hh in range(ref.shape[0]):
        ref[hh] = jnp.concatenate([k_nope[:, hh * B_NOPE:(hh + 1) * B_NOPE], tail], axis=1).astype(ref.dtype)


def _store_heads(ref, val, width):
    for hh in range(ref.shape[0]):
        ref[hh] = val[:, hh * width:(hh + 1) * width].astype(ref.dtype)


def _rope_tables(length, dim, lead, tail, reps):
    half = dim // 2
    nf = half // 2
    t = jnp.arange(length)
    row = (t // GRID_W).astype(F32)
    col = (t % GRID_W).astype(F32)
    inv = ROPE_BASE ** (-jnp.arange(nf, dtype=F32) / nf)
    ang = jnp.concatenate([row[:, None] * inv, col[:, None] * inv], axis=-1)
    cos, sin = jnp.cos(ang), jnp.sin(ang)
    c = jnp.concatenate([jnp.ones((length, lead), F32), cos, cos, jnp.ones((length, tail), F32)] * reps, axis=1)
    s = jnp.concatenate([jnp.zeros((length, lead), F32), sin, sin, jnp.zeros((length, tail), F32)] * reps, axis=1)
    return (jnp.stack([jnp.ones_like(c), c]), jnp.stack([jnp.zeros_like(s), s]))


def _rot_cols(w, dim):
    k, n = w.shape
    wb = w.reshape(k, n // dim, dim)
    half = dim // 2
    return jnp.concatenate([-wb[..., half:], wb[..., :half]], axis=-1).reshape(k, n)


def _cs(phase, n, sign):
    ang = (2.0 * np.pi / n) * (phase % n).astype(F32)
    return jnp.cos(ang), sign * jnp.sin(ang)


def _cblock(wr, wi):
    return jnp.concatenate([jnp.concatenate([wr, -wi], axis=-1), jnp.concatenate([wi, wr], axis=-1)], axis=-2)


SUB = 8


def slow_stage(a, x, *, epi=None, name=None):
    p_n, k, s_n, d = x.shape
    m = a.shape[0]
    x5 = x.reshape(p_n, k, s_n // SUB, SUB, d)

    def kern(*refs):
        if epi is None:
            a_ref, x_ref, o_ref = refs
        else:
            a_ref, x_ref, g_ref, z_ref, s_ref, o_ref = refs
        xt = pltpu.einshape("ksd->skd", x_ref[...])
        av = a_ref[...]
        y = jnp.stack([_dot3(av, xt[s]) for s in range(SUB)], axis=0)
        y = pltpu.einshape("smd->msd", y)
        if epi is not None:
            y = g_ref[...] * (y + s_ref[...][None] * z_ref[...])
        o_ref[...] = y

    blk = lambda rows: pl.BlockSpec((None, rows, None, SUB, d), lambda p, t: (p, 0, t, 0, 0))
    ins = [a, x5]
    in_specs = [pl.BlockSpec((m, k), lambda p, t: (0, 0)), blk(k)]
    if epi is not None:
        gate, z, skip = epi
        ins += [gate.reshape(p_n, m, s_n // SUB, SUB, d), z.reshape(p_n, m, s_n // SUB, SUB, d), skip]
        in_specs += [blk(m), blk(m), pl.BlockSpec((1, d), lambda p, t: (0, 0))]
    out = pl.pallas_call(
        kern, grid=(p_n, s_n // SUB), in_specs=in_specs, out_specs=blk(m),
        out_shape=jax.ShapeDtypeStruct((p_n, m, s_n // SUB, SUB, d), F32),
        compiler_params=_params(2), name=name)(*ins)
    return out.reshape(p_n, m, s_n, d)


def spectral(x, mf, g=None, mi=None, *, dt, g_index=0, epi=None, name=None):
    p_n, planes, k1_n, nin, d = x.shape
    nf = mf.shape[1] // 2
    nout = nf if g is None else mi.shape[1] // 2

    def kern(*refs):
        it = iter(refs)
        x_ref, mf_ref = next(it), next(it)
        g_ref = next(it) if g is not None else None
        mi_ref = next(it) if g is not None else None
        if epi is not None:
            gate_ref, z_ref, skip_ref = next(it), next(it), next(it)
        o_ref = next(it)
        xs = [x_ref[0, pp, 0] for pp in range(planes)]
        xin = xs[0] if planes == 1 else jnp.concatenate(xs, axis=0)
        f = _dot3(mf_ref[0], xin)
        if g is not None:
            fr, fi = f[:nf], f[nf:]
            gr, gi = g_ref[0, 0, 0], g_ref[0, 1, 0]
            y = jnp.concatenate([fr * gr - fi * gi, fr * gi + fi * gr], axis=0)
            f = _dot3(mi_ref[0], y)
        for pp in range(2):
            y = f[pp * nout:(pp + 1) * nout]
            if epi is not None:
                y = gate_ref[0, pp, 0] * (y + skip_ref[...] * z_ref[0, pp, 0])
            o_ref[0, pp, 0] = y

    ins = [x, mf]
    in_specs = [pl.BlockSpec((1, planes, 1, nin, dt), lambda k, j, p: (p, 0, k, 0, j)),
                pl.BlockSpec((1,) + mf.shape[1:], lambda k, j, p: (k, 0, 0))]
    if g is not None:
        ins += [g, mi]
        in_specs += [pl.BlockSpec((1, 2, 1, nf, dt), lambda k, j, p: (g_index, 0, k, 0, j)),
                     pl.BlockSpec((1,) + mi.shape[1:], lambda k, j, p: (k, 0, 0))]
    if epi is not None:
        ins += list(epi)
        in_specs += [pl.BlockSpec((1, 2, 1, nout, dt), lambda k, j, p: (p, 0, k, 0, j)),
                     pl.BlockSpec((1, 2, 1, nout, dt), lambda k, j, p: (p, 0, k, 0, j)),
                     pl.BlockSpec((1, dt), lambda k, j, p: (0, j))]
    return pl.pallas_call(
        kern, grid=(k1_n, d // dt, p_n), in_specs=in_specs,
        out_specs=pl.BlockSpec((1, 2, 1, nout, dt), lambda k, j, p: (p, 0, k, 0, j)),
        out_shape=jax.ShapeDtypeStruct((p_n, 2, k1_n, nout, d), F32),
        compiler_params=_params(3), name=name)(*ins)


def hyena_two_sided_filters(length, w1, b1, w2, b2, w3, freq, log_decay, name):
    hid = w2.shape[0]
    d = w3.shape[1] // 4
    t = jnp.linspace(0.0, 1.0, length, dtype=F32)[:, None]
    ang = 2.0 * jnp.pi * t * jnp.arange(1, C_POS_BANDS + 1, dtype=F32)
    z = jnp.concatenate([t, jnp.cos(ang), jnp.sin(ang)], axis=-1)
    kpad = LANES - z.shape[1]
    z = jnp.pad(z, ((0, 0), (0, kpad)))
    z2 = jnp.concatenate([z, z[::-1]], axis=0)
    w1p = jnp.pad(w1, ((0, kpad), (0, 0)))
    full = lambda a: pl.BlockSpec(a.shape, lambda *_: (0,) * a.ndim)

    def ffn_kern(z_ref, w1_ref, b1_ref, w2_ref, b2_ref, f_ref, a_ref):
        fr = f_ref[...]
        a = jnp.sin(fr * (_dot3(z_ref[...], w1_ref[...]) + b1_ref[...]))
        a_ref[...] = jnp.sin(fr * (_dot3(a, w2_ref[...]) + b2_ref[...]))

    args = [z2, w1p, b1.reshape(1, hid), w2, b2.reshape(1, hid), freq.reshape(1, hid)]
    act = pl.pallas_call(
        ffn_kern, grid=(1,), in_specs=[full(a) for a in args], out_specs=pl.BlockSpec((2 * length, hid), lambda i: (0, 0)),
        out_shape=jax.ShapeDtypeStruct((2 * length, hid), F32), compiler_params=_params(1), name=name + "_ffn")(*args)

    tn = 256
    per = d // tn

    def kern(af_ref, ar_ref, w3f_ref, w3b_ref, ldf_ref, ldb_ref, o_ref):
        row = lax.broadcasted_iota(jnp.int32, (length, tn), 0)
        pos = row.astype(F32) * (1.0 / (length - 1))
        pos_rev = (length - 1 - row).astype(F32) * (1.0 / (length - 1))

        def filt(a_ref, w_ref, ld_ref, tt):
            f = _dot3(a_ref[...], w_ref[...]) * jnp.exp(-jnp.exp(ld_ref[...]) * tt)
            return f / (jnp.sum(jnp.abs(f), axis=0, keepdims=True) + 1e-6)

        hf = filt(af_ref, w3f_ref, ldf_ref, pos)
        hb_rev = filt(ar_ref, w3b_ref, ldb_ref, pos_rev)
        o_ref[0, 0] = hf + jnp.where(row == 0, hb_rev[length - 1:length], 0.0)
        o_ref[0, 1] = jnp.where(row == 0, 0.0, pltpu.roll(hb_rev, 1, 0))

    col = lambda direction: (lambda o, j: (0, (2 * o + direction) * per + j))
    ld = log_decay.reshape(1, 4 * d)
    g = pl.pallas_call(
        kern, grid=(2, per),
        in_specs=[pl.BlockSpec((length, hid), lambda o, j: (0, 0)), pl.BlockSpec((length, hid), lambda o, j: (1, 0)),
                  pl.BlockSpec((hid, tn), col(0)), pl.BlockSpec((hid, tn), col(1)),
                  pl.BlockSpec((1, tn), col(0)), pl.BlockSpec((1, tn), col(1))],
        out_specs=pl.BlockSpec((1, 2, length, tn), lambda o, j: (o, 0, 0, j)),
        out_shape=jax.ShapeDtypeStruct((2, 2, length, d), F32),
        compiler_params=_params(2), name=name)(act, act, w3, w3, ld, ld)
    return g.reshape(2, 2 * length, d)


def short_conv(u, w, b, *, row_off, n_seq, length, parts, name):
    c = u.shape[1]
    dt = 256
    per = c // parts // dt
    first = row_off // length

    def kern(u_ref, w_ref, b_ref, *o_refs):
        x = u_ref[...]
        r = lax.broadcasted_iota(jnp.int32, x.shape, 0)
        prev = jnp.where(r == 0, 0.0, pltpu.roll(x, 1, 0))
        nxt = jnp.where(r == length - 1, 0.0, pltpu.roll(x, length - 1, 0))
        wv = w_ref[...]
        y = prev * wv[0:1] + x * wv[1:2] + nxt * wv[2:3] + b_ref[...]
        part = pl.program_id(1) // per
        for k, o_ref in enumerate(o_refs):
            @pl.when(part == k)
            def _(o_ref=o_ref):
                o_ref[0] = y

    out_spec = lambda k: pl.BlockSpec((1, length, dt), lambda s, j: (s, 0, jnp.clip(j - k * per, 0, per - 1)))
    return pl.pallas_call(
        kern, grid=(n_seq, c // dt),
        in_specs=[pl.BlockSpec((length, dt), lambda s, j: (first + s, j)),
                  pl.BlockSpec((3, dt), lambda s, j: (0, j)),
                  pl.BlockSpec((1, dt), lambda s, j: (0, j))],
        out_specs=[out_spec(k) for k in range(parts)],
        out_shape=[jax.ShapeDtypeStruct((n_seq, length, c // parts), F32)] * parts,
        compiler_params=_params(2), name=name)(u, w, b)


def _dft_mats_single(n, nin_data, nout):
    k = jnp.arange(n, dtype=jnp.int32)
    fr, fi = _cs(k[:, None] * k[None, :nin_data], n, -1.0)
    mf_data = _cblock(fr, fi)[None]
    gr, gi = _cs(k[:, None] * k[None, :], n, -1.0)
    mf_filt = jnp.concatenate([gr, gi], axis=0)[None]
    ir, ii = _cs(k[:nout, None] * k[None, :], n, 1.0)
    mi = _cblock(ir / n, ii / n)[None]
    return mf_data, mf_filt, mi


def _dft_mats_two_stage(n, n2):
    n1 = n // n2
    k1 = jnp.arange(n1, dtype=jnp.int32)
    t1h = jnp.arange(n1 // 2, dtype=jnp.int32)
    ar, ai = _cs((n // n1) * k1[:, None] * t1h[None, :], n, -1.0)
    a_data = _cblock(ar, ai)
    fr, fi = _cs((n // n1) * k1[:, None] * k1[None, :], n, -1.0)
    a_filt = jnp.concatenate([fr, fi], axis=0)
    br, bi = _cs((n // n1) * t1h[:, None] * k1[None, :], n, 1.0)
    a_inv = _cblock(br, bi)
    t2 = jnp.arange(n2, dtype=jnp.int32)
    kk = k1[:, None, None] + n1 * t2[None, :, None]
    mr, mi_ = _cs(kk * t2[None, None, :], n, -1.0)
    mf = _cblock(mr, mi_)
    vr, vi = _cs(jnp.swapaxes(kk, 1, 2) * t2[None, :, None], n, 1.0)
    mi = _cblock(vr / n, vi / n)
    return a_data, a_filt, a_inv, mf, mi


def _route(logits):
    lane = lax.broadcasted_iota(jnp.int32, logits.shape, 1).astype(F32)
    big = 1e9
    lg = jnp.where(lane < N_GROUPS, logits, -jnp.inf)
    mg = jnp.max(lg, axis=1, keepdims=True)
    gi = jnp.min(jnp.where(lg == mg, lane, big), axis=1, keepdims=True)
    p_g = 1.0 / jnp.sum(jnp.exp(lg - mg), axis=1, keepdims=True)
    lo = N_GROUPS + EXPERTS_PER_GROUP * gi
    le = jnp.where((lane >= lo) & (lane < lo + EXPERTS_PER_GROUP), logits, -jnp.inf)
    m1 = jnp.max(le, axis=1, keepdims=True)
    i1 = jnp.min(jnp.where(le == m1, lane, big), axis=1, keepdims=True)
    le2 = jnp.where(lane == i1, -jnp.inf, le)
    m2 = jnp.max(le2, axis=1, keepdims=True)
    i2 = jnp.min(jnp.where(le2 == m2, lane, big), axis=1, keepdims=True)
    e2 = jnp.exp(m2 - m1)
    w1 = p_g / (1.0 + e2)
    w2 = p_g * e2 / (1.0 + e2)
    return lane, i1 - N_GROUPS, i2 - N_GROUPS, w1, w2


def _rank_in_tile(lane, e1, e2):
    tm = lane.shape[0]
    picks = jnp.where(lane == e1, 1.0, 0.0) + jnp.where(lane == e2, 1.0, 0.0)
    r = lax.broadcasted_iota(jnp.int32, (tm, tm), 0)
    c = lax.broadcasted_iota(jnp.int32, (tm, tm), 1)
    earlier = jnp.where(c < r, 1.0, 0.0).astype(BF16)
    return _dot(earlier, picks.astype(BF16)), picks


def _pack_bf16_pairs(y):
    c = y.shape[1] // 2
    hi = lax.bitcast_convert_type(y[:, :c].astype(BF16).astype(F32), jnp.uint32)
    lo = lax.bitcast_convert_type(y[:, c:].astype(BF16).astype(F32), jnp.uint32)
    return hi | (lo >> 16)


def _unpack_bf16_pairs(w):
    hi = lax.bitcast_convert_type(w & jnp.uint32(0xFFFF0000), F32)
    lo = lax.bitcast_convert_type(w << 16, F32)
    return hi, lo


def expert_mlp(xs, blk_e, n_used, w_gate, w_up, w_down, layer):
    p_rows, d = xs.shape
    de = w_gate.shape[3]
    nblk = p_rows // MOE_BLK

    def kern(be_ref, nu_ref, x_ref, wg_ref, wu_ref, wd_ref, o_ref, wg_s, wu_s, wd_s):
        i = pl.program_id(0)
        used = i < nu_ref[0]
        fresh = jnp.logical_or(i == 0, be_ref[i] != be_ref[jnp.maximum(i - 1, 0)])

        @pl.when(jnp.logical_and(used, fresh))
        def _():
            wg_s[...] = wg_ref[0, 0].astype(BF16)
            wu_s[...] = wu_ref[0, 0].astype(BF16)
            wd_s[...] = wd_ref[0, 0].astype(BF16)

        @pl.when(used)
        def _():
            x = x_ref[...].astype(BF16)
            g = _dot(x, wg_s[...])
            u = _dot(x, wu_s[...])
            a = (g * jax.nn.sigmoid(g) * u).astype(BF16)
            o_ref[...] = _pack_bf16_pairs(_dot(a, wd_s[...]))

        @pl.when(jnp.logical_not(used))
        def _():
            o_ref[...] = jnp.zeros_like(o_ref)

    last = lambda i, nu: jnp.minimum(i, nu[0] - 1)
    grid_spec = pltpu.PrefetchScalarGridSpec(
        num_scalar_prefetch=2, grid=(nblk,),
        in_specs=[pl.BlockSpec((MOE_BLK, d), lambda i, be, nu: (last(i, nu), 0)),
                  pl.BlockSpec((1, 1, d, de), lambda i, be, nu: (layer, be[last(i, nu)], 0, 0)),
                  pl.BlockSpec((1, 1, d, de), lambda i, be, nu: (layer, be[last(i, nu)], 0, 0)),
                  pl.BlockSpec((1, 1, de, d), lambda i, be, nu: (layer, be[last(i, nu)], 0, 0))],
        out_specs=pl.BlockSpec((MOE_BLK, d // 2), lambda i, be, nu: (i, 0)),
        scratch_shapes=[pltpu.VMEM((d, de), BF16), pltpu.VMEM((d, de), BF16), pltpu.VMEM((de, d), BF16)])
    return pl.pallas_call(
        kern, grid_spec=grid_spec, out_shape=jax.ShapeDtypeStruct((p_rows, d // 2), jnp.uint32),
        compiler_params=_params(1), name="expert_mlp")(blk_e, n_used, xs, w_gate, w_up, w_down)


def _slot_plan(experts, ranks):
    n = 2 * experts.shape[0]
    ids = jnp.arange(N_EXPERTS, dtype=jnp.int32)
    onehot = experts[..., None] == ids
    counts = jnp.sum(onehot, axis=(0, 1)).astype(jnp.int32)
    padded = (counts + MOE_BLK - 1) // MOE_BLK * MOE_BLK
    pend = jnp.cumsum(padded)
    pstart = pend - padded
    p_rows = -(-n // MOE_BLK) * MOE_BLK + N_EXPERTS * MOE_BLK
    nblk = p_rows // MOE_BLK
    blk_first = jnp.arange(nblk, dtype=jnp.int32) * MOE_BLK
    blk_e = jnp.minimum(jnp.sum(pend[None, :] <= blk_first[:, None], axis=1), N_EXPERTS - 1).astype(jnp.int32)
    n_used = (pend[-1:] // MOE_BLK).astype(jnp.int32)
    blk_ids = jnp.arange(nblk, dtype=jnp.int32)
    partly = jnp.any((blk_ids[:, None] == (pend // MOE_BLK - 1)[None, :]) & (counts % MOE_BLK != 0)[None, :], axis=1)
    zero_blk = (partly | (blk_ids >= n_used[0])).astype(jnp.int32)
    slot = ranks + jnp.sum(jnp.where(onehot, pstart, 0), axis=-1)
    return slot.reshape(n).astype(jnp.int32), blk_e, n_used, zero_blk, p_rows


def _dma_params():
    return pltpu.CompilerParams(dimension_semantics=("arbitrary",), vmem_limit_bytes=VMEM_LIMIT,
                                disable_bounds_checks=True)


DMA_UNROLL = 8


def moe_dispatch(x1, mod, slot, zero_blk, p_rows, group_len, name):
    t, d = x1.shape
    tm = ROW_TILE

    def body(slot_ref, zb_ref, x_ref, mod_ref, xs_ref, h_ref, zero_ref, sem):
        h_ref[...] = _modulate(x_ref, mod_ref, 3, 4)

        @pl.when(pl.program_id(0) == 0)
        def _():
            zero_ref[...] = jnp.zeros_like(zero_ref)

            def zstart(b, c):
                @pl.when(zb_ref[b] != 0)
                def _():
                    first = pl.multiple_of(b * MOE_BLK, MOE_BLK)
                    pltpu.make_async_copy(zero_ref, xs_ref.at[pl.ds(first, MOE_BLK)], sem).start()
                return c

            def zwait(b, c):
                @pl.when(zb_ref[b] != 0)
                def _():
                    pltpu.make_async_copy(zero_ref, xs_ref.at[pl.ds(0, MOE_BLK)], sem).wait()
                return c

            lax.fori_loop(0, p_rows // MOE_BLK, zstart, 0)
            lax.fori_loop(0, p_rows // MOE_BLK, zwait, 0)

        def start(rr, c):
            for u in range(DMA_UNROLL):
                r = rr * DMA_UNROLL + u
                for k in range(2):
                    pltpu.make_async_copy(h_ref.at[pl.ds(r, 1)], xs_ref.at[pl.ds(slot_ref[2 * r + k], 1)], sem).start()
            return c

        lax.fori_loop(0, tm // DMA_UNROLL, start, 0)
        for k in range(2):
            pltpu.make_async_copy(h_ref, xs_ref.at[pl.ds(0, tm)], sem).wait()

    return row_call(body, t, [(slot, "smem_rows"), (zero_blk, "smem"), (x1, "row"), (mod, "group")],
                    [(p_rows, d, F32, "any")], group_len=group_len,
                    scratch=[pltpu.VMEM((tm, d), F32), pltpu.VMEM((MOE_BLK, d), F32), pltpu.SemaphoreType.DMA(())],
                    params=_dma_params(), name=name)[0]


def kernel(x_prompt, x_sample, cache_a_k, cache_a_v, cache_b_ckv, cache_b_krope, cache_d_k, cache_d_v, c_ctx, c, mod_w, mod_b, ln_g, ln_b, a_wq, a_wk, a_wv, a_wo, a_sink, b_wq_a, b_q_norm, b_wq_b, b_wkv_a, b_kv_norm, b_wk_b, b_wv_b, b_wo, c_w_in, c_b_in, c_conv_w, c_conv_b, c_ffn_w1, c_ffn_b1, c_ffn_w2, c_ffn_b2, c_ffn_w3, c_ffn_freq, c_log_decay, c_skip, c_wo, d_wq, d_wk, d_wv, d_wo, d_rel_bias, moe_wr_g, moe_br_g, moe_wr_e, moe_br_e, moe_w_gate, moe_w_up, moe_w_down):
    bc, lc, d = x_prompt.shape
    bl, ll, _ = x_sample.shape
    past = cache_a_k.shape[2]
    gl = ll
    assert bc * lc == gl and d == A_HEADS * HEAD_DIM
    ng = 1 + bl
    t_all = ng * gl
    x_ctx, x_lat = x_prompt.reshape(gl, d), x_sample.reshape(bl * gl, d)
    cvec = jnp.concatenate([c_ctx[None, :], c], axis=0)
    mods = modulation_all(cvec, mod_w, mod_b)
    rc = functools.partial(row_call, group_len=gl)
    row2 = lambda v: v.reshape(1, -1)
    n_ctx_tiles = gl // ROW_TILE

    def post_mixer(i, att_ctx, att_lat, wo, x_in):
        unused = LANES - N_GROUPS - N_EXPERTS
        wr = jnp.concatenate([moe_wr_g[i], moe_wr_e[i], jnp.zeros((d, unused), F32)], axis=1)
        br = jnp.concatenate([moe_br_g[i], moe_br_e[i], jnp.zeros((unused,), F32)])[None, :]

        x_split = isinstance(x_in, tuple)

        def body(attc_ref, attl_ref, *rest):
            is_ctx = pl.program_id(0) < n_ctx_tiles
            if x_split:
                x_old = jnp.where(is_ctx, rest[0][...], rest[1][...])
                rest = rest[2:]
            else:
                x_old = rest[0][...]
                rest = rest[1:]
            mod_ref, wo_ref, g_ref, b_ref, wr_ref, br_ref, x1_ref, info_ref, w_ref, seen_ref = rest

            @pl.when(pl.program_id(0) == 0)
            def _():
                seen_ref[...] = jnp.zeros_like(seen_ref)

            m = mod_ref[0]
            att = jnp.where(is_ctx, attc_ref[...], attl_ref[...])
            o = _dot(att.astype(BF16), wo_ref[...])
            x1 = _layer_norm(DEEPNORM_ALPHA * x_old + m[2:3] * o, g_ref[...], b_ref[...])
            x1_ref[...] = x1
            h = x1 * (1.0 + m[4:5]) + m[3:4]
            lane, e1, e2, w1, w2 = _route(_dot3(h, wr_ref[...]) + br_ref[...])
            before, picks = _rank_in_tile(lane, e1, e2)
            before = before + seen_ref[...]
            r1 = jnp.sum(jnp.where(lane == e1, before, 0.0), axis=1, keepdims=True)
            r2 = jnp.sum(jnp.where(lane == e2, before, 0.0), axis=1, keepdims=True)
            seen_ref[...] += jnp.sum(picks, axis=0, keepdims=True)
            info = jnp.where(lane == 0, e1, jnp.where(lane == 1, e2, jnp.where(lane == 2, r1, jnp.where(lane == 3, r2, 0.0))))
            info_ref[...] = info.astype(jnp.int32)
            w_ref[...] = jnp.where(lane == 0, w1, jnp.where(lane == 1, w2, 0.0))

        return rc(body, t_all,
                  [(att_ctx, "row_head"), (att_lat, "row_tail")]
                  + ([(x_in[0], "row_head"), (x_in[1], "row_tail")] if x_split else [(x_in, "row")])
                  + [(mods[i], "group"), (wo.astype(BF16), "full"),
                   (row2(ln_g[i, 0]), "full"), (row2(ln_b[i, 0]), "full"), (wr, "full"), (br, "full")],
                  [(d, F32), (LANES, jnp.int32), (LANES, F32)],
                  scratch=[pltpu.VMEM((1, LANES), F32)], name=f"post_mixer{i}")

    def moe(i, x1, info, w):
        slot, blk_e, n_used, zero_blk, p_rows = _slot_plan(info[:, 0:2], info[:, 2:4])
        xs = moe_dispatch(x1, mods[i], slot, zero_blk, p_rows, gl, f"moe_dispatch{i}")
        ys = expert_mlp(xs, blk_e, n_used, moe_w_gate, moe_w_up, moe_w_down, i)
        tm = ROW_TILE

        split_out = i == DEPTH - 1

        def body(slot_ref, slot_next_ref, x_ref, w_ref, mod_ref, g_ref, b_ref, ys_ref, *rest):
            out_refs, (buf, sem) = rest[:-2], rest[-2:]
            t = pl.program_id(0)
            par = t % 2

            def request(s_ref, p):
                def start(rr, c):
                    for u in range(DMA_UNROLL):
                        r = rr * DMA_UNROLL + u
                        for k in range(2):
                            pltpu.make_async_copy(ys_ref.at[pl.ds(s_ref[2 * r + k], 1)], buf.at[p, k, pl.ds(r, 1)],
                                                  sem.at[p]).start()
                    return c

                lax.fori_loop(0, tm // DMA_UNROLL, start, 0)

            @pl.when(t == 0)
            def _():
                request(slot_ref, 0)

            @pl.when(t + 1 < pl.num_programs(0))
            def _():
                request(slot_next_ref, 1 - par)

            for k in range(2):
                pltpu.make_async_copy(ys_ref.at[pl.ds(0, tm)], buf.at[par, k], sem.at[par]).wait()
            m = mod_ref[0]
            wv = w_ref[...]
            a_hi, a_lo = _unpack_bf16_pairs(buf[par, 0])
            b_hi, b_lo = _unpack_bf16_pairs(buf[par, 1])
            w0, w1 = wv[:, 0:1], wv[:, 1:2]
            y = jnp.concatenate([w0 * a_hi + w1 * b_hi, w0 * a_lo + w1 * b_lo], axis=1)
            x2 = _layer_norm(DEEPNORM_ALPHA * x_ref[...] + m[5:6] * y, g_ref[...], b_ref[...])
            if not split_out:
                out_refs[0][...] = x2
            else:
                is_ctx = pl.program_id(0) < n_ctx_tiles

                @pl.when(is_ctx)
                def _():
                    out_refs[0][...] = x2

                @pl.when(jnp.logical_not(is_ctx))
                def _():
                    out_refs[1][...] = x2

        outs = [(gl, d, F32, "ctx"), (t_all - gl, d, F32, "tail")] if split_out else [(d, F32)]
        res = rc(body, t_all,
                 [(slot, "smem_rows"), (slot, "smem_rows_next"), (x1, "row"), (w, "row"), (mods[i], "group"),
                  (row2(ln_g[i, 1]), "full"), (row2(ln_b[i, 1]), "full"), (ys, "any")],
                 outs, scratch=[pltpu.VMEM((2, 2, tm, d // 2), jnp.uint32), pltpu.SemaphoreType.DMA((2,))],
                 params=_dma_params(), name=f"moe_combine{i}")
        return res if split_out else res[0]

    def finish_layer(i, att_ctx, att_lat, wo, x_in):
        x1, info, w = post_mixer(i, att_ctx, att_lat, wo, x_in)
        return moe(i, x1, info, w)

    tq = 256
    i = 0
    hd = HEAD_DIM
    kvw = A_KV_HEADS * hd
    cq, sq = _rope_tables(gl, hd, 0, 0, LANES // hd)
    wq, wk, wv = a_wq[0], a_wk[0], a_wv[0]

    def body_a(xc_ref, xl_ref, mod_ref, wq_ref, wqr_ref, wk_ref, wkr_ref, wv_ref, c_ref, s_ref, q_ref, kh_ref, vh_ref, kc_ref, vc_ref):
        x_in = jnp.where(pl.program_id(0) < n_ctx_tiles, xc_ref[...], xl_ref[...])
        m = mod_ref[0]
        h = (x_in * (1.0 + m[1:2]) + m[0:1]).astype(BF16)
        cc, ss = c_ref[0], s_ref[0]
        q = _dot(h, wq_ref[...]) * _tile_lanes(cc, d) + _dot(h, wqr_ref[...]) * _tile_lanes(ss, d)
        k = _dot(h, wk_ref[...]) * _tile_lanes(cc, kvw) + _dot(h, wkr_ref[...]) * _tile_lanes(ss, kvw)
        v = _dot(h, wv_ref[...])
        q_ref[...] = (q * QSCALE_64).astype(BF16)
        _store_heads(kh_ref, k, hd)
        _store_value_heads(vh_ref, v, hd)

        @pl.when(pl.program_id(0) < n_ctx_tiles)
        def _():
            kc_ref[...] = k
            vc_ref[...] = v

    q, kh, vh, k_new, v_new = rc(
        body_a, t_all,
        [(x_ctx, "row_head"), (x_lat, "row_tail"), (mods[i], "group"), (wq.astype(BF16), "full"),
         (_rot_cols(wq, hd).astype(BF16), "full"),
         (wk.astype(BF16), "full"), (_rot_cols(wk, hd).astype(BF16), "full"), (wv.astype(BF16), "full"),
         (cq, "pos"), (sq, "pos")],
        [(d, BF16), (A_KV_HEADS, hd, BF16, "heads"), (A_KV_HEADS, V_LANES, BF16, "heads"),
         (gl, kvw, F32, "ctx"), (gl, kvw, F32, "ctx")], name="proj_a")
    out_a_k = k_new.reshape(bc, 1, lc, A_KV_HEADS, hd)
    out_a_v = v_new.reshape(bc, 1, lc, A_KV_HEADS, hd)
    common = dict(n_heads=A_HEADS, dq=hd, dv=hd, hpb=8, kvpb=2, tq=tq, sink=a_sink[0])
    att_c = attention(q, kh, vh, n_seq=bc, lq=lc, row_off=0, win=lc, name="attn_a_ctx", **common)
    win_a = 2 * tq
    nq_l = gl // tq
    qi = jnp.arange(tq)[:, None]
    ki = jnp.arange(win_a)[None, :]
    band = jnp.stack([jnp.where(jnp.abs(ki - (qi + off)) <= A_WINDOW, 0.0, NEG_INF)
                      for off in (0, A_WINDOW, 2 * A_WINDOW)]).astype(F32)[None]
    att_l = attention(q, kh, vh, n_seq=bl, lq=gl, row_off=gl, win=win_a,
                      start_fn=lambda ii: jnp.clip(ii * tq - A_WINDOW, 0, gl - win_a),
                      kc=_cache_heads_major(cache_a_k[:, 0]), vc=_cache_heads_major(cache_a_v[:, 0], values=True),
                      bias=band, type_fn=lambda ii: jnp.where(ii == 0, 0, jnp.where(ii == nq_l - 1, 2, 1)),
                      name="attn_a_lat", **common)
    x = finish_layer(i, att_c, att_l, a_wo[0], (x_ctx, x_lat))

    i = 1
    hq = B_NOPE + B_ROPE
    qw = B_HEADS * B_QPAD
    wqb = b_wq_b[0].reshape(B_Q_RANK, B_HEADS, hq)
    wqb_rot = jnp.concatenate([jnp.zeros_like(wqb[..., :B_NOPE]),
                               _rot_cols(wqb[..., B_NOPE:].reshape(B_Q_RANK, -1), B_ROPE).reshape(B_Q_RANK, B_HEADS, B_ROPE)],
                              axis=-1)
    padq = lambda wz: jnp.pad(wz, ((0, 0), (0, 0), (0, B_QPAD - hq))).reshape(B_Q_RANK, qw).astype(BF16)
    wkv_c, wkv_r = b_wkv_a[0][:, :B_KV_RANK], b_wkv_a[0][:, B_KV_RANK:]
    cqb, sqb = _rope_tables(gl, B_ROPE, B_NOPE, B_QPAD - hq, 1)
    ckr, skr = _rope_tables(gl, B_ROPE, 0, 0, 1)

    def body_b(x_ref, mod_ref, wqa_ref, qn_ref, wqb_ref, wqbr_ref, wc_ref, kn_ref, wr_ref, wrr_ref, wkb_ref, wvb_ref,
               cq_ref, sq_ref, ck_ref, sk_ref, q_ref, kh_ref, vh_ref, ckv_ref, kr_ref):
        h = _modulate(x_ref, mod_ref, 0, 1).astype(BF16)
        qa = _rms(_dot(h, wqa_ref[...]), qn_ref[...]).astype(BF16)
        q = (_dot(qa, wqb_ref[...]) * _tile_lanes(cq_ref[0], qw) + _dot(qa, wqbr_ref[...]) * _tile_lanes(sq_ref[0], qw))
        q_ref[...] = (q * QSCALE_MLA).astype(BF16)
        ckv = _rms(_dot(h, wc_ref[...]), kn_ref[...])
        kr = _dot(h, wr_ref[...]) * ck_ref[0] + _dot(h, wrr_ref[...]) * sk_ref[0]
        cb = ckv.astype(BF16)
        _store_mla_keys(kh_ref, _dot(cb, wkb_ref[...]), kr)
        _store_value_heads(vh_ref, _dot(cb, wvb_ref[...]), B_VDIM)

        @pl.when(pl.program_id(0) < n_ctx_tiles)
        def _():
            ckv_ref[...] = ckv
            kr_ref[...] = kr

    wkb, wvb = b_wk_b[0].astype(BF16), b_wv_b[0].astype(BF16)
    q, kh, vh, ckv_new, kr_new = rc(
        body_b, t_all,
        [(x, "row"), (mods[i], "group"), (b_wq_a[0].astype(BF16), "full"), (row2(b_q_norm[0]), "full"),
         (padq(wqb), "full"), (padq(wqb_rot), "full"), (wkv_c.astype(BF16), "full"), (row2(b_kv_norm[0]), "full"),
         (wkv_r.astype(BF16), "full"), (_rot_cols(wkv_r, B_ROPE).astype(BF16), "full"), (wkb, "full"), (wvb, "full"),
         (cqb, "pos"), (sqb, "pos"), (ckr, "pos"), (skr, "pos")],
        [(qw, BF16), (B_HEADS, B_QPAD, BF16, "heads"), (B_HEADS, V_LANES, BF16, "heads"),
         (gl, B_KV_RANK, F32, "ctx"), (gl, B_ROPE, F32, "ctx")], name="proj_b")
    out_b_ckv = ckv_new.reshape(bc, 1, lc, B_KV_RANK)
    out_b_krope = kr_new.reshape(bc, 1, lc, B_ROPE)

    def body_bc(c_ref, r_ref, wkb_ref, wvb_ref, kh_ref, vh_ref):
        cb = c_ref[...].astype(BF16)
        _store_mla_keys(kh_ref, _dot(cb, wkb_ref[...]), r_ref[...])
        _store_value_heads(vh_ref, _dot(cb, wvb_ref[...]), B_VDIM)

    n_pc = bl * past
    kh_p, vh_p = row_call(body_bc, n_pc,
                          [(cache_b_ckv[:, 0].reshape(n_pc, B_KV_RANK), "row"), (cache_b_krope[:, 0].reshape(n_pc, B_ROPE), "row"),
                           (wkb, "full"), (wvb, "full")],
                          [(B_HEADS, B_QPAD, BF16, "heads"), (B_HEADS, V_LANES, BF16, "heads")],
                          tm=min(ROW_TILE, n_pc), name="proj_b_past")
    common = dict(n_heads=B_HEADS, dq=B_QPAD, dv=B_VDIM, hpb=2, kvpb=2)
    att_c = attention(q, kh, vh, n_seq=bc, lq=lc, row_off=0, win=lc, tq=tq, name="attn_b_ctx", **common)
    att_l = attention(q, kh, vh, n_seq=bl, lq=gl, row_off=gl, win=gl, kc=kh_p, vc=vh_p, tq=min(2 * tq, gl),
                      name="attn_b_lat", **common)
    x = finish_layer(i, att_c, att_l, b_wo[0], x)

    i = 2
    d3 = 3 * d

    def body_c(x_ref, mod_ref, w_ref, b_ref, u_ref):
        h = _modulate(x_ref, mod_ref, 0, 1).astype(BF16)
        for j in range(3):
            u_ref[:, j * d:(j + 1) * d] = _dot(h, w_ref[:, j * d:(j + 1) * d]) + b_ref[:, j * d:(j + 1) * d]

    (u,) = rc(body_c, t_all, [(x, "row"), (mods[i], "group"), (c_w_in[0].astype(BF16), "full"), (row2(c_b_in[0]), "full")],
              [(d3, F32)], name="proj_c")
    u_c = short_conv(u, c_conv_w[0], row2(c_conv_b[0]), row_off=0, n_seq=bc, length=lc, parts=3, name="short_conv_ctx")
    u_l = short_conv(u, c_conv_w[0], row2(c_conv_b[0]), row_off=gl, n_seq=bl, length=gl, parts=3, name="short_conv_lat")
    fargs = (c_ffn_w1[0], c_ffn_b1[0], c_ffn_w2[0], c_ffn_b2[0], c_ffn_w3[0], c_ffn_freq[0], c_log_decay[0])
    g_c = hyena_two_sided_filters(lc, *fargs, name="filters_ctx")
    g_l = hyena_two_sided_filters(gl, *fargs, name="filters_lat")

    nc = 2 * lc
    mf_data, mf_filt, mi_c = _dft_mats_single(nc, lc, lc)
    spec_c = spectral(g_c.reshape(2, 1, 1, nc, d), mf_filt, dt=512, name="filt_spec_ctx")
    z_c = u_c[0].reshape(bc // 2, 2, 1, lc, d)
    for o in range(2):
        gate = u_c[o + 1].reshape(bc // 2, 2, 1, lc, d)
        z_c = spectral(z_c, mf_data, spec_c, mi_c, dt=512, g_index=o, epi=(gate, z_c, row2(c_skip[0, o])),
                       name=f"conv_ctx{o}")
    zc_out = z_c.reshape(gl, d)

    nl = 2 * gl
    n1 = nl // FFT_N2
    a_data, a_filt, a_inv, mf_l, mi_l = _dft_mats_two_stage(nl, FFT_N2)
    ga = slow_stage(a_filt, g_l.reshape(2, n1, FFT_N2, d), name="filt_stage_a")
    spec_l = spectral(ga.reshape(2, 2, n1, FFT_N2, d), mf_l, dt=1024, name="filt_spec_lat")
    z_l = u_l[0].reshape(bl // 2, n1, FFT_N2, d)
    for o in range(2):
        za = slow_stage(a_data, z_l, name=f"conv_lat_a{o}")
        zb = spectral(za.reshape(bl // 2, 2, n1, FFT_N2, d), mf_l, spec_l, mi_l, dt=1024, g_index=o,
                      name=f"conv_lat_c{o}")
        gate = u_l[o + 1].reshape(bl // 2, n1, FFT_N2, d)
        z_l = slow_stage(a_inv, zb.reshape(bl // 2, 2 * n1, FFT_N2, d), epi=(gate, z_l, row2(c_skip[0, o])),
                         name=f"conv_lat_i{o}")
    x = finish_layer(i, zc_out, z_l.reshape(bl * gl, d), c_wo[0], x)

    i = 3

    def body_d(x_ref, mod_ref, wq_ref, wk_ref, wv_ref, q_ref, kh_ref, vh_ref, kc_ref, vc_ref):
        h = _modulate(x_ref, mod_ref, 0, 1).astype(BF16)
        q_ref[...] = (_dot(h, wq_ref[...]) * QSCALE_64).astype(BF16)
        k = _dot(h, wk_ref[...])
        v = _dot(h, wv_ref[...])
        _store_heads(kh_ref, k, hd)
        _store_value_heads(vh_ref, v, hd)

        @pl.when(pl.program_id(0) < n_ctx_tiles)
        def _():
            kc_ref[...] = k
            vc_ref[...] = v

    q, k_heads, v_heads, k_new, v_new = rc(
        body_d, t_all,
        [(x, "row"), (mods[i], "group"), (d_wq[0].astype(BF16), "full"), (d_wk[0].astype(BF16), "full"),
         (d_wv[0].astype(BF16), "full")],
        [(d, BF16), (D_HEADS, hd, BF16, "heads"), (D_HEADS, V_LANES, BF16, "heads"), (gl, d, F32, "ctx"), (gl, d, F32, "ctx")],
        name="proj_d")
    out_d_k = k_new.reshape(bc, 1, lc, D_HEADS, hd)
    out_d_v = v_new.reshape(bc, 1, lc, D_HEADS, hd)
    common = dict(n_heads=D_HEADS, dq=hd, dv=hd, hpb=4, kvpb=4, tq=tq)
    att_c = attention(q, k_heads, v_heads, n_seq=bc, lq=lc, row_off=0, win=lc, name="attn_d_ctx", **common)
    rows = gl // GRID_W
    kh = min(MAX_NBR_ROWS, rows)
    qrows = tq // GRID_W
    krows = qrows + kh
    win_d = krows * GRID_W
    nq_l = gl // tq
    qr_l, kr_l = jnp.arange(qrows)[:, None], jnp.arange(krows)[None, :]
    qc, kc_ = jnp.arange(GRID_W)[:, None], jnp.arange(GRID_W)[None, :]
    c0 = jnp.clip(qc - NBR_COLS // 2, 0, GRID_W - NBR_COLS)
    col_ok = (kc_ >= c0) & (kc_ < c0 + NBR_COLS)
    dc = jnp.clip(kc_ - qc, 1 - NBR_COLS, NBR_COLS - 1) + NBR_COLS - 1
    dc_hot = (dc[..., None] == jnp.arange(2 * NBR_COLS - 1)).astype(F32)
    by_col = jnp.einsum("hrc,xyc->hrxy", d_rel_bias[0], dc_hot, precision=lax.Precision.HIGHEST)
    tabs = []
    for off, lo in ((0, jnp.zeros_like(qr_l)), (kh // 2, qr_l), (kh, jnp.full_like(qr_l, qrows))):
        row_ok = (kr_l >= lo) & (kr_l < lo + kh)
        dr = jnp.clip(kr_l - off - qr_l + MAX_NBR_ROWS - 1, 0, 2 * MAX_NBR_ROWS - 2)
        dr_hot = (dr[..., None] == jnp.arange(2 * MAX_NBR_ROWS - 1)).astype(F32)
        tab = jnp.einsum("qkr,hrxy->hqxky", dr_hot, by_col, precision=lax.Precision.HIGHEST)
        ok = row_ok[:, None, :, None] & col_ok[None, :, None, :]
        tabs.append(jnp.where(ok[None], tab, NEG_INF).reshape(D_HEADS, tq, win_d))
    nbr_bias = jnp.stack(tabs, axis=1).astype(F32)
    att_l = attention(q, k_heads, v_heads, n_seq=bl, lq=gl, row_off=gl, win=win_d,
                      start_fn=lambda ii: jnp.clip(ii * qrows - kh // 2, 0, rows - krows) * GRID_W,
                      kc=_cache_heads_major(cache_d_k[:, 0]), vc=_cache_heads_major(cache_d_v[:, 0], values=True),
                      bias=nbr_bias, type_fn=lambda ii: jnp.where(ii == 0, 0, jnp.where(ii == nq_l - 1, 2, 1)),
                      name="attn_d_lat", **common)
    y_ctx, y_lat = finish_layer(i, att_c, att_l, d_wo[0], x)

    y_prompt = y_ctx.reshape(bc, lc, d)
    y_sample = y_lat.reshape(bl, gl, d)
    return (y_prompt, y_sample, out_a_k, out_a_v, out_b_ckv, out_b_krope, out_d_k, out_d_v)
```

```python
import functools

import jax
import jax.numpy as jnp
import numpy as np
from jax import lax
from jax.experimental import pallas as pl
from jax.experimental.pallas import tpu as pltpu

F32 = jnp.float32
BF16 = jnp.bfloat16

GRID_W = 64
HEAD_DIM = 64
ROPE_BASE = 10000.0
LN_EPS = 1e-5
RMS_EPS = 1e-6
NEG_INF = -1e30
DEPTH = 4
DEEPNORM_ALPHA = (2 * DEPTH) ** 0.25
A_HEADS = 16
A_KV_HEADS = 4
A_WINDOW = 128
B_HEADS = 16
B_Q_RANK = 384
B_KV_RANK = 256
B_NOPE = 64
B_ROPE = 32
B_VDIM = 64
B_QPAD = 128
C_POS_BANDS = 16
D_HEADS = 16
MAX_NBR_ROWS = 8
NBR_COLS = 16
N_GROUPS = 4
EXPERTS_PER_GROUP = 8
N_EXPERTS = N_GROUPS * EXPERTS_PER_GROUP
D_EXPERT = 512
MOE_BLK = 512
LANES = 128
V_LANES = 128
LOG2E = 1.4426950408889634
QSCALE_64 = HEAD_DIM ** -0.5 * LOG2E
QSCALE_MLA = (B_NOPE + B_ROPE) ** -0.5 * LOG2E
FFT_N2 = 128
VMEM_LIMIT = 56 * 1024 * 1024
ROW_TILE = 512


def _params(n_axes):
    return pltpu.CompilerParams(dimension_semantics=("arbitrary",) * n_axes, vmem_limit_bytes=VMEM_LIMIT)


def _dot(a, b):
    return jnp.dot(a, b, preferred_element_type=F32)


def _split(x):
    hi = x.astype(BF16)
    lo = (x - hi.astype(F32)).astype(BF16)
    return hi, lo


def _dot3(a, b):
    ah, al = _split(a)
    bh, bl = _split(b)
    return _dot(ah, bh) + _dot(ah, bl) + _dot(al, bh)


def _layer_norm(y, g, b):
    mu = jnp.mean(y, axis=-1, keepdims=True)
    d = y - mu
    var = jnp.mean(d * d, axis=-1, keepdims=True)
    return d * lax.rsqrt(var + LN_EPS) * g + b


def _rms(y, g):
    return y * lax.rsqrt(jnp.mean(y * y, axis=-1, keepdims=True) + RMS_EPS) * g


def _tile_lanes(t, n):
    reps = n // t.shape[-1]
    return t if reps == 1 else jnp.concatenate([t] * reps, axis=-1)


def row_call(body, m_rows, ins, outs, *, tm=ROW_TILE, group_len=None, name=None, scratch=(), params=None):
    nb = None if group_len is None else group_len // tm
    n_tiles = m_rows // tm
    in_specs = []
    for a, kind in ins:
        if kind == "row":
            in_specs.append(pl.BlockSpec((tm, a.shape[1]), lambda i: (i, 0)))
        elif kind == "row_head":
            nh = a.shape[0] // tm
            in_specs.append(pl.BlockSpec((tm, a.shape[1]), lambda i, nh=nh: (jnp.minimum(i, nh - 1), 0)))
        elif kind == "row_tail":
            nh = n_tiles - a.shape[0] // tm
            in_specs.append(pl.BlockSpec((tm, a.shape[1]), lambda i, nh=nh: (jnp.maximum(i - nh, 0), 0)))
        elif kind == "any":
            in_specs.append(pl.BlockSpec(memory_space=pl.ANY))
        elif kind == "smem":
            in_specs.append(pl.BlockSpec(memory_space=pltpu.SMEM))
        elif kind == "smem_rows":
            in_specs.append(pl.BlockSpec((a.shape[0] // n_tiles,), lambda i: (i,), memory_space=pltpu.SMEM))
        elif kind == "smem_rows_next":
            in_specs.append(pl.BlockSpec((a.shape[0] // n_tiles,), lambda i: (jnp.minimum(i + 1, n_tiles - 1),),
                                         memory_space=pltpu.SMEM))
        elif kind == "full":
            in_specs.append(pl.BlockSpec(a.shape, lambda i, nd=a.ndim: (0,) * nd))
        elif kind == "group":
            in_specs.append(pl.BlockSpec((1,) + a.shape[1:], lambda i: (i // nb, 0, 0)))
        elif kind == "pos":
            in_specs.append(pl.BlockSpec((1, tm, a.shape[2]), lambda i: (jnp.minimum(i // nb, 1), i % nb, 0)))
        else:
            raise ValueError(kind)
    out_specs, out_shape = [], []
    for o in outs:
        if len(o) == 4 and o[3] == "any":
            out_specs.append(pl.BlockSpec(memory_space=pl.ANY))
            out_shape.append(jax.ShapeDtypeStruct((o[0], o[1]), o[2]))
        elif len(o) == 4 and o[3] == "heads":
            out_specs.append(pl.BlockSpec((o[0], tm, o[1]), lambda i: (0, i, 0)))
            out_shape.append(jax.ShapeDtypeStruct((o[0], m_rows, o[1]), o[2]))
        elif len(o) == 4 and o[3] == "tail":
            nh = n_tiles - o[0] // tm
            out_specs.append(pl.BlockSpec((tm, o[1]), lambda i, nh=nh: (jnp.maximum(i - nh, 0), 0)))
            out_shape.append(jax.ShapeDtypeStruct((o[0], o[1]), o[2]))
        elif len(o) == 4 and o[3] == "ctx":
            nkeep = o[0] // tm
            out_specs.append(pl.BlockSpec((tm, o[1]), lambda i, nkeep=nkeep: (jnp.minimum(i, nkeep - 1), 0)))
            out_shape.append(jax.ShapeDtypeStruct((o[0], o[1]), o[2]))
        else:
            out_specs.append(pl.BlockSpec((tm, o[0]), lambda i: (i, 0)))
            out_shape.append(jax.ShapeDtypeStruct((m_rows, o[0]), o[1]))
    return pl.pallas_call(
        body, grid=(n_tiles,), in_specs=in_specs, out_specs=out_specs, out_shape=out_shape,
        scratch_shapes=list(scratch), compiler_params=params or _params(1), name=name)(*[a for a, _ in ins])


def _modulate(x_ref, mod_ref, shift_row, scale_row):
    m = mod_ref[0]
    return x_ref[...] * (1.0 + m[scale_row:scale_row + 1]) + m[shift_row:shift_row + 1]


def modulation_all(cvec, mod_w, mod_b):
    depth, d, d6 = mod_w.shape
    g = cvec.shape[0]
    gp = -(-g // 16) * 16
    cp = jnp.zeros((gp, d), F32).at[:g].set(cvec)
    tn = 1024

    def body(c_ref, w_ref, b_ref, o_ref):
        c = c_ref[...]
        a = (c * jax.nn.sigmoid(c)).astype(BF16)
        o_ref[0] = _dot(a, w_ref[0].astype(BF16)) + b_ref[0]

    out = pl.pallas_call(
        body, grid=(depth, d6 // tn),
        in_specs=[pl.BlockSpec((gp, d), lambda l, j: (0, 0)),
                  pl.BlockSpec((1, d, tn), lambda l, j: (l, 0, j)),
                  pl.BlockSpec((1, 1, tn), lambda l, j: (l, 0, j))],
        out_specs=pl.BlockSpec((1, gp, tn), lambda l, j: (l, 0, j)),
        out_shape=jax.ShapeDtypeStruct((depth, gp, d6), F32),
        compiler_params=_params(2), name="modulation")(cp, mod_w, mod_b.reshape(depth, 1, d6))
    return out[:, :g].reshape(depth, g, 6, d)


def attention(q, kl, vl, *, n_seq, lq, row_off, n_heads, dq, dv, hpb, kvpb, tq, win,
              start_fn=None, kc=None, vc=None, bias=None, type_fn=None, sink=None, name=None):
    nq = lq // tq
    koff = row_off // lq
    rep = hpb // kvpb
    has_ctx, has_bias, has_sink = kc is not None, bias is not None, sink is not None
    bias_heads = has_bias and bias.shape[0] > 1
    off_blk = row_off // tq

    def kern(*refs):
        it = iter(refs)
        q_ref, kl_ref, vl_ref = next(it), next(it), next(it)
        kc_ref = next(it) if has_ctx else None
        vc_ref = next(it) if has_ctx else None
        b_ref = next(it) if has_bias else None
        s_ref = next(it) if has_sink else None
        o_ref = next(it)
        i = pl.program_id(2)
        hb = pl.program_id(1)
        if start_fn is None:
            start = 0
        else:
            start = pl.multiple_of(start_fn(i), 64)
        nt = (((1,), (1,)), ((), ()))
        scores = []
        for j in range(hpb):
            kv = j // rep
            qj = q_ref[:, j * dq:(j + 1) * dq]
            s = lax.dot_general(qj, kl_ref[kv, pl.ds(start, win), :], nt, preferred_element_type=F32)
            if has_bias:
                s = s + b_ref[j if bias_heads else 0, 0]
            m = jnp.max(s, axis=1, keepdims=True)
            sc = None
            if has_ctx:
                sc = lax.dot_general(qj, kc_ref[kv], nt, preferred_element_type=F32)
                m = jnp.maximum(m, jnp.max(sc, axis=1, keepdims=True))
            scores.append((s, sc, m))
        outs = []
        for j in range(hpb):
            kv = j // rep
            s, sc, m = scores[j]
            if has_sink:
                sk = s_ref[hb * hpb + j]
                m = jnp.maximum(m, sk)
            acc = _dot(jnp.exp2((s - m).astype(BF16)), vl_ref[kv, pl.ds(start, win), :])
            if has_ctx:
                acc = acc + _dot(jnp.exp2((sc - m).astype(BF16)), vc_ref[kv])
            l = acc[:, dv:dv + 1]
            if has_sink:
                l = l + jnp.exp2(sk - m)
            outs.append(acc[:, :dv] / l)
        o_ref[...] = jnp.concatenate(outs, axis=1).astype(o_ref.dtype)

    ins = [q, kl, vl]
    in_specs = [pl.BlockSpec((tq, hpb * dq), lambda b, h, i: (off_blk + b * nq + i, h)),
                pl.BlockSpec((kvpb, lq, dq), lambda b, h, i: (h, koff + b, 0)),
                pl.BlockSpec((kvpb, lq, V_LANES), lambda b, h, i: (h, koff + b, 0))]
    if has_ctx:
        lc = kc.shape[1] // n_seq
        ins += [kc, vc]
        in_specs += [pl.BlockSpec((kvpb, lc, dq), lambda b, h, i: (h, b, 0)),
                     pl.BlockSpec((kvpb, lc, V_LANES), lambda b, h, i: (h, b, 0))]
    if has_bias:
        ins.append(bias * LOG2E)
        hb_blk = hpb if bias_heads else 1
        in_specs.append(pl.BlockSpec((hb_blk, 1, tq, win),
                                     lambda b, h, i: (h if bias_heads else 0, type_fn(i), 0, 0)))
    if has_sink:
        ins.append(sink.astype(F32) * LOG2E)
        in_specs.append(pl.BlockSpec(memory_space=pltpu.SMEM))
    return pl.pallas_call(
        kern, grid=(n_seq, n_heads // hpb, nq), in_specs=in_specs,
        out_specs=pl.BlockSpec((tq, hpb * dv), lambda b, h, i: (b * nq + i, h)),
        out_shape=jax.ShapeDtypeStruct((n_seq * lq, n_heads * dv), BF16),
        compiler_params=_params(3), name=name)(*ins)


def _cache_heads_major(c, values=False):
    n_seq, length, n_heads, d = c.shape
    out = c.astype(BF16).transpose(2, 0, 1, 3).reshape(n_heads, n_seq * length, d)
    if values:
        rows = n_seq * length
        out = jnp.concatenate([out, jnp.ones((n_heads, rows, 1), BF16),
                               jnp.zeros((n_heads, rows, V_LANES - d - 1), BF16)], axis=-1)
    return out


def _store_value_heads(ref, val, width):
    rows = val.shape[0]
    lane = lax.broadcasted_iota(jnp.int32, (rows, V_LANES - width), 1)
    tail = jnp.where(lane == 0, 1.0, 0.0).astype(ref.dtype)
    for hh in range(ref.shape[0]):
        ref[hh] = jnp.concatenate([val[:, hh * width:(hh + 1) * width].astype(ref.dtype), tail], axis=1)


def _store_mla_keys(ref, k_nope, k_rope):
    rows = k_nope.shape[0]
    tail = jnp.concatenate([k_rope, jnp.zeros((rows, B_QPAD - B_NOPE - B_ROPE), k_rope.dtype)], axis=1)
    for hh in range(ref.shape[0]):
        ref[hh] = jnp.concatenate([k_nope[:, hh * B_NOPE:(hh + 1) * B_NOPE], tail], axis=1).astype(ref.dtype)


def _store_heads(ref, val, width):
    for hh in range(ref.shape[0]):
        ref[hh] = val[:, hh * width:(hh + 1) * width].astype(ref.dtype)


def _rope_tables(length, dim, lead, tail, reps):
    half = dim // 2
    nf = half // 2
    t = jnp.arange(length)
    row = (t // GRID_W).astype(F32)
    col = (t % GRID_W).astype(F32)
    inv = ROPE_BASE ** (-jnp.arange(nf, dtype=F32) / nf)
    ang = jnp.concatenate([row[:, None] * inv, col[:, None] * inv], axis=-1)
    cos, sin = jnp.cos(ang), jnp.sin(ang)
    c = jnp.concatenate([jnp.ones((length, lead), F32), cos, cos, jnp.ones((length, tail), F32)] * reps, axis=1)
    s = jnp.concatenate([jnp.zeros((length, lead), F32), sin, sin, jnp.zeros((length, tail), F32)] * reps, axis=1)
    return (jnp.stack([jnp.ones_like(c), c]), jnp.stack([jnp.zeros_like(s), s]))


def _rot_cols(w, dim):
    k, n = w.shape
    wb = w.reshape(k, n // dim, dim)
    half = dim // 2
    return jnp.concatenate([-wb[..., half:], wb[..., :half]], axis=-1).reshape(k, n)


def _cs(phase, n, sign):
    ang = (2.0 * np.pi / n) * (phase % n).astype(F32)
    return jnp.cos(ang), sign * jnp.sin(ang)


def _cblock(wr, wi):
    return jnp.concatenate([jnp.concatenate([wr, -wi], axis=-1), jnp.concatenate([wi, wr], axis=-1)], axis=-2)


SUB = 8


def slow_stage(a, x, *, epi=None, name=None):
    p_n, k, s_n, d = x.shape
    m = a.shape[0]
    x5 = x.reshape(p_n, k, s_n // SUB, SUB, d)

    def kern(*refs):
        if epi is None:
            a_ref, x_ref, o_ref = refs
        else:
            a_ref, x_ref, g_ref, z_ref, s_ref, o_ref = refs
        xt = pltpu.einshape("ksd->skd", x_ref[...])
        av = a_ref[...]
        y = jnp.stack([_dot3(av, xt[s]) for s in range(SUB)], axis=0)
        y = pltpu.einshape("smd->msd", y)
        if epi is not None:
            y = g_ref[...] * (y + s_ref[...][None] * z_ref[...])
        o_ref[...] = y

    blk = lambda rows: pl.BlockSpec((None, rows, None, SUB, d), lambda p, t: (p, 0, t, 0, 0))
    ins = [a, x5]
    in_specs = [pl.BlockSpec((m, k), lambda p, t: (0, 0)), blk(k)]
    if epi is not None:
        gate, z, skip = epi
        ins += [gate.reshape(p_n, m, s_n // SUB, SUB, d), z.reshape(p_n, m, s_n // SUB, SUB, d), skip]
        in_specs += [blk(m), blk(m), pl.BlockSpec((1, d), lambda p, t: (0, 0))]
    out = pl.pallas_call(
        kern, grid=(p_n, s_n // SUB), in_specs=in_specs, out_specs=blk(m),
        out_shape=jax.ShapeDtypeStruct((p_n, m, s_n // SUB, SUB, d), F32),
        compiler_params=_params(2), name=name)(*ins)
    return out.reshape(p_n, m, s_n, d)


def spectral(x, mf, g=None, mi=None, *, dt, g_index=0, epi=None, name=None):
    p_n, planes, k1_n, nin, d = x.shape
    nf = mf.shape[1] // 2
    nout = nf if g is None else mi.shape[1] // 2

    def kern(*refs):
        it = iter(refs)
        x_ref, mf_ref = next(it), next(it)
        g_ref = next(it) if g is not None else None
        mi_ref = next(it) if g is not None else None
        if epi is not None:
            gate_ref, z_ref, skip_ref = next(it), next(it), next(it)
        o_ref = next(it)
        xs = [x_ref[0, pp, 0] for pp in range(planes)]
        xin = xs[0] if planes == 1 else jnp.concatenate(xs, axis=0)
        f = _dot3(mf_ref[0], xin)
        if g is not None:
            fr, fi = f[:nf], f[nf:]
            gr, gi = g_ref[0, 0, 0], g_ref[0, 1, 0]
            y = jnp.concatenate([fr * gr - fi * gi, fr * gi + fi * gr], axis=0)
            f = _dot3(mi_ref[0], y)
        for pp in range(2):
            y = f[pp * nout:(pp + 1) * nout]
            if epi is not None:
                y = gate_ref[0, pp, 0] * (y + skip_ref[...] * z_ref[0, pp, 0])
            o_ref[0, pp, 0] = y

    ins = [x, mf]
    in_specs = [pl.BlockSpec((1, planes, 1, nin, dt), lambda k, j, p: (p, 0, k, 0, j)),
                pl.BlockSpec((1,) + mf.shape[1:], lambda k, j, p: (k, 0, 0))]
    if g is not None:
        ins += [g, mi]
        in_specs += [pl.BlockSpec((1, 2, 1, nf, dt), lambda k, j, p: (g_index, 0, k, 0, j)),
                     pl.BlockSpec((1,) + mi.shape[1:], lambda k, j, p: (k, 0, 0))]
    if epi is not None:
        ins += list(epi)
        in_specs += [pl.BlockSpec((1, 2, 1, nout, dt), lambda k, j, p: (p, 0, k, 0, j)),
                     pl.BlockSpec((1, 2, 1, nout, dt), lambda k, j, p: (p, 0, k, 0, j)),
                     pl.BlockSpec((1, dt), lambda k, j, p: (0, j))]
    return pl.pallas_call(
        kern, grid=(k1_n, d // dt, p_n), in_specs=in_specs,
        out_specs=pl.BlockSpec((1, 2, 1, nout, dt), lambda k, j, p: (p, 0, k, 0, j)),
        out_shape=jax.ShapeDtypeStruct((p_n, 2, k1_n, nout, d), F32),
        compiler_params=_params(3), name=name)(*ins)


def hyena_two_sided_filters(length, w1, b1, w2, b2, w3, freq, log_decay, name):
    hid = w2.shape[0]
    d = w3.shape[1] // 4
    t = jnp.linspace(0.0, 1.0, length, dtype=F32)[:, None]
    ang = 2.0 * jnp.pi * t * jnp.arange(1, C_POS_BANDS + 1, dtype=F32)
    z = jnp.concatenate([t, jnp.cos(ang), jnp.sin(ang)], axis=-1)
    kpad = LANES - z.shape[1]
    z = jnp.pad(z, ((0, 0), (0, kpad)))
    z2 = jnp.concatenate([z, z[::-1]], axis=0)
    w1p = jnp.pad(w1, ((0, kpad), (0, 0)))
    full = lambda a: pl.BlockSpec(a.shape, lambda *_: (0,) * a.ndim)

    def ffn_kern(z_ref, w1_ref, b1_ref, w2_ref, b2_ref, f_ref, a_ref):
        fr = f_ref[...]
        a = jnp.sin(fr * (_dot3(z_ref[...], w1_ref[...]) + b1_ref[...]))
        a_ref[...] = jnp.sin(fr * (_dot3(a, w2_ref[...]) + b2_ref[...]))

    args = [z2, w1p, b1.reshape(1, hid), w2, b2.reshape(1, hid), freq.reshape(1, hid)]
    act = pl.pallas_call(
        ffn_kern, grid=(1,), in_specs=[full(a) for a in args], out_specs=pl.BlockSpec((2 * length, hid), lambda i: (0, 0)),
        out_shape=jax.ShapeDtypeStruct((2 * length, hid), F32), compiler_params=_params(1), name=name + "_ffn")(*args)

    tn = 256
    per = d // tn

    def kern(af_ref, ar_ref, w3f_ref, w3b_ref, ldf_ref, ldb_ref, o_ref):
        row = lax.broadcasted_iota(jnp.int32, (length, tn), 0)
        pos = row.astype(F32) * (1.0 / (length - 1))
        pos_rev = (length - 1 - row).astype(F32) * (1.0 / (length - 1))

        def filt(a_ref, w_ref, ld_ref, tt):
            f = _dot3(a_ref[...], w_ref[...]) * jnp.exp(-jnp.exp(ld_ref[...]) * tt)
            return f / (jnp.sum(jnp.abs(f), axis=0, keepdims=True) + 1e-6)

        hf = filt(af_ref, w3f_ref, ldf_ref, pos)
        hb_rev = filt(ar_ref, w3b_ref, ldb_ref, pos_rev)
        o_ref[0, 0] = hf + jnp.where(row == 0, hb_rev[length - 1:length], 0.0)
        o_ref[0, 1] = jnp.where(row == 0, 0.0, pltpu.roll(hb_rev, 1, 0))

    col = lambda direction: (lambda o, j: (0, (2 * o + direction) * per + j))
    ld = log_decay.reshape(1, 4 * d)
    g = pl.pallas_call(
        kern, grid=(2, per),
        in_specs=[pl.BlockSpec((length, hid), lambda o, j: (0, 0)), pl.BlockSpec((length, hid), lambda o, j: (1, 0)),
                  pl.BlockSpec((hid, tn), col(0)), pl.BlockSpec((hid, tn), col(1)),
                  pl.BlockSpec((1, tn), col(0)), pl.BlockSpec((1, tn), col(1))],
        out_specs=pl.BlockSpec((1, 2, length, tn), lambda o, j: (o, 0, 0, j)),
        out_shape=jax.ShapeDtypeStruct((2, 2, length, d), F32),
        compiler_params=_params(2), name=name)(act, act, w3, w3, ld, ld)
    return g.reshape(2, 2 * length, d)


def short_conv(u, w, b, *, row_off, n_seq, length, parts, name):
    c = u.shape[1]
    dt = 256
    per = c // parts // dt
    first = row_off // length

    def kern(u_ref, w_ref, b_ref, *o_refs):
        x = u_ref[...]
        r = lax.broadcasted_iota(jnp.int32, x.shape, 0)
        prev = jnp.where(r == 0, 0.0, pltpu.roll(x, 1, 0))
        nxt = jnp.where(r == length - 1, 0.0, pltpu.roll(x, length - 1, 0))
        wv = w_ref[...]
        y = prev * wv[0:1] + x * wv[1:2] + nxt * wv[2:3] + b_ref[...]
        part = pl.program_id(1) // per
        for k, o_ref in enumerate(o_refs):
            @pl.when(part == k)
            def _(o_ref=o_ref):
                o_ref[0] = y

    out_spec = lambda k: pl.BlockSpec((1, length, dt), lambda s, j: (s, 0, jnp.clip(j - k * per, 0, per - 1)))
    return pl.pallas_call(
        kern, grid=(n_seq, c // dt),
        in_specs=[pl.BlockSpec((length, dt), lambda s, j: (first + s, j)),
                  pl.BlockSpec((3, dt), lambda s, j: (0, j)),
                  pl.BlockSpec((1, dt), lambda s, j: (0, j))],
        out_specs=[out_spec(k) for k in range(parts)],
        out_shape=[jax.ShapeDtypeStruct((n_seq, length, c // parts), F32)] * parts,
        compiler_params=_params(2), name=name)(u, w, b)


def _dft_mats_single(n, nin_data, nout):
    k = jnp.arange(n, dtype=jnp.int32)
    fr, fi = _cs(k[:, None] * k[None, :nin_data], n, -1.0)
    mf_data = _cblock(fr, fi)[None]
    gr, gi = _cs(k[:, None] * k[None, :], n, -1.0)
    mf_filt = jnp.concatenate([gr, gi], axis=0)[None]
    ir, ii = _cs(k[:nout, None] * k[None, :], n, 1.0)
    mi = _cblock(ir / n, ii / n)[None]
    return mf_data, mf_filt, mi


def _dft_mats_two_stage(n, n2):
    n1 = n // n2
    k1 = jnp.arange(n1, dtype=jnp.int32)
    t1h = jnp.arange(n1 // 2, dtype=jnp.int32)
    ar, ai = _cs((n // n1) * k1[:, None] * t1h[None, :], n, -1.0)
    a_data = _cblock(ar, ai)
    fr, fi = _cs((n // n1) * k1[:, None] * k1[None, :], n, -1.0)
    a_filt = jnp.concatenate([fr, fi], axis=0)
    br, bi = _cs((n // n1) * t1h[:, None] * k1[None, :], n, 1.0)
    a_inv = _cblock(br, bi)
    t2 = jnp.arange(n2, dtype=jnp.int32)
    kk = k1[:, None, None] + n1 * t2[None, :, None]
    mr, mi_ = _cs(kk * t2[None, None, :], n, -1.0)
    mf = _cblock(mr, mi_)
    vr, vi = _cs(jnp.swapaxes(kk, 1, 2) * t2[None, :, None], n, 1.0)
    mi = _cblock(vr / n, vi / n)
    return a_data, a_filt, a_inv, mf, mi


def _route(logits):
    lane = lax.broadcasted_iota(jnp.int32, logits.shape, 1).astype(F32)
    big = 1e9
    lg = jnp.where(lane < N_GROUPS, logits, -jnp.inf)
    mg = jnp.max(lg, axis=1, keepdims=True)
    gi = jnp.min(jnp.where(lg == mg, lane, big), axis=1, keepdims=True)
    p_g = 1.0 / jnp.sum(jnp.exp(lg - mg), axis=1, keepdims=True)
    lo = N_GROUPS + EXPERTS_PER_GROUP * gi
    le = jnp.where((lane >= lo) & (lane < lo + EXPERTS_PER_GROUP), logits, -jnp.inf)
    m1 = jnp.max(le, axis=1, keepdims=True)
    i1 = jnp.min(jnp.where(le == m1, lane, big), axis=1, keepdims=True)
    le2 = jnp.where(lane == i1, -jnp.inf, le)
    m2 = jnp.max(le2, axis=1, keepdims=True)
    i2 = jnp.min(jnp.where(le2 == m2, lane, big), axis=1, keepdims=True)
    e2 = jnp.exp(m2 - m1)
    w1 = p_g / (1.0 + e2)
    w2 = p_g * e2 / (1.0 + e2)
    return lane, i1 - N_GROUPS, i2 - N_GROUPS, w1, w2


def _rank_in_tile(lane, e1, e2):
    tm = lane.shape[0]
    picks = jnp.where(lane == e1, 1.0, 0.0) + jnp.where(lane == e2, 1.0, 0.0)
    r = lax.broadcasted_iota(jnp.int32, (tm, tm), 0)
    c = lax.broadcasted_iota(jnp.int32, (tm, tm), 1)
    earlier = jnp.where(c < r, 1.0, 0.0).astype(BF16)
    return _dot(earlier, picks.astype(BF16)), picks


def _pack_bf16_pairs(y):
    c = y.shape[1] // 2
    hi = lax.bitcast_convert_type(y[:, :c].astype(BF16).astype(F32), jnp.uint32)
    lo = lax.bitcast_convert_type(y[:, c:].astype(BF16).astype(F32), jnp.uint32)
    return hi | (lo >> 16)


def _unpack_bf16_pairs(w):
    hi = lax.bitcast_convert_type(w & jnp.uint32(0xFFFF0000), F32)
    lo = lax.bitcast_convert_type(w << 16, F32)
    return hi, lo


def expert_mlp(xs, blk_e, n_used, w_gate, w_up, w_down, layer):
    p_rows, d = xs.shape
    de = w_gate.shape[3]
    nblk = p_rows // MOE_BLK

    def kern(be_ref, nu_ref, x_ref, wg_ref, wu_ref, wd_ref, o_ref, wg_s, wu_s, wd_s):
        i = pl.program_id(0)
        used = i < nu_ref[0]
        fresh = jnp.logical_or(i == 0, be_ref[i] != be_ref[jnp.maximum(i - 1, 0)])

        @pl.when(jnp.logical_and(used, fresh))
        def _():
            wg_s[...] = wg_ref[0, 0].astype(BF16)
            wu_s[...] = wu_ref[0, 0].astype(BF16)
            wd_s[...] = wd_ref[0, 0].astype(BF16)

        @pl.when(used)
        def _():
            x = x_ref[...].astype(BF16)
            g = _dot(x, wg_s[...])
            u = _dot(x, wu_s[...])
            a = (g * jax.nn.sigmoid(g) * u).astype(BF16)
            o_ref[...] = _pack_bf16_pairs(_dot(a, wd_s[...]))

        @pl.when(jnp.logical_not(used))
        def _():
            o_ref[...] = jnp.zeros_like(o_ref)

    last = lambda i, nu: jnp.minimum(i, nu[0] - 1)
    grid_spec = pltpu.PrefetchScalarGridSpec(
        num_scalar_prefetch=2, grid=(nblk,),
        in_specs=[pl.BlockSpec((MOE_BLK, d), lambda i, be, nu: (last(i, nu), 0)),
                  pl.BlockSpec((1, 1, d, de), lambda i, be, nu: (layer, be[last(i, nu)], 0, 0)),
                  pl.BlockSpec((1, 1, d, de), lambda i, be, nu: (layer, be[last(i, nu)], 0, 0)),
                  pl.BlockSpec((1, 1, de, d), lambda i, be, nu: (layer, be[last(i, nu)], 0, 0))],
        out_specs=pl.BlockSpec((MOE_BLK, d // 2), lambda i, be, nu: (i, 0)),
        scratch_shapes=[pltpu.VMEM((d, de), BF16), pltpu.VMEM((d, de), BF16), pltpu.VMEM((de, d), BF16)])
    return pl.pallas_call(
        kern, grid_spec=grid_spec, out_shape=jax.ShapeDtypeStruct((p_rows, d // 2), jnp.uint32),
        compiler_params=_params(1), name="expert_mlp")(blk_e, n_used, xs, w_gate, w_up, w_down)


def _slot_plan(experts, ranks):
    n = 2 * experts.shape[0]
    ids = jnp.arange(N_EXPERTS, dtype=jnp.int32)
    onehot = experts[..., None] == ids
    counts = jnp.sum(onehot, axis=(0, 1)).astype(jnp.int32)
    padded = (counts + MOE_BLK - 1) // MOE_BLK * MOE_BLK
    pend = jnp.cumsum(padded)
    pstart = pend - padded
    p_rows = -(-n // MOE_BLK) * MOE_BLK + N_EXPERTS * MOE_BLK
    nblk = p_rows // MOE_BLK
    blk_first = jnp.arange(nblk, dtype=jnp.int32) * MOE_BLK
    blk_e = jnp.minimum(jnp.sum(pend[None, :] <= blk_first[:, None], axis=1), N_EXPERTS - 1).astype(jnp.int32)
    n_used = (pend[-1:] // MOE_BLK).astype(jnp.int32)
    blk_ids = jnp.arange(nblk, dtype=jnp.int32)
    partly = jnp.any((blk_ids[:, None] == (pend // MOE_BLK - 1)[None, :]) & (counts % MOE_BLK != 0)[None, :], axis=1)
    zero_blk = (partly | (blk_ids >= n_used[0])).astype(jnp.int32)
    slot = ranks + jnp.sum(jnp.where(onehot, pstart, 0), axis=-1)
    return slot.reshape(n).astype(jnp.int32), blk_e, n_used, zero_blk, p_rows


def _dma_params():
    return pltpu.CompilerParams(dimension_semantics=("arbitrary",), vmem_limit_bytes=VMEM_LIMIT,
                                disable_bounds_checks=True)


DMA_UNROLL = 8


def moe_dispatch(x1, mod, slot, zero_blk, p_rows, group_len, name):
    t, d = x1.shape
    tm = ROW_TILE

    def body(slot_ref, zb_ref, x_ref, mod_ref, xs_ref, h_ref, zero_ref, sem):
        t_id = pl.program_id(0)
        par = t_id % 2
        zsem = sem.at[2]
        h_ref[par] = _modulate(x_ref, mod_ref, 3, 4)

        @pl.when(t_id == 0)
        def _():
            zero_ref[...] = jnp.zeros_like(zero_ref)

            def zstart(b, c):
                @pl.when(zb_ref[b] != 0)
                def _():
                    first = pl.multiple_of(b * MOE_BLK, MOE_BLK)
                    pltpu.make_async_copy(zero_ref, xs_ref.at[pl.ds(first, MOE_BLK)], zsem).start()
                return c

            def zwait(b, c):
                @pl.when(zb_ref[b] != 0)
                def _():
                    pltpu.make_async_copy(zero_ref, xs_ref.at[pl.ds(0, MOE_BLK)], zsem).wait()
                return c

            lax.fori_loop(0, p_rows // MOE_BLK, zstart, 0)
            lax.fori_loop(0, p_rows // MOE_BLK, zwait, 0)

        def start(rr, c):
            for u in range(DMA_UNROLL):
                r = rr * DMA_UNROLL + u
                for k in range(2):
                    pltpu.make_async_copy(h_ref.at[par, pl.ds(r, 1)], xs_ref.at[pl.ds(slot_ref[2 * r + k], 1)],
                                          sem.at[par]).start()
            return c

        lax.fori_loop(0, tm // DMA_UNROLL, start, 0)

        def wait_tile(p):
            for k in range(2):
                pltpu.make_async_copy(h_ref.at[p], xs_ref.at[pl.ds(0, tm)], sem.at[p]).wait()

        @pl.when(t_id > 0)
        def _():
            wait_tile(1 - par)

        @pl.when(t_id == pl.num_programs(0) - 1)
        def _():
            wait_tile(par)

    return row_call(body, t, [(slot, "smem_rows"), (zero_blk, "smem"), (x1, "row"), (mod, "group")],
                    [(p_rows, d, F32, "any")], group_len=group_len,
                    scratch=[pltpu.VMEM((2, tm, d), F32), pltpu.VMEM((MOE_BLK, d), F32), pltpu.SemaphoreType.DMA((3,))],
                    params=_dma_params(), name=name)[0]


def kernel(x_prompt, x_sample, cache_a_k, cache_a_v, cache_b_ckv, cache_b_krope, cache_d_k, cache_d_v, c_ctx, c, mod_w, mod_b, ln_g, ln_b, a_wq, a_wk, a_wv, a_wo, a_sink, b_wq_a, b_q_norm, b_wq_b, b_wkv_a, b_kv_norm, b_wk_b, b_wv_b, b_wo, c_w_in, c_b_in, c_conv_w, c_conv_b, c_ffn_w1, c_ffn_b1, c_ffn_w2, c_ffn_b2, c_ffn_w3, c_ffn_freq, c_log_decay, c_skip, c_wo, d_wq, d_wk, d_wv, d_wo, d_rel_bias, moe_wr_g, moe_br_g, moe_wr_e, moe_br_e, moe_w_gate, moe_w_up, moe_w_down):
    bc, lc, d = x_prompt.shape
    bl, ll, _ = x_sample.shape
    past = cache_a_k.shape[2]
    gl = ll
    assert bc * lc == gl and d == A_HEADS * HEAD_DIM
    ng = 1 + bl
    t_all = ng * gl
    x_ctx, x_lat = x_prompt.reshape(gl, d), x_sample.reshape(bl * gl, d)
    cvec = jnp.concatenate([c_ctx[None, :], c], axis=0)
    mods = modulation_all(cvec, mod_w, mod_b)
    rc = functools.partial(row_call, group_len=gl)
    row2 = lambda v: v.reshape(1, -1)
    n_ctx_tiles = gl // ROW_TILE

    def post_mixer(i, att_ctx, att_lat, wo, x_in):
        unused = LANES - N_GROUPS - N_EXPERTS
        wr = jnp.concatenate([moe_wr_g[i], moe_wr_e[i], jnp.zeros((d, unused), F32)], axis=1)
        br = jnp.concatenate([moe_br_g[i], moe_br_e[i], jnp.zeros((unused,), F32)])[None, :]

        x_split = isinstance(x_in, tuple)

        def body(attc_ref, attl_ref, *rest):
            is_ctx = pl.program_id(0) < n_ctx_tiles
            if x_split:
                x_old = jnp.where(is_ctx, rest[0][...], rest[1][...])
                rest = rest[2:]
            else:
                x_old = rest[0][...]
                rest = rest[1:]
            mod_ref, wo_ref, g_ref, b_ref, wr_ref, br_ref, x1_ref, info_ref, w_ref, seen_ref = rest

            @pl.when(pl.program_id(0) == 0)
            def _():
                seen_ref[...] = jnp.zeros_like(seen_ref)

            m = mod_ref[0]
            att = jnp.where(is_ctx, attc_ref[...], attl_ref[...])
            o = _dot(att.astype(BF16), wo_ref[...])
            x1 = _layer_norm(DEEPNORM_ALPHA * x_old + m[2:3] * o, g_ref[...], b_ref[...])
            x1_ref[...] = x1
            h = x1 * (1.0 + m[4:5]) + m[3:4]
            lane, e1, e2, w1, w2 = _route(_dot3(h, wr_ref[...]) + br_ref[...])
            before, picks = _rank_in_tile(lane, e1, e2)
            before = before + seen_ref[...]
            r1 = jnp.sum(jnp.where(lane == e1, before, 0.0), axis=1, keepdims=True)
            r2 = jnp.sum(jnp.where(lane == e2, before, 0.0), axis=1, keepdims=True)
            seen_ref[...] += jnp.sum(picks, axis=0, keepdims=True)
            info = jnp.where(lane == 0, e1, jnp.where(lane == 1, e2, jnp.where(lane == 2, r1, jnp.where(lane == 3, r2, 0.0))))
            info_ref[...] = info.astype(jnp.int32)
            w_ref[...] = jnp.where(lane == 0, w1, jnp.where(lane == 1, w2, 0.0))

        return rc(body, t_all,
                  [(att_ctx, "row_head"), (att_lat, "row_tail")]
                  + ([(x_in[0], "row_head"), (x_in[1], "row_tail")] if x_split else [(x_in, "row")])
                  + [(mods[i], "group"), (wo.astype(BF16), "full"),
                   (row2(ln_g[i, 0]), "full"), (row2(ln_b[i, 0]), "full"), (wr, "full"), (br, "full")],
                  [(d, F32), (LANES, jnp.int32), (LANES, F32)],
                  scratch=[pltpu.VMEM((1, LANES), F32)], name=f"post_mixer{i}")

    def moe(i, x1, info, w):
        slot, blk_e, n_used, zero_blk, p_rows = _slot_plan(info[:, 0:2], info[:, 2:4])
        xs = moe_dispatch(x1, mods[i], slot, zero_blk, p_rows, gl, f"moe_dispatch{i}")
        ys = expert_mlp(xs, blk_e, n_used, moe_w_gate, moe_w_up, moe_w_down, i)
        tm = ROW_TILE

        split_out = i == DEPTH - 1

        def body(slot_ref, slot_next_ref, x_ref, w_ref, mod_ref, g_ref, b_ref, ys_ref, *rest):
            out_refs, (buf, sem) = rest[:-2], rest[-2:]
            t = pl.program_id(0)
            par = t % 2

            def request(s_ref, p):
                def start(rr, c):
                    for u in range(DMA_UNROLL):
                        r = rr * DMA_UNROLL + u
                        for k in range(2):
                            pltpu.make_async_copy(ys_ref.at[pl.ds(s_ref[2 * r + k], 1)], buf.at[p, k, pl.ds(r, 1)],
                                                  sem.at[p]).start()
                    return c

                lax.fori_loop(0, tm // DMA_UNROLL, start, 0)

            @pl.when(t == 0)
            def _():
                request(slot_ref, 0)

            @pl.when(t + 1 < pl.num_programs(0))
            def _():
                request(slot_next_ref, 1 - par)

            for k in range(2):
                pltpu.make_async_copy(ys_ref.at[pl.ds(0, tm)], buf.at[par, k], sem.at[par]).wait()
            m = mod_ref[0]
            wv = w_ref[...]
            a_hi, a_lo = _unpack_bf16_pairs(buf[par, 0])
            b_hi, b_lo = _unpack_bf16_pairs(buf[par, 1])
            w0, w1 = wv[:, 0:1], wv[:, 1:2]
            y = jnp.concatenate([w0 * a_hi + w1 * b_hi, w0 * a_lo + w1 * b_lo], axis=1)
            x2 = _layer_norm(DEEPNORM_ALPHA * x_ref[...] + m[5:6] * y, g_ref[...], b_ref[...])
            if not split_out:
                out_refs[0][...] = x2
            else:
                is_ctx = pl.program_id(0) < n_ctx_tiles

                @pl.when(is_ctx)
                def _():
                    out_refs[0][...] = x2

                @pl.when(jnp.logical_not(is_ctx))
                def _():
                    out_refs[1][...] = x2

        outs = [(gl, d, F32, "ctx"), (t_all - gl, d, F32, "tail")] if split_out else [(d, F32)]
        res = rc(body, t_all,
                 [(slot, "smem_rows"), (slot, "smem_rows_next"), (x1, "row"), (w, "row"), (mods[i], "group"),
                  (row2(ln_g[i, 1]), "full"), (row2(ln_b[i, 1]), "full"), (ys, "any")],
                 outs, scratch=[pltpu.VMEM((2, 2, tm, d // 2), jnp.uint32), pltpu.SemaphoreType.DMA((2,))],
                 params=_dma_params(), name=f"moe_combine{i}")
        return res if split_out else res[0]

    def finish_layer(i, att_ctx, att_lat, wo, x_in):
        x1, info, w = post_mixer(i, att_ctx, att_lat, wo, x_in)
        return moe(i, x1, info, w)

    tq = 256
    i = 0
    hd = HEAD_DIM
    kvw = A_KV_HEADS * hd
    cq, sq = _rope_tables(gl, hd, 0, 0, LANES // hd)
    wq, wk, wv = a_wq[0], a_wk[0], a_wv[0]

    def body_a(xc_ref, xl_ref, mod_ref, wq_ref, wqr_ref, wk_ref, wkr_ref, wv_ref, c_ref, s_ref, q_ref, kh_ref, vh_ref, kc_ref, vc_ref):
        x_in = jnp.where(pl.program_id(0) < n_ctx_tiles, xc_ref[...], xl_ref[...])
        m = mod_ref[0]
        h = (x_in * (1.0 + m[1:2]) + m[0:1]).astype(BF16)
        cc, ss = c_ref[0], s_ref[0]
        q = _dot(h, wq_ref[...]) * _tile_lanes(cc, d) + _dot(h, wqr_ref[...]) * _tile_lanes(ss, d)
        k = _dot(h, wk_ref[...]) * _tile_lanes(cc, kvw) + _dot(h, wkr_ref[...]) * _tile_lanes(ss, kvw)
        v = _dot(h, wv_ref[...])
        q_ref[...] = (q * QSCALE_64).astype(BF16)
        _store_heads(kh_ref, k, hd)
        _store_value_heads(vh_ref, v, hd)

        @pl.when(pl.program_id(0) < n_ctx_tiles)
        def _():
            kc_ref[...] = k
            vc_ref[...] = v

    q, kh, vh, k_new, v_new = rc(
        body_a, t_all,
        [(x_ctx, "row_head"), (x_lat, "row_tail"), (mods[i], "group"), (wq.astype(BF16), "full"),
         (_rot_cols(wq, hd).astype(BF16), "full"),
         (wk.astype(BF16), "full"), (_rot_cols(wk, hd).astype(BF16), "full"), (wv.astype(BF16), "full"),
         (cq, "pos"), (sq, "pos")],
        [(d, BF16), (A_KV_HEADS, hd, BF16, "heads"), (A_KV_HEADS, V_LANES, BF16, "heads"),
         (gl, kvw, F32, "ctx"), (gl, kvw, F32, "ctx")], name="proj_a")
    out_a_k = k_new.reshape(bc, 1, lc, A_KV_HEADS, hd)
    out_a_v = v_new.reshape(bc, 1, lc, A_KV_HEADS, hd)
    common = dict(n_heads=A_HEADS, dq=hd, dv=hd, hpb=8, kvpb=2, tq=tq, sink=a_sink[0])
    att_c = attention(q, kh, vh, n_seq=bc, lq=lc, row_off=0, win=lc, name="attn_a_ctx", **common)
    win_a = 2 * tq
    nq_l = gl // tq
    qi = jnp.arange(tq)[:, None]
    ki = jnp.arange(win_a)[None, :]
    band = jnp.stack([jnp.where(jnp.abs(ki - (qi + off)) <= A_WINDOW, 0.0, NEG_INF)
                      for off in (0, A_WINDOW, 2 * A_WINDOW)]).astype(F32)[None]
    att_l = attention(q, kh, vh, n_seq=bl, lq=gl, row_off=gl, win=win_a,
                      start_fn=lambda ii: jnp.clip(ii * tq - A_WINDOW, 0, gl - win_a),
                      kc=_cache_heads_major(cache_a_k[:, 0]), vc=_cache_heads_major(cache_a_v[:, 0], values=True),
                      bias=band, type_fn=lambda ii: jnp.where(ii == 0, 0, jnp.where(ii == nq_l - 1, 2, 1)),
                      name="attn_a_lat", **common)
    x = finish_layer(i, att_c, att_l, a_wo[0], (x_ctx, x_lat))

    i = 1
    hq = B_NOPE + B_ROPE
    qw = B_HEADS * B_QPAD
    wqb = b_wq_b[0].reshape(B_Q_RANK, B_HEADS, hq)
    wqb_rot = jnp.concatenate([jnp.zeros_like(wqb[..., :B_NOPE]),
                               _rot_cols(wqb[..., B_NOPE:].reshape(B_Q_RANK, -1), B_ROPE).reshape(B_Q_RANK, B_HEADS, B_ROPE)],
                              axis=-1)
    padq = lambda wz: jnp.pad(wz, ((0, 0), (0, 0), (0, B_QPAD - hq))).reshape(B_Q_RANK, qw).astype(BF16)
    wkv_c, wkv_r = b_wkv_a[0][:, :B_KV_RANK], b_wkv_a[0][:, B_KV_RANK:]
    cqb, sqb = _rope_tables(gl, B_ROPE, B_NOPE, B_QPAD - hq, 1)
    ckr, skr = _rope_tables(gl, B_ROPE, 0, 0, 1)

    def body_b(x_ref, mod_ref, wqa_ref, qn_ref, wqb_ref, wqbr_ref, wc_ref, kn_ref, wr_ref, wrr_ref, wkb_ref, wvb_ref,
               cq_ref, sq_ref, ck_ref, sk_ref, q_ref, kh_ref, vh_ref, ckv_ref, kr_ref):
        h = _modulate(x_ref, mod_ref, 0, 1).astype(BF16)
        qa = _rms(_dot(h, wqa_ref[...]), qn_ref[...]).astype(BF16)
        q = (_dot(qa, wqb_ref[...]) * _tile_lanes(cq_ref[0], qw) + _dot(qa, wqbr_ref[...]) * _tile_lanes(sq_ref[0], qw))
        q_ref[...] = (q * QSCALE_MLA).astype(BF16)
        ckv = _rms(_dot(h, wc_ref[...]), kn_ref[...])
        kr = _dot(h, wr_ref[...]) * ck_ref[0] + _dot(h, wrr_ref[...]) * sk_ref[0]
        cb = ckv.astype(BF16)
        _store_mla_keys(kh_ref, _dot(cb, wkb_ref[...]), kr)
        _store_value_heads(vh_ref, _dot(cb, wvb_ref[...]), B_VDIM)

        @pl.when(pl.program_id(0) < n_ctx_tiles)
        def _():
            ckv_ref[...] = ckv
            kr_ref[...] = kr

    wkb, wvb = b_wk_b[0].astype(BF16), b_wv_b[0].astype(BF16)
    q, kh, vh, ckv_new, kr_new = rc(
        body_b, t_all,
        [(x, "row"), (mods[i], "group"), (b_wq_a[0].astype(BF16), "full"), (row2(b_q_norm[0]), "full"),
         (padq(wqb), "full"), (padq(wqb_rot), "full"), (wkv_c.astype(BF16), "full"), (row2(b_kv_norm[0]), "full"),
         (wkv_r.astype(BF16), "full"), (_rot_cols(wkv_r, B_ROPE).astype(BF16), "full"), (wkb, "full"), (wvb, "full"),
         (cqb, "pos"), (sqb, "pos"), (ckr, "pos"), (skr, "pos")],
        [(qw, BF16), (B_HEADS, B_QPAD, BF16, "heads"), (B_HEADS, V_LANES, BF16, "heads"),
         (gl, B_KV_RANK, F32, "ctx"), (gl, B_ROPE, F32, "ctx")], name="proj_b")
    out_b_ckv = ckv_new.reshape(bc, 1, lc, B_KV_RANK)
    out_b_krope = kr_new.reshape(bc, 1, lc, B_ROPE)

    def body_bc(c_ref, r_ref, wkb_ref, wvb_ref, kh_ref, vh_ref):
        cb = c_ref[...].astype(BF16)
        _store_mla_keys(kh_ref, _dot(cb, wkb_ref[...]), r_ref[...])
        _store_value_heads(vh_ref, _dot(cb, wvb_ref[...]), B_VDIM)

    n_pc = bl * past
    kh_p, vh_p = row_call(body_bc, n_pc,
                          [(cache_b_ckv[:, 0].reshape(n_pc, B_KV_RANK), "row"), (cache_b_krope[:, 0].reshape(n_pc, B_ROPE), "row"),
                           (wkb, "full"), (wvb, "full")],
                          [(B_HEADS, B_QPAD, BF16, "heads"), (B_HEADS, V_LANES, BF16, "heads")],
                          tm=min(ROW_TILE, n_pc), name="proj_b_past")
    common = dict(n_heads=B_HEADS, dq=B_QPAD, dv=B_VDIM, hpb=2, kvpb=2)
    att_c = attention(q, kh, vh, n_seq=bc, lq=lc, row_off=0, win=lc, tq=tq, name="attn_b_ctx", **common)
    att_l = attention(q, kh, vh, n_seq=bl, lq=gl, row_off=gl, win=gl, kc=kh_p, vc=vh_p, tq=min(2 * tq, gl),
                      name="attn_b_lat", **common)
    x = finish_layer(i, att_c, att_l, b_wo[0], x)

    i = 2
    d3 = 3 * d

    def body_c(x_ref, mod_ref, w_ref, b_ref, u_ref):
        h = _modulate(x_ref, mod_ref, 0, 1).astype(BF16)
        for j in range(3):
            u_ref[:, j * d:(j + 1) * d] = _dot(h, w_ref[:, j * d:(j + 1) * d]) + b_ref[:, j * d:(j + 1) * d]

    (u,) = rc(body_c, t_all, [(x, "row"), (mods[i], "group"), (c_w_in[0].astype(BF16), "full"), (row2(c_b_in[0]), "full")],
              [(d3, F32)], name="proj_c")
    u_c = short_conv(u, c_conv_w[0], row2(c_conv_b[0]), row_off=0, n_seq=bc, length=lc, parts=3, name="short_conv_ctx")
    u_l = short_conv(u, c_conv_w[0], row2(c_conv_b[0]), row_off=gl, n_seq=bl, length=gl, parts=3, name="short_conv_lat")
    fargs = (c_ffn_w1[0], c_ffn_b1[0], c_ffn_w2[0], c_ffn_b2[0], c_ffn_w3[0], c_ffn_freq[0], c_log_decay[0])
    g_c = hyena_two_sided_filters(lc, *fargs, name="filters_ctx")
    g_l = hyena_two_sided_filters(gl, *fargs, name="filters_lat")

    nc = 2 * lc
    mf_data, mf_filt, mi_c = _dft_mats_single(nc, lc, lc)
    spec_c = spectral(g_c.reshape(2, 1, 1, nc, d), mf_filt, dt=512, name="filt_spec_ctx")
    z_c = u_c[0].reshape(bc // 2, 2, 1, lc, d)
    for o in range(2):
        gate = u_c[o + 1].reshape(bc // 2, 2, 1, lc, d)
        z_c = spectral(z_c, mf_data, spec_c, mi_c, dt=512, g_index=o, epi=(gate, z_c, row2(c_skip[0, o])),
                       name=f"conv_ctx{o}")
    zc_out = z_c.reshape(gl, d)

    nl = 2 * gl
    n1 = nl // FFT_N2
    a_data, a_filt, a_inv, mf_l, mi_l = _dft_mats_two_stage(nl, FFT_N2)
    ga = slow_stage(a_filt, g_l.reshape(2, n1, FFT_N2, d), name="filt_stage_a")
    spec_l = spectral(ga.reshape(2, 2, n1, FFT_N2, d), mf_l, dt=1024, name="filt_spec_lat")
    z_l = u_l[0].reshape(bl // 2, n1, FFT_N2, d)
    for o in range(2):
        za = slow_stage(a_data, z_l, name=f"conv_lat_a{o}")
        zb = spectral(za.reshape(bl // 2, 2, n1, FFT_N2, d), mf_l, spec_l, mi_l, dt=1024, g_index=o,
                      name=f"conv_lat_c{o}")
        gate = u_l[o + 1].reshape(bl // 2, n1, FFT_N2, d)
        z_l = slow_stage(a_inv, zb.reshape(bl // 2, 2 * n1, FFT_N2, d), epi=(gate, z_l, row2(c_skip[0, o])),
                         name=f"conv_lat_i{o}")
    x = finish_layer(i, zc_out, z_l.reshape(bl * gl, d), c_wo[0], x)

    i = 3

    def body_d(x_ref, mod_ref, wq_ref, wk_ref, wv_ref, q_ref, kh_ref, vh_ref, kc_ref, vc_ref):
        h = _modulate(x_ref, mod_ref, 0, 1).astype(BF16)
        q_ref[...] = (_dot(h, wq_ref[...]) * QSCALE_64).astype(BF16)
        k = _dot(h, wk_ref[...])
        v = _dot(h, wv_ref[...])
        _store_heads(kh_ref, k, hd)
        _store_value_heads(vh_ref, v, hd)

        @pl.when(pl.program_id(0) < n_ctx_tiles)
        def _():
            kc_ref[...] = k
            vc_ref[...] = v

    q, k_heads, v_heads, k_new, v_new = rc(
        body_d, t_all,
        [(x, "row"), (mods[i], "group"), (d_wq[0].astype(BF16), "full"), (d_wk[0].astype(BF16), "full"),
         (d_wv[0].astype(BF16), "full")],
        [(d, BF16), (D_HEADS, hd, BF16, "heads"), (D_HEADS, V_LANES, BF16, "heads"), (gl, d, F32, "ctx"), (gl, d, F32, "ctx")],
        name="proj_d")
    out_d_k = k_new.reshape(bc, 1, lc, D_HEADS, hd)
    out_d_v = v_new.reshape(bc, 1, lc, D_HEADS, hd)
    common = dict(n_heads=D_HEADS, dq=hd, dv=hd, hpb=4, kvpb=4, tq=tq)
    att_c = attention(q, k_heads, v_heads, n_seq=bc, lq=lc, row_off=0, win=lc, name="attn_d_ctx", **common)
    rows = gl // GRID_W
    kh = min(MAX_NBR_ROWS, rows)
    qrows = tq // GRID_W
    krows = qrows + kh
    win_d = krows * GRID_W
    nq_l = gl // tq
    qr_l, kr_l = jnp.arange(qrows)[:, None], jnp.arange(krows)[None, :]
    qc, kc_ = jnp.arange(GRID_W)[:, None], jnp.arange(GRID_W)[None, :]
    c0 = jnp.clip(qc - NBR_COLS // 2, 0, GRID_W - NBR_COLS)
    col_ok = (kc_ >= c0) & (kc_ < c0 + NBR_COLS)
    dc = jnp.clip(kc_ - qc, 1 - NBR_COLS, NBR_COLS - 1) + NBR_COLS - 1
    dc_hot = (dc[..., None] == jnp.arange(2 * NBR_COLS - 1)).astype(F32)
    by_col = jnp.einsum("hrc,xyc->hrxy", d_rel_bias[0], dc_hot, precision=lax.Precision.HIGHEST)
    tabs = []
    for off, lo in ((0, jnp.zeros_like(qr_l)), (kh // 2, qr_l), (kh, jnp.full_like(qr_l, qrows))):
        row_ok = (kr_l >= lo) & (kr_l < lo + kh)
        dr = jnp.clip(kr_l - off - qr_l + MAX_NBR_ROWS - 1, 0, 2 * MAX_NBR_ROWS - 2)
        dr_hot = (dr[..., None] == jnp.arange(2 * MAX_NBR_ROWS - 1)).astype(F32)
        tab = jnp.einsum("qkr,hrxy->hqxky", dr_hot, by_col, precision=lax.Precision.HIGHEST)
        ok = row_ok[:, None, :, None] & col_ok[None, :, None, :]
        tabs.append(jnp.where(ok[None], tab, NEG_INF).reshape(D_HEADS, tq, win_d))
    nbr_bias = jnp.stack(tabs, axis=1).astype(F32)
    att_l = attention(q, k_heads, v_heads, n_seq=bl, lq=gl, row_off=gl, win=win_d,
                      start_fn=lambda ii: jnp.clip(ii * qrows - kh // 2, 0, rows - krows) * GRID_W,
                      kc=_cache_heads_major(cache_d_k[:, 0]), vc=_cache_heads_major(cache_d_v[:, 0], values=True),
                      bias=nbr_bias, type_fn=lambda ii: jnp.where(ii == 0, 0, jnp.where(ii == nq_l - 1, 2, 1)),
                      name="attn_d_lat", **common)
    y_ctx, y_lat = finish_layer(i, att_c, att_l, d_wo[0], x)

    y_prompt = y_ctx.reshape(bc, lc, d)
    y_sample = y_lat.reshape(bl, gl, d)
    return (y_prompt, y_sample, out_a_k, out_a_v, out_b_ckv, out_b_krope, out_d_k, out_d_v)
```

```python
import functools

import jax
import jax.numpy as jnp
import numpy as np
from jax import lax
from jax.experimental import pallas as pl
from jax.experimental.pallas import tpu as pltpu

F32 = jnp.float32
BF16 = jnp.bfloat16

GRID_W = 64
HEAD_DIM = 64
ROPE_BASE = 10000.0
LN_EPS = 1e-5
RMS_EPS = 1e-6
NEG_INF = -1e30
DEPTH = 4
DEEPNORM_ALPHA = (2 * DEPTH) ** 0.25
A_HEADS = 16
A_KV_HEADS = 4
A_WINDOW = 128
B_HEADS = 16
B_Q_RANK = 384
B_KV_RANK = 256
B_NOPE = 64
B_ROPE = 32
B_VDIM = 64
B_QPAD = 128
C_POS_BANDS = 16
D_HEADS = 16
MAX_NBR_ROWS = 8
NBR_COLS = 16
N_GROUPS = 4
EXPERTS_PER_GROUP = 8
N_EXPERTS = N_GROUPS * EXPERTS_PER_GROUP
D_EXPERT = 512
MOE_BLK = 512
LANES = 128
V_LANES = 128
LOG2E = 1.4426950408889634
QSCALE_64 = HEAD_DIM ** -0.5 * LOG2E
QSCALE_MLA = (B_NOPE + B_ROPE) ** -0.5 * LOG2E
FFT_N2 = 128
VMEM_LIMIT = 56 * 1024 * 1024
ROW_TILE = 512


def _params(n_axes):
    return pltpu.CompilerParams(dimension_semantics=("arbitrary",) * n_axes, vmem_limit_bytes=VMEM_LIMIT)


def _dot(a, b):
    return jnp.dot(a, b, preferred_element_type=F32)


def _split(x):
    hi = x.astype(BF16)
    lo = (x - hi.astype(F32)).astype(BF16)
    return hi, lo


def _dot3(a, b):
    ah, al = _split(a)
    bh, bl = _split(b)
    return _dot(ah, bh) + _dot(ah, bl) + _dot(al, bh)


def _layer_norm(y, g, b):
    mu = jnp.mean(y, axis=-1, keepdims=True)
    d = y - mu
    var = jnp.mean(d * d, axis=-1, keepdims=True)
    return d * lax.rsqrt(var + LN_EPS) * g + b


def _rms(y, g):
    return y * lax.rsqrt(jnp.mean(y * y, axis=-1, keepdims=True) + RMS_EPS) * g


def _tile_lanes(t, n):
    reps = n // t.shape[-1]
    return t if reps == 1 else jnp.concatenate([t] * reps, axis=-1)


def row_call(body, m_rows, ins, outs, *, tm=ROW_TILE, group_len=None, name=None, scratch=(), params=None):
    nb = None if group_len is None else group_len // tm
    n_tiles = m_rows // tm
    in_specs = []
    for a, kind in ins:
        if kind == "row":
            in_specs.append(pl.BlockSpec((tm, a.shape[1]), lambda i: (i, 0)))
        elif kind == "row_head":
            nh = a.shape[0] // tm
            in_specs.append(pl.BlockSpec((tm, a.shape[1]), lambda i, nh=nh: (jnp.minimum(i, nh - 1), 0)))
        elif kind == "row_tail":
            nh = n_tiles - a.shape[0] // tm
            in_specs.append(pl.BlockSpec((tm, a.shape[1]), lambda i, nh=nh: (jnp.maximum(i - nh, 0), 0)))
        elif kind == "any":
            in_specs.append(pl.BlockSpec(memory_space=pl.ANY))
        elif kind == "smem":
            in_specs.append(pl.BlockSpec(memory_space=pltpu.SMEM))
        elif kind == "smem_rows":
            in_specs.append(pl.BlockSpec((a.shape[0] // n_tiles,), lambda i: (i,), memory_space=pltpu.SMEM))
        elif kind == "smem_rows_next":
            in_specs.append(pl.BlockSpec((a.shape[0] // n_tiles,), lambda i: (jnp.minimum(i + 1, n_tiles - 1),),
                                         memory_space=pltpu.SMEM))
        elif kind == "full":
            in_specs.append(pl.BlockSpec(a.shape, lambda i, nd=a.ndim: (0,) * nd))
        elif kind == "group":
            in_specs.append(pl.BlockSpec((1,) + a.shape[1:], lambda i: (i // nb, 0, 0)))
        elif kind == "pos":
            in_specs.append(pl.BlockSpec((1, tm, a.shape[2]), lambda i: (jnp.minimum(i // nb, 1), i % nb, 0)))
        else:
            raise ValueError(kind)
    out_specs, out_shape = [], []
    for o in outs:
        if len(o) == 4 and o[3] == "any":
            out_specs.append(pl.BlockSpec(memory_space=pl.ANY))
            out_shape.append(jax.ShapeDtypeStruct((o[0], o[1]), o[2]))
        elif len(o) == 4 and o[3] == "heads":
            out_specs.append(pl.BlockSpec((o[0], tm, o[1]), lambda i: (0, i, 0)))
            out_shape.append(jax.ShapeDtypeStruct((o[0], m_rows, o[1]), o[2]))
        elif len(o) == 4 and o[3] == "tail":
            nh = n_tiles - o[0] // tm
            out_specs.append(pl.BlockSpec((tm, o[1]), lambda i, nh=nh: (jnp.maximum(i - nh, 0), 0)))
            out_shape.append(jax.ShapeDtypeStruct((o[0], o[1]), o[2]))
        elif len(o) == 4 and o[3] == "ctx":
            nkeep = o[0] // tm
            out_specs.append(pl.BlockSpec((tm, o[1]), lambda i, nkeep=nkeep: (jnp.minimum(i, nkeep - 1), 0)))
            out_shape.append(jax.ShapeDtypeStruct((o[0], o[1]), o[2]))
        else:
            out_specs.append(pl.BlockSpec((tm, o[0]), lambda i: (i, 0)))
            out_shape.append(jax.ShapeDtypeStruct((m_rows, o[0]), o[1]))
    return pl.pallas_call(
        body, grid=(n_tiles,), in_specs=in_specs, out_specs=out_specs, out_shape=out_shape,
        scratch_shapes=list(scratch), compiler_params=params or _params(1), name=name)(*[a for a, _ in ins])


def _modulate(x_ref, mod_ref, shift_row, scale_row):
    m = mod_ref[0]
    return x_ref[...] * (1.0 + m[scale_row:scale_row + 1]) + m[shift_row:shift_row + 1]


def modulation_all(cvec, mod_w, mod_b):
    depth, d, d6 = mod_w.shape
    g = cvec.shape[0]
    gp = -(-g // 16) * 16
    cp = jnp.zeros((gp, d), F32).at[:g].set(cvec)
    tn = 1024

    def body(c_ref, w_ref, b_ref, o_ref):
        c = c_ref[...]
        a = (c * jax.nn.sigmoid(c)).astype(BF16)
        o_ref[0] = _dot(a, w_ref[0].astype(BF16)) + b_ref[0]

    out = pl.pallas_call(
        body, grid=(depth, d6 // tn),
        in_specs=[pl.BlockSpec((gp, d), lambda l, j: (0, 0)),
                  pl.BlockSpec((1, d, tn), lambda l, j: (l, 0, j)),
                  pl.BlockSpec((1, 1, tn), lambda l, j: (l, 0, j))],
        out_specs=pl.BlockSpec((1, gp, tn), lambda l, j: (l, 0, j)),
        out_shape=jax.ShapeDtypeStruct((depth, gp, d6), F32),
        compiler_params=_params(2), name="modulation")(cp, mod_w, mod_b.reshape(depth, 1, d6))
    return out[:, :g].reshape(depth, g, 6, d)


def attention(q, kl, vl, *, n_seq, lq, row_off, n_heads, dq, dv, hpb, kvpb, tq, win,
              start_fn=None, kc=None, vc=None, bias=None, type_fn=None, sink=None, name=None):
    nq = lq // tq
    koff = row_off // lq
    rep = hpb // kvpb
    has_ctx, has_bias, has_sink = kc is not None, bias is not None, sink is not None
    bias_heads = has_bias and bias.shape[0] > 1
    off_blk = row_off // tq

    def kern(*refs):
        it = iter(refs)
        q_ref, kl_ref, vl_ref = next(it), next(it), next(it)
        kc_ref = next(it) if has_ctx else None
        vc_ref = next(it) if has_ctx else None
        b_ref = next(it) if has_bias else None
        s_ref = next(it) if has_sink else None
        o_ref = next(it)
        i = pl.program_id(2)
        hb = pl.program_id(1)
        if start_fn is None:
            start = 0
        else:
            start = pl.multiple_of(start_fn(i), 64)
        nt = (((1,), (1,)), ((), ()))
        scores = []
        for j in range(hpb):
            kv = j // rep
            qj = q_ref[:, j * dq:(j + 1) * dq]
            s = lax.dot_general(qj, kl_ref[kv, pl.ds(start, win), :], nt, preferred_element_type=F32)
            if has_bias:
                s = s + b_ref[j if bias_heads else 0, 0]
            m = jnp.max(s, axis=1, keepdims=True)
            sc = None
            if has_ctx:
                sc = lax.dot_general(qj, kc_ref[kv], nt, preferred_element_type=F32)
                m = jnp.maximum(m, jnp.max(sc, axis=1, keepdims=True))
            scores.append((s, sc, m))
        outs = []
        for j in range(hpb):
            kv = j // rep
            s, sc, m = scores[j]
            if has_sink:
                sk = s_ref[hb * hpb + j]
                m = jnp.maximum(m, sk)
            acc = _dot(jnp.exp2((s - m).astype(BF16)), vl_ref[kv, pl.ds(start, win), :])
            if has_ctx:
                acc = acc + _dot(jnp.exp2((sc - m).astype(BF16)), vc_ref[kv])
            l = acc[:, dv:dv + 1]
            if has_sink:
                l = l + jnp.exp2(sk - m)
            outs.append(acc[:, :dv] / l)
        o_ref[...] = jnp.concatenate(outs, axis=1).astype(o_ref.dtype)

    ins = [q, kl, vl]
    in_specs = [pl.BlockSpec((tq, hpb * dq), lambda b, h, i: (off_blk + b * nq + i, h)),
                pl.BlockSpec((kvpb, lq, dq), lambda b, h, i: (h, koff + b, 0)),
                pl.BlockSpec((kvpb, lq, V_LANES), lambda b, h, i: (h, koff + b, 0))]
    if has_ctx:
        lc = kc.shape[1] // n_seq
        ins += [kc, vc]
        in_specs += [pl.BlockSpec((kvpb, lc, dq), lambda b, h, i: (h, b, 0)),
                     pl.BlockSpec((kvpb, lc, V_LANES), lambda b, h, i: (h, b, 0))]
    if has_bias:
        ins.append(bias * LOG2E)
        hb_blk = hpb if bias_heads else 1
        in_specs.append(pl.BlockSpec((hb_blk, 1, tq, win),
                                     lambda b, h, i: (h if bias_heads else 0, type_fn(i), 0, 0)))
    if has_sink:
        ins.append(sink.astype(F32) * LOG2E)
        in_specs.append(pl.BlockSpec(memory_space=pltpu.SMEM))
    return pl.pallas_call(
        kern, grid=(n_seq, n_heads // hpb, nq), in_specs=in_specs,
        out_specs=pl.BlockSpec((tq, hpb * dv), lambda b, h, i: (b * nq + i, h)),
        out_shape=jax.ShapeDtypeStruct((n_seq * lq, n_heads * dv), BF16),
        compiler_params=_params(3), name=name)(*ins)


def _cache_heads_major(c, values=False):
    n_seq, length, n_heads, d = c.shape
    out = c.astype(BF16).transpose(2, 0, 1, 3).reshape(n_heads, n_seq * length, d)
    if values:
        rows = n_seq * length
        out = jnp.concatenate([out, jnp.ones((n_heads, rows, 1), BF16),
                               jnp.zeros((n_heads, rows, V_LANES - d - 1), BF16)], axis=-1)
    return out


def _store_value_heads(ref, val, width):
    rows = val.shape[0]
    lane = lax.broadcasted_iota(jnp.int32, (rows, V_LANES - width), 1)
    tail = jnp.where(lane == 0, 1.0, 0.0).astype(ref.dtype)
    for hh in range(ref.shape[0]):
        ref[hh] = jnp.concatenate([val[:, hh * width:(hh + 1) * width].astype(ref.dtype), tail], axis=1)


def _store_mla_keys(ref, k_nope, k_rope):
    rows = k_nope.shape[0]
    tail = jnp.concatenate([k_rope, jnp.zeros((rows, B_QPAD - B_NOPE - B_ROPE), k_rope.dtype)], axis=1)
    for hh in range(ref.shape[0]):
        ref[hh] = jnp.concatenate([k_nope[:, hh * B_NOPE:(hh + 1) * B_NOPE], tail], axis=1).astype(ref.dtype)


def _store_heads(ref, val, width):
    for hh in range(ref.shape[0]):
        ref[hh] = val[:, hh * width:(hh + 1) * width].astype(ref.dtype)


def _rope_tables(length, dim, lead, tail, reps):
    half = dim // 2
    nf = half // 2
    t = jnp.arange(length)
    row = (t // GRID_W).astype(F32)
    col = (t % GRID_W).astype(F32)
    inv = ROPE_BASE ** (-jnp.arange(nf, dtype=F32) / nf)
    ang = jnp.concatenate([row[:, None] * inv, col[:, None] * inv], axis=-1)
    cos, sin = jnp.cos(ang), jnp.sin(ang)
    c = jnp.concatenate([jnp.ones((length, lead), F32), cos, cos, jnp.ones((length, tail), F32)] * reps, axis=1)
    s = jnp.concatenate([jnp.zeros((length, lead), F32), sin, sin, jnp.zeros((length, tail), F32)] * reps, axis=1)
    return (jnp.stack([jnp.ones_like(c), c]), jnp.stack([jnp.zeros_like(s), s]))


def _rot_cols(w, dim):
    k, n = w.shape
    wb = w.reshape(k, n // dim, dim)
    half = dim // 2
    return jnp.concatenate([-wb[..., half:], wb[..., :half]], axis=-1).reshape(k, n)


def _cs(phase, n, sign):
    ang = (2.0 * np.pi / n) * (phase % n).astype(F32)
    return jnp.cos(ang), sign * jnp.sin(ang)


def _cblock(wr, wi):
    return jnp.concatenate([jnp.concatenate([wr, -wi], axis=-1), jnp.concatenate([wi, wr], axis=-1)], axis=-2)


SUB = 8


def slow_stage(a, x, *, epi=None, name=None):
    p_n, k, s_n, d = x.shape
    m = a.shape[0]
    x5 = x.reshape(p_n, k, s_n // SUB, SUB, d)

    def kern(*refs):
        if epi is None:
            a_ref, x_ref, o_ref = refs
        else:
            a_ref, x_ref, g_ref, z_ref, s_ref, o_ref = refs
        xt = pltpu.einshape("ksd->skd", x_ref[...])
        av = a_ref[...]
        y = jnp.stack([_dot3(av, xt[s]) for s in range(SUB)], axis=0)
        y = pltpu.einshape("smd->msd", y)
        if epi is not None:
            y = g_ref[...] * (y + s_ref[...][None] * z_ref[...])
        o_ref[...] = y

    blk = lambda rows: pl.BlockSpec((None, rows, None, SUB, d), lambda p, t: (p, 0, t, 0, 0))
    ins = [a, x5]
    in_specs = [pl.BlockSpec((m, k), lambda p, t: (0, 0)), blk(k)]
    if epi is not None:
        gate, z, skip = epi
        ins += [gate.reshape(p_n, m, s_n // SUB, SUB, d), z.reshape(p_n, m, s_n // SUB, SUB, d), skip]
        in_specs += [blk(m), blk(m), pl.BlockSpec((1, d), lambda p, t: (0, 0))]
    out = pl.pallas_call(
        kern, grid=(p_n, s_n // SUB), in_specs=in_specs, out_specs=blk(m),
        out_shape=jax.ShapeDtypeStruct((p_n, m, s_n // SUB, SUB, d), F32),
        compiler_params=_params(2), name=name)(*ins)
    return out.reshape(p_n, m, s_n, d)


def spectral(x, mf, g=None, mi=None, *, dt, g_index=0, epi=None, name=None):
    p_n, planes, k1_n, nin, d = x.shape
    nf = mf.shape[1] // 2
    nout = nf if g is None else mi.shape[1] // 2

    def kern(*refs):
        it = iter(refs)
        x_ref, mf_ref = next(it), next(it)
        g_ref = next(it) if g is not None else None
        mi_ref = next(it) if g is not None else None
        if epi is not None:
            gate_ref, z_ref, skip_ref = next(it), next(it), next(it)
        o_ref = next(it)
        xs = [x_ref[0, pp, 0] for pp in range(planes)]
        xin = xs[0] if planes == 1 else jnp.concatenate(xs, axis=0)
        f = _dot3(mf_ref[0], xin)
        if g is not None:
            fr, fi = f[:nf], f[nf:]
            gr, gi = g_ref[0, 0, 0], g_ref[0, 1, 0]
            y = jnp.concatenate([fr * gr - fi * gi, fr * gi + fi * gr], axis=0)
            f = _dot3(mi_ref[0], y)
        for pp in range(2):
            y = f[pp * nout:(pp + 1) * nout]
            if epi is not None:
                y = gate_ref[0, pp, 0] * (y + skip_ref[...] * z_ref[0, pp, 0])
            o_ref[0, pp, 0] = y

    ins = [x, mf]
    in_specs = [pl.BlockSpec((1, planes, 1, nin, dt), lambda k, j, p: (p, 0, k, 0, j)),
                pl.BlockSpec((1,) + mf.shape[1:], lambda k, j, p: (k, 0, 0))]
    if g is not None:
        ins += [g, mi]
        in_specs += [pl.BlockSpec((1, 2, 1, nf, dt), lambda k, j, p: (g_index, 0, k, 0, j)),
                     pl.BlockSpec((1,) + mi.shape[1:], lambda k, j, p: (k, 0, 0))]
    if epi is not None:
        ins += list(epi)
        in_specs += [pl.BlockSpec((1, 2, 1, nout, dt), lambda k, j, p: (p, 0, k, 0, j)),
                     pl.BlockSpec((1, 2, 1, nout, dt), lambda k, j, p: (p, 0, k, 0, j)),
                     pl.BlockSpec((1, dt), lambda k, j, p: (0, j))]
    return pl.pallas_call(
        kern, grid=(k1_n, d // dt, p_n), in_specs=in_specs,
        out_specs=pl.BlockSpec((1, 2, 1, nout, dt), lambda k, j, p: (p, 0, k, 0, j)),
        out_shape=jax.ShapeDtypeStruct((p_n, 2, k1_n, nout, d), F32),
        compiler_params=_params(3), name=name)(*ins)


def hyena_two_sided_filters(length, w1, b1, w2, b2, w3, freq, log_decay, name):
    hid = w2.shape[0]
    d = w3.shape[1] // 4
    t = jnp.linspace(0.0, 1.0, length, dtype=F32)[:, None]
    ang = 2.0 * jnp.pi * t * jnp.arange(1, C_POS_BANDS + 1, dtype=F32)
    z = jnp.concatenate([t, jnp.cos(ang), jnp.sin(ang)], axis=-1)
    kpad = LANES - z.shape[1]
    z = jnp.pad(z, ((0, 0), (0, kpad)))
    z2 = jnp.concatenate([z, z[::-1]], axis=0)
    w1p = jnp.pad(w1, ((0, kpad), (0, 0)))
    full = lambda a: pl.BlockSpec(a.shape, lambda *_: (0,) * a.ndim)

    def ffn_kern(z_ref, w1_ref, b1_ref, w2_ref, b2_ref, f_ref, a_ref):
        fr = f_ref[...]
        a = jnp.sin(fr * (_dot3(z_ref[...], w1_ref[...]) + b1_ref[...]))
        a_ref[...] = jnp.sin(fr * (_dot3(a, w2_ref[...]) + b2_ref[...]))

    args = [z2, w1p, b1.reshape(1, hid), w2, b2.reshape(1, hid), freq.reshape(1, hid)]
    act = pl.pallas_call(
        ffn_kern, grid=(1,), in_specs=[full(a) for a in args], out_specs=pl.BlockSpec((2 * length, hid), lambda i: (0, 0)),
        out_shape=jax.ShapeDtypeStruct((2 * length, hid), F32), compiler_params=_params(1), name=name + "_ffn")(*args)

    tn = 256
    per = d // tn

    def kern(af_ref, ar_ref, w3f_ref, w3b_ref, ldf_ref, ldb_ref, o_ref):
        row = lax.broadcasted_iota(jnp.int32, (length, tn), 0)
        pos = row.astype(F32) * (1.0 / (length - 1))
        pos_rev = (length - 1 - row).astype(F32) * (1.0 / (length - 1))

        def filt(a_ref, w_ref, ld_ref, tt):
            f = _dot3(a_ref[...], w_ref[...]) * jnp.exp(-jnp.exp(ld_ref[...]) * tt)
            return f / (jnp.sum(jnp.abs(f), axis=0, keepdims=True) + 1e-6)

        hf = filt(af_ref, w3f_ref, ldf_ref, pos)
        hb_rev = filt(ar_ref, w3b_ref, ldb_ref, pos_rev)
        o_ref[0, 0] = hf + jnp.where(row == 0, hb_rev[length - 1:length], 0.0)
        o_ref[0, 1] = jnp.where(row == 0, 0.0, pltpu.roll(hb_rev, 1, 0))

    col = lambda direction: (lambda o, j: (0, (2 * o + direction) * per + j))
    ld = log_decay.reshape(1, 4 * d)
    g = pl.pallas_call(
        kern, grid=(2, per),
        in_specs=[pl.BlockSpec((length, hid), lambda o, j: (0, 0)), pl.BlockSpec((length, hid), lambda o, j: (1, 0)),
                  pl.BlockSpec((hid, tn), col(0)), pl.BlockSpec((hid, tn), col(1)),
                  pl.BlockSpec((1, tn), col(0)), pl.BlockSpec((1, tn), col(1))],
        out_specs=pl.BlockSpec((1, 2, length, tn), lambda o, j: (o, 0, 0, j)),
        out_shape=jax.ShapeDtypeStruct((2, 2, length, d), F32),
        compiler_params=_params(2), name=name)(act, act, w3, w3, ld, ld)
    return g.reshape(2, 2 * length, d)


def short_conv(u, w, b, *, row_off, n_seq, length, parts, name):
    c = u.shape[1]
    dt = 256
    per = c // parts // dt
    first = row_off // length

    def kern(u_ref, w_ref, b_ref, *o_refs):
        x = u_ref[...]
        r = lax.broadcasted_iota(jnp.int32, x.shape, 0)
        prev = jnp.where(r == 0, 0.0, pltpu.roll(x, 1, 0))
        nxt = jnp.where(r == length - 1, 0.0, pltpu.roll(x, length - 1, 0))
        wv = w_ref[...]
        y = prev * wv[0:1] + x * wv[1:2] + nxt * wv[2:3] + b_ref[...]
        part = pl.program_id(1) // per
        for k, o_ref in enumerate(o_refs):
            @pl.when(part == k)
            def _(o_ref=o_ref):
                o_ref[0] = y

    out_spec = lambda k: pl.BlockSpec((1, length, dt), lambda s, j: (s, 0, jnp.clip(j - k * per, 0, per - 1)))
    return pl.pallas_call(
        kern, grid=(n_seq, c // dt),
        in_specs=[pl.BlockSpec((length, dt), lambda s, j: (first + s, j)),
                  pl.BlockSpec((3, dt), lambda s, j: (0, j)),
                  pl.BlockSpec((1, dt), lambda s, j: (0, j))],
        out_specs=[out_spec(k) for k in range(parts)],
        out_shape=[jax.ShapeDtypeStruct((n_seq, length, c // parts), F32)] * parts,
        compiler_params=_params(2), name=name)(u, w, b)


def _dft_mats_single(n, nin_data, nout):
    k = jnp.arange(n, dtype=jnp.int32)
    fr, fi = _cs(k[:, None] * k[None, :nin_data], n, -1.0)
    mf_data = _cblock(fr, fi)[None]
    gr, gi = _cs(k[:, None] * k[None, :], n, -1.0)
    mf_filt = jnp.concatenate([gr, gi], axis=0)[None]
    ir, ii = _cs(k[:nout, None] * k[None, :], n, 1.0)
    mi = _cblock(ir / n, ii / n)[None]
    return mf_data, mf_filt, mi


def _dft_mats_two_stage(n, n2):
    n1 = n // n2
    k1 = jnp.arange(n1, dtype=jnp.int32)
    t1h = jnp.arange(n1 // 2, dtype=jnp.int32)
    ar, ai = _cs((n // n1) * k1[:, None] * t1h[None, :], n, -1.0)
    a_data = _cblock(ar, ai)
    fr, fi = _cs((n // n1) * k1[:, None] * k1[None, :], n, -1.0)
    a_filt = jnp.concatenate([fr, fi], axis=0)
    br, bi = _cs((n // n1) * t1h[:, None] * k1[None, :], n, 1.0)
    a_inv = _cblock(br, bi)
    t2 = jnp.arange(n2, dtype=jnp.int32)
    kk = k1[:, None, None] + n1 * t2[None, :, None]
    mr, mi_ = _cs(kk * t2[None, None, :], n, -1.0)
    mf = _cblock(mr, mi_)
    vr, vi = _cs(jnp.swapaxes(kk, 1, 2) * t2[None, :, None], n, 1.0)
    mi = _cblock(vr / n, vi / n)
    return a_data, a_filt, a_inv, mf, mi


def _route(logits):
    lane = lax.broadcasted_iota(jnp.int32, logits.shape, 1).astype(F32)
    big = 1e9
    lg = jnp.where(lane < N_GROUPS, logits, -jnp.inf)
    mg = jnp.max(lg, axis=1, keepdims=True)
    gi = jnp.min(jnp.where(lg == mg, lane, big), axis=1, keepdims=True)
    p_g = 1.0 / jnp.sum(jnp.exp(lg - mg), axis=1, keepdims=True)
    lo = N_GROUPS + EXPERTS_PER_GROUP * gi
    le = jnp.where((lane >= lo) & (lane < lo + EXPERTS_PER_GROUP), logits, -jnp.inf)
    m1 = jnp.max(le, axis=1, keepdims=True)
    i1 = jnp.min(jnp.where(le == m1, lane, big), axis=1, keepdims=True)
    le2 = jnp.where(lane == i1, -jnp.inf, le)
    m2 = jnp.max(le2, axis=1, keepdims=True)
    i2 = jnp.min(jnp.where(le2 == m2, lane, big), axis=1, keepdims=True)
    e2 = jnp.exp(m2 - m1)
    w1 = p_g / (1.0 + e2)
    w2 = p_g * e2 / (1.0 + e2)
    return lane, i1 - N_GROUPS, i2 - N_GROUPS, w1, w2


def _rank_in_tile(lane, e1, e2):
    tm = lane.shape[0]
    picks = jnp.where(lane == e1, 1.0, 0.0) + jnp.where(lane == e2, 1.0, 0.0)
    r = lax.broadcasted_iota(jnp.int32, (tm, tm), 0)
    c = lax.broadcasted_iota(jnp.int32, (tm, tm), 1)
    earlier = jnp.where(c < r, 1.0, 0.0).astype(BF16)
    return _dot(earlier, picks.astype(BF16)), picks


def _pack_bf16_pairs(y):
    c = y.shape[1] // 2
    hi = lax.bitcast_convert_type(y[:, :c].astype(BF16).astype(F32), jnp.uint32)
    lo = lax.bitcast_convert_type(y[:, c:].astype(BF16).astype(F32), jnp.uint32)
    return hi | (lo >> 16)


def _unpack_bf16_pairs(w):
    hi = lax.bitcast_convert_type(w & jnp.uint32(0xFFFF0000), F32)
    lo = lax.bitcast_convert_type(w << 16, F32)
    return hi, lo


def expert_mlp(xs, blk_e, n_used, w_gate, w_up, w_down, layer):
    p_rows, d = xs.shape
    de = w_gate.shape[3]
    nblk = p_rows // MOE_BLK

    def kern(be_ref, nu_ref, x_ref, wg_ref, wu_ref, wd_ref, o_ref, wg_s, wu_s, wd_s):
        i = pl.program_id(0)
        used = i < nu_ref[0]
        fresh = jnp.logical_or(i == 0, be_ref[i] != be_ref[jnp.maximum(i - 1, 0)])

        @pl.when(jnp.logical_and(used, fresh))
        def _():
            wg_s[...] = wg_ref[0, 0].astype(BF16)
            wu_s[...] = wu_ref[0, 0].astype(BF16)
            wd_s[...] = wd_ref[0, 0].astype(BF16)

        @pl.when(used)
        def _():
            x = x_ref[...].astype(BF16)
            g = _dot(x, wg_s[...])
            u = _dot(x, wu_s[...])
            a = (g * jax.nn.sigmoid(g) * u).astype(BF16)
            o_ref[...] = _pack_bf16_pairs(_dot(a, wd_s[...]))

        @pl.when(jnp.logical_not(used))
        def _():
            o_ref[...] = jnp.zeros_like(o_ref)

    last = lambda i, nu: jnp.minimum(i, nu[0] - 1)
    grid_spec = pltpu.PrefetchScalarGridSpec(
        num_scalar_prefetch=2, grid=(nblk,),
        in_specs=[pl.BlockSpec((MOE_BLK, d), lambda i, be, nu: (last(i, nu), 0)),
                  pl.BlockSpec((1, 1, d, de), lambda i, be, nu: (layer, be[last(i, nu)], 0, 0)),
                  pl.BlockSpec((1, 1, d, de), lambda i, be, nu: (layer, be[last(i, nu)], 0, 0)),
                  pl.BlockSpec((1, 1, de, d), lambda i, be, nu: (layer, be[last(i, nu)], 0, 0))],
        out_specs=pl.BlockSpec((MOE_BLK, d // 2), lambda i, be, nu: (i, 0)),
        scratch_shapes=[pltpu.VMEM((d, de), BF16), pltpu.VMEM((d, de), BF16), pltpu.VMEM((de, d), BF16)])
    return pl.pallas_call(
        kern, grid_spec=grid_spec, out_shape=jax.ShapeDtypeStruct((p_rows, d // 2), jnp.uint32),
        compiler_params=_params(1), name="expert_mlp")(blk_e, n_used, xs, w_gate, w_up, w_down)


def _slot_plan(experts, ranks):
    n = 2 * experts.shape[0]
    ids = jnp.arange(N_EXPERTS, dtype=jnp.int32)
    onehot = experts[..., None] == ids
    counts = jnp.sum(onehot, axis=(0, 1)).astype(jnp.int32)
    padded = (counts + MOE_BLK - 1) // MOE_BLK * MOE_BLK
    pend = jnp.cumsum(padded)
    pstart = pend - padded
    p_rows = -(-n // MOE_BLK) * MOE_BLK + N_EXPERTS * MOE_BLK
    nblk = p_rows // MOE_BLK
    blk_first = jnp.arange(nblk, dtype=jnp.int32) * MOE_BLK
    blk_e = jnp.minimum(jnp.sum(pend[None, :] <= blk_first[:, None], axis=1), N_EXPERTS - 1).astype(jnp.int32)
    n_used = (pend[-1:] // MOE_BLK).astype(jnp.int32)
    blk_ids = jnp.arange(nblk, dtype=jnp.int32)
    partly = jnp.any((blk_ids[:, None] == (pend // MOE_BLK - 1)[None, :]) & (counts % MOE_BLK != 0)[None, :], axis=1)
    zero_blk = (partly | (blk_ids >= n_used[0])).astype(jnp.int32)
    slot = ranks + jnp.sum(jnp.where(onehot, pstart, 0), axis=-1)
    return slot.reshape(n).astype(jnp.int32), blk_e, n_used, zero_blk, p_rows


def _dma_params():
    return pltpu.CompilerParams(dimension_semantics=("arbitrary",), vmem_limit_bytes=VMEM_LIMIT,
                                disable_bounds_checks=True)


DMA_UNROLL = 8


def moe_dispatch(x1, mod, slot, zero_blk, p_rows, group_len, name):
    t, d = x1.shape
    tm = ROW_TILE

    def body(slot_ref, zb_ref, x_ref, mod_ref, xs_ref, h_ref, zero_ref, sem):
        h_ref[...] = _modulate(x_ref, mod_ref, 3, 4)

        @pl.when(pl.program_id(0) == 0)
        def _():
            zero_ref[...] = jnp.zeros_like(zero_ref)

            def zstart(b, c):
                @pl.when(zb_ref[b] != 0)
                def _():
                    first = pl.multiple_of(b * MOE_BLK, MOE_BLK)
                    pltpu.make_async_copy(zero_ref, xs_ref.at[pl.ds(first, MOE_BLK)], sem).start()
                return c

            def zwait(b, c):
                @pl.when(zb_ref[b] != 0)
                def _():
                    pltpu.make_async_copy(zero_ref, xs_ref.at[pl.ds(0, MOE_BLK)], sem).wait()
                return c

            lax.fori_loop(0, p_rows // MOE_BLK, zstart, 0)
            lax.fori_loop(0, p_rows // MOE_BLK, zwait, 0)

        def start(rr, c):
            for u in range(DMA_UNROLL):
                r = rr * DMA_UNROLL + u
                for k in range(2):
                    pltpu.make_async_copy(h_ref.at[pl.ds(r, 1)], xs_ref.at[pl.ds(slot_ref[2 * r + k], 1)],
                                          sem).start(priority=k)
            return c

        lax.fori_loop(0, tm // DMA_UNROLL, start, 0)
        for k in range(2):
            pltpu.make_async_copy(h_ref, xs_ref.at[pl.ds(0, tm)], sem).wait()

    return row_call(body, t, [(slot, "smem_rows"), (zero_blk, "smem"), (x1, "row"), (mod, "group")],
                    [(p_rows, d, F32, "any")], group_len=group_len,
                    scratch=[pltpu.VMEM((tm, d), F32), pltpu.VMEM((MOE_BLK, d), F32), pltpu.SemaphoreType.DMA(())],
                    params=_dma_params(), name=name)[0]


def kernel(x_prompt, x_sample, cache_a_k, cache_a_v, cache_b_ckv, cache_b_krope, cache_d_k, cache_d_v, c_ctx, c, mod_w, mod_b, ln_g, ln_b, a_wq, a_wk, a_wv, a_wo, a_sink, b_wq_a, b_q_norm, b_wq_b, b_wkv_a, b_kv_norm, b_wk_b, b_wv_b, b_wo, c_w_in, c_b_in, c_conv_w, c_conv_b, c_ffn_w1, c_ffn_b1, c_ffn_w2, c_ffn_b2, c_ffn_w3, c_ffn_freq, c_log_decay, c_skip, c_wo, d_wq, d_wk, d_wv, d_wo, d_rel_bias, moe_wr_g, moe_br_g, moe_wr_e, moe_br_e, moe_w_gate, moe_w_up, moe_w_down):
    bc, lc, d = x_prompt.shape
    bl, ll, _ = x_sample.shape
    past = cache_a_k.shape[2]
    gl = ll
    assert bc * lc == gl and d == A_HEADS * HEAD_DIM
    ng = 1 + bl
    t_all = ng * gl
    x_ctx, x_lat = x_prompt.reshape(gl, d), x_sample.reshape(bl * gl, d)
    cvec = jnp.concatenate([c_ctx[None, :], c], axis=0)
    mods = modulation_all(cvec, mod_w, mod_b)
    rc = functools.partial(row_call, group_len=gl)
    row2 = lambda v: v.reshape(1, -1)
    n_ctx_tiles = gl // ROW_TILE

    def post_mixer(i, att_ctx, att_lat, wo, x_in):
        unused = LANES - N_GROUPS - N_EXPERTS
        wr = jnp.concatenate([moe_wr_g[i], moe_wr_e[i], jnp.zeros((d, unused), F32)], axis=1)
        br = jnp.concatenate([moe_br_g[i], moe_br_e[i], jnp.zeros((unused,), F32)])[None, :]

        x_split = isinstance(x_in, tuple)

        def body(attc_ref, attl_ref, *rest):
            is_ctx = pl.program_id(0) < n_ctx_tiles
            if x_split:
                x_old = jnp.where(is_ctx, rest[0][...], rest[1][...])
                rest = rest[2:]
            else:
                x_old = rest[0][...]
                rest = rest[1:]
            mod_ref, wo_ref, g_ref, b_ref, wr_ref, br_ref, x1_ref, info_ref, w_ref, seen_ref = rest

            @pl.when(pl.program_id(0) == 0)
            def _():
                seen_ref[...] = jnp.zeros_like(seen_ref)

            m = mod_ref[0]
            att = jnp.where(is_ctx, attc_ref[...], attl_ref[...])
            o = _dot(att.astype(BF16), wo_ref[...])
            x1 = _layer_norm(DEEPNORM_ALPHA * x_old + m[2:3] * o, g_ref[...], b_ref[...])
            x1_ref[...] = x1
            h = x1 * (1.0 + m[4:5]) + m[3:4]
            lane, e1, e2, w1, w2 = _route(_dot3(h, wr_ref[...]) + br_ref[...])
            before, picks = _rank_in_tile(lane, e1, e2)
            before = before + seen_ref[...]
            r1 = jnp.sum(jnp.where(lane == e1, before, 0.0), axis=1, keepdims=True)
            r2 = jnp.sum(jnp.where(lane == e2, before, 0.0), axis=1, keepdims=True)
            seen_ref[...] += jnp.sum(picks, axis=0, keepdims=True)
            info = jnp.where(lane == 0, e1, jnp.where(lane == 1, e2, jnp.where(lane == 2, r1, jnp.where(lane == 3, r2, 0.0))))
            info_ref[...] = info.astype(jnp.int32)
            w_ref[...] = jnp.where(lane == 0, w1, jnp.where(lane == 1, w2, 0.0))

        return rc(body, t_all,
                  [(att_ctx, "row_head"), (att_lat, "row_tail")]
                  + ([(x_in[0], "row_head"), (x_in[1], "row_tail")] if x_split else [(x_in, "row")])
                  + [(mods[i], "group"), (wo.astype(BF16), "full"),
                   (row2(ln_g[i, 0]), "full"), (row2(ln_b[i, 0]), "full"), (wr, "full"), (br, "full")],
                  [(d, F32), (LANES, jnp.int32), (LANES, F32)],
                  scratch=[pltpu.VMEM((1, LANES), F32)], name=f"post_mixer{i}")

    def moe(i, x1, info, w):
        slot, blk_e, n_used, zero_blk, p_rows = _slot_plan(info[:, 0:2], info[:, 2:4])
        xs = moe_dispatch(x1, mods[i], slot, zero_blk, p_rows, gl, f"moe_dispatch{i}")
        ys = expert_mlp(xs, blk_e, n_used, moe_w_gate, moe_w_up, moe_w_down, i)
        tm = ROW_TILE

        split_out = i == DEPTH - 1

        def body(slot_ref, slot_next_ref, x_ref, w_ref, mod_ref, g_ref, b_ref, ys_ref, *rest):
            out_refs, (buf, sem) = rest[:-2], rest[-2:]
            t = pl.program_id(0)
            par = t % 2

            def request(s_ref, p):
                def start(rr, c):
                    for u in range(DMA_UNROLL):
                        r = rr * DMA_UNROLL + u
                        for k in range(2):
                            pltpu.make_async_copy(ys_ref.at[pl.ds(s_ref[2 * r + k], 1)], buf.at[p, k, pl.ds(r, 1)],
                                                  sem.at[p]).start(priority=k)
                    return c

                lax.fori_loop(0, tm // DMA_UNROLL, start, 0)

            @pl.when(t == 0)
            def _():
                request(slot_ref, 0)

            @pl.when(t + 1 < pl.num_programs(0))
            def _():
                request(slot_next_ref, 1 - par)

            for k in range(2):
                pltpu.make_async_copy(ys_ref.at[pl.ds(0, tm)], buf.at[par, k], sem.at[par]).wait()
            m = mod_ref[0]
            wv = w_ref[...]
            a_hi, a_lo = _unpack_bf16_pairs(buf[par, 0])
            b_hi, b_lo = _unpack_bf16_pairs(buf[par, 1])
            w0, w1 = wv[:, 0:1], wv[:, 1:2]
            y = jnp.concatenate([w0 * a_hi + w1 * b_hi, w0 * a_lo + w1 * b_lo], axis=1)
            x2 = _layer_norm(DEEPNORM_ALPHA * x_ref[...] + m[5:6] * y, g_ref[...], b_ref[...])
            if not split_out:
                out_refs[0][...] = x2
            else:
                is_ctx = pl.program_id(0) < n_ctx_tiles

                @pl.when(is_ctx)
                def _():
                    out_refs[0][...] = x2

                @pl.when(jnp.logical_not(is_ctx))
                def _():
                    out_refs[1][...] = x2

        outs = [(gl, d, F32, "ctx"), (t_all - gl, d, F32, "tail")] if split_out else [(d, F32)]
        res = rc(body, t_all,
                 [(slot, "smem_rows"), (slot, "smem_rows_next"), (x1, "row"), (w, "row"), (mods[i], "group"),
                  (row2(ln_g[i, 1]), "full"), (row2(ln_b[i, 1]), "full"), (ys, "any")],
                 outs, scratch=[pltpu.VMEM((2, 2, tm, d // 2), jnp.uint32), pltpu.SemaphoreType.DMA((2,))],
                 params=_dma_params(), name=f"moe_combine{i}")
        return res if split_out else res[0]

    def finish_layer(i, att_ctx, att_lat, wo, x_in):
        x1, info, w = post_mixer(i, att_ctx, att_lat, wo, x_in)
        return moe(i, x1, info, w)

    tq = 256
    i = 0
    hd = HEAD_DIM
    kvw = A_KV_HEADS * hd
    cq, sq = _rope_tables(gl, hd, 0, 0, LANES // hd)
    wq, wk, wv = a_wq[0], a_wk[0], a_wv[0]

    def body_a(xc_ref, xl_ref, mod_ref, wq_ref, wqr_ref, wk_ref, wkr_ref, wv_ref, c_ref, s_ref, q_ref, kh_ref, vh_ref, kc_ref, vc_ref):
        x_in = jnp.where(pl.program_id(0) < n_ctx_tiles, xc_ref[...], xl_ref[...])
        m = mod_ref[0]
        h = (x_in * (1.0 + m[1:2]) + m[0:1]).astype(BF16)
        cc, ss = c_ref[0], s_ref[0]
        q = _dot(h, wq_ref[...]) * _tile_lanes(cc, d) + _dot(h, wqr_ref[...]) * _tile_lanes(ss, d)
        k = _dot(h, wk_ref[...]) * _tile_lanes(cc, kvw) + _dot(h, wkr_ref[...]) * _tile_lanes(ss, kvw)
        v = _dot(h, wv_ref[...])
        q_ref[...] = (q * QSCALE_64).astype(BF16)
        _store_heads(kh_ref, k, hd)
        _store_value_heads(vh_ref, v, hd)

        @pl.when(pl.program_id(0) < n_ctx_tiles)
        def _():
            kc_ref[...] = k
            vc_ref[...] = v

    q, kh, vh, k_new, v_new = rc(
        body_a, t_all,
        [(x_ctx, "row_head"), (x_lat, "row_tail"), (mods[i], "group"), (wq.astype(BF16), "full"),
         (_rot_cols(wq, hd).astype(BF16), "full"),
         (wk.astype(BF16), "full"), (_rot_cols(wk, hd).astype(BF16), "full"), (wv.astype(BF16), "full"),
         (cq, "pos"), (sq, "pos")],
        [(d, BF16), (A_KV_HEADS, hd, BF16, "heads"), (A_KV_HEADS, V_LANES, BF16, "heads"),
         (gl, kvw, F32, "ctx"), (gl, kvw, F32, "ctx")], name="proj_a")
    out_a_k = k_new.reshape(bc, 1, lc, A_KV_HEADS, hd)
    out_a_v = v_new.reshape(bc, 1, lc, A_KV_HEADS, hd)
    common = dict(n_heads=A_HEADS, dq=hd, dv=hd, hpb=8, kvpb=2, tq=tq, sink=a_sink[0])
    att_c = attention(q, kh, vh, n_seq=bc, lq=lc, row_off=0, win=lc, name="attn_a_ctx", **common)
    win_a = 2 * tq
    nq_l = gl // tq
    qi = jnp.arange(tq)[:, None]
    ki = jnp.arange(win_a)[None, :]
    band = jnp.stack([jnp.where(jnp.abs(ki - (qi + off)) <= A_WINDOW, 0.0, NEG_INF)
                      for off in (0, A_WINDOW, 2 * A_WINDOW)]).astype(F32)[None]
    att_l = attention(q, kh, vh, n_seq=bl, lq=gl, row_off=gl, win=win_a,
                      start_fn=lambda ii: jnp.clip(ii * tq - A_WINDOW, 0, gl - win_a),
                      kc=_cache_heads_major(cache_a_k[:, 0]), vc=_cache_heads_major(cache_a_v[:, 0], values=True),
                      bias=band, type_fn=lambda ii: jnp.where(ii == 0, 0, jnp.where(ii == nq_l - 1, 2, 1)),
                      name="attn_a_lat", **common)
    x = finish_layer(i, att_c, att_l, a_wo[0], (x_ctx, x_lat))

    i = 1
    hq = B_NOPE + B_ROPE
    qw = B_HEADS * B_QPAD
    wqb = b_wq_b[0].reshape(B_Q_RANK, B_HEADS, hq)
    wqb_rot = jnp.concatenate([jnp.zeros_like(wqb[..., :B_NOPE]),
                               _rot_cols(wqb[..., B_NOPE:].reshape(B_Q_RANK, -1), B_ROPE).reshape(B_Q_RANK, B_HEADS, B_ROPE)],
                              axis=-1)
    padq = lambda wz: jnp.pad(wz, ((0, 0), (0, 0), (0, B_QPAD - hq))).reshape(B_Q_RANK, qw).astype(BF16)
    wkv_c, wkv_r = b_wkv_a[0][:, :B_KV_RANK], b_wkv_a[0][:, B_KV_RANK:]
    cqb, sqb = _rope_tables(gl, B_ROPE, B_NOPE, B_QPAD - hq, 1)
    ckr, skr = _rope_tables(gl, B_ROPE, 0, 0, 1)

    def body_b(x_ref, mod_ref, wqa_ref, qn_ref, wqb_ref, wqbr_ref, wc_ref, kn_ref, wr_ref, wrr_ref, wkb_ref, wvb_ref,
               cq_ref, sq_ref, ck_ref, sk_ref, q_ref, kh_ref, vh_ref, ckv_ref, kr_ref):
        h = _modulate(x_ref, mod_ref, 0, 1).astype(BF16)
        qa = _rms(_dot(h, wqa_ref[...]), qn_ref[...]).astype(BF16)
        q = (_dot(qa, wqb_ref[...]) * _tile_lanes(cq_ref[0], qw) + _dot(qa, wqbr_ref[...]) * _tile_lanes(sq_ref[0], qw))
        q_ref[...] = (q * QSCALE_MLA).astype(BF16)
        ckv = _rms(_dot(h, wc_ref[...]), kn_ref[...])
        kr = _dot(h, wr_ref[...]) * ck_ref[0] + _dot(h, wrr_ref[...]) * sk_ref[0]
        cb = ckv.astype(BF16)
        _store_mla_keys(kh_ref, _dot(cb, wkb_ref[...]), kr)
        _store_value_heads(vh_ref, _dot(cb, wvb_ref[...]), B_VDIM)

        @pl.when(pl.program_id(0) < n_ctx_tiles)
        def _():
            ckv_ref[...] = ckv
            kr_ref[...] = kr

    wkb, wvb = b_wk_b[0].astype(BF16), b_wv_b[0].astype(BF16)
    q, kh, vh, ckv_new, kr_new = rc(
        body_b, t_all,
        [(x, "row"), (mods[i], "group"), (b_wq_a[0].astype(BF16), "full"), (row2(b_q_norm[0]), "full"),
         (padq(wqb), "full"), (padq(wqb_rot), "full"), (wkv_c.astype(BF16), "full"), (row2(b_kv_norm[0]), "full"),
         (wkv_r.astype(BF16), "full"), (_rot_cols(wkv_r, B_ROPE).astype(BF16), "full"), (wkb, "full"), (wvb, "full"),
         (cqb, "pos"), (sqb, "pos"), (ckr, "pos"), (skr, "pos")],
        [(qw, BF16), (B_HEADS, B_QPAD, BF16, "heads"), (B_HEADS, V_LANES, BF16, "heads"),
         (gl, B_KV_RANK, F32, "ctx"), (gl, B_ROPE, F32, "ctx")], name="proj_b")
    out_b_ckv = ckv_new.reshape(bc, 1, lc, B_KV_RANK)
    out_b_krope = kr_new.reshape(bc, 1, lc, B_ROPE)

    def body_bc(c_ref, r_ref, wkb_ref, wvb_ref, kh_ref, vh_ref):
        cb = c_ref[...].astype(BF16)
        _store_mla_keys(kh_ref, _dot(cb, wkb_ref[...]), r_ref[...])
        _store_value_heads(vh_ref, _dot(cb, wvb_ref[...]), B_VDIM)

    n_pc = bl * past
    kh_p, vh_p = row_call(body_bc, n_pc,
                          [(cache_b_ckv[:, 0].reshape(n_pc, B_KV_RANK), "row"), (cache_b_krope[:, 0].reshape(n_pc, B_ROPE), "row"),
                           (wkb, "full"), (wvb, "full")],
                          [(B_HEADS, B_QPAD, BF16, "heads"), (B_HEADS, V_LANES, BF16, "heads")],
                          tm=min(ROW_TILE, n_pc), name="proj_b_past")
    common = dict(n_heads=B_HEADS, dq=B_QPAD, dv=B_VDIM, hpb=2, kvpb=2)
    att_c = attention(q, kh, vh, n_seq=bc, lq=lc, row_off=0, win=lc, tq=tq, name="attn_b_ctx", **common)
    att_l = attention(q, kh, vh, n_seq=bl, lq=gl, row_off=gl, win=gl, kc=kh_p, vc=vh_p, tq=min(2 * tq, gl),
                      name="attn_b_lat", **common)
    x = finish_layer(i, att_c, att_l, b_wo[0], x)

    i = 2
    d3 = 3 * d

    def body_c(x_ref, mod_ref, w_ref, b_ref, u_ref):
        h = _modulate(x_ref, mod_ref, 0, 1).astype(BF16)
        for j in range(3):
            u_ref[:, j * d:(j + 1) * d] = _dot(h, w_ref[:, j * d:(j + 1) * d]) + b_ref[:, j * d:(j + 1) * d]

    (u,) = rc(body_c, t_all, [(x, "row"), (mods[i], "group"), (c_w_in[0].astype(BF16), "full"), (row2(c_b_in[0]), "full")],
              [(d3, F32)], name="proj_c")
    u_c = short_conv(u, c_conv_w[0], row2(c_conv_b[0]), row_off=0, n_seq=bc, length=lc, parts=3, name="short_conv_ctx")
    u_l = short_conv(u, c_conv_w[0], row2(c_conv_b[0]), row_off=gl, n_seq=bl, length=gl, parts=3, name="short_conv_lat")
    fargs = (c_ffn_w1[0], c_ffn_b1[0], c_ffn_w2[0], c_ffn_b2[0], c_ffn_w3[0], c_ffn_freq[0], c_log_decay[0])
    g_c = hyena_two_sided_filters(lc, *fargs, name="filters_ctx")
    g_l = hyena_two_sided_filters(gl, *fargs, name="filters_lat")

    nc = 2 * lc
    mf_data, mf_filt, mi_c = _dft_mats_single(nc, lc, lc)
    spec_c = spectral(g_c.reshape(2, 1, 1, nc, d), mf_filt, dt=512, name="filt_spec_ctx")
    z_c = u_c[0].reshape(bc // 2, 2, 1, lc, d)
    for o in range(2):
        gate = u_c[o + 1].reshape(bc // 2, 2, 1, lc, d)
        z_c = spectral(z_c, mf_data, spec_c, mi_c, dt=512, g_index=o, epi=(gate, z_c, row2(c_skip[0, o])),
                       name=f"conv_ctx{o}")
    zc_out = z_c.reshape(gl, d)

    nl = 2 * gl
    n1 = nl // FFT_N2
    a_data, a_filt, a_inv, mf_l, mi_l = _dft_mats_two_stage(nl, FFT_N2)
    ga = slow_stage(a_filt, g_l.reshape(2, n1, FFT_N2, d), name="filt_stage_a")
    spec_l = spectral(ga.reshape(2, 2, n1, FFT_N2, d), mf_l, dt=1024, name="filt_spec_lat")
    z_l = u_l[0].reshape(bl // 2, n1, FFT_N2, d)
    for o in range(2):
        za = slow_stage(a_data, z_l, name=f"conv_lat_a{o}")
        zb = spectral(za.reshape(bl // 2, 2, n1, FFT_N2, d), mf_l, spec_l, mi_l, dt=1024, g_index=o,
                      name=f"conv_lat_c{o}")
        gate = u_l[o + 1].reshape(bl // 2, n1, FFT_N2, d)
        z_l = slow_stage(a_inv, zb.reshape(bl // 2, 2 * n1, FFT_N2, d), epi=(gate, z_l, row2(c_skip[0, o])),
                         name=f"conv_lat_i{o}")
    x = finish_layer(i, zc_out, z_l.reshape(bl * gl, d), c_wo[0], x)

    i = 3

    def body_d(x_ref, mod_ref, wq_ref, wk_ref, wv_ref, q_ref, kh_ref, vh_ref, kc_ref, vc_ref):
        h = _modulate(x_ref, mod_ref, 0, 1).astype(BF16)
        q_ref[...] = (_dot(h, wq_ref[...]) * QSCALE_64).astype(BF16)
        k = _dot(h, wk_ref[...])
        v = _dot(h, wv_ref[...])
        _store_heads(kh_ref, k, hd)
        _store_value_heads(vh_ref, v, hd)

        @pl.when(pl.program_id(0) < n_ctx_tiles)
        def _():
            kc_ref[...] = k
            vc_ref[...] = v

    q, k_heads, v_heads, k_new, v_new = rc(
        body_d, t_all,
        [(x, "row"), (mods[i], "group"), (d_wq[0].astype(BF16), "full"), (d_wk[0].astype(BF16), "full"),
         (d_wv[0].astype(BF16), "full")],
        [(d, BF16), (D_HEADS, hd, BF16, "heads"), (D_HEADS, V_LANES, BF16, "heads"), (gl, d, F32, "ctx"), (gl, d, F32, "ctx")],
        name="proj_d")
    out_d_k = k_new.reshape(bc, 1, lc, D_HEADS, hd)
    out_d_v = v_new.reshape(bc, 1, lc, D_HEADS, hd)
    common = dict(n_heads=D_HEADS, dq=hd, dv=hd, hpb=4, kvpb=4, tq=tq)
    att_c = attention(q, k_heads, v_heads, n_seq=bc, lq=lc, row_off=0, win=lc, name="attn_d_ctx", **common)
    rows = gl // GRID_W
    kh = min(MAX_NBR_ROWS, rows)
    qrows = tq // GRID_W
    krows = qrows + kh
    win_d = krows * GRID_W
    nq_l = gl // tq
    qr_l, kr_l = jnp.arange(qrows)[:, None], jnp.arange(krows)[None, :]
    qc, kc_ = jnp.arange(GRID_W)[:, None], jnp.arange(GRID_W)[None, :]
    c0 = jnp.clip(qc - NBR_COLS // 2, 0, GRID_W - NBR_COLS)
    col_ok = (kc_ >= c0) & (kc_ < c0 + NBR_COLS)
    dc = jnp.clip(kc_ - qc, 1 - NBR_COLS, NBR_COLS - 1) + NBR_COLS - 1
    dc_hot = (dc[..., None] == jnp.arange(2 * NBR_COLS - 1)).astype(F32)
    by_col = jnp.einsum("hrc,xyc->hrxy", d_rel_bias[0], dc_hot, precision=lax.Precision.HIGHEST)
    tabs = []
    for off, lo in ((0, jnp.zeros_like(qr_l)), (kh // 2, qr_l), (kh, jnp.full_like(qr_l, qrows))):
        row_ok = (kr_l >= lo) & (kr_l < lo + kh)
        dr = jnp.clip(kr_l - off - qr_l + MAX_NBR_ROWS - 1, 0, 2 * MAX_NBR_ROWS - 2)
        dr_hot = (dr[..., None] == jnp.arange(2 * MAX_NBR_ROWS - 1)).astype(F32)
        tab = jnp.einsum("qkr,hrxy->hqxky", dr_hot, by_col, precision=lax.Precision.HIGHEST)
        ok = row_ok[:, None, :, None] & col_ok[None, :, None, :]
        tabs.append(jnp.where(ok[None], tab, NEG_INF).reshape(D_HEADS, tq, win_d))
    nbr_bias = jnp.stack(tabs, axis=1).astype(F32)
    att_l = attention(q, k_heads, v_heads, n_seq=bl, lq=gl, row_off=gl, win=win_d,
                      start_fn=lambda ii: jnp.clip(ii * qrows - kh // 2, 0, rows - krows) * GRID_W,
                      kc=_cache_heads_major(cache_d_k[:, 0]), vc=_cache_heads_major(cache_d_v[:, 0], values=True),
                      bias=nbr_bias, type_fn=lambda ii: jnp.where(ii == 0, 0, jnp.where(ii == nq_l - 1, 2, 1)),
                      name="attn_d_lat", **common)
    y_ctx, y_lat = finish_layer(i, att_c, att_l, d_wo[0], x)

    y_prompt = y_ctx.reshape(bc, lc, d)
    y_sample = y_lat.reshape(bl, gl, d)
    return (y_prompt, y_sample, out_a_k, out_a_v, out_b_ckv, out_b_krope, out_d_k, out_d_v)
```
